```python
import math
import jax, jax.numpy as jnp
from jax import lax
import numpy as np

D_MODEL = 1024
BATCH = 8
SEQ = 4096
DEPTH = 2
DEC_BATCH = 128
DEC_SEQ = 4
PAST_LEN = 16384
PAGE_SIZE = 128

N_MIXERS = 2
HEAD_DIM = 64
A_HEADS = 12
A_KV_HEADS = 4
A_GROUP = A_HEADS // A_KV_HEADS
WINDOW = 128
BLOCK = 128
W_BUF = min(WINDOW, PAST_LEN)
B_WIDTH = 768
B_GROUPS = 4
B_GROUP_DIM = B_WIDTH // B_GROUPS
CHUNK = 128
X_HEADS = 4
X_WIDTH = X_HEADS * HEAD_DIM
N_MEM = 256
D_FF = 2816
CONV_W = 3
N_A = (DEPTH + 1) // 2
N_B = DEPTH // 2
A_Q = A_HEADS * HEAD_DIM
A_KV = A_KV_HEADS * HEAD_DIM
A_IN = A_Q + 2 * A_KV + X_WIDTH
A_OUT = A_Q + X_WIDTH
B_IN = 2 * B_WIDTH + X_WIDTH
B_OUT = B_WIDTH + X_WIDTH
EPS = 1e-6
NEG = -1e30

kernel_name = 'swa_sink_gmlp_chunk_memattn_convffn_step'


def rms_norm(x, g):
    xf = x.astype(jnp.float32)
    y = xf * lax.rsqrt(jnp.mean(xf * xf, axis=-1, keepdims=True) + EPS)
    return (y * g.astype(jnp.float32)).astype(x.dtype)


def alibi_slopes(n):
    def pow2(m):
        start = 2.0 ** (-8.0 / m)
        return [start ** (i + 1) for i in range(m)]
    p = 2 ** int(math.floor(math.log2(n)))
    s = pow2(p)
    if p < n:
        s = s + pow2(2 * p)[0::2][: n - p]
    return jnp.asarray(s, jnp.float32)


def sink_gqa_attention(q, k, v, dist, valid, slopes, sinks):
    *lead, nq, _, hd = q.shape
    qg = q.reshape(*lead, nq, A_KV_HEADS, A_GROUP, hd)
    s = jnp.einsum('...qhgd,...khd->...hgqk', qg, k).astype(jnp.float32) * (hd ** -0.5)
    s = s - slopes.reshape(A_KV_HEADS, A_GROUP, 1, 1) * dist[..., None, None, :, :]
    s = jnp.where(valid[..., None, None, :, :], s, NEG)
    sink = sinks.astype(jnp.float32).reshape(A_KV_HEADS, A_GROUP, 1, 1)
    m = jnp.maximum(jnp.max(s, axis=-1, keepdims=True), sink)
    e = jnp.exp(s - m)
    p = e / (jnp.sum(e, axis=-1, keepdims=True) + jnp.exp(sink - m))
    o = jnp.einsum('...hgqk,...khd->...qhgd', p.astype(v.dtype), v)
    return o.reshape(*lead, nq, A_HEADS * hd)


def swa_prompt(q, k, v, slopes, sinks):
    n, t = q.shape[:2]
    nb = t // BLOCK
    qb = q.reshape(n, nb, BLOCK, A_HEADS, HEAD_DIM)

    def band(x):
        xb = x.reshape(n, nb, BLOCK, A_KV_HEADS, HEAD_DIM)
        prev = jnp.pad(xb[:, :-1], ((0, 0), (1, 0), (0, 0), (0, 0), (0, 0)))
        return jnp.concatenate([prev, xb], axis=2)

    i = jnp.arange(BLOCK)[:, None]
    j = jnp.arange(2 * BLOCK)[None, :]
    d = i + BLOCK - j
    kpos = jnp.arange(nb)[:, None, None] * BLOCK - BLOCK + j[None]
    valid = (d >= 0) & (d < WINDOW) & (kpos >= 0)
    dist = jnp.broadcast_to(d.astype(jnp.float32), valid.shape)
    o = sink_gqa_attention(qb, band(k), band(v), dist[None], valid[None], slopes, sinks)
    return o.reshape(n, t, A_Q)


def swa_sample(q, k_new, v_new, k_buf, v_buf, slopes, sinks):
    s = q.shape[1]
    k = jnp.concatenate([k_buf, k_new], axis=1)
    v = jnp.concatenate([v_buf, v_new], axis=1)
    i = jnp.arange(s)[:, None]
    j = jnp.arange(W_BUF + s)[None, :]
    d = i + W_BUF - j
    valid = (d >= 0) & (d < WINDOW)
    o = sink_gqa_attention(q, k, v, d.astype(jnp.float32)[None], valid[None], slopes, sinks)
    return o, k[:, -W_BUF:], v[:, -W_BUF:]


def chunk_spatial_gate(u, vn, w_s, b_s):
    n, t, _ = vn.shape
    L = min(t, CHUNK)
    vc = vn.reshape(n, t // L, L, B_GROUPS, B_GROUP_DIM)
    w = jnp.tril(w_s[:, :L, :L])
    mixed = jnp.einsum('gij,ncjgd->ncigd', w, vc) + b_s[:, :L].T[None, None, :, :, None]
    return u * mixed.reshape(n, t, B_WIDTH)


def memory_kv(mem, g, w, kg):
    n, m, _ = mem.shape
    kv = rms_norm(mem, g) @ w
    mk, mv = jnp.split(kv, 2, axis=-1)
    mk = rms_norm(mk.reshape(n, m, X_HEADS, HEAD_DIM), kg)
    return mk, mv.reshape(n, m, X_HEADS, HEAD_DIM)


def memory_attention(q, mk, mv):
    n, t = q.shape[:2]
    s = jnp.einsum('nthd,nmhd->nhtm', q, mk).astype(jnp.float32) * (HEAD_DIM ** -0.5)
    p = jax.nn.softmax(s, axis=-1)
    o = jnp.einsum('nhtm,nmhd->nthd', p.astype(mv.dtype), mv)
    return o.reshape(n, t, X_WIDTH)


def conv_ffn(x, g, w_in, conv_w, conv_b, w_down, prefix):
    t = x.shape[1]
    gu = rms_norm(x, g) @ w_in
    a, up = jnp.split(gu, 2, axis=-1)
    full = jnp.concatenate([prefix, a], axis=1)
    c = conv_b + sum(conv_w[r] * full[:, r:r + t] for r in range(CONV_W))
    y = (jax.nn.silu(c) * up) @ w_down
    return y, full[:, -(CONV_W - 1):]


def run_group(x, p, is_prompt, mem=None, mem_k_in=None, mem_v_in=None,
              win_k_in=None, win_v_in=None, conv_in=None):
    n, t, _ = x.shape
    slopes = alibi_slopes(A_HEADS)
    win_k, win_v, chunk_v, mem_ks, mem_vs, convs = [], [], [], [], [], []
    for i in range(DEPTH):
        if is_prompt:
            mk, mv = memory_kv(mem, p['mem_norm_g'][i], p['w_mem_kv'][i], p['xk_norm'][i])
            mem_ks.append(mk)
            mem_vs.append(mv)
        else:
            mk, mv = mem_k_in[i], mem_v_in[i]
        h = rms_norm(x, p['norm1_g'][i])
        if i % N_MIXERS == 0:
            a = i // N_MIXERS
            z = h @ p['w_in_a'][a]
            q, k, v, xq = jnp.split(z, [A_Q, A_Q + A_KV, A_Q + 2 * A_KV], axis=-1)
            q = rms_norm(q.reshape(n, t, A_HEADS, HEAD_DIM), p['q_norm_a'][a])
            k = rms_norm(k.reshape(n, t, A_KV_HEADS, HEAD_DIM), p['k_norm_a'][a])
            v = v.reshape(n, t, A_KV_HEADS, HEAD_DIM)
            if is_prompt:
                o = swa_prompt(q, k, v, slopes, p['sinks_a'][a])
                nk, nv = k[:, t - W_BUF:], v[:, t - W_BUF:]
            else:
                o, nk, nv = swa_sample(q, k, v, win_k_in[a], win_v_in[a], slopes, p['sinks_a'][a])
            win_k.append(nk)
            win_v.append(nv)
            w_out = p['w_out_a'][a]
        else:
            b = i // N_MIXERS
            z = h @ p['w_in_b'][b]
            uv, xq = z[..., :2 * B_WIDTH], z[..., 2 * B_WIDTH:]
            u, v = jnp.split(jax.nn.gelu(uv, approximate=False), 2, axis=-1)
            vn = rms_norm(v, p['v_norm_b'][b])
            o = chunk_spatial_gate(u, vn, p['w_s_b'][b], p['b_s_b'][b])
            chunk_v.append(vn[:, ((t - 1) // CHUNK) * CHUNK:])
            w_out = p['w_out_b'][b]
        xq = rms_norm(xq.reshape(n, t, X_HEADS, HEAD_DIM), p['xq_norm'][i])
        xo = memory_attention(xq, mk, mv)
        x = x + jnp.concatenate([o, xo], axis=-1) @ w_out
        prefix = jnp.zeros((n, CONV_W - 1, D_FF), x.dtype) if is_prompt else conv_in[i]
        yf, cst = conv_ffn(x, p['norm2_g'][i], p['w_ffn_in'][i], p['conv_w'][i],
                           p['conv_b'][i], p['w_down'][i], prefix)
        convs.append(cst)
        x = x + yf
    return (x, jnp.stack(win_k), jnp.stack(win_v), jnp.stack(chunk_v),
            mem_ks, mem_vs, jnp.stack(convs))


def setup_inputs(seed: int = 0) -> dict:
    key = jax.random.key(seed)
    ks = jax.random.split(key, 32)
    f32 = jnp.float32
    res = (2 * DEPTH) ** -0.5

    def nrm(k, shape, scale):
        return jax.random.normal(k, shape, f32) * scale

    def gain(k, shape):
        return 1.0 + 0.02 * jax.random.normal(k, shape, f32)

    return {
        'x_prompt': nrm(ks[0], (BATCH, SEQ, D_MODEL), 1.0),
        'x_sample': nrm(ks[1], (DEC_BATCH, DEC_SEQ, D_MODEL), 1.0),
        'cache_win_k': nrm(ks[2], (N_A, DEC_BATCH, W_BUF, A_KV_HEADS, HEAD_DIM), 1.0),
        'cache_win_v': nrm(ks[3], (N_A, DEC_BATCH, W_BUF, A_KV_HEADS, HEAD_DIM), 1.0),
        'cache_mem_k': nrm(ks[4], (DEPTH, DEC_BATCH, N_MEM, X_HEADS, HEAD_DIM), 1.0),
        'cache_mem_v': nrm(ks[5], (DEPTH, DEC_BATCH, N_MEM, X_HEADS, HEAD_DIM), 1.0),
        'state_conv': nrm(ks[6], (DEPTH, DEC_BATCH, CONV_W - 1, D_FF), 1.0),
        'mem_prompt': nrm(ks[7], (BATCH, N_MEM, D_MODEL), 1.0),
        'norm1_g': gain(ks[8], (DEPTH, D_MODEL)),
        'norm2_g': gain(ks[9], (DEPTH, D_MODEL)),
        'mem_norm_g': gain(ks[10], (DEPTH, D_MODEL)),
        'w_in_a': nrm(ks[11], (N_A, D_MODEL, A_IN), D_MODEL ** -0.5),
        'q_norm_a': gain(ks[12], (N_A, HEAD_DIM)),
        'k_norm_a': gain(ks[13], (N_A, HEAD_DIM)),
        'sinks_a': nrm(ks[14], (N_A, A_HEADS), 0.5),
        'w_out_a': nrm(ks[15], (N_A, A_OUT, D_MODEL), A_OUT ** -0.5 * res),
        'w_in_b': nrm(ks[16], (N_B, D_MODEL, B_IN), D_MODEL ** -0.5),
        'v_norm_b': gain(ks[17], (N_B, B_WIDTH)),
        'w_s_b': nrm(ks[18], (N_B, B_GROUPS, CHUNK, CHUNK), CHUNK ** -0.5),
        'b_s_b': gain(ks[19], (N_B, B_GROUPS, CHUNK)),
        'w_out_b': nrm(ks[20], (N_B, B_OUT, D_MODEL), B_OUT ** -0.5 * res),
        'w_mem_kv': nrm(ks[21], (DEPTH, D_MODEL, 2 * X_WIDTH), D_MODEL ** -0.5),
        'xq_norm': gain(ks[22], (DEPTH, HEAD_DIM)),
        'xk_norm': gain(ks[23], (DEPTH, HEAD_DIM)),
        'w_ffn_in': nrm(ks[24], (DEPTH, D_MODEL, 2 * D_FF), D_MODEL ** -0.5),
        'conv_w': nrm(ks[25], (DEPTH, CONV_W, D_FF), CONV_W ** -0.5),
        'conv_b': nrm(ks[26], (DEPTH, D_FF), 0.02),
        'w_down': nrm(ks[27], (DEPTH, D_FF, D_MODEL), D_FF ** -0.5 * res),
    }


def reference(x_prompt, x_sample, cache_win_k, cache_win_v, cache_mem_k, cache_mem_v,
              state_conv, mem_prompt, norm1_g, norm2_g, mem_norm_g, w_in_a, q_norm_a,
              k_norm_a, sinks_a, w_out_a, w_in_b, v_norm_b, w_s_b, b_s_b, w_out_b,
              w_mem_kv, xq_norm, xk_norm, w_ffn_in, conv_w, conv_b, w_down):
    p = dict(norm1_g=norm1_g, norm2_g=norm2_g, mem_norm_g=mem_norm_g, w_in_a=w_in_a,
             q_norm_a=q_norm_a, k_norm_a=k_norm_a, sinks_a=sinks_a, w_out_a=w_out_a,
             w_in_b=w_in_b, v_norm_b=v_norm_b, w_s_b=w_s_b, b_s_b=b_s_b, w_out_b=w_out_b,
             w_mem_kv=w_mem_kv, xq_norm=xq_norm, xk_norm=xk_norm, w_ffn_in=w_ffn_in,
             conv_w=conv_w, conv_b=conv_b, w_down=w_down)
    (y_prompt, win_k_prompt, win_v_prompt, chunk_v_prompt, mem_ks, mem_vs,
     conv_prompt) = run_group(x_prompt, p, True, mem=mem_prompt)
    mem_k_prompt = jnp.stack(mem_ks)
    mem_v_prompt = jnp.stack(mem_vs)
    (y_sample, win_k_sample, win_v_sample, chunk_v_sample, _, _,
     conv_sample) = run_group(x_sample, p, False, mem_k_in=cache_mem_k, mem_v_in=cache_mem_v,
                              win_k_in=cache_win_k, win_v_in=cache_win_v, conv_in=state_conv)
    return (y_prompt, y_sample, win_k_prompt, win_v_prompt, chunk_v_prompt, mem_k_prompt,
            mem_v_prompt, conv_prompt, win_k_sample, win_v_sample, chunk_v_sample, conv_sample)
```

```python
import functools
import math

import numpy as np
import jax
import jax.numpy as jnp
from jax import lax
from jax.experimental import pallas as pl
from jax.experimental.pallas import tpu as pltpu

F32 = jnp.float32
BF16 = jnp.bfloat16

D_MODEL = 1024
HEAD_DIM = 64
A_HEADS = 12
A_KV_HEADS = 4
A_GROUP = A_HEADS // A_KV_HEADS
WINDOW = 128
BLOCK = 128
B_WIDTH = 768
B_GROUPS = 4
B_GROUP_DIM = B_WIDTH // B_GROUPS
CHUNK = 128
X_HEADS = 4
X_WIDTH = X_HEADS * HEAD_DIM
N_MEM = 256
D_FF = 2816
CONV_W = 3
A_Q = A_HEADS * HEAD_DIM
A_KV = A_KV_HEADS * HEAD_DIM
EPS = 1e-6
NEG = -1e30
SCALE = HEAD_DIM ** -0.5
SQRT_HALF = float(np.sqrt(0.5).astype(np.float32))

VMEM_LIMIT_BYTES = 56 * 1024 * 1024
FFN_CHUNKS = ((0, 1024), (1024, 2048), (2048, D_FF))
PROMPT_ROWS = 256
SAMPLE_SEQS = 8
SAMPLE_ROWS = 128


def _alibi_slopes(n):
    def pow2(m):
        start = 2.0 ** (-8.0 / m)
        return [start ** (i + 1) for i in range(m)]

    p = 2 ** int(math.floor(math.log2(n)))
    s = pow2(p)
    if p < n:
        s = s + pow2(2 * p)[0::2][: n - p]
    return [float(np.float32(v)) for v in s]


SLOPES = _alibi_slopes(A_HEADS)


def _params(*sem):
    return pltpu.CompilerParams(dimension_semantics=sem, vmem_limit_bytes=VMEM_LIMIT_BYTES)


def _const_spec(shape):
    nd = len(shape)
    return pl.BlockSpec(shape, lambda *_: (0,) * nd, pipeline_mode=pl.Buffered(1))


def _rms_rows(x, g):
    ms = jnp.mean(x * x, axis=-1, keepdims=True)
    return x * lax.rsqrt(ms + EPS) * g


def _pair_head_norm(z, g2):
    t, w = z.shape
    lane = lax.broadcasted_iota(jnp.int32, (t, 128), 1)
    lo = lane < HEAD_DIM
    out = []
    for p in range(w // 128):
        zz = z[:, p * 128:(p + 1) * 128]
        sq = zz * zz
        s_lo = jnp.sum(jnp.where(lo, sq, 0.0), axis=-1, keepdims=True)
        s_hi = jnp.sum(jnp.where(lo, 0.0, sq), axis=-1, keepdims=True)
        r_lo = lax.rsqrt(s_lo * (1.0 / HEAD_DIM) + EPS)
        r_hi = lax.rsqrt(s_hi * (1.0 / HEAD_DIM) + EPS)
        out.append(zz * jnp.where(lo, r_lo, r_hi) * g2)
    return out


def _slot_head_norm(z, g4):
    ms = jnp.sum(z * z, axis=-1, keepdims=True) * (1.0 / HEAD_DIM)
    return z * lax.rsqrt(ms + EPS) * g4


def _nt_dot(a, b):
    return lax.dot_general(a, b, (((1,), (1,)), ((), ())), preferred_element_type=F32)


def _dot(a, b):
    return jnp.dot(a, b, preferred_element_type=F32)


def _mem_attention_rows(xq_b, mk_b, mv_b):
    s = _nt_dot(xq_b, mk_b) * SCALE
    m = jnp.max(s, axis=-1, keepdims=True)
    e = jnp.exp(s - m)
    den = jnp.sum(e, axis=-1, keepdims=True)
    return _dot(e.astype(BF16), mv_b) / den


def _pos_in_seq(shape, seq):
    r = lax.broadcasted_iota(jnp.int32, shape, 0)
    return r & (seq - 1) if seq & (seq - 1) == 0 else r % seq


def _gelu_exact(x):
    return 0.5 * x * (1.0 + lax.erf(x * SQRT_HALF))


def _memkv_kernel(mem_ref, g_ref, w_ref, kg_ref, mk_ref, mv_ref):
    m = mem_ref[0]
    h = _rms_rows(m, g_ref[0]).astype(BF16)
    kv = _dot(h, w_ref[0])
    pairs = _pair_head_norm(kv[:, :X_WIDTH], kg_ref[0])
    for p, val in enumerate(pairs):
        mk_ref[0, 0, :, p * 128:(p + 1) * 128] = val
    mv_ref[0, 0] = kv[:, X_WIDTH:]


def _memory_kv(mem, mem_norm_g, w_mem_kv_b, xk_norm):
    depth = w_mem_kv_b.shape[0]
    n = mem.shape[0]
    g = mem_norm_g.reshape(depth, 1, D_MODEL)
    kg2 = jnp.tile(xk_norm, (1, 2)).reshape(depth, 1, 128)
    out = jax.ShapeDtypeStruct((depth, n, N_MEM, X_WIDTH), F32)
    return pl.pallas_call(
        _memkv_kernel,
        grid=(depth, n),
        in_specs=[
            pl.BlockSpec((1, N_MEM, D_MODEL), lambda l, b: (b, 0, 0)),
            pl.BlockSpec((1, 1, D_MODEL), lambda l, b: (l, 0, 0)),
            pl.BlockSpec((1, D_MODEL, 2 * X_WIDTH), lambda l, b: (l, 0, 0)),
            pl.BlockSpec((1, 1, 128), lambda l, b: (l, 0, 0)),
        ],
        out_specs=[
            pl.BlockSpec((1, 1, N_MEM, X_WIDTH), lambda l, b: (l, b, 0, 0)),
            pl.BlockSpec((1, 1, N_MEM, X_WIDTH), lambda l, b: (l, b, 0, 0)),
        ],
        out_shape=[out, out],
        compiler_params=_params("arbitrary", "arbitrary"),
        name="memory_kv",
    )(mem, g, w_mem_kv_b, kg2)


def _prompt_mem_attention(xq_pairs, mk_ref, mv_ref, ocat, col0):
    mk = mk_ref[0, 0].astype(BF16)
    mv = mv_ref[0, 0].astype(BF16)
    for h in range(X_HEADS):
        xq_b = xq_pairs[h // 2][:, (h % 2) * HEAD_DIM:(h % 2 + 1) * HEAD_DIM].astype(BF16)
        sl = slice(h * HEAD_DIM, (h + 1) * HEAD_DIM)
        o = _mem_attention_rows(xq_b, mk[:, sl], mv[:, sl])
        ocat[:, col0 + h * HEAD_DIM:col0 + (h + 1) * HEAD_DIM] = o.astype(BF16)


def _mixer_a_kernel(sinks_ref, x_ref, g1_ref, win_ref, qg_ref, kg_ref, xqg_ref, mk_ref, mv_ref, wout_ref,
                    x1_ref, wk_ref, wv_ref, kbuf, vbuf, ocat, *, tb):
    t = pl.program_id(1)

    @pl.when(t == 0)
    def _():
        kbuf[0:BLOCK, :] = jnp.zeros((BLOCK, A_KV), BF16)
        vbuf[0:BLOCK, :] = jnp.zeros((BLOCK, A_KV), BF16)

    x = x_ref[0]
    h = _rms_rows(x, g1_ref[...]).astype(BF16)
    z = _dot(h, win_ref[...])
    q_pairs = _pair_head_norm(z[:, :A_Q], qg_ref[...])
    k_pairs = _pair_head_norm(z[:, A_Q:A_Q + A_KV], kg_ref[...])
    v = z[:, A_Q + A_KV:A_Q + 2 * A_KV]
    xq_pairs = _pair_head_norm(z[:, A_Q + 2 * A_KV:], xqg_ref[...])

    for p, kp in enumerate(k_pairs):
        wk_ref[0, :, p * 128:(p + 1) * 128] = kp[tb - BLOCK:, :]
        kbuf[BLOCK:BLOCK + tb, p * 128:(p + 1) * 128] = kp.astype(BF16)
    wv_ref[0] = v[tb - BLOCK:, :]
    vbuf[BLOCK:BLOCK + tb, :] = v.astype(BF16)

    qi = lax.broadcasted_iota(jnp.int32, (BLOCK, 2 * BLOCK), 0)
    kj = lax.broadcasted_iota(jnp.int32, (BLOCK, 2 * BLOCK), 1)
    d = qi + BLOCK - kj
    in_window = (d >= 0) & (d < WINDOW)
    dist = d.astype(F32)

    for s in range(tb // BLOCK):
        first_key = jnp.where(t > 0, 0, BLOCK) if s == 0 else 0
        valid = in_window & (kj >= first_key)
        for hq in range(A_HEADS):
            hk = hq // A_GROUP
            qh = q_pairs[hq // 2][s * BLOCK:(s + 1) * BLOCK, (hq % 2) * HEAD_DIM:(hq % 2 + 1) * HEAD_DIM]
            kw = kbuf[s * BLOCK:(s + 2) * BLOCK, hk * HEAD_DIM:(hk + 1) * HEAD_DIM]
            vw = vbuf[s * BLOCK:(s + 2) * BLOCK, hk * HEAD_DIM:(hk + 1) * HEAD_DIM]
            sc = _nt_dot(qh.astype(BF16), kw) * SCALE
            sc = sc - SLOPES[hq] * dist
            sc = jnp.where(valid, sc, NEG)
            sink = sinks_ref[hq]
            m = jnp.maximum(jnp.max(sc, axis=-1, keepdims=True), sink)
            e = jnp.exp(sc - m)
            den = jnp.sum(e, axis=-1, keepdims=True) + jnp.exp(sink - m)
            o = _dot(e.astype(BF16), vw) / den
            ocat[s * BLOCK:(s + 1) * BLOCK, hq * HEAD_DIM:(hq + 1) * HEAD_DIM] = o.astype(BF16)

    _prompt_mem_attention(xq_pairs, mk_ref, mv_ref, ocat, A_Q)
    x1_ref[0] = x + _dot(ocat[...], wout_ref[...])

    kbuf[0:BLOCK, :] = kbuf[tb:tb + BLOCK, :]
    vbuf[0:BLOCK, :] = vbuf[tb:tb + BLOCK, :]


def _mixer_a_prompt(x, layer, norm1_g, w_in_b, q_norm, k_norm, xq_norm, sinks, mk_all, mv_all, w_out_b):
    n, t, _ = x.shape
    tb = min(PROMPT_ROWS, t)
    a_in = w_in_b.shape[1]
    xspec = pl.BlockSpec((1, tb, D_MODEL), lambda b, i: (b, i, 0))
    mspec = pl.BlockSpec((1, 1, N_MEM, X_WIDTH), lambda b, i: (layer, b, 0, 0))
    wspec = pl.BlockSpec((1, BLOCK, A_KV), lambda b, i: (b, 0, 0))
    return pl.pallas_call(
        functools.partial(_mixer_a_kernel, tb=tb),
        grid=(n, t // tb),
        in_specs=[
            pl.BlockSpec(memory_space=pltpu.SMEM),
            xspec,
            _const_spec((1, D_MODEL)),
            _const_spec((D_MODEL, a_in)),
            _const_spec((1, 128)),
            _const_spec((1, 128)),
            _const_spec((1, 128)),
            mspec,
            mspec,
            _const_spec((A_Q + X_WIDTH, D_MODEL)),
        ],
        out_specs=[xspec, wspec, wspec],
        out_shape=[
            jax.ShapeDtypeStruct((n, t, D_MODEL), F32),
            jax.ShapeDtypeStruct((n, BLOCK, A_KV), F32),
            jax.ShapeDtypeStruct((n, BLOCK, A_KV), F32),
        ],
        scratch_shapes=[
            pltpu.VMEM((tb + BLOCK, A_KV), BF16),
            pltpu.VMEM((tb + BLOCK, A_KV), BF16),
            pltpu.VMEM((tb, A_Q + X_WIDTH), BF16),
        ],
        compiler_params=_params("arbitrary", "arbitrary"),
        name="mixer_a_prompt",
    )(sinks, x, norm1_g.reshape(1, D_MODEL), w_in_b, jnp.tile(q_norm, 2).reshape(1, 128),
      jnp.tile(k_norm, 2).reshape(1, 128), jnp.tile(xq_norm, 2).reshape(1, 128), mk_all, mv_all, w_out_b)


def _mixer_b_kernel(x_ref, g1_ref, win_ref, vg_ref, ws_ref, bias_ref, xqg_ref, mk_ref, mv_ref, wout_ref,
                    x1_ref, cv_ref, ocat, *, tb):
    x = x_ref[0]
    h = _rms_rows(x, g1_ref[...]).astype(BF16)
    z = _dot(h, win_ref[...])
    uv = _gelu_exact(z[:, :2 * B_WIDTH])
    u = uv[:, :B_WIDTH]
    vn = _rms_rows(uv[:, B_WIDTH:], vg_ref[...])
    cv_ref[0] = vn[tb - CHUNK:, :]
    xq_pairs = _pair_head_norm(z[:, 2 * B_WIDTH:], xqg_ref[...])

    wi = lax.broadcasted_iota(jnp.int32, (CHUNK, CHUNK), 0)
    wj = lax.broadcasted_iota(jnp.int32, (CHUNK, CHUNK), 1)
    causal = wi >= wj
    w_tril = [jnp.where(causal, ws_ref[g], 0.0).astype(BF16) for g in range(B_GROUPS)]
    lane = lax.broadcasted_iota(jnp.int32, (CHUNK, B_WIDTH), 1)
    bias = bias_ref[...]
    for c in range(tb // CHUNK):
        vc = vn[c * CHUNK:(c + 1) * CHUNK, :].astype(BF16)
        mixed = _dot(w_tril[B_GROUPS - 1], vc)
        for g in range(B_GROUPS - 2, -1, -1):
            mixed = jnp.where(lane < (g + 1) * B_GROUP_DIM, _dot(w_tril[g], vc), mixed)
        o = u[c * CHUNK:(c + 1) * CHUNK, :] * (mixed + bias)
        ocat[c * CHUNK:(c + 1) * CHUNK, 0:B_WIDTH] = o.astype(BF16)

    _prompt_mem_attention(xq_pairs, mk_ref, mv_ref, ocat, B_WIDTH)
    x1_ref[0] = x + _dot(ocat[...], wout_ref[...])


def _gate_bias_rows(b_s):
    return jnp.repeat(b_s.T, B_GROUP_DIM, axis=1)


def _mixer_b_prompt(x, layer, norm1_g, w_in_b, v_norm, w_s, b_s, xq_norm, mk_all, mv_all, w_out_b):
    n, t, _ = x.shape
    tb = min(PROMPT_ROWS, t)
    b_in = w_in_b.shape[1]
    xspec = pl.BlockSpec((1, tb, D_MODEL), lambda b, i: (b, i, 0))
    mspec = pl.BlockSpec((1, 1, N_MEM, X_WIDTH), lambda b, i: (layer, b, 0, 0))
    return pl.pallas_call(
        functools.partial(_mixer_b_kernel, tb=tb),
        grid=(n, t // tb),
        in_specs=[
            xspec,
            _const_spec((1, D_MODEL)),
            _const_spec((D_MODEL, b_in)),
            _const_spec((1, B_WIDTH)),
            _const_spec((B_GROUPS, CHUNK, CHUNK)),
            _const_spec((CHUNK, B_WIDTH)),
            _const_spec((1, 128)),
            mspec,
            mspec,
            _const_spec((B_WIDTH + X_WIDTH, D_MODEL)),
        ],
        out_specs=[xspec, pl.BlockSpec((1, CHUNK, B_WIDTH), lambda b, i: (b, 0, 0))],
        out_shape=[
            jax.ShapeDtypeStruct((n, t, D_MODEL), F32),
            jax.ShapeDtypeStruct((n, CHUNK, B_WIDTH), F32),
        ],
        scratch_shapes=[pltpu.VMEM((tb, B_WIDTH + X_WIDTH), BF16)],
        compiler_params=_params("arbitrary", "arbitrary"),
        name="mixer_b_prompt",
    )(x, norm1_g.reshape(1, D_MODEL), w_in_b, v_norm.reshape(1, B_WIDTH), w_s, _gate_bias_rows(b_s),
      jnp.tile(xq_norm, 2).reshape(1, 128), mk_all, mv_all, w_out_b)


def _ffn_chunks(xin, h, prev1, prev2, wfi_ref, cw_ref, cb_ref, wd_ref, on_a):
    acc = xin
    for c0, c1 in FFN_CHUNKS:
        a = _dot(h, wfi_ref[:, c0:c1])
        up = _dot(h, wfi_ref[:, D_FF + c0:D_FF + c1])
        am1 = prev1(a, c0, c1)
        am2 = prev2(a, c0, c1)
        on_a(a, c0, c1)
        c = cb_ref[:, c0:c1] + ((cw_ref[0:1, c0:c1] * am2 + cw_ref[1:2, c0:c1] * am1) + cw_ref[2:3, c0:c1] * a)
        act = (c * (1.0 / (1.0 + jnp.exp(-c)))) * up
        acc = acc + _dot(act.astype(BF16), wd_ref[c0:c1, :])
    return acc


def _ffn_prompt_kernel(x_ref, g2_ref, wfi_ref, cw_ref, cb_ref, wd_ref, y_ref, tail_ref, *, tb):
    t = pl.program_id(1)

    @pl.when(t == 0)
    def _():
        tail_ref[0] = jnp.zeros((8, D_FF), F32)

    x = x_ref[0]
    h = _rms_rows(x, g2_ref[...]).astype(BF16)
    row = lax.broadcasted_iota(jnp.int32, (tb, 1), 0)

    def prev1(a, c0, c1):
        return jnp.where(row == 0, tail_ref[0, 7:8, c0:c1], pltpu.roll(a, 1, 0))

    def prev2(a, c0, c1):
        return jnp.where(row == 0, tail_ref[0, 6:7, c0:c1],
                         jnp.where(row == 1, tail_ref[0, 7:8, c0:c1], pltpu.roll(a, 2, 0)))

    def on_a(a, c0, c1):
        tail_ref[0, :, c0:c1] = a[tb - 8:, :]

    y_ref[0] = _ffn_chunks(x, h, prev1, prev2, wfi_ref, cw_ref, cb_ref, wd_ref, on_a)


def _ffn_prompt(x, norm2_g, w_ffn_in_b, conv_w, conv_b, w_down_b):
    n, t, _ = x.shape
    tb = min(PROMPT_ROWS, t)
    xspec = pl.BlockSpec((1, tb, D_MODEL), lambda b, i: (b, i, 0))
    y, tail = pl.pallas_call(
        functools.partial(_ffn_prompt_kernel, tb=tb),
        grid=(n, t // tb),
        in_specs=[
            xspec,
            _const_spec((1, D_MODEL)),
            _const_spec((D_MODEL, 2 * D_FF)),
            _const_spec((CONV_W, D_FF)),
            _const_spec((1, D_FF)),
            _const_spec((D_FF, D_MODEL)),
        ],
        out_specs=[xspec, pl.BlockSpec((1, 8, D_FF), lambda b, i: (b, 0, 0))],
        out_shape=[
            jax.ShapeDtypeStruct((n, t, D_MODEL), F32),
            jax.ShapeDtypeStruct((n, 8, D_FF), F32),
        ],
        compiler_params=_params("arbitrary", "arbitrary"),
        name="ffn_prompt",
    )(x, norm2_g.reshape(1, D_MODEL), w_ffn_in_b, conv_w, conv_b.reshape(1, D_FF), w_down_b)
    return y, tail[:, 8 - (CONV_W - 1):, :]


def _ffn_sample_kernel(x_ref, ocat_ref, wout_ref, g2_ref, wfi_ref, cw_ref, cb_ref, wd_ref, pm1_ref, pm2_ref,
                       y_ref, a_ref, *, rows, seq):
    x1 = x_ref[...] + _dot(ocat_ref[...].astype(BF16), wout_ref[...])
    h = _rms_rows(x1, g2_ref[...]).astype(BF16)
    pos = _pos_in_seq((rows, 1), seq)

    def prev1(a, c0, c1):
        return jnp.where(pos == 0, pm1_ref[:, c0:c1], pltpu.roll(a, 1, 0))

    def prev2(a, c0, c1):
        return jnp.where(pos < 2, pm2_ref[:, c0:c1], pltpu.roll(a, 2, 0))

    def on_a(a, c0, c1):
        a_ref[:, c0:c1] = a

    y_ref[...] = _ffn_chunks(x1, h, prev1, prev2, wfi_ref, cw_ref, cb_ref, wd_ref, on_a)


def _ffn_sample(x, ocat, w_out_e, norm2_g, w_ffn_in_b, conv_w, conv_b, w_down_b, prefix):
    n, s, _ = x.shape
    rows = n * s
    kc = ocat.shape[1]
    pm1 = jnp.pad(prefix[:, 1:2], ((0, 0), (0, s - 1), (0, 0))).reshape(rows, D_FF)
    pm2 = jnp.pad(prefix, ((0, 0), (0, s - 2), (0, 0))).reshape(rows, D_FF)
    rb = min(SAMPLE_ROWS, rows)
    assert rb % s == 0 and rows % rb == 0

    def rspec(width):
        return pl.BlockSpec((rb, width), lambda i: (i, 0))

    y, a = pl.pallas_call(
        functools.partial(_ffn_sample_kernel, rows=rb, seq=s),
        grid=(rows // rb,),
        in_specs=[
            rspec(D_MODEL),
            rspec(kc),
            _const_spec((kc, D_MODEL)),
            _const_spec((1, D_MODEL)),
            _const_spec((D_MODEL, 2 * D_FF)),
            _const_spec((CONV_W, D_FF)),
            _const_spec((1, D_FF)),
            _const_spec((D_FF, D_MODEL)),
            rspec(D_FF),
            rspec(D_FF),
        ],
        out_specs=[rspec(D_MODEL), rspec(D_FF)],
        out_shape=[
            jax.ShapeDtypeStruct((rows, D_MODEL), F32),
            jax.ShapeDtypeStruct((rows, D_FF), F32),
        ],
        compiler_params=_params("arbitrary"),
        name="ffn_sample",
    )(x.reshape(rows, D_MODEL), ocat, w_out_e, norm2_g.reshape(1, D_MODEL), w_ffn_in_b, conv_w,
      conv_b.reshape(1, D_FF), w_down_b, pm1, pm2)
    return y.reshape(n, s, D_MODEL), a.reshape(n, s, D_FF)[:, s - (CONV_W - 1):, :]


def _sample_in_a_kernel(x_ref, g1_ref, win_ref, qg_ref, kg_ref, xqg_ref, qs_ref, k_ref, v_ref, xs_ref):
    h = _rms_rows(x_ref[...], g1_ref[...]).astype(BF16)
    z = _dot(h, win_ref[...])
    for hq in range(A_HEADS):
        qs_ref[hq] = _slot_head_norm(z[:, hq * A_KV:(hq + 1) * A_KV], qg_ref[...])
    c0 = A_HEADS * A_KV
    for p, kp in enumerate(_pair_head_norm(z[:, c0:c0 + A_KV], kg_ref[...])):
        k_ref[:, p * 128:(p + 1) * 128] = kp
    v_ref[...] = z[:, c0 + A_KV:c0 + 2 * A_KV]
    c0 += 2 * A_KV
    for hx in range(X_HEADS):
        xs_ref[hx] = _slot_head_norm(z[:, c0 + hx * X_WIDTH:c0 + (hx + 1) * X_WIDTH], xqg_ref[...])


def _expand_cols(w, heads, slot_of):
    d = w.shape[0]
    wh = w.reshape(d, heads, 1, HEAD_DIM)
    sel = np.zeros((heads, X_HEADS, 1), np.float32)
    for h in range(heads):
        sel[h, slot_of(h), 0] = 1.0
    return (wh * jnp.asarray(sel, w.dtype)).reshape(d, heads * X_WIDTH)


def _expand_rows(w, heads, slot_of):
    return _expand_cols(w.T, heads, slot_of).T


def _sample_in_a(x2, norm1_g, w_in_b, q_norm, k_norm, xq_norm):
    rows = x2.shape[0]
    wq = _expand_cols(w_in_b[:, :A_Q], A_HEADS, lambda h: h // A_GROUP)
    wx = _expand_cols(w_in_b[:, A_Q + 2 * A_KV:], X_HEADS, lambda h: h)
    w_e = jnp.concatenate([wq, w_in_b[:, A_Q:A_Q + 2 * A_KV], wx], axis=1)
    return pl.pallas_call(
        _sample_in_a_kernel,
        grid=(1,),
        in_specs=[
            _const_spec((rows, D_MODEL)),
            _const_spec((1, D_MODEL)),
            _const_spec(w_e.shape),
            _const_spec((1, A_KV)),
            _const_spec((1, 128)),
            _const_spec((1, X_WIDTH)),
        ],
        out_specs=[
            pl.BlockSpec((A_HEADS, rows, A_KV), lambda i: (0, 0, 0)),
            pl.BlockSpec((rows, A_KV), lambda i: (0, 0)),
            pl.BlockSpec((rows, A_KV), lambda i: (0, 0)),
            pl.BlockSpec((X_HEADS, rows, X_WIDTH), lambda i: (0, 0, 0)),
        ],
        out_shape=[
            jax.ShapeDtypeStruct((A_HEADS, rows, A_KV), F32),
            jax.ShapeDtypeStruct((rows, A_KV), F32),
            jax.ShapeDtypeStruct((rows, A_KV), F32),
            jax.ShapeDtypeStruct((X_HEADS, rows, X_WIDTH), F32),
        ],
        compiler_params=_params("arbitrary"),
        name="sample_in_a",
    )(x2, norm1_g.reshape(1, D_MODEL), w_e, jnp.tile(q_norm, 4).reshape(1, A_KV),
      jnp.tile(k_norm, 2).reshape(1, 128), jnp.tile(xq_norm, 4).reshape(1, X_WIDTH))


def _sample_in_b_kernel(x_ref, g1_ref, win_ref, vg_ref, coef_ref, gbias_ref, xqg_ref, o_ref, vn_ref, xs_ref,
                        *, rows, seq):
    h = _rms_rows(x_ref[...], g1_ref[...]).astype(BF16)
    z = _dot(h, win_ref[...])
    uv = _gelu_exact(z[:, :2 * B_WIDTH])
    u = uv[:, :B_WIDTH]
    vn = _rms_rows(uv[:, B_WIDTH:], vg_ref[...])
    vn_ref[...] = vn
    mixed = gbias_ref[...] + coef_ref[0] * vn
    for delta in range(1, seq):
        mixed = mixed + coef_ref[delta] * pltpu.roll(vn, delta, 0)
    o_ref[...] = u * mixed
    c0 = 2 * B_WIDTH
    for hx in range(X_HEADS):
        xs_ref[hx] = _slot_head_norm(z[:, c0 + hx * X_WIDTH:c0 + (hx + 1) * X_WIDTH], xqg_ref[...])


def _sample_in_b(x2, seq, norm1_g, w_in_b, v_norm, w_s, b_s, xq_norm):
    rows = x2.shape[0]
    n = rows // seq
    wx = _expand_cols(w_in_b[:, 2 * B_WIDTH:], X_HEADS, lambda h: h)
    w_e = jnp.concatenate([w_in_b[:, :2 * B_WIDTH], wx], axis=1)
    wt = jnp.tril(w_s[:, :seq, :seq])
    coef = []
    for delta in range(seq):
        rows_d = [wt[:, i, i - delta] if i - delta >= 0 else jnp.zeros((B_GROUPS,), w_s.dtype) for i in range(seq)]
        cd = jnp.repeat(jnp.stack(rows_d), B_GROUP_DIM, axis=1)
        coef.append(jnp.tile(cd, (n, 1)))
    coef = jnp.stack(coef)
    gbias = jnp.tile(jnp.repeat(b_s[:, :seq].T, B_GROUP_DIM, axis=1), (n, 1))
    return pl.pallas_call(
        functools.partial(_sample_in_b_kernel, rows=rows, seq=seq),
        grid=(1,),
        in_specs=[
            _const_spec((rows, D_MODEL)),
            _const_spec((1, D_MODEL)),
            _const_spec(w_e.shape),
            _const_spec((1, B_WIDTH)),
            _const_spec((seq, rows, B_WIDTH)),
            _const_spec((rows, B_WIDTH)),
            _const_spec((1, X_WIDTH)),
        ],
        out_specs=[
            pl.BlockSpec((rows, B_WIDTH), lambda i: (0, 0)),
            pl.BlockSpec((rows, B_WIDTH), lambda i: (0, 0)),
            pl.BlockSpec((X_HEADS, rows, X_WIDTH), lambda i: (0, 0, 0)),
        ],
        out_shape=[
            jax.ShapeDtypeStruct((rows, B_WIDTH), F32),
            jax.ShapeDtypeStruct((rows, B_WIDTH), F32),
            jax.ShapeDtypeStruct((X_HEADS, rows, X_WIDTH), F32),
        ],
        compiler_params=_params("arbitrary"),
        name="sample_in_b",
    )(x2, norm1_g.reshape(1, D_MODEL), w_e, v_norm.reshape(1, B_WIDTH), coef, gbias,
      jnp.tile(xq_norm, 4).reshape(1, X_WIDTH))


def _seq_mem_attention(xs, mk, mv):
    s = _nt_dot(xs.astype(BF16), mk.astype(BF16)) * SCALE
    m = jnp.max(s, axis=-1, keepdims=True)
    e = jnp.exp(s - m)
    den = jnp.sum(e, axis=-1, keepdims=True)
    return _dot(e.astype(BF16), mv.astype(BF16)) / den


def _sample_attn_a_kernel(qs_ref, k8_ref, v8_ref, xs_ref, slope_ref, sink_ref, ck_ref, cv_ref, mk_ref, mv_ref,
                          o_ref, xo_ref, wk_ref, wv_ref, *, sb, seq):
    rows = A_HEADS * seq
    w = WINDOW
    tq = _pos_in_seq((rows, w), seq)
    j1 = lax.broadcasted_iota(jnp.int32, (rows, w), 1)
    d1 = tq + w - j1
    valid1 = d1 < WINDOW
    tq2 = _pos_in_seq((rows, 8), seq)
    j2 = lax.broadcasted_iota(jnp.int32, (rows, 8), 1)
    d2 = tq2 - j2
    valid2 = (d2 >= 0) & (j2 < seq)
    slope = slope_ref[...]
    sink = sink_ref[...]
    bias1 = slope * d1.astype(F32)
    bias2 = slope * d2.astype(F32)
    sub8 = lax.broadcasted_iota(jnp.int32, (8, A_KV), 0)

    def body(n, carry):
        q = qs_ref[n].astype(BF16)
        ck = ck_ref[n]
        cv = cv_ref[n]
        k8 = k8_ref[n]
        v8 = v8_ref[n]
        s1 = _nt_dot(q, ck.astype(BF16)) * SCALE - bias1
        s2 = _nt_dot(q, k8.astype(BF16)) * SCALE - bias2
        s1 = jnp.where(valid1, s1, NEG)
        s2 = jnp.where(valid2, s2, NEG)
        m = jnp.maximum(jnp.maximum(jnp.max(s1, axis=-1, keepdims=True), jnp.max(s2, axis=-1, keepdims=True)), sink)
        e1 = jnp.exp(s1 - m)
        e2 = jnp.exp(s2 - m)
        den = jnp.sum(e1, axis=-1, keepdims=True) + jnp.sum(e2, axis=-1, keepdims=True) + jnp.exp(sink - m)
        o = _dot(e1.astype(BF16), cv.astype(BF16)) + _dot(e2.astype(BF16), v8.astype(BF16))
        o_ref[n] = o / den
        xo_ref[n] = _seq_mem_attention(xs_ref[n], mk_ref[n], mv_ref[n])
        for src, new8, dst in ((ck, k8, wk_ref), (cv, v8, wv_ref)):
            shifted = pltpu.roll(src, w - seq, 0)
            dst[n] = shifted
            dst[n, w - 8:w, :] = jnp.where(sub8 >= 8 - seq, pltpu.roll(new8, 8 - seq, 0), shifted[w - 8:, :])
        return carry

    lax.fori_loop(0, sb, body, 0)


def _sample_attn_a(qs, k8, v8, xs, slope_rows, sink_rows, ck, cv, mk, mv, seq):
    n = qs.shape[0]
    sb = min(SAMPLE_SEQS, n)
    rows = A_HEADS * seq
    xrows = X_HEADS * seq

    def blk(*tail):
        nz = len(tail)
        return pl.BlockSpec((sb,) + tail, lambda i: (i,) + (0,) * nz)

    return pl.pallas_call(
        functools.partial(_sample_attn_a_kernel, sb=sb, seq=seq),
        grid=(n // sb,),
        in_specs=[
            blk(rows, A_KV), blk(8, A_KV), blk(8, A_KV), blk(xrows, X_WIDTH),
            _const_spec((rows, 1)), _const_spec((rows, 1)),
            blk(WINDOW, A_KV), blk(WINDOW, A_KV), blk(N_MEM, X_WIDTH), blk(N_MEM, X_WIDTH),
        ],
        out_specs=[blk(rows, A_KV), blk(xrows, X_WIDTH), blk(WINDOW, A_KV), blk(WINDOW, A_KV)],
        out_shape=[
            jax.ShapeDtypeStruct((n, rows, A_KV), F32),
            jax.ShapeDtypeStruct((n, xrows, X_WIDTH), F32),
            jax.ShapeDtypeStruct((n, WINDOW, A_KV), F32),
            jax.ShapeDtypeStruct((n, WINDOW, A_KV), F32),
        ],
        compiler_params=_params("arbitrary"),
        name="sample_attn_a",
    )(qs, k8, v8, xs, slope_rows, sink_rows, ck, cv, mk, mv)


def _sample_attn_b_kernel(xs_ref, mk_ref, mv_ref, xo_ref, *, sb):
    def body(n, carry):
        xo_ref[n] = _seq_mem_attention(xs_ref[n], mk_ref[n], mv_ref[n])
        return carry

    lax.fori_loop(0, sb, body, 0)


def _sample_attn_b(xs, mk, mv):
    n, xrows, _ = xs.shape
    sb = min(SAMPLE_SEQS, n)

    def blk(*tail):
        nz = len(tail)
        return pl.BlockSpec((sb,) + tail, lambda i: (i,) + (0,) * nz)

    return pl.pallas_call(
        functools.partial(_sample_attn_b_kernel, sb=sb),
        grid=(n // sb,),
        in_specs=[blk(xrows, X_WIDTH), blk(N_MEM, X_WIDTH), blk(N_MEM, X_WIDTH)],
        out_specs=blk(xrows, X_WIDTH),
        out_shape=jax.ShapeDtypeStruct((n, xrows, X_WIDTH), F32),
        compiler_params=_params("arbitrary"),
        name="sample_attn_b",
    )(xs, mk, mv)


def _slots_to_seq(a, n, seq):
    hh, _, w = a.shape
    return a.reshape(hh, n, seq, w).transpose(1, 0, 2, 3).reshape(n, hh * seq, w)


def _seq_to_cols(a, heads, seq):
    n, _, w = a.shape
    return a.reshape(n, heads, seq, w).transpose(0, 2, 1, 3).reshape(n * seq, heads * w)


def kernel(x_prompt, x_sample, cache_win_k, cache_win_v, cache_mem_k, cache_mem_v, state_conv, mem_prompt,
           norm1_g, norm2_g, mem_norm_g, w_in_a, q_norm_a, k_norm_a, sinks_a, w_out_a, w_in_b, v_norm_b, w_s_b,
           b_s_b, w_out_b, w_mem_kv, xq_norm, xk_norm, w_ffn_in, conv_w, conv_b, w_down):
    depth = norm1_g.shape[0]
    assert depth == 2 and w_in_a.shape[0] == 1 and w_in_b.shape[0] == 1
    nb = x_prompt.shape[0]
    ns, seq, _ = x_sample.shape
    w_buf = cache_win_k.shape[2]
    assert w_buf == WINDOW and 2 <= seq <= 8

    w_in_a_b = w_in_a[0].astype(BF16)
    w_in_b_b = w_in_b[0].astype(BF16)
    w_out_a_b = w_out_a[0].astype(BF16)
    w_out_b_b = w_out_b[0].astype(BF16)
    w_mem_kv_b = w_mem_kv.astype(BF16)
    w_ffn_in_b = w_ffn_in.astype(BF16)
    w_down_b = w_down.astype(BF16)

    mk_all, mv_all = _memory_kv(mem_prompt, mem_norm_g, w_mem_kv_b, xk_norm)
    x1, win_k_p, win_v_p = _mixer_a_prompt(x_prompt, 0, norm1_g[0], w_in_a_b, q_norm_a[0], k_norm_a[0], xq_norm[0],
                                           sinks_a[0], mk_all, mv_all, w_out_a_b)
    x2, conv_p0 = _ffn_prompt(x1, norm2_g[0], w_ffn_in_b[0], conv_w[0], conv_b[0], w_down_b[0])
    x3, chunk_v_p = _mixer_b_prompt(x2, 1, norm1_g[1], w_in_b_b, v_norm_b[0], w_s_b[0], b_s_b[0], xq_norm[1],
                                    mk_all, mv_all, w_out_b_b)
    y_prompt, conv_p1 = _ffn_prompt(x3, norm2_g[1], w_ffn_in_b[1], conv_w[1], conv_b[1], w_down_b[1])

    rows = ns * seq
    xs2 = x_sample.reshape(rows, D_MODEL)
    qs, k_new, v_new, xslots = _sample_in_a(xs2, norm1_g[0], w_in_a_b, q_norm_a[0], k_norm_a[0], xq_norm[0])
    qs = _slots_to_seq(qs, ns, seq)
    xslots = _slots_to_seq(xslots, ns, seq)
    k8 = jnp.pad(k_new.reshape(ns, seq, A_KV), ((0, 0), (0, 8 - seq), (0, 0)))
    v8 = jnp.pad(v_new.reshape(ns, seq, A_KV), ((0, 0), (0, 8 - seq), (0, 0)))
    slope_rows = jnp.repeat(jnp.asarray(SLOPES, F32), seq).reshape(A_HEADS * seq, 1)
    sink_rows = jnp.repeat(sinks_a[0].astype(F32), seq).reshape(A_HEADS * seq, 1)
    o_s, xo_s, win_k_s, win_v_s = _sample_attn_a(
        qs, k8, v8, xslots, slope_rows, sink_rows,
        cache_win_k[0].reshape(ns, w_buf, A_KV), cache_win_v[0].reshape(ns, w_buf, A_KV),
        cache_mem_k[0].reshape(ns, N_MEM, X_WIDTH), cache_mem_v[0].reshape(ns, N_MEM, X_WIDTH), seq)
    ocat = jnp.concatenate([_seq_to_cols(o_s, A_HEADS, seq), _seq_to_cols(xo_s, X_HEADS, seq)], axis=1)
    w_out_a_e = jnp.concatenate([_expand_rows(w_out_a_b[:A_Q], A_HEADS, lambda h: h // A_GROUP),
                                 _expand_rows(w_out_a_b[A_Q:], X_HEADS, lambda h: h)], axis=0)
    ys1, conv_s0 = _ffn_sample(x_sample, ocat, w_out_a_e, norm2_g[0], w_ffn_in_b[0], conv_w[0], conv_b[0],
                               w_down_b[0], state_conv[0])

    o_b, vn_s, xslots = _sample_in_b(ys1.reshape(rows, D_MODEL), seq, norm1_g[1], w_in_b_b, v_norm_b[0], w_s_b[0],
                                     b_s_b[0], xq_norm[1])
    xo_s = _sample_attn_b(_slots_to_seq(xslots, ns, seq), cache_mem_k[1].reshape(ns, N_MEM, X_WIDTH),
                          cache_mem_v[1].reshape(ns, N_MEM, X_WIDTH))
    ocat = jnp.concatenate([o_b, _seq_to_cols(xo_s, X_HEADS, seq)], axis=1)
    w_out_b_e = jnp.concatenate([w_out_b_b[:B_WIDTH], _expand_rows(w_out_b_b[B_WIDTH:], X_HEADS, lambda h: h)], axis=0)
    y_sample, conv_s1 = _ffn_sample(ys1, ocat, w_out_b_e, norm2_g[1], w_ffn_in_b[1], conv_w[1], conv_b[1],
                                    w_down_b[1], state_conv[1])

    kv5 = (A_KV_HEADS, HEAD_DIM)
    return (
        y_prompt,
        y_sample,
        win_k_p.reshape(1, nb, WINDOW, *kv5),
        win_v_p.reshape(1, nb, WINDOW, *kv5),
        chunk_v_p.reshape(1, nb, CHUNK, B_WIDTH),
        mk_all.reshape(depth, nb, N_MEM, X_HEADS, HEAD_DIM),
        mv_all.reshape(depth, nb, N_MEM, X_HEADS, HEAD_DIM),
        jnp.stack([conv_p0, conv_p1]),
        win_k_s.reshape(1, ns, WINDOW, *kv5),
        win_v_s.reshape(1, ns, WINDOW, *kv5),
        vn_s.reshape(1, ns, seq, B_WIDTH),
        jnp.stack([conv_s0, conv_s1]),
    )
```

```python
import functools
import math

import numpy as np
import jax
import jax.numpy as jnp
from jax import lax
from jax.experimental import pallas as pl
from jax.experimental.pallas import tpu as pltpu

F32 = jnp.float32
BF16 = jnp.bfloat16

D_MODEL = 1024
HEAD_DIM = 64
A_HEADS = 12
A_KV_HEADS = 4
A_GROUP = A_HEADS // A_KV_HEADS
WINDOW = 128
BLOCK = 128
B_WIDTH = 768
B_GROUPS = 4
B_GROUP_DIM = B_WIDTH // B_GROUPS
CHUNK = 128
X_HEADS = 4
X_WIDTH = X_HEADS * HEAD_DIM
N_MEM = 256
D_FF = 2816
CONV_W = 3
A_Q = A_HEADS * HEAD_DIM
A_KV = A_KV_HEADS * HEAD_DIM
EPS = 1e-6
NEG = -1e30
SCALE = HEAD_DIM ** -0.5
SQRT_HALF = float(np.sqrt(0.5).astype(np.float32))

VMEM_LIMIT_BYTES = 56 * 1024 * 1024
FFN_CHUNKS = ((0, 1024), (1024, 2048), (2048, D_FF))
GATE_WINDOW_START = (0, 128, 384, 512)
assert B_GROUP_DIM == 192 and B_GROUPS == 4
PROMPT_ROWS = 256
SAMPLE_SEQS = 8
SAMPLE_ROWS = 128


def _alibi_slopes(n):
    def pow2(m):
        start = 2.0 ** (-8.0 / m)
        return [start ** (i + 1) for i in range(m)]

    p = 2 ** int(math.floor(math.log2(n)))
    s = pow2(p)
    if p < n:
        s = s + pow2(2 * p)[0::2][: n - p]
    return [float(np.float32(v)) for v in s]


SLOPES = _alibi_slopes(A_HEADS)

Q_LANE_ORDER = (0, 3, 1, 4, 2, 5, 6, 9, 7, 10, 8, 11)
assert all(Q_LANE_ORDER[p] // A_GROUP == 2 * (p // (2 * A_GROUP)) + p % 2 for p in range(A_HEADS))


def _reorder_heads(w, order, axis):
    n = len(order) * HEAD_DIM
    idx = np.concatenate([np.arange(h * HEAD_DIM, (h + 1) * HEAD_DIM) for h in order] + [np.arange(n, w.shape[axis])])
    return jnp.take(w, jnp.asarray(idx), axis=axis)


def _params(*sem):
    return pltpu.CompilerParams(dimension_semantics=sem, vmem_limit_bytes=VMEM_LIMIT_BYTES)


def _const_spec(shape):
    nd = len(shape)
    return pl.BlockSpec(shape, lambda *_: (0,) * nd, pipeline_mode=pl.Buffered(1))


def _rms_rows(x, g):
    ms = jnp.mean(x * x, axis=-1, keepdims=True)
    return x * lax.rsqrt(ms + EPS) * g


def _pair_head_norm(z, g2):
    t, w = z.shape
    lane = lax.broadcasted_iota(jnp.int32, (t, 128), 1)
    lo = lane < HEAD_DIM
    out = []
    for p in range(w // 128):
        zz = z[:, p * 128:(p + 1) * 128]
        sq = zz * zz
        s_lo = jnp.sum(jnp.where(lo, sq, 0.0), axis=-1, keepdims=True)
        s_hi = jnp.sum(jnp.where(lo, 0.0, sq), axis=-1, keepdims=True)
        r_lo = lax.rsqrt(s_lo * (1.0 / HEAD_DIM) + EPS)
        r_hi = lax.rsqrt(s_hi * (1.0 / HEAD_DIM) + EPS)
        out.append(zz * jnp.where(lo, r_lo, r_hi) * g2)
    return out


def _slot_head_norm(z, g4):
    ms = jnp.sum(z * z, axis=-1, keepdims=True) * (1.0 / HEAD_DIM)
    return z * lax.rsqrt(ms + EPS) * g4


def _nt_dot(a, b):
    return lax.dot_general(a, b, (((1,), (1,)), ((), ())), preferred_element_type=F32)


def _dot(a, b):
    return jnp.dot(a, b, preferred_element_type=F32)


def _mem_attention_rows(xq_b, mk_b, mv_b):
    s = _nt_dot(xq_b, mk_b) * SCALE
    m = jnp.max(s, axis=-1, keepdims=True)
    e = jnp.exp(s - m)
    den = jnp.sum(e, axis=-1, keepdims=True)
    return _dot(e.astype(BF16), mv_b) / den


def _pos_in_seq(shape, seq):
    r = lax.broadcasted_iota(jnp.int32, shape, 0)
    return r & (seq - 1) if seq & (seq - 1) == 0 else r % seq


def _gelu_exact(x):
    return 0.5 * x * (1.0 + lax.erf(x * SQRT_HALF))


def _memkv_kernel(mem_ref, g_ref, w_ref, kg_ref, mk_ref, mv_ref):
    m = mem_ref[0]
    h = _rms_rows(m, g_ref[0]).astype(BF16)
    kv = _dot(h, w_ref[0])
    pairs = _pair_head_norm(kv[:, :X_WIDTH], kg_ref[0])
    for p, val in enumerate(pairs):
        mk_ref[0, 0, :, p * 128:(p + 1) * 128] = val
    mv_ref[0, 0] = kv[:, X_WIDTH:]


def _memory_kv(mem, mem_norm_g, w_mem_kv_b, xk_norm):
    depth = w_mem_kv_b.shape[0]
    n = mem.shape[0]
    g = mem_norm_g.reshape(depth, 1, D_MODEL)
    kg2 = jnp.tile(xk_norm, (1, 2)).reshape(depth, 1, 128)
    out = jax.ShapeDtypeStruct((depth, n, N_MEM, X_WIDTH), F32)
    return pl.pallas_call(
        _memkv_kernel,
        grid=(depth, n),
        in_specs=[
            pl.BlockSpec((1, N_MEM, D_MODEL), lambda l, b: (b, 0, 0)),
            pl.BlockSpec((1, 1, D_MODEL), lambda l, b: (l, 0, 0)),
            pl.BlockSpec((1, D_MODEL, 2 * X_WIDTH), lambda l, b: (l, 0, 0)),
            pl.BlockSpec((1, 1, 128), lambda l, b: (l, 0, 0)),
        ],
        out_specs=[
            pl.BlockSpec((1, 1, N_MEM, X_WIDTH), lambda l, b: (l, b, 0, 0)),
            pl.BlockSpec((1, 1, N_MEM, X_WIDTH), lambda l, b: (l, b, 0, 0)),
        ],
        out_shape=[out, out],
        compiler_params=_params("arbitrary", "arbitrary"),
        name="memory_kv",
    )(mem, g, w_mem_kv_b, kg2)


def _half_masked_rows(pair, lo):
    return jnp.concatenate([jnp.where(lo, pair, 0.0), jnp.where(lo, 0.0, pair)], axis=0).astype(BF16)


def _prompt_mem_attention(xq_pairs, mk_ref, mv_ref, ocat, col0):
    t = xq_pairs[0].shape[0]
    lo = lax.broadcasted_iota(jnp.int32, (t, 128), 1) < HEAD_DIM
    for j, pair in enumerate(xq_pairs):
        mk = mk_ref[0, 0, :, j * 128:(j + 1) * 128].astype(BF16)
        mv = mv_ref[0, 0, :, j * 128:(j + 1) * 128].astype(BF16)
        s = _nt_dot(_half_masked_rows(pair * SCALE, lo), mk)
        m = jnp.max(s, axis=-1, keepdims=True)
        e = jnp.exp(s - m)
        den = jnp.sum(e, axis=-1, keepdims=True)
        o = _dot(e.astype(BF16), mv) / den
        ocat[:, col0 + j * 128:col0 + (j + 1) * 128] = jnp.where(lo, o[:t], o[t:]).astype(BF16)


def _mixer_a_kernel(sinks_ref, x_ref, g1_ref, win_ref, qg_ref, kg_ref, xqg_ref, mk_ref, mv_ref, wout_ref,
                    x1_ref, wk_ref, wv_ref, kbuf, vbuf, ocat, *, tb):
    t = pl.program_id(1)

    @pl.when(t == 0)
    def _():
        kbuf[0:BLOCK, :] = jnp.zeros((BLOCK, A_KV), BF16)
        vbuf[0:BLOCK, :] = jnp.zeros((BLOCK, A_KV), BF16)

    x = x_ref[0]
    h = _rms_rows(x, g1_ref[...]).astype(BF16)
    z = _dot(h, win_ref[...])
    q_pairs = _pair_head_norm(z[:, :A_Q], qg_ref[...])
    k_pairs = _pair_head_norm(z[:, A_Q:A_Q + A_KV], kg_ref[...])
    v = z[:, A_Q + A_KV:A_Q + 2 * A_KV]
    xq_pairs = _pair_head_norm(z[:, A_Q + 2 * A_KV:], xqg_ref[...])

    for p, kp in enumerate(k_pairs):
        wk_ref[0, :, p * 128:(p + 1) * 128] = kp[tb - BLOCK:, :]
        kbuf[BLOCK:BLOCK + tb, p * 128:(p + 1) * 128] = kp.astype(BF16)
    wv_ref[0] = v[tb - BLOCK:, :]
    vbuf[BLOCK:BLOCK + tb, :] = v.astype(BF16)

    qi = lax.broadcasted_iota(jnp.int32, (BLOCK, 2 * BLOCK), 0)
    kj = lax.broadcasted_iota(jnp.int32, (BLOCK, 2 * BLOCK), 1)
    d = qi + BLOCK - kj
    in_window = (d >= 0) & (d < WINDOW)
    dist = d.astype(F32)

    lo = lax.broadcasted_iota(jnp.int32, (BLOCK, 128), 1) < HEAD_DIM
    pairs_per_slot = A_HEADS // A_KV_HEADS
    for s in range(tb // BLOCK):
        rows = slice(s * BLOCK, (s + 1) * BLOCK)
        first_key = jnp.where(t > 0, 0, BLOCK) if s == 0 else 0
        valid = in_window & (kj >= first_key)
        for j in range(A_KV // 128):
            kw = kbuf[s * BLOCK:(s + 2) * BLOCK, j * 128:(j + 1) * 128]
            vw = vbuf[s * BLOCK:(s + 2) * BLOCK, j * 128:(j + 1) * 128]
            q_rows = jnp.concatenate(
                [_half_masked_rows(q_pairs[pairs_per_slot * j + c][rows] * SCALE, lo) for c in range(pairs_per_slot)],
                axis=0)
            sc_all = _nt_dot(q_rows, kw)
            e_parts, den_parts = [], []
            for r in range(2 * pairs_per_slot):
                hq = Q_LANE_ORDER[2 * pairs_per_slot * j + r]
                sc = sc_all[r * BLOCK:(r + 1) * BLOCK] - SLOPES[hq] * dist
                sc = jnp.where(valid, sc, NEG)
                sink = sinks_ref[hq]
                m = jnp.maximum(jnp.max(sc, axis=-1, keepdims=True), sink)
                e = jnp.exp(sc - m)
                den_parts.append(jnp.sum(e, axis=-1, keepdims=True) + jnp.exp(sink - m))
                e_parts.append(e.astype(BF16))
            o_all = _dot(jnp.concatenate(e_parts, axis=0), vw)
            for c in range(pairs_per_slot):
                o_lo = o_all[2 * c * BLOCK:(2 * c + 1) * BLOCK] / den_parts[2 * c]
                o_hi = o_all[(2 * c + 1) * BLOCK:(2 * c + 2) * BLOCK] / den_parts[2 * c + 1]
                col = pairs_per_slot * j + c
                ocat[rows, col * 128:(col + 1) * 128] = jnp.where(lo, o_lo, o_hi).astype(BF16)

    _prompt_mem_attention(xq_pairs, mk_ref, mv_ref, ocat, A_Q)
    x1_ref[0] = x + _dot(ocat[...], wout_ref[...])

    kbuf[0:BLOCK, :] = kbuf[tb:tb + BLOCK, :]
    vbuf[0:BLOCK, :] = vbuf[tb:tb + BLOCK, :]


def _mixer_a_prompt(x, layer, norm1_g, w_in_b, q_norm, k_norm, xq_norm, sinks, mk_all, mv_all, w_out_b):
    n, t, _ = x.shape
    tb = min(PROMPT_ROWS, t)
    a_in = w_in_b.shape[1]
    xspec = pl.BlockSpec((1, tb, D_MODEL), lambda b, i: (b, i, 0))
    mspec = pl.BlockSpec((1, 1, N_MEM, X_WIDTH), lambda b, i: (layer, b, 0, 0))
    wspec = pl.BlockSpec((1, BLOCK, A_KV), lambda b, i: (b, 0, 0))
    return pl.pallas_call(
        functools.partial(_mixer_a_kernel, tb=tb),
        grid=(n, t // tb),
        in_specs=[
            pl.BlockSpec(memory_space=pltpu.SMEM),
            xspec,
            _const_spec((1, D_MODEL)),
            _const_spec((D_MODEL, a_in)),
            _const_spec((1, 128)),
            _const_spec((1, 128)),
            _const_spec((1, 128)),
            mspec,
            mspec,
            _const_spec((A_Q + X_WIDTH, D_MODEL)),
        ],
        out_specs=[xspec, wspec, wspec],
        out_shape=[
            jax.ShapeDtypeStruct((n, t, D_MODEL), F32),
            jax.ShapeDtypeStruct((n, BLOCK, A_KV), F32),
            jax.ShapeDtypeStruct((n, BLOCK, A_KV), F32),
        ],
        scratch_shapes=[
            pltpu.VMEM((tb + BLOCK, A_KV), BF16),
            pltpu.VMEM((tb + BLOCK, A_KV), BF16),
            pltpu.VMEM((tb, A_Q + X_WIDTH), BF16),
        ],
        compiler_params=_params("arbitrary", "arbitrary"),
        name="mixer_a_prompt",
    )(sinks, x, norm1_g.reshape(1, D_MODEL), w_in_b, jnp.tile(q_norm, 2).reshape(1, 128),
      jnp.tile(k_norm, 2).reshape(1, 128), jnp.tile(xq_norm, 2).reshape(1, 128), mk_all, mv_all, w_out_b)


def _mixer_b_kernel(x_ref, g1_ref, win_ref, vg_ref, ws_ref, bias_ref, xqg_ref, mk_ref, mv_ref, wout_ref,
                    x1_ref, cv_ref, ocat, *, tb):
    x = x_ref[0]
    h = _rms_rows(x, g1_ref[...]).astype(BF16)
    z = _dot(h, win_ref[...])
    uv = _gelu_exact(z[:, :2 * B_WIDTH])
    u = uv[:, :B_WIDTH]
    vn = _rms_rows(uv[:, B_WIDTH:], vg_ref[...])
    cv_ref[0] = vn[tb - CHUNK:, :]
    xq_pairs = _pair_head_norm(z[:, 2 * B_WIDTH:], xqg_ref[...])

    wi = lax.broadcasted_iota(jnp.int32, (CHUNK, CHUNK), 0)
    wj = lax.broadcasted_iota(jnp.int32, (CHUNK, CHUNK), 1)
    causal = wi >= wj
    w_tril = [jnp.where(causal, ws_ref[g], 0.0).astype(BF16) for g in range(B_GROUPS)]
    lo = lax.broadcasted_iota(jnp.int32, (CHUNK, 128), 1) < HEAD_DIM
    bias = bias_ref[...]
    for c in range(tb // CHUNK):
        vc = vn[c * CHUNK:(c + 1) * CHUNK, :].astype(BF16)
        r = [_dot(w_tril[g], vc[:, c0:c0 + 256]) for g, c0 in enumerate(GATE_WINDOW_START)]
        mixed = jnp.concatenate(
            [r[0][:, :128], jnp.where(lo, r[0][:, 128:], r[1][:, :128]), r[1][:, 128:],
             r[2][:, :128], jnp.where(lo, r[2][:, 128:], r[3][:, :128]), r[3][:, 128:]], axis=1)
        o = u[c * CHUNK:(c + 1) * CHUNK, :] * (mixed + bias)
        ocat[c * CHUNK:(c + 1) * CHUNK, 0:B_WIDTH] = o.astype(BF16)

    _prompt_mem_attention(xq_pairs, mk_ref, mv_ref, ocat, B_WIDTH)
    x1_ref[0] = x + _dot(ocat[...], wout_ref[...])


def _gate_bias_rows(b_s):
    return jnp.repeat(b_s.T, B_GROUP_DIM, axis=1)


def _mixer_b_prompt(x, layer, norm1_g, w_in_b, v_norm, w_s, b_s, xq_norm, mk_all, mv_all, w_out_b):
    n, t, _ = x.shape
    tb = min(PROMPT_ROWS, t)
    b_in = w_in_b.shape[1]
    xspec = pl.BlockSpec((1, tb, D_MODEL), lambda b, i: (b, i, 0))
    mspec = pl.BlockSpec((1, 1, N_MEM, X_WIDTH), lambda b, i: (layer, b, 0, 0))
    return pl.pallas_call(
        functools.partial(_mixer_b_kernel, tb=tb),
        grid=(n, t // tb),
        in_specs=[
            xspec,
            _const_spec((1, D_MODEL)),
            _const_spec((D_MODEL, b_in)),
            _const_spec((1, B_WIDTH)),
            _const_spec((B_GROUPS, CHUNK, CHUNK)),
            _const_spec((CHUNK, B_WIDTH)),
            _const_spec((1, 128)),
            mspec,
            mspec,
            _const_spec((B_WIDTH + X_WIDTH, D_MODEL)),
        ],
        out_specs=[xspec, pl.BlockSpec((1, CHUNK, B_WIDTH), lambda b, i: (b, 0, 0))],
        out_shape=[
            jax.ShapeDtypeStruct((n, t, D_MODEL), F32),
            jax.ShapeDtypeStruct((n, CHUNK, B_WIDTH), F32),
        ],
        scratch_shapes=[pltpu.VMEM((tb, B_WIDTH + X_WIDTH), BF16)],
        compiler_params=_params("arbitrary", "arbitrary"),
        name="mixer_b_prompt",
    )(x, norm1_g.reshape(1, D_MODEL), w_in_b, v_norm.reshape(1, B_WIDTH), w_s, _gate_bias_rows(b_s),
      jnp.tile(xq_norm, 2).reshape(1, 128), mk_all, mv_all, w_out_b)


def _ffn_chunks(xin, h, prev1, prev2, wfi_ref, cw_ref, cb_ref, wd_ref, on_a):
    acc = xin
    for c0, c1 in FFN_CHUNKS:
        a = _dot(h, wfi_ref[:, c0:c1])
        up = _dot(h, wfi_ref[:, D_FF + c0:D_FF + c1])
        am1 = prev1(a, c0, c1)
        am2 = prev2(a, c0, c1)
        on_a(a, c0, c1)
        c = cb_ref[:, c0:c1] + ((cw_ref[0:1, c0:c1] * am2 + cw_ref[1:2, c0:c1] * am1) + cw_ref[2:3, c0:c1] * a)
        act = (c * (1.0 / (1.0 + jnp.exp(-c)))) * up
        acc = acc + _dot(act.astype(BF16), wd_ref[c0:c1, :])
    return acc


def _ffn_prompt_kernel(x_ref, g2_ref, wfi_ref, cw_ref, cb_ref, wd_ref, y_ref, tail_ref, *, tb):
    t = pl.program_id(1)

    @pl.when(t == 0)
    def _():
        tail_ref[0] = jnp.zeros((8, D_FF), F32)

    x = x_ref[0]
    h = _rms_rows(x, g2_ref[...]).astype(BF16)
    row = lax.broadcasted_iota(jnp.int32, (tb, 1), 0)

    def prev1(a, c0, c1):
        return jnp.where(row == 0, tail_ref[0, 7:8, c0:c1], pltpu.roll(a, 1, 0))

    def prev2(a, c0, c1):
        return jnp.where(row == 0, tail_ref[0, 6:7, c0:c1],
                         jnp.where(row == 1, tail_ref[0, 7:8, c0:c1], pltpu.roll(a, 2, 0)))

    def on_a(a, c0, c1):
        tail_ref[0, :, c0:c1] = a[tb - 8:, :]

    y_ref[0] = _ffn_chunks(x, h, prev1, prev2, wfi_ref, cw_ref, cb_ref, wd_ref, on_a)


def _ffn_prompt(x, norm2_g, w_ffn_in_b, conv_w, conv_b, w_down_b):
    n, t, _ = x.shape
    tb = min(PROMPT_ROWS, t)
    xspec = pl.BlockSpec((1, tb, D_MODEL), lambda b, i: (b, i, 0))
    y, tail = pl.pallas_call(
        functools.partial(_ffn_prompt_kernel, tb=tb),
        grid=(n, t // tb),
        in_specs=[
            xspec,
            _const_spec((1, D_MODEL)),
            _const_spec((D_MODEL, 2 * D_FF)),
            _const_spec((CONV_W, D_FF)),
            _const_spec((1, D_FF)),
            _const_spec((D_FF, D_MODEL)),
        ],
        out_specs=[xspec, pl.BlockSpec((1, 8, D_FF), lambda b, i: (b, 0, 0))],
        out_shape=[
            jax.ShapeDtypeStruct((n, t, D_MODEL), F32),
            jax.ShapeDtypeStruct((n, 8, D_FF), F32),
        ],
        compiler_params=_params("arbitrary", "arbitrary"),
        name="ffn_prompt",
    )(x, norm2_g.reshape(1, D_MODEL), w_ffn_in_b, conv_w, conv_b.reshape(1, D_FF), w_down_b)
    return y, tail[:, 8 - (CONV_W - 1):, :]


def _ffn_sample_kernel(x_ref, ocat_ref, wout_ref, g2_ref, wfi_ref, cw_ref, cb_ref, wd_ref, pm1_ref, pm2_ref,
                       y_ref, a_ref, *, rows, seq):
    x1 = x_ref[...] + _dot(ocat_ref[...].astype(BF16), wout_ref[...])
    h = _rms_rows(x1, g2_ref[...]).astype(BF16)
    pos = _pos_in_seq((rows, 1), seq)

    def prev1(a, c0, c1):
        return jnp.where(pos == 0, pm1_ref[:, c0:c1], pltpu.roll(a, 1, 0))

    def prev2(a, c0, c1):
        return jnp.where(pos < 2, pm2_ref[:, c0:c1], pltpu.roll(a, 2, 0))

    def on_a(a, c0, c1):
        a_ref[:, c0:c1] = a

    y_ref[...] = _ffn_chunks(x1, h, prev1, prev2, wfi_ref, cw_ref, cb_ref, wd_ref, on_a)


def _ffn_sample(x, ocat, w_out_e, norm2_g, w_ffn_in_b, conv_w, conv_b, w_down_b, prefix):
    n, s, _ = x.shape
    rows = n * s
    kc = ocat.shape[1]
    pm1 = jnp.pad(prefix[:, 1:2], ((0, 0), (0, s - 1), (0, 0))).reshape(rows, D_FF)
    pm2 = jnp.pad(prefix, ((0, 0), (0, s - 2), (0, 0))).reshape(rows, D_FF)
    rb = min(SAMPLE_ROWS, rows)
    assert rb % s == 0 and rows % rb == 0

    def rspec(width):
        return pl.BlockSpec((rb, width), lambda i: (i, 0))

    y, a = pl.pallas_call(
        functools.partial(_ffn_sample_kernel, rows=rb, seq=s),
        grid=(rows // rb,),
        in_specs=[
            rspec(D_MODEL),
            rspec(kc),
            _const_spec((kc, D_MODEL)),
            _const_spec((1, D_MODEL)),
            _const_spec((D_MODEL, 2 * D_FF)),
            _const_spec((CONV_W, D_FF)),
            _const_spec((1, D_FF)),
            _const_spec((D_FF, D_MODEL)),
            rspec(D_FF),
            rspec(D_FF),
        ],
        out_specs=[rspec(D_MODEL), rspec(D_FF)],
        out_shape=[
            jax.ShapeDtypeStruct((rows, D_MODEL), F32),
            jax.ShapeDtypeStruct((rows, D_FF), F32),
        ],
        compiler_params=_params("arbitrary"),
        name="ffn_sample",
    )(x.reshape(rows, D_MODEL), ocat, w_out_e, norm2_g.reshape(1, D_MODEL), w_ffn_in_b, conv_w,
      conv_b.reshape(1, D_FF), w_down_b, pm1, pm2)
    return y.reshape(n, s, D_MODEL), a.reshape(n, s, D_FF)[:, s - (CONV_W - 1):, :]


def _sample_in_a_kernel(x_ref, g1_ref, win_ref, qg_ref, kg_ref, xqg_ref, qs_ref, k_ref, v_ref, xs_ref):
    h = _rms_rows(x_ref[...], g1_ref[...]).astype(BF16)
    z = _dot(h, win_ref[...])
    for hq in range(A_HEADS):
        qs_ref[hq] = _slot_head_norm(z[:, hq * A_KV:(hq + 1) * A_KV], qg_ref[...])
    c0 = A_HEADS * A_KV
    for p, kp in enumerate(_pair_head_norm(z[:, c0:c0 + A_KV], kg_ref[...])):
        k_ref[:, p * 128:(p + 1) * 128] = kp
    v_ref[...] = z[:, c0 + A_KV:c0 + 2 * A_KV]
    c0 += 2 * A_KV
    for hx in range(X_HEADS):
        xs_ref[hx] = _slot_head_norm(z[:, c0 + hx * X_WIDTH:c0 + (hx + 1) * X_WIDTH], xqg_ref[...])


def _expand_cols(w, heads, slot_of):
    d = w.shape[0]
    wh = w.reshape(d, heads, 1, HEAD_DIM)
    sel = np.zeros((heads, X_HEADS, 1), np.float32)
    for h in range(heads):
        sel[h, slot_of(h), 0] = 1.0
    return (wh * jnp.asarray(sel, w.dtype)).reshape(d, heads * X_WIDTH)


def _expand_rows(w, heads, slot_of):
    return _expand_cols(w.T, heads, slot_of).T


def _sample_in_a(x2, norm1_g, w_in_b, q_norm, k_norm, xq_norm):
    rows = x2.shape[0]
    wq = _expand_cols(w_in_b[:, :A_Q], A_HEADS, lambda h: h // A_GROUP)
    wx = _expand_cols(w_in_b[:, A_Q + 2 * A_KV:], X_HEADS, lambda h: h)
    w_e = jnp.concatenate([wq, w_in_b[:, A_Q:A_Q + 2 * A_KV], wx], axis=1)
    return pl.pallas_call(
        _sample_in_a_kernel,
        grid=(1,),
        in_specs=[
            _const_spec((rows, D_MODEL)),
            _const_spec((1, D_MODEL)),
            _const_spec(w_e.shape),
            _const_spec((1, A_KV)),
            _const_spec((1, 128)),
            _const_spec((1, X_WIDTH)),
        ],
        out_specs=[
            pl.BlockSpec((A_HEADS, rows, A_KV), lambda i: (0, 0, 0)),
            pl.BlockSpec((rows, A_KV), lambda i: (0, 0)),
            pl.BlockSpec((rows, A_KV), lambda i: (0, 0)),
            pl.BlockSpec((X_HEADS, rows, X_WIDTH), lambda i: (0, 0, 0)),
        ],
        out_shape=[
            jax.ShapeDtypeStruct((A_HEADS, rows, A_KV), F32),
            jax.ShapeDtypeStruct((rows, A_KV), F32),
            jax.ShapeDtypeStruct((rows, A_KV), F32),
            jax.ShapeDtypeStruct((X_HEADS, rows, X_WIDTH), F32),
        ],
        compiler_params=_params("arbitrary"),
        name="sample_in_a",
    )(x2, norm1_g.reshape(1, D_MODEL), w_e, jnp.tile(q_norm, 4).reshape(1, A_KV),
      jnp.tile(k_norm, 2).reshape(1, 128), jnp.tile(xq_norm, 4).reshape(1, X_WIDTH))


def _sample_in_b_kernel(x_ref, g1_ref, win_ref, vg_ref, coef_ref, gbias_ref, xqg_ref, o_ref, vn_ref, xs_ref,
                        *, rows, seq):
    h = _rms_rows(x_ref[...], g1_ref[...]).astype(BF16)
    z = _dot(h, win_ref[...])
    uv = _gelu_exact(z[:, :2 * B_WIDTH])
    u = uv[:, :B_WIDTH]
    vn = _rms_rows(uv[:, B_WIDTH:], vg_ref[...])
    vn_ref[...] = vn
    mixed = gbias_ref[...] + coef_ref[0] * vn
    for delta in range(1, seq):
        mixed = mixed + coef_ref[delta] * pltpu.roll(vn, delta, 0)
    o_ref[...] = u * mixed
    c0 = 2 * B_WIDTH
    for hx in range(X_HEADS):
        xs_ref[hx] = _slot_head_norm(z[:, c0 + hx * X_WIDTH:c0 + (hx + 1) * X_WIDTH], xqg_ref[...])


def _sample_in_b(x2, seq, norm1_g, w_in_b, v_norm, w_s, b_s, xq_norm):
    rows = x2.shape[0]
    n = rows // seq
    wx = _expand_cols(w_in_b[:, 2 * B_WIDTH:], X_HEADS, lambda h: h)
    w_e = jnp.concatenate([w_in_b[:, :2 * B_WIDTH], wx], axis=1)
    wt = jnp.tril(w_s[:, :seq, :seq])
    coef = []
    for delta in range(seq):
        rows_d = [wt[:, i, i - delta] if i - delta >= 0 else jnp.zeros((B_GROUPS,), w_s.dtype) for i in range(seq)]
        cd = jnp.repeat(jnp.stack(rows_d), B_GROUP_DIM, axis=1)
        coef.append(jnp.tile(cd, (n, 1)))
    coef = jnp.stack(coef)
    gbias = jnp.tile(jnp.repeat(b_s[:, :seq].T, B_GROUP_DIM, axis=1), (n, 1))
    return pl.pallas_call(
        functools.partial(_sample_in_b_kernel, rows=rows, seq=seq),
        grid=(1,),
        in_specs=[
            _const_spec((rows, D_MODEL)),
            _const_spec((1, D_MODEL)),
            _const_spec(w_e.shape),
            _const_spec((1, B_WIDTH)),
            _const_spec((seq, rows, B_WIDTH)),
            _const_spec((rows, B_WIDTH)),
            _const_spec((1, X_WIDTH)),
        ],
        out_specs=[
            pl.BlockSpec((rows, B_WIDTH), lambda i: (0, 0)),
            pl.BlockSpec((rows, B_WIDTH), lambda i: (0, 0)),
            pl.BlockSpec((X_HEADS, rows, X_WIDTH), lambda i: (0, 0, 0)),
        ],
        out_shape=[
            jax.ShapeDtypeStruct((rows, B_WIDTH), F32),
            jax.ShapeDtypeStruct((rows, B_WIDTH), F32),
            jax.ShapeDtypeStruct((X_HEADS, rows, X_WIDTH), F32),
        ],
        compiler_params=_params("arbitrary"),
        name="sample_in_b",
    )(x2, norm1_g.reshape(1, D_MODEL), w_e, v_norm.reshape(1, B_WIDTH), coef, gbias,
      jnp.tile(xq_norm, 4).reshape(1, X_WIDTH))


def _seq_mem_attention(xs, mk, mv):
    s = _nt_dot(xs.astype(BF16), mk.astype(BF16)) * SCALE
    m = jnp.max(s, axis=-1, keepdims=True)
    e = jnp.exp(s - m)
    den = jnp.sum(e, axis=-1, keepdims=True)
    return _dot(e.astype(BF16), mv.astype(BF16)) / den


def _sample_attn_a_kernel(qs_ref, k8_ref, v8_ref, xs_ref, slope_ref, sink_ref, ck_ref, cv_ref, mk_ref, mv_ref,
                          o_ref, xo_ref, wk_ref, wv_ref, *, sb, seq):
    rows = A_HEADS * seq
    w = WINDOW
    tq = _pos_in_seq((rows, w), seq)
    j1 = lax.broadcasted_iota(jnp.int32, (rows, w), 1)
    d1 = tq + w - j1
    valid1 = d1 < WINDOW
    tq2 = _pos_in_seq((rows, 8), seq)
    j2 = lax.broadcasted_iota(jnp.int32, (rows, 8), 1)
    d2 = tq2 - j2
    valid2 = (d2 >= 0) & (j2 < seq)
    slope = slope_ref[...]
    sink = sink_ref[...]
    bias1 = slope * d1.astype(F32)
    bias2 = slope * d2.astype(F32)
    sub8 = lax.broadcasted_iota(jnp.int32, (8, A_KV), 0)

    def body(n, carry):
        q = qs_ref[n].astype(BF16)
        ck = ck_ref[n]
        cv = cv_ref[n]
        k8 = k8_ref[n]
        v8 = v8_ref[n]
        s1 = _nt_dot(q, ck.astype(BF16)) * SCALE - bias1
        s2 = _nt_dot(q, k8.astype(BF16)) * SCALE - bias2
        s1 = jnp.where(valid1, s1, NEG)
        s2 = jnp.where(valid2, s2, NEG)
        m = jnp.maximum(jnp.maximum(jnp.max(s1, axis=-1, keepdims=True), jnp.max(s2, axis=-1, keepdims=True)), sink)
        e1 = jnp.exp(s1 - m)
        e2 = jnp.exp(s2 - m)
        den = jnp.sum(e1, axis=-1, keepdims=True) + jnp.sum(e2, axis=-1, keepdims=True) + jnp.exp(sink - m)
        o = _dot(e1.astype(BF16), cv.astype(BF16)) + _dot(e2.astype(BF16), v8.astype(BF16))
        o_ref[n] = o / den
        xo_ref[n] = _seq_mem_attention(xs_ref[n], mk_ref[n], mv_ref[n])
        for src, new8, dst in ((ck, k8, wk_ref), (cv, v8, wv_ref)):
            shifted = pltpu.roll(src, w - seq, 0)
            dst[n] = shifted
            dst[n, w - 8:w, :] = jnp.where(sub8 >= 8 - seq, pltpu.roll(new8, 8 - seq, 0), shifted[w - 8:, :])
        return carry

    lax.fori_loop(0, sb, body, 0)


def _sample_attn_a(qs, k8, v8, xs, slope_rows, sink_rows, ck, cv, mk, mv, seq):
    n = qs.shape[0]
    sb = min(SAMPLE_SEQS, n)
    rows = A_HEADS * seq
    xrows = X_HEADS * seq

    def blk(*tail):
        nz = len(tail)
        return pl.BlockSpec((sb,) + tail, lambda i: (i,) + (0,) * nz)

    return pl.pallas_call(
        functools.partial(_sample_attn_a_kernel, sb=sb, seq=seq),
        grid=(n // sb,),
        in_specs=[
            blk(rows, A_KV), blk(8, A_KV), blk(8, A_KV), blk(xrows, X_WIDTH),
            _const_spec((rows, 1)), _const_spec((rows, 1)),
            blk(WINDOW, A_KV), blk(WINDOW, A_KV), blk(N_MEM, X_WIDTH), blk(N_MEM, X_WIDTH),
        ],
        out_specs=[blk(rows, A_KV), blk(xrows, X_WIDTH), blk(WINDOW, A_KV), blk(WINDOW, A_KV)],
        out_shape=[
            jax.ShapeDtypeStruct((n, rows, A_KV), F32),
            jax.ShapeDtypeStruct((n, xrows, X_WIDTH), F32),
            jax.ShapeDtypeStruct((n, WINDOW, A_KV), F32),
            jax.ShapeDtypeStruct((n, WINDOW, A_KV), F32),
        ],
        compiler_params=_params("arbitrary"),
        name="sample_attn_a",
    )(qs, k8, v8, xs, slope_rows, sink_rows, ck, cv, mk, mv)


def _sample_attn_b_kernel(xs_ref, mk_ref, mv_ref, xo_ref, *, sb):
    def body(n, carry):
        xo_ref[n] = _seq_mem_attention(xs_ref[n], mk_ref[n], mv_ref[n])
        return carry

    lax.fori_loop(0, sb, body, 0)


def _sample_attn_b(xs, mk, mv):
    n, xrows, _ = xs.shape
    sb = min(SAMPLE_SEQS, n)

    def blk(*tail):
        nz = len(tail)
        return pl.BlockSpec((sb,) + tail, lambda i: (i,) + (0,) * nz)

    return pl.pallas_call(
        functools.partial(_sample_attn_b_kernel, sb=sb),
        grid=(n // sb,),
        in_specs=[blk(xrows, X_WIDTH), blk(N_MEM, X_WIDTH), blk(N_MEM, X_WIDTH)],
        out_specs=blk(xrows, X_WIDTH),
        out_shape=jax.ShapeDtypeStruct((n, xrows, X_WIDTH), F32),
        compiler_params=_params("arbitrary"),
        name="sample_attn_b",
    )(xs, mk, mv)


def _slots_to_seq(a, n, seq):
    hh, _, w = a.shape
    return a.reshape(hh, n, seq, w).transpose(1, 0, 2, 3).reshape(n, hh * seq, w)


def _seq_to_cols(a, heads, seq):
    n, _, w = a.shape
    return a.reshape(n, heads, seq, w).transpose(0, 2, 1, 3).reshape(n * seq, heads * w)


def kernel(x_prompt, x_sample, cache_win_k, cache_win_v, cache_mem_k, cache_mem_v, state_conv, mem_prompt,
           norm1_g, norm2_g, mem_norm_g, w_in_a, q_norm_a, k_norm_a, sinks_a, w_out_a, w_in_b, v_norm_b, w_s_b,
           b_s_b, w_out_b, w_mem_kv, xq_norm, xk_norm, w_ffn_in, conv_w, conv_b, w_down):
    depth = norm1_g.shape[0]
    assert depth == 2 and w_in_a.shape[0] == 1 and w_in_b.shape[0] == 1
    nb = x_prompt.shape[0]
    ns, seq, _ = x_sample.shape
    w_buf = cache_win_k.shape[2]
    assert w_buf == WINDOW and 2 <= seq <= 8

    w_in_a_b = w_in_a[0].astype(BF16)
    w_in_b_b = w_in_b[0].astype(BF16)
    w_out_a_b = w_out_a[0].astype(BF16)
    w_out_b_b = w_out_b[0].astype(BF16)
    w_mem_kv_b = w_mem_kv.astype(BF16)
    w_ffn_in_b = w_ffn_in.astype(BF16)
    w_down_b = w_down.astype(BF16)

    mk_all, mv_all = _memory_kv(mem_prompt, mem_norm_g, w_mem_kv_b, xk_norm)
    x1, win_k_p, win_v_p = _mixer_a_prompt(x_prompt, 0, norm1_g[0], _reorder_heads(w_in_a_b, Q_LANE_ORDER, 1),
                                           q_norm_a[0], k_norm_a[0], xq_norm[0], sinks_a[0], mk_all, mv_all,
                                           _reorder_heads(w_out_a_b, Q_LANE_ORDER, 0))
    x2, conv_p0 = _ffn_prompt(x1, norm2_g[0], w_ffn_in_b[0], conv_w[0], conv_b[0], w_down_b[0])
    x3, chunk_v_p = _mixer_b_prompt(x2, 1, norm1_g[1], w_in_b_b, v_norm_b[0], w_s_b[0], b_s_b[0], xq_norm[1],
                                    mk_all, mv_all, w_out_b_b)
    y_prompt, conv_p1 = _ffn_prompt(x3, norm2_g[1], w_ffn_in_b[1], conv_w[1], conv_b[1], w_down_b[1])

    rows = ns * seq
    xs2 = x_sample.reshape(rows, D_MODEL)
    qs, k_new, v_new, xslots = _sample_in_a(xs2, norm1_g[0], w_in_a_b, q_norm_a[0], k_norm_a[0], xq_norm[0])
    qs = _slots_to_seq(qs, ns, seq)
    xslots = _slots_to_seq(xslots, ns, seq)
    k8 = jnp.pad(k_new.reshape(ns, seq, A_KV), ((0, 0), (0, 8 - seq), (0, 0)))
    v8 = jnp.pad(v_new.reshape(ns, seq, A_KV), ((0, 0), (0, 8 - seq), (0, 0)))
    slope_rows = jnp.repeat(jnp.asarray(SLOPES, F32), seq).reshape(A_HEADS * seq, 1)
    sink_rows = jnp.repeat(sinks_a[0].astype(F32), seq).reshape(A_HEADS * seq, 1)
    o_s, xo_s, win_k_s, win_v_s = _sample_attn_a(
        qs, k8, v8, xslots, slope_rows, sink_rows,
        cache_win_k[0].reshape(ns, w_buf, A_KV), cache_win_v[0].reshape(ns, w_buf, A_KV),
        cache_mem_k[0].reshape(ns, N_MEM, X_WIDTH), cache_mem_v[0].reshape(ns, N_MEM, X_WIDTH), seq)
    ocat = jnp.concatenate([_seq_to_cols(o_s, A_HEADS, seq), _seq_to_cols(xo_s, X_HEADS, seq)], axis=1)
    w_out_a_e = jnp.concatenate([_expand_rows(w_out_a_b[:A_Q], A_HEADS, lambda h: h // A_GROUP),
                                 _expand_rows(w_out_a_b[A_Q:], X_HEADS, lambda h: h)], axis=0)
    ys1, conv_s0 = _ffn_sample(x_sample, ocat, w_out_a_e, norm2_g[0], w_ffn_in_b[0], conv_w[0], conv_b[0],
                               w_down_b[0], state_conv[0])

    o_b, vn_s, xslots = _sample_in_b(ys1.reshape(rows, D_MODEL), seq, norm1_g[1], w_in_b_b, v_norm_b[0], w_s_b[0],
                                     b_s_b[0], xq_norm[1])
    xo_s = _sample_attn_b(_slots_to_seq(xslots, ns, seq), cache_mem_k[1].reshape(ns, N_MEM, X_WIDTH),
                          cache_mem_v[1].reshape(ns, N_MEM, X_WIDTH))
    ocat = jnp.concatenate([o_b, _seq_to_cols(xo_s, X_HEADS, seq)], axis=1)
    w_out_b_e = jnp.concatenate([w_out_b_b[:B_WIDTH], _expand_rows(w_out_b_b[B_WIDTH:], X_HEADS, lambda h: h)], axis=0)
    y_sample, conv_s1 = _ffn_sample(ys1, ocat, w_out_b_e, norm2_g[1], w_ffn_in_b[1], conv_w[1], conv_b[1],
                                    w_down_b[1], state_conv[1])

    kv5 = (A_KV_HEADS, HEAD_DIM)
    return (
        y_prompt,
        y_sample,
        win_k_p.reshape(1, nb, WINDOW, *kv5),
        win_v_p.reshape(1, nb, WINDOW, *kv5),
        chunk_v_p.reshape(1, nb, CHUNK, B_WIDTH),
        mk_all.reshape(depth, nb, N_MEM, X_HEADS, HEAD_DIM),
        mv_all.reshape(depth, nb, N_MEM, X_HEADS, HEAD_DIM),
        jnp.stack([conv_p0, conv_p1]),
        win_k_s.reshape(1, ns, WINDOW, *kv5),
        win_v_s.reshape(1, ns, WINDOW, *kv5),
        vn_s.reshape(1, ns, seq, B_WIDTH),
        jnp.stack([conv_s0, conv_s1]),
    )
```

```python
import functools
import math

import numpy as np
import jax
import jax.numpy as jnp
from jax import lax
from jax.experimental import pallas as pl
from jax.experimental.pallas import tpu as pltpu

F32 = jnp.float32
BF16 = jnp.bfloat16

D_MODEL = 1024
HEAD_DIM = 64
A_HEADS = 12
A_KV_HEADS = 4
A_GROUP = A_HEADS // A_KV_HEADS
WINDOW = 128
BLOCK = 128
B_WIDTH = 768
B_GROUPS = 4
B_GROUP_DIM = B_WIDTH // B_GROUPS
CHUNK = 128
X_HEADS = 4
X_WIDTH = X_HEADS * HEAD_DIM
N_MEM = 256
D_FF = 2816
CONV_W = 3
A_Q = A_HEADS * HEAD_DIM
A_KV = A_KV_HEADS * HEAD_DIM
EPS = 1e-6
NEG = -1e30
SCALE = HEAD_DIM ** -0.5
SQRT_HALF = float(np.sqrt(0.5).astype(np.float32))

LANES = 128
VMEM_LIMIT_BYTES = 56 * 1024 * 1024
GATE_WINDOW_START = (0, 128, 384, 512)
assert B_GROUP_DIM == 192 and B_GROUPS == 4
PROMPT_ROWS = 256
SAMPLE_MEM_SEQS = 16
SAMPLE_FFN_SEQS = 64


def _alibi_slopes(n):
    def pow2(m):
        start = 2.0 ** (-8.0 / m)
        return [start ** (i + 1) for i in range(m)]

    p = 2 ** int(math.floor(math.log2(n)))
    s = pow2(p)
    if p < n:
        s = s + pow2(2 * p)[0::2][: n - p]
    return [float(np.float32(v)) for v in s]


SLOPES = _alibi_slopes(A_HEADS)

Q_LANE_ORDER = (0, 3, 1, 4, 2, 5, 6, 9, 7, 10, 8, 11)
assert all(Q_LANE_ORDER[p] // A_GROUP == 2 * (p // (2 * A_GROUP)) + p % 2 for p in range(A_HEADS))


def _reorder_heads(w, order, axis):
    n = len(order) * HEAD_DIM
    idx = np.concatenate([np.arange(h * HEAD_DIM, (h + 1) * HEAD_DIM) for h in order] + [np.arange(n, w.shape[axis])])
    return jnp.take(w, jnp.asarray(idx), axis=axis)


def _params(*sem):
    return pltpu.CompilerParams(dimension_semantics=sem, vmem_limit_bytes=VMEM_LIMIT_BYTES)


def _const_spec(shape):
    nd = len(shape)
    return pl.BlockSpec(shape, lambda *_: (0,) * nd, pipeline_mode=pl.Buffered(1))


def _layer_spec(layer, shape):
    nd = len(shape)
    return pl.BlockSpec((None,) + tuple(shape), lambda *_: (layer,) + (0,) * nd, pipeline_mode=pl.Buffered(1))


def _rms_rows(x, g):
    ms = jnp.mean(x * x, axis=-1, keepdims=True)
    return x * lax.rsqrt(ms + EPS) * g


def _pair_head_norm(z, g2):
    t, w = z.shape
    lo = lax.broadcasted_iota(jnp.int32, (t, LANES), 1) < HEAD_DIM
    out = []
    for p in range(w // LANES):
        zz = z[:, p * LANES:(p + 1) * LANES]
        sq = zz * zz
        s_lo = jnp.sum(jnp.where(lo, sq, 0.0), axis=-1, keepdims=True)
        s_hi = jnp.sum(jnp.where(lo, 0.0, sq), axis=-1, keepdims=True)
        r_lo = lax.rsqrt(s_lo * (1.0 / HEAD_DIM) + EPS)
        r_hi = lax.rsqrt(s_hi * (1.0 / HEAD_DIM) + EPS)
        out.append(zz * jnp.where(lo, r_lo, r_hi) * g2)
    return out


def _slot_head_norm(z, g4):
    ms = jnp.sum(z * z, axis=-1, keepdims=True) * (1.0 / HEAD_DIM)
    return z * lax.rsqrt(ms + EPS) * g4


def _nt_dot(a, b):
    return lax.dot_general(a, b, (((1,), (1,)), ((), ())), preferred_element_type=F32)


def _dot(a, b):
    return jnp.dot(a, b, preferred_element_type=F32)


def _div_mod(i, n):
    if n & (n - 1) == 0:
        return i >> (n.bit_length() - 1), i & (n - 1)
    return i // n, i % n


def _gelu_exact(x):
    return 0.5 * x * (1.0 + lax.erf(x * SQRT_HALF))


def _conv_gate(a, am1, am2, up, cw_ref, cb_ref):
    c = cb_ref[...] + ((cw_ref[0:1, :] * am2 + cw_ref[1:2, :] * am1) + cw_ref[2:3, :] * a)
    return ((c * (1.0 / (1.0 + jnp.exp(-c)))) * up).astype(BF16)


def _memkv_kernel(mem_ref, g_ref, w_ref, kg_ref, mk_ref, mv_ref):
    m = mem_ref[0]
    h = _rms_rows(m, g_ref[0]).astype(BF16)
    kv = _dot(h, w_ref[0])
    pairs = _pair_head_norm(kv[:, :X_WIDTH], kg_ref[0])
    for p, val in enumerate(pairs):
        mk_ref[0, 0, :, p * LANES:(p + 1) * LANES] = val
    mv_ref[0, 0] = kv[:, X_WIDTH:]


def _memory_kv(mem, mem_norm_g, w_mem_kv_b, xk_norm):
    depth = w_mem_kv_b.shape[0]
    n = mem.shape[0]
    g = mem_norm_g.reshape(depth, 1, D_MODEL)
    kg2 = jnp.tile(xk_norm, (1, 2)).reshape(depth, 1, LANES)
    out = jax.ShapeDtypeStruct((depth, n, N_MEM, X_WIDTH), F32)
    return pl.pallas_call(
        _memkv_kernel,
        grid=(depth, n),
        in_specs=[
            pl.BlockSpec((1, N_MEM, D_MODEL), lambda l, b: (b, 0, 0)),
            pl.BlockSpec((1, 1, D_MODEL), lambda l, b: (l, 0, 0)),
            pl.BlockSpec((1, D_MODEL, 2 * X_WIDTH), lambda l, b: (l, 0, 0)),
            pl.BlockSpec((1, 1, LANES), lambda l, b: (l, 0, 0)),
        ],
        out_specs=[
            pl.BlockSpec((1, 1, N_MEM, X_WIDTH), lambda l, b: (l, b, 0, 0)),
            pl.BlockSpec((1, 1, N_MEM, X_WIDTH), lambda l, b: (l, b, 0, 0)),
        ],
        out_shape=[out, out],
        compiler_params=_params("arbitrary", "arbitrary"),
        name="memory_kv",
    )(mem, g, w_mem_kv_b, kg2)


def _half_masked_rows(pair, lo):
    return jnp.concatenate([jnp.where(lo, pair, 0.0), jnp.where(lo, 0.0, pair)], axis=0).astype(BF16)


def _prompt_mem_attention(xq_pairs, mk_ref, mv_ref, ocat, col0):
    t = xq_pairs[0].shape[0]
    lo = lax.broadcasted_iota(jnp.int32, (t, LANES), 1) < HEAD_DIM
    for j, pair in enumerate(xq_pairs):
        mk = mk_ref[0, 0, :, j * LANES:(j + 1) * LANES].astype(BF16)
        mv = mv_ref[0, 0, :, j * LANES:(j + 1) * LANES].astype(BF16)
        s = _nt_dot(_half_masked_rows(pair * SCALE, lo), mk)
        m = jnp.max(s, axis=-1, keepdims=True)
        e = jnp.exp(s - m)
        den = jnp.sum(e, axis=-1, keepdims=True)
        o = _dot(e.astype(BF16), mv) / den
        ocat[:, col0 + j * LANES:col0 + (j + 1) * LANES] = jnp.where(lo, o[:t], o[t:]).astype(BF16)


def _mixer_a_kernel(sinks_ref, x_ref, g1_ref, win_ref, qg_ref, kg_ref, xqg_ref, mk_ref, mv_ref, wout_ref,
                    x1_ref, wk_ref, wv_ref, kbuf, vbuf, ocat, *, tb):
    t = pl.program_id(1)

    @pl.when(t == 0)
    def _():
        kbuf[0:BLOCK, :] = jnp.zeros((BLOCK, A_KV), BF16)
        vbuf[0:BLOCK, :] = jnp.zeros((BLOCK, A_KV), BF16)

    x = x_ref[0]
    h = _rms_rows(x, g1_ref[...]).astype(BF16)
    z = _dot(h, win_ref[...])
    q_pairs = _pair_head_norm(z[:, :A_Q], qg_ref[...])
    k_pairs = _pair_head_norm(z[:, A_Q:A_Q + A_KV], kg_ref[...])
    v = z[:, A_Q + A_KV:A_Q + 2 * A_KV]
    xq_pairs = _pair_head_norm(z[:, A_Q + 2 * A_KV:], xqg_ref[...])

    for p, kp in enumerate(k_pairs):
        wk_ref[0, :, p * LANES:(p + 1) * LANES] = kp[tb - BLOCK:, :]
        kbuf[BLOCK:BLOCK + tb, p * LANES:(p + 1) * LANES] = kp.astype(BF16)
    wv_ref[0] = v[tb - BLOCK:, :]
    vbuf[BLOCK:BLOCK + tb, :] = v.astype(BF16)

    qi = lax.broadcasted_iota(jnp.int32, (BLOCK, 2 * BLOCK), 0)
    kj = lax.broadcasted_iota(jnp.int32, (BLOCK, 2 * BLOCK), 1)
    d = qi + BLOCK - kj
    in_window = (d >= 0) & (d < WINDOW)
    dist = d.astype(F32)

    lo = lax.broadcasted_iota(jnp.int32, (BLOCK, LANES), 1) < HEAD_DIM
    pairs_per_slot = A_HEADS // A_KV_HEADS
    for s in range(tb // BLOCK):
        rows = slice(s * BLOCK, (s + 1) * BLOCK)
        first_key = jnp.where(t > 0, 0, BLOCK) if s == 0 else 0
        valid = in_window & (kj >= first_key)
        for j in range(A_KV // LANES):
            kw = kbuf[s * BLOCK:(s + 2) * BLOCK, j * LANES:(j + 1) * LANES]
            vw = vbuf[s * BLOCK:(s + 2) * BLOCK, j * LANES:(j + 1) * LANES]
            q_rows = jnp.concatenate(
                [_half_masked_rows(q_pairs[pairs_per_slot * j + c][rows] * SCALE, lo) for c in range(pairs_per_slot)],
                axis=0)
            sc_all = _nt_dot(q_rows, kw)
            e_parts, den_parts = [], []
            for r in range(2 * pairs_per_slot):
                hq = Q_LANE_ORDER[2 * pairs_per_slot * j + r]
                sc = sc_all[r * BLOCK:(r + 1) * BLOCK] - SLOPES[hq] * dist
                sc = jnp.where(valid, sc, NEG)
                sink = sinks_ref[hq]
                m = jnp.maximum(jnp.max(sc, axis=-1, keepdims=True), sink)
                e = jnp.exp(sc - m)
                den_parts.append(jnp.sum(e, axis=-1, keepdims=True) + jnp.exp(sink - m))
                e_parts.append(e.astype(BF16))
            o_all = _dot(jnp.concatenate(e_parts, axis=0), vw)
            for c in range(pairs_per_slot):
                o_lo = o_all[2 * c * BLOCK:(2 * c + 1) * BLOCK] / den_parts[2 * c]
                o_hi = o_all[(2 * c + 1) * BLOCK:(2 * c + 2) * BLOCK] / den_parts[2 * c + 1]
                col = pairs_per_slot * j + c
                ocat[rows, col * LANES:(col + 1) * LANES] = jnp.where(lo, o_lo, o_hi).astype(BF16)

    _prompt_mem_attention(xq_pairs, mk_ref, mv_ref, ocat, A_Q)
    x1_ref[0] = x + _dot(ocat[...], wout_ref[...])

    kbuf[0:BLOCK, :] = kbuf[tb:tb + BLOCK, :]
    vbuf[0:BLOCK, :] = vbuf[tb:tb + BLOCK, :]


def _mixer_a_prompt(x, layer, norm1_g, w_in_b, q_norm, k_norm, xq_norm, sinks, mk_all, mv_all, w_out_b):
    n, t, _ = x.shape
    tb = min(PROMPT_ROWS, t)
    a_in = w_in_b.shape[1]
    xspec = pl.BlockSpec((1, tb, D_MODEL), lambda b, i: (b, i, 0))
    mspec = pl.BlockSpec((1, 1, N_MEM, X_WIDTH), lambda b, i: (layer, b, 0, 0))
    wspec = pl.BlockSpec((1, BLOCK, A_KV), lambda b, i: (b, 0, 0))
    return pl.pallas_call(
        functools.partial(_mixer_a_kernel, tb=tb),
        grid=(n, t // tb),
        in_specs=[
            pl.BlockSpec(memory_space=pltpu.SMEM),
            xspec,
            _const_spec((1, D_MODEL)),
            _const_spec((D_MODEL, a_in)),
            _const_spec((1, LANES)),
            _const_spec((1, LANES)),
            _const_spec((1, LANES)),
            mspec,
            mspec,
            _const_spec((A_Q + X_WIDTH, D_MODEL)),
        ],
        out_specs=[xspec, wspec, wspec],
        out_shape=[
            jax.ShapeDtypeStruct((n, t, D_MODEL), F32),
            jax.ShapeDtypeStruct((n, BLOCK, A_KV), F32),
            jax.ShapeDtypeStruct((n, BLOCK, A_KV), F32),
        ],
        scratch_shapes=[
            pltpu.VMEM((tb + BLOCK, A_KV), BF16),
            pltpu.VMEM((tb + BLOCK, A_KV), BF16),
            pltpu.VMEM((tb, A_Q + X_WIDTH), BF16),
        ],
        compiler_params=_params("arbitrary", "arbitrary"),
        name="mixer_a_prompt",
    )(sinks, x, norm1_g.reshape(1, D_MODEL), w_in_b, jnp.tile(q_norm, 2).reshape(1, LANES),
      jnp.tile(k_norm, 2).reshape(1, LANES), jnp.tile(xq_norm, 2).reshape(1, LANES), mk_all, mv_all, w_out_b)


def _mixer_b_kernel(x_ref, g1_ref, win_ref, vg_ref, ws_ref, bias_ref, xqg_ref, mk_ref, mv_ref, wout_ref,
                    x1_ref, cv_ref, ocat, *, tb):
    x = x_ref[0]
    h = _rms_rows(x, g1_ref[...]).astype(BF16)
    z = _dot(h, win_ref[...])
    uv = _gelu_exact(z[:, :2 * B_WIDTH])
    u = uv[:, :B_WIDTH]
    vn = _rms_rows(uv[:, B_WIDTH:], vg_ref[...])
    cv_ref[0] = vn[tb - CHUNK:, :]
    xq_pairs = _pair_head_norm(z[:, 2 * B_WIDTH:], xqg_ref[...])

    wi = lax.broadcasted_iota(jnp.int32, (CHUNK, CHUNK), 0)
    wj = lax.broadcasted_iota(jnp.int32, (CHUNK, CHUNK), 1)
    causal = wi >= wj
    w_tril = [jnp.where(causal, ws_ref[g], 0.0).astype(BF16) for g in range(B_GROUPS)]
    lo = lax.broadcasted_iota(jnp.int32, (CHUNK, LANES), 1) < HEAD_DIM
    bias = bias_ref[...]
    for c in range(tb // CHUNK):
        vc = vn[c * CHUNK:(c + 1) * CHUNK, :].astype(BF16)
        r = [_dot(w_tril[g], vc[:, c0:c0 + 256]) for g, c0 in enumerate(GATE_WINDOW_START)]
        mixed = jnp.concatenate(
            [r[0][:, :128], jnp.where(lo, r[0][:, 128:], r[1][:, :128]), r[1][:, 128:],
             r[2][:, :128], jnp.where(lo, r[2][:, 128:], r[3][:, :128]), r[3][:, 128:]], axis=1)
        o = u[c * CHUNK:(c + 1) * CHUNK, :] * (mixed + bias)
        ocat[c * CHUNK:(c + 1) * CHUNK, 0:B_WIDTH] = o.astype(BF16)

    _prompt_mem_attention(xq_pairs, mk_ref, mv_ref, ocat, B_WIDTH)
    x1_ref[0] = x + _dot(ocat[...], wout_ref[...])


def _gate_bias_rows(b_s):
    return jnp.repeat(b_s.T, B_GROUP_DIM, axis=1)


def _mixer_b_prompt(x, layer, norm1_g, w_in_b, v_norm, w_s, b_s, xq_norm, mk_all, mv_all, w_out_b):
    n, t, _ = x.shape
    tb = min(PROMPT_ROWS, t)
    b_in = w_in_b.shape[1]
    xspec = pl.BlockSpec((1, tb, D_MODEL), lambda b, i: (b, i, 0))
    mspec = pl.BlockSpec((1, 1, N_MEM, X_WIDTH), lambda b, i: (layer, b, 0, 0))
    return pl.pallas_call(
        functools.partial(_mixer_b_kernel, tb=tb),
        grid=(n, t // tb),
        in_specs=[
            xspec,
            _const_spec((1, D_MODEL)),
            _const_spec((D_MODEL, b_in)),
            _const_spec((1, B_WIDTH)),
            _const_spec((B_GROUPS, CHUNK, CHUNK)),
            _const_spec((CHUNK, B_WIDTH)),
            _const_spec((1, LANES)),
            mspec,
            mspec,
            _const_spec((B_WIDTH + X_WIDTH, D_MODEL)),
        ],
        out_specs=[xspec, pl.BlockSpec((1, CHUNK, B_WIDTH), lambda b, i: (b, 0, 0))],
        out_shape=[
            jax.ShapeDtypeStruct((n, t, D_MODEL), F32),
            jax.ShapeDtypeStruct((n, CHUNK, B_WIDTH), F32),
        ],
        scratch_shapes=[pltpu.VMEM((tb, B_WIDTH + X_WIDTH), BF16)],
        compiler_params=_params("arbitrary", "arbitrary"),
        name="mixer_b_prompt",
    )(x, norm1_g.reshape(1, D_MODEL), w_in_b, v_norm.reshape(1, B_WIDTH), w_s, _gate_bias_rows(b_s),
      jnp.tile(xq_norm, 2).reshape(1, LANES), mk_all, mv_all, w_out_b)


def _ffn_prompt_kernel(x_ref, g2_ref, wfi_ref, cw_ref, cb_ref, wd_ref, y_ref, tail_ref, *, tb):
    t = pl.program_id(1)

    @pl.when(t == 0)
    def _():
        tail_ref[0] = jnp.zeros((8, D_FF), F32)

    x = x_ref[0]
    h = _rms_rows(x, g2_ref[...]).astype(BF16)
    gu = _dot(h, wfi_ref[...])
    a = gu[:, :D_FF]
    row = lax.broadcasted_iota(jnp.int32, (tb, 1), 0)
    prev_last = tail_ref[0, 7:8, :]
    am1 = jnp.where(row == 0, prev_last, pltpu.roll(a, 1, 0))
    am2 = jnp.where(row == 0, tail_ref[0, 6:7, :], jnp.where(row == 1, prev_last, pltpu.roll(a, 2, 0)))
    tail_ref[0] = a[tb - 8:, :]
    y_ref[0] = x + _dot(_conv_gate(a, am1, am2, gu[:, D_FF:], cw_ref, cb_ref), wd_ref[...])


def _ffn_prompt(x, layer, norm2_g, w_ffn_in_b, conv_w, conv_b, w_down_b):
    n, t, _ = x.shape
    tb = min(PROMPT_ROWS, t)
    xspec = pl.BlockSpec((1, tb, D_MODEL), lambda b, i: (b, i, 0))
    y, tail = pl.pallas_call(
        functools.partial(_ffn_prompt_kernel, tb=tb),
        grid=(n, t // tb),
        in_specs=[
            xspec,
            _const_spec((1, D_MODEL)),
            _layer_spec(layer, (D_MODEL, 2 * D_FF)),
            _const_spec((CONV_W, D_FF)),
            _const_spec((1, D_FF)),
            _layer_spec(layer, (D_FF, D_MODEL)),
        ],
        out_specs=[xspec, pl.BlockSpec((1, 8, D_FF), lambda b, i: (b, 0, 0))],
        out_shape=[
            jax.ShapeDtypeStruct((n, t, D_MODEL), F32),
            jax.ShapeDtypeStruct((n, 8, D_FF), F32),
        ],
        compiler_params=_params("arbitrary", "arbitrary"),
        name="ffn_prompt",
    )(x, norm2_g.reshape(1, D_MODEL), w_ffn_in_b, conv_w, conv_b.reshape(1, D_FF), w_down_b)
    return y, tail[:, 8 - (CONV_W - 1):, :]


def _stack_positions(ref, seq):
    return jnp.concatenate([ref[t] for t in range(seq)], axis=0)


def _ffn_sample_kernel(x_ref, oc_ref, wout_ref, g2_ref, wfi_ref, cw_ref, cb_ref, wd_ref, p0_ref, p1_ref,
                       y_ref, atail_ref, *, seq, nb):
    x1 = _stack_positions(x_ref, seq) + _dot(_stack_positions(oc_ref, seq), wout_ref[...])
    h = _rms_rows(x1, g2_ref[...]).astype(BF16)
    gu = _dot(h, wfi_ref[...])
    a = gu[:, :D_FF]
    blocks = [p0_ref[...], p1_ref[...]] + [a[t * nb:(t + 1) * nb] for t in range(seq)]
    am2 = jnp.concatenate(blocks[:seq], axis=0)
    am1 = jnp.concatenate(blocks[1:seq + 1], axis=0)
    y = x1 + _dot(_conv_gate(a, am1, am2, gu[:, D_FF:], cw_ref, cb_ref), wd_ref[...])
    for t in range(seq):
        y_ref[t] = y[t * nb:(t + 1) * nb]
    for r in range(CONV_W - 1):
        atail_ref[r] = blocks[seq + r]


def _ffn_sample(x3, oc3, w_out_e, layer, norm2_g, w_ffn_in_b, conv_w, conv_b, w_down_b, prefix):
    s, n, _ = x3.shape
    kc = oc3.shape[2]
    nb = min(SAMPLE_FFN_SEQS, n)
    assert n % nb == 0 and s >= CONV_W - 1

    def blk(lead, width):
        return pl.BlockSpec((lead, nb, width), lambda i: (0, i, 0))

    pspec = pl.BlockSpec((nb, D_FF), lambda i: (i, 0))
    return pl.pallas_call(
        functools.partial(_ffn_sample_kernel, seq=s, nb=nb),
        grid=(n // nb,),
        in_specs=[
            blk(s, D_MODEL),
            blk(s, kc),
            _const_spec((kc, D_MODEL)),
            _const_spec((1, D_MODEL)),
            _layer_spec(layer, (D_MODEL, 2 * D_FF)),
            _const_spec((CONV_W, D_FF)),
            _const_spec((1, D_FF)),
            _layer_spec(layer, (D_FF, D_MODEL)),
            pspec,
            pspec,
        ],
        out_specs=[blk(s, D_MODEL), blk(CONV_W - 1, D_FF)],
        out_shape=[
            jax.ShapeDtypeStruct((s, n, D_MODEL), F32),
            jax.ShapeDtypeStruct((CONV_W - 1, n, D_FF), F32),
        ],
        compiler_params=_params("arbitrary"),
        name="ffn_sample",
    )(x3, oc3, w_out_e, norm2_g.reshape(1, D_MODEL), w_ffn_in_b, conv_w, conv_b.reshape(1, D_FF), w_down_b,
      prefix[:, 0, :], prefix[:, 1, :])


def _write_slots(z, c0, heads, gain, out_ref):
    for hh in range(heads):
        out_ref[hh] = _slot_head_norm(z[:, c0 + hh * X_WIDTH:c0 + (hh + 1) * X_WIDTH], gain)


def _sample_in_a_kernel(x_ref, g1_ref, win_ref, wkvt_ref, qg_ref, kgc_ref, xqg_ref,
                        qs_ref, v_ref, xs_ref, kt_ref, vt_ref, *, seq):
    h = _rms_rows(_stack_positions(x_ref, seq), g1_ref[...]).astype(BF16)
    z = _dot(h, win_ref[...])
    _write_slots(z, 0, A_HEADS, qg_ref[...], qs_ref)
    c0 = A_HEADS * A_KV
    v_ref[...] = z[:, c0:c0 + A_KV]
    _write_slots(z, c0 + A_KV, X_HEADS, xqg_ref[...], xs_ref)
    kvt = _nt_dot(wkvt_ref[...], h)
    for hk in range(A_KV_HEADS):
        sl = slice(hk * HEAD_DIM, (hk + 1) * HEAD_DIM)
        kh = kvt[sl, :]
        ms = jnp.mean(kh * kh, axis=0, keepdims=True)
        kt_ref[sl, :] = kh * lax.rsqrt(ms + EPS) * kgc_ref[sl, :]
    vt_ref[...] = kvt[A_KV:, :]


def _expand_cols(w, heads, slot_of):
    d = w.shape[0]
    wh = w.reshape(d, heads, 1, HEAD_DIM)
    sel = np.zeros((heads, X_HEADS, 1), np.float32)
    for h in range(heads):
        sel[h, slot_of(h), 0] = 1.0
    return (wh * jnp.asarray(sel, w.dtype)).reshape(d, heads * X_WIDTH)


def _expand_rows(w, heads, slot_of):
    return _expand_cols(w.T, heads, slot_of).T


def _sample_in_a(x3, norm1_g, w_in_b, q_norm, k_norm, xq_norm):
    s, n, _ = x3.shape
    rows = s * n
    wq = _expand_cols(w_in_b[:, :A_Q], A_HEADS, lambda h: h // A_GROUP)
    wx = _expand_cols(w_in_b[:, A_Q + 2 * A_KV:], X_HEADS, lambda h: h)
    w_e = jnp.concatenate([wq, w_in_b[:, A_Q + A_KV:A_Q + 2 * A_KV], wx], axis=1)
    w_kvt = w_in_b[:, A_Q:A_Q + 2 * A_KV].T
    slots = lambda heads: jax.ShapeDtypeStruct((heads, rows, X_WIDTH), F32)
    full = lambda shape: pl.BlockSpec(shape, lambda i: (0,) * len(shape))
    return pl.pallas_call(
        functools.partial(_sample_in_a_kernel, seq=s),
        grid=(1,),
        in_specs=[
            _const_spec((s, n, D_MODEL)),
            _const_spec((1, D_MODEL)),
            _const_spec(w_e.shape),
            _const_spec(w_kvt.shape),
            _const_spec((1, A_KV)),
            _const_spec((A_KV, 1)),
            _const_spec((1, X_WIDTH)),
        ],
        out_specs=[
            full((A_HEADS, rows, A_KV)), full((rows, A_KV)), full((X_HEADS, rows, X_WIDTH)),
            full((A_KV, rows)), full((A_KV, rows)),
        ],
        out_shape=[
            slots(A_HEADS),
            jax.ShapeDtypeStruct((rows, A_KV), F32),
            slots(X_HEADS),
            jax.ShapeDtypeStruct((A_KV, rows), F32),
            jax.ShapeDtypeStruct((A_KV, rows), F32),
        ],
        compiler_params=_params("arbitrary"),
        name="sample_in_a",
    )(x3, norm1_g.reshape(1, D_MODEL), w_e, w_kvt, jnp.tile(q_norm, 4).reshape(1, A_KV),
      jnp.tile(k_norm, 4).reshape(A_KV, 1), jnp.tile(xq_norm, 4).reshape(1, X_WIDTH))


def _sample_in_b_kernel(x_ref, g1_ref, win_ref, vg_ref, coef_ref, gbias_ref, xqg_ref, o_ref, vn_ref, xs_ref,
                        *, seq, n):
    h = _rms_rows(_stack_positions(x_ref, seq), g1_ref[...]).astype(BF16)
    z = _dot(h, win_ref[...])
    uv = _gelu_exact(z[:, :2 * B_WIDTH])
    u = uv[:, :B_WIDTH]
    vn = _rms_rows(uv[:, B_WIDTH:], vg_ref[...])
    vn_ref[...] = vn
    for t in range(seq):
        mixed = gbias_ref[t:t + 1, :] + coef_ref[t * seq:t * seq + 1, :] * vn[0:n]
        for j in range(1, t + 1):
            mixed = mixed + coef_ref[t * seq + j:t * seq + j + 1, :] * vn[j * n:(j + 1) * n]
        o_ref[t * n:(t + 1) * n, :] = (u[t * n:(t + 1) * n] * mixed).astype(o_ref.dtype)
    _write_slots(z, 2 * B_WIDTH, X_HEADS, xqg_ref[...], xs_ref)


def _sample_in_b(x3, norm1_g, w_in_b, v_norm, w_s, b_s, xq_norm):
    s, n, _ = x3.shape
    rows = s * n
    wx = _expand_cols(w_in_b[:, 2 * B_WIDTH:], X_HEADS, lambda h: h)
    w_e = jnp.concatenate([w_in_b[:, :2 * B_WIDTH], wx], axis=1)
    coef = jnp.repeat(w_s[:, :s, :s].transpose(1, 2, 0).reshape(s * s, B_GROUPS), B_GROUP_DIM, axis=1)
    gbias = jnp.repeat(b_s[:, :s].T, B_GROUP_DIM, axis=1)
    full = lambda shape: pl.BlockSpec(shape, lambda i: (0,) * len(shape))
    return pl.pallas_call(
        functools.partial(_sample_in_b_kernel, seq=s, n=n),
        grid=(1,),
        in_specs=[
            _const_spec((s, n, D_MODEL)),
            _const_spec((1, D_MODEL)),
            _const_spec(w_e.shape),
            _const_spec((1, B_WIDTH)),
            _const_spec((s * s, B_WIDTH)),
            _const_spec((s, B_WIDTH)),
            _const_spec((1, X_WIDTH)),
        ],
        out_specs=[full((rows, B_WIDTH)), full((rows, B_WIDTH)), full((X_HEADS, rows, X_WIDTH))],
        out_shape=[
            jax.ShapeDtypeStruct((rows, B_WIDTH), BF16),
            jax.ShapeDtypeStruct((rows, B_WIDTH), F32),
            jax.ShapeDtypeStruct((X_HEADS, rows, X_WIDTH), F32),
        ],
        compiler_params=_params("arbitrary"),
        name="sample_in_b",
    )(x3, norm1_g.reshape(1, D_MODEL), w_e, v_norm.reshape(1, B_WIDTH), coef, gbias,
      jnp.tile(xq_norm, 4).reshape(1, X_WIDTH))


def _sample_attn_win_kernel(qs_ref, ktn_ref, vtn_ref, vn_ref, slope_ref, sink_ref, ck_ref, cv_ref,
                            o_ref, wk_ref, wv_ref, *, sb, seq):
    rows = A_HEADS * seq
    tq = _div_mod(lax.broadcasted_iota(jnp.int32, (rows, WINDOW), 0), seq)[1]
    lane = lax.broadcasted_iota(jnp.int32, (rows, WINDOW), 1)
    slope = slope_ref[...]
    sink = sink_ref[...]
    d1 = tq + WINDOW - lane
    valid1 = d1 < WINDOW
    bias1 = slope * d1.astype(F32)
    new_seq, new_pos = _div_mod(lane, seq)
    d2 = tq - new_pos
    causal2 = d2 >= 0
    bias2 = slope * d2.astype(F32)
    ktn = ktn_ref[...]
    vtn = vtn_ref[...]
    ktn_b = ktn.astype(BF16)
    vn_b = vn_ref[...].astype(BF16)
    keep_old = lax.broadcasted_iota(jnp.int32, (A_KV, WINDOW), 1) < WINDOW - seq

    def body(n, carry):
        q = (qs_ref[n] * SCALE).astype(BF16)
        ckt = ck_ref[n]
        cvt = cv_ref[n]
        s1 = jnp.where(valid1, _dot(q, ckt.astype(BF16)) - bias1, NEG)
        s2 = jnp.where(causal2 & (new_seq == n), _dot(q, ktn_b) - bias2, NEG)
        m = jnp.maximum(jnp.maximum(jnp.max(s1, axis=-1, keepdims=True), jnp.max(s2, axis=-1, keepdims=True)), sink)
        e1 = jnp.exp(s1 - m)
        e2 = jnp.exp(s2 - m)
        den = jnp.sum(e1, axis=-1, keepdims=True) + jnp.sum(e2, axis=-1, keepdims=True) + jnp.exp(sink - m)
        o = _nt_dot(e1.astype(BF16), cvt.astype(BF16)) + _dot(e2.astype(BF16), vn_b)
        o_ref[n] = o / den
        shift_new = WINDOW - seq - n * seq
        wk_ref[n] = jnp.where(keep_old, pltpu.roll(ckt, WINDOW - seq, 1), pltpu.roll(ktn, shift_new, 1))
        wv_ref[n] = jnp.where(keep_old, pltpu.roll(cvt, WINDOW - seq, 1), pltpu.roll(vtn, shift_new, 1))
        return carry

    lax.fori_loop(0, sb, body, 0)


def _sample_attn_win(qs, kt_new, vt_new, v_new, slope_rows, sink_rows, ckt, cvt, layer, seq):
    n = qs.shape[0]
    rows = A_HEADS * seq
    assert LANES % seq == 0
    sb = LANES // seq
    assert n % sb == 0

    def seqs(*tail):
        return pl.BlockSpec((sb,) + tail, lambda i: (i,) + (0,) * len(tail))

    cache = pl.BlockSpec((None, sb, A_KV, WINDOW), lambda i: (layer, i, 0, 0))
    win = jax.ShapeDtypeStruct((n, A_KV, WINDOW), F32)
    return pl.pallas_call(
        functools.partial(_sample_attn_win_kernel, sb=sb, seq=seq),
        grid=(n // sb,),
        in_specs=[
            seqs(rows, A_KV),
            pl.BlockSpec((A_KV, LANES), lambda i: (0, i)),
            pl.BlockSpec((A_KV, LANES), lambda i: (0, i)),
            pl.BlockSpec((LANES, A_KV), lambda i: (i, 0)),
            _const_spec((rows, 1)),
            _const_spec((rows, 1)),
            cache,
            cache,
        ],
        out_specs=[seqs(rows, A_KV), seqs(A_KV, WINDOW), seqs(A_KV, WINDOW)],
        out_shape=[jax.ShapeDtypeStruct((n, rows, A_KV), F32), win, win],
        compiler_params=_params("arbitrary"),
        name="sample_attn_win",
    )(qs, kt_new, vt_new, v_new, slope_rows, sink_rows, ckt, cvt)


def _sample_attn_mem_kernel(xs_ref, mk_ref, mv_ref, xo_ref, *, sb):
    def body(n, carry):
        q = (xs_ref[n] * SCALE).astype(BF16)
        s = _dot(q, mk_ref[n].astype(BF16))
        m = jnp.max(s, axis=-1, keepdims=True)
        e = jnp.exp(s - m)
        den = jnp.sum(e, axis=-1, keepdims=True)
        xo_ref[n] = _nt_dot(e.astype(BF16), mv_ref[n].astype(BF16)) / den
        return carry

    lax.fori_loop(0, sb, body, 0)


def _sample_attn_mem(xs, mkt, mvt, layer):
    n, xrows, _ = xs.shape
    sb = min(SAMPLE_MEM_SEQS, n)
    assert n % sb == 0
    qspec = pl.BlockSpec((sb, xrows, X_WIDTH), lambda i: (i, 0, 0))
    cache = pl.BlockSpec((None, sb, X_WIDTH, N_MEM), lambda i: (layer, i, 0, 0))
    return pl.pallas_call(
        functools.partial(_sample_attn_mem_kernel, sb=sb),
        grid=(n // sb,),
        in_specs=[qspec, cache, cache],
        out_specs=qspec,
        out_shape=jax.ShapeDtypeStruct((n, xrows, X_WIDTH), F32),
        compiler_params=_params("arbitrary"),
        name="sample_attn_mem",
    )(xs, mkt, mvt)


def _slots_to_seq(a, seq, n):
    hh, _, w = a.shape
    return a.reshape(hh, seq, n, w).transpose(2, 0, 1, 3).reshape(n, hh * seq, w)


def _seq_to_cols(a, heads, seq):
    n, _, w = a.shape
    return a.reshape(n, heads, seq, w).transpose(2, 0, 1, 3).reshape(seq, n, heads * w)


def _feature_major(cache):
    l, n, p, kv, d = cache.shape
    return cache.transpose(0, 1, 3, 4, 2).reshape(l, n, kv * d, p)


def kernel(x_prompt, x_sample, cache_win_k, cache_win_v, cache_mem_k, cache_mem_v, state_conv, mem_prompt,
           norm1_g, norm2_g, mem_norm_g, w_in_a, q_norm_a, k_norm_a, sinks_a, w_out_a, w_in_b, v_norm_b, w_s_b,
           b_s_b, w_out_b, w_mem_kv, xq_norm, xk_norm, w_ffn_in, conv_w, conv_b, w_down):
    depth = norm1_g.shape[0]
    assert depth == 2 and w_in_a.shape[0] == 1 and w_in_b.shape[0] == 1
    nb = x_prompt.shape[0]
    ns, seq, _ = x_sample.shape
    assert cache_win_k.shape[2] == WINDOW

    w_in_a_b = w_in_a[0].astype(BF16)
    w_in_b_b = w_in_b[0].astype(BF16)
    w_out_a_b = w_out_a[0].astype(BF16)
    w_out_b_b = w_out_b[0].astype(BF16)
    w_mem_kv_b = w_mem_kv.astype(BF16)
    w_ffn_in_b = w_ffn_in.astype(BF16)
    w_down_b = w_down.astype(BF16)

    mk_all, mv_all = _memory_kv(mem_prompt, mem_norm_g, w_mem_kv_b, xk_norm)
    x1, win_k_p, win_v_p = _mixer_a_prompt(x_prompt, 0, norm1_g[0], _reorder_heads(w_in_a_b, Q_LANE_ORDER, 1),
                                           q_norm_a[0], k_norm_a[0], xq_norm[0], sinks_a[0], mk_all, mv_all,
                                           _reorder_heads(w_out_a_b, Q_LANE_ORDER, 0))
    x2, conv_p0 = _ffn_prompt(x1, 0, norm2_g[0], w_ffn_in_b, conv_w[0], conv_b[0], w_down_b)
    x3, chunk_v_p = _mixer_b_prompt(x2, 1, norm1_g[1], w_in_b_b, v_norm_b[0], w_s_b[0], b_s_b[0], xq_norm[1],
                                    mk_all, mv_all, w_out_b_b)
    y_prompt, conv_p1 = _ffn_prompt(x3, 1, norm2_g[1], w_ffn_in_b, conv_w[1], conv_b[1], w_down_b)

    xs3 = x_sample.transpose(1, 0, 2)
    mkt = _feature_major(cache_mem_k)
    mvt = _feature_major(cache_mem_v)

    def by_seq_cols(a):
        return a.reshape(a.shape[0], seq, ns).transpose(0, 2, 1).reshape(a.shape[0], ns * seq)

    qs, v_new, xslots, kt_new, vt_new = _sample_in_a(xs3, norm1_g[0], w_in_a_b, q_norm_a[0], k_norm_a[0], xq_norm[0])
    slope_rows = jnp.repeat(jnp.asarray(SLOPES, F32), seq).reshape(A_HEADS * seq, 1)
    sink_rows = jnp.repeat(sinks_a[0].astype(F32), seq).reshape(A_HEADS * seq, 1)
    o_s, win_kt_s, win_vt_s = _sample_attn_win(
        _slots_to_seq(qs, seq, ns), by_seq_cols(kt_new), by_seq_cols(vt_new),
        v_new.reshape(seq, ns, A_KV).transpose(1, 0, 2).reshape(ns * seq, A_KV), slope_rows, sink_rows,
        _feature_major(cache_win_k), _feature_major(cache_win_v), 0, seq)
    xo_s = _sample_attn_mem(_slots_to_seq(xslots, seq, ns), mkt, mvt, 0)
    oc3 = jnp.concatenate([_seq_to_cols(o_s, A_HEADS, seq), _seq_to_cols(xo_s, X_HEADS, seq)], axis=2).astype(BF16)
    w_out_a_e = jnp.concatenate([_expand_rows(w_out_a_b[:A_Q], A_HEADS, lambda h: h // A_GROUP),
                                 _expand_rows(w_out_a_b[A_Q:], X_HEADS, lambda h: h)], axis=0)
    ys1, conv_s0 = _ffn_sample(xs3, oc3, w_out_a_e, 0, norm2_g[0], w_ffn_in_b, conv_w[0], conv_b[0], w_down_b,
                               state_conv[0])

    o_b, vn_s, xslots = _sample_in_b(ys1, norm1_g[1], w_in_b_b, v_norm_b[0], w_s_b[0], b_s_b[0], xq_norm[1])
    xo_s = _sample_attn_mem(_slots_to_seq(xslots, seq, ns), mkt, mvt, 1)
    oc3 = jnp.concatenate([o_b.reshape(seq, ns, B_WIDTH), _seq_to_cols(xo_s, X_HEADS, seq).astype(BF16)], axis=2)
    w_out_b_e = jnp.concatenate([w_out_b_b[:B_WIDTH], _expand_rows(w_out_b_b[B_WIDTH:], X_HEADS, lambda h: h)], axis=0)
    ys2, conv_s1 = _ffn_sample(ys1, oc3, w_out_b_e, 1, norm2_g[1], w_ffn_in_b, conv_w[1], conv_b[1], w_down_b,
                               state_conv[1])

    kv5 = (A_KV_HEADS, HEAD_DIM)

    def window_out(wt):
        return wt.reshape(1, ns, A_KV_HEADS, HEAD_DIM, WINDOW).transpose(0, 1, 4, 2, 3)

    return (
        y_prompt,
        ys2.transpose(1, 0, 2),
        win_k_p.reshape(1, nb, WINDOW, *kv5),
        win_v_p.reshape(1, nb, WINDOW, *kv5),
        chunk_v_p.reshape(1, nb, CHUNK, B_WIDTH),
        mk_all.reshape(depth, nb, N_MEM, X_HEADS, HEAD_DIM),
        mv_all.reshape(depth, nb, N_MEM, X_HEADS, HEAD_DIM),
        jnp.stack([conv_p0, conv_p1]),
        window_out(win_kt_s),
        window_out(win_vt_s),
        vn_s.reshape(seq, ns, B_WIDTH).transpose(1, 0, 2)[None],
        jnp.stack([conv_s0.transpose(1, 0, 2), conv_s1.transpose(1, 0, 2)]),
    )
```

```python
import functools
import math

import numpy as np
import jax
import jax.numpy as jnp
from jax import lax
from jax.experimental import pallas as pl
from jax.experimental.pallas import tpu as pltpu

F32 = jnp.float32
BF16 = jnp.bfloat16

D_MODEL = 1024
HEAD_DIM = 64
A_HEADS = 12
A_KV_HEADS = 4
A_GROUP = A_HEADS // A_KV_HEADS
WINDOW = 128
BLOCK = 128
B_WIDTH = 768
B_GROUPS = 4
B_GROUP_DIM = B_WIDTH // B_GROUPS
CHUNK = 128
X_HEADS = 4
X_WIDTH = X_HEADS * HEAD_DIM
N_MEM = 256
D_FF = 2816
CONV_W = 3
A_Q = A_HEADS * HEAD_DIM
A_KV = A_KV_HEADS * HEAD_DIM
EPS = 1e-6
NEG = -1e30
SCALE = HEAD_DIM ** -0.5
SQRT_HALF = float(np.sqrt(0.5).astype(np.float32))

LANES = 128
VMEM_LIMIT_BYTES = 56 * 1024 * 1024
GATE_WINDOW_START = (0, 128, 384, 512)
assert B_GROUP_DIM == 192 and B_GROUPS == 4
PROMPT_ROWS = 512
SAMPLE_MEM_SEQS = 16
SAMPLE_FFN_SEQS = 64
SAMPLE_SEQ_UNROLL = 4


def _alibi_slopes(n):
    def pow2(m):
        start = 2.0 ** (-8.0 / m)
        return [start ** (i + 1) for i in range(m)]

    p = 2 ** int(math.floor(math.log2(n)))
    s = pow2(p)
    if p < n:
        s = s + pow2(2 * p)[0::2][: n - p]
    return [float(np.float32(v)) for v in s]


SLOPES = _alibi_slopes(A_HEADS)

Q_LANE_ORDER = (0, 3, 1, 4, 2, 5, 6, 9, 7, 10, 8, 11)
assert all(Q_LANE_ORDER[p] // A_GROUP == 2 * (p // (2 * A_GROUP)) + p % 2 for p in range(A_HEADS))


def _reorder_heads(w, order, axis):
    n = len(order) * HEAD_DIM
    idx = np.concatenate([np.arange(h * HEAD_DIM, (h + 1) * HEAD_DIM) for h in order] + [np.arange(n, w.shape[axis])])
    return jnp.take(w, jnp.asarray(idx), axis=axis)


def _params(*sem):
    return pltpu.CompilerParams(dimension_semantics=sem, vmem_limit_bytes=VMEM_LIMIT_BYTES)


def _const_spec(shape):
    nd = len(shape)
    return pl.BlockSpec(shape, lambda *_: (0,) * nd, pipeline_mode=pl.Buffered(1))


def _layer_spec(layer, shape):
    nd = len(shape)
    return pl.BlockSpec((None,) + tuple(shape), lambda *_: (layer,) + (0,) * nd, pipeline_mode=pl.Buffered(1))


def _rms_rows(x, g):
    ms = jnp.mean(x * x, axis=-1, keepdims=True)
    return x * lax.rsqrt(ms + EPS) * g


def _pair_head_norm(z, g2):
    t, w = z.shape
    lo = lax.broadcasted_iota(jnp.int32, (t, LANES), 1) < HEAD_DIM
    out = []
    for p in range(w // LANES):
        zz = z[:, p * LANES:(p + 1) * LANES]
        sq = zz * zz
        s_lo = jnp.sum(jnp.where(lo, sq, 0.0), axis=-1, keepdims=True)
        s_hi = jnp.sum(jnp.where(lo, 0.0, sq), axis=-1, keepdims=True)
        r_lo = lax.rsqrt(s_lo * (1.0 / HEAD_DIM) + EPS)
        r_hi = lax.rsqrt(s_hi * (1.0 / HEAD_DIM) + EPS)
        out.append(zz * jnp.where(lo, r_lo, r_hi) * g2)
    return out


def _slot_head_norm(z, g4):
    ms = jnp.sum(z * z, axis=-1, keepdims=True) * (1.0 / HEAD_DIM)
    return z * lax.rsqrt(ms + EPS) * g4


def _nt_dot(a, b):
    return lax.dot_general(a, b, (((1,), (1,)), ((), ())), preferred_element_type=F32)


def _dot(a, b):
    return jnp.dot(a, b, preferred_element_type=F32)


def _div_mod(i, n):
    if n & (n - 1) == 0:
        return i >> (n.bit_length() - 1), i & (n - 1)
    return i // n, i % n


def _gelu_exact(x):
    return 0.5 * x * (1.0 + lax.erf(x * SQRT_HALF))


def _conv_gate(a, am1, am2, up, cw_ref, cb_ref):
    c = cb_ref[...] + ((cw_ref[0:1, :] * am2 + cw_ref[1:2, :] * am1) + cw_ref[2:3, :] * a)
    return ((c * (1.0 / (1.0 + jnp.exp(-c)))) * up).astype(BF16)


def _memkv_kernel(mem_ref, g_ref, w_ref, kg_ref, mk_ref, mv_ref):
    m = mem_ref[0]
    h = _rms_rows(m, g_ref[0]).astype(BF16)
    kv = _dot(h, w_ref[0])
    pairs = _pair_head_norm(kv[:, :X_WIDTH], kg_ref[0])
    for p, val in enumerate(pairs):
        mk_ref[0, 0, :, p * LANES:(p + 1) * LANES] = val
    mv_ref[0, 0] = kv[:, X_WIDTH:]


def _memory_kv(mem, mem_norm_g, w_mem_kv_b, xk_norm):
    depth = w_mem_kv_b.shape[0]
    n = mem.shape[0]
    g = mem_norm_g.reshape(depth, 1, D_MODEL)
    kg2 = jnp.tile(xk_norm, (1, 2)).reshape(depth, 1, LANES)
    out = jax.ShapeDtypeStruct((depth, n, N_MEM, X_WIDTH), F32)
    return pl.pallas_call(
        _memkv_kernel,
        grid=(depth, n),
        in_specs=[
            pl.BlockSpec((1, N_MEM, D_MODEL), lambda l, b: (b, 0, 0)),
            pl.BlockSpec((1, 1, D_MODEL), lambda l, b: (l, 0, 0)),
            pl.BlockSpec((1, D_MODEL, 2 * X_WIDTH), lambda l, b: (l, 0, 0)),
            pl.BlockSpec((1, 1, LANES), lambda l, b: (l, 0, 0)),
        ],
        out_specs=[
            pl.BlockSpec((1, 1, N_MEM, X_WIDTH), lambda l, b: (l, b, 0, 0)),
            pl.BlockSpec((1, 1, N_MEM, X_WIDTH), lambda l, b: (l, b, 0, 0)),
        ],
        out_shape=[out, out],
        compiler_params=_params("arbitrary", "arbitrary"),
        name="memory_kv",
    )(mem, g, w_mem_kv_b, kg2)


def _half_masked_rows(pair, lo):
    return jnp.concatenate([jnp.where(lo, pair, 0.0), jnp.where(lo, 0.0, pair)], axis=0).astype(BF16)


def _prompt_mem_attention(xq_pairs, mk_ref, mv_ref, ocat, col0):
    t = xq_pairs[0].shape[0]
    lo = lax.broadcasted_iota(jnp.int32, (t, LANES), 1) < HEAD_DIM
    for j, pair in enumerate(xq_pairs):
        mk = mk_ref[0, 0, :, j * LANES:(j + 1) * LANES].astype(BF16)
        mv = mv_ref[0, 0, :, j * LANES:(j + 1) * LANES].astype(BF16)
        s = _nt_dot(_half_masked_rows(pair * SCALE, lo), mk)
        m = jnp.max(s, axis=-1, keepdims=True)
        e = jnp.exp(s - m)
        den = jnp.sum(e, axis=-1, keepdims=True)
        o = _dot(e.astype(BF16), mv) / den
        ocat[:, col0 + j * LANES:col0 + (j + 1) * LANES] = jnp.where(lo, o[:t], o[t:]).astype(BF16)


def _mixer_a_kernel(sinks_ref, x_ref, g1_ref, win_ref, qg_ref, kg_ref, xqg_ref, mk_ref, mv_ref, wout_ref,
                    x1_ref, wk_ref, wv_ref, kbuf, vbuf, ocat, *, tb):
    t = pl.program_id(1)

    @pl.when(t == 0)
    def _():
        kbuf[0:BLOCK, :] = jnp.zeros((BLOCK, A_KV), BF16)
        vbuf[0:BLOCK, :] = jnp.zeros((BLOCK, A_KV), BF16)

    x = x_ref[0]
    h = _rms_rows(x, g1_ref[...]).astype(BF16)
    z = _dot(h, win_ref[...])
    q_pairs = _pair_head_norm(z[:, :A_Q], qg_ref[...])
    k_pairs = _pair_head_norm(z[:, A_Q:A_Q + A_KV], kg_ref[...])
    v = z[:, A_Q + A_KV:A_Q + 2 * A_KV]
    xq_pairs = _pair_head_norm(z[:, A_Q + 2 * A_KV:], xqg_ref[...])

    for p, kp in enumerate(k_pairs):
        wk_ref[0, :, p * LANES:(p + 1) * LANES] = kp[tb - BLOCK:, :]
        kbuf[BLOCK:BLOCK + tb, p * LANES:(p + 1) * LANES] = kp.astype(BF16)
    wv_ref[0] = v[tb - BLOCK:, :]
    vbuf[BLOCK:BLOCK + tb, :] = v.astype(BF16)

    qi = lax.broadcasted_iota(jnp.int32, (BLOCK, 2 * BLOCK), 0)
    kj = lax.broadcasted_iota(jnp.int32, (BLOCK, 2 * BLOCK), 1)
    d = qi + BLOCK - kj
    in_window = (d >= 0) & (d < WINDOW)
    dist = d.astype(F32)

    lo = lax.broadcasted_iota(jnp.int32, (BLOCK, LANES), 1) < HEAD_DIM
    pairs_per_slot = A_HEADS // A_KV_HEADS
    for s in range(tb // BLOCK):
        rows = slice(s * BLOCK, (s + 1) * BLOCK)
        first_key = jnp.where(t > 0, 0, BLOCK) if s == 0 else 0
        valid = in_window & (kj >= first_key)
        for j in range(A_KV // LANES):
            kw = kbuf[s * BLOCK:(s + 2) * BLOCK, j * LANES:(j + 1) * LANES]
            vw = vbuf[s * BLOCK:(s + 2) * BLOCK, j * LANES:(j + 1) * LANES]
            q_rows = jnp.concatenate(
                [_half_masked_rows(q_pairs[pairs_per_slot * j + c][rows] * SCALE, lo) for c in range(pairs_per_slot)],
                axis=0)
            sc_all = _nt_dot(q_rows, kw)
            e_parts, den_parts = [], []
            for r in range(2 * pairs_per_slot):
                hq = Q_LANE_ORDER[2 * pairs_per_slot * j + r]
                sc = sc_all[r * BLOCK:(r + 1) * BLOCK] - SLOPES[hq] * dist
                sc = jnp.where(valid, sc, NEG)
                sink = sinks_ref[hq]
                m = jnp.maximum(jnp.max(sc, axis=-1, keepdims=True), sink)
                e = jnp.exp(sc - m)
                den_parts.append(jnp.sum(e, axis=-1, keepdims=True) + jnp.exp(sink - m))
                e_parts.append(e.astype(BF16))
            o_all = _dot(jnp.concatenate(e_parts, axis=0), vw)
            for c in range(pairs_per_slot):
                o_lo = o_all[2 * c * BLOCK:(2 * c + 1) * BLOCK] / den_parts[2 * c]
                o_hi = o_all[(2 * c + 1) * BLOCK:(2 * c + 2) * BLOCK] / den_parts[2 * c + 1]
                col = pairs_per_slot * j + c
                ocat[rows, col * LANES:(col + 1) * LANES] = jnp.where(lo, o_lo, o_hi).astype(BF16)

    _prompt_mem_attention(xq_pairs, mk_ref, mv_ref, ocat, A_Q)
    x1_ref[0] = x + _dot(ocat[...], wout_ref[...])

    kbuf[0:BLOCK, :] = kbuf[tb:tb + BLOCK, :]
    vbuf[0:BLOCK, :] = vbuf[tb:tb + BLOCK, :]


def _mixer_a_prompt(x, layer, norm1_g, w_in_b, q_norm, k_norm, xq_norm, sinks, mk_all, mv_all, w_out_b):
    n, t, _ = x.shape
    tb = min(PROMPT_ROWS, t)
    a_in = w_in_b.shape[1]
    xspec = pl.BlockSpec((1, tb, D_MODEL), lambda b, i: (b, i, 0))
    mspec = pl.BlockSpec((1, 1, N_MEM, X_WIDTH), lambda b, i: (layer, b, 0, 0))
    wspec = pl.BlockSpec((1, BLOCK, A_KV), lambda b, i: (b, 0, 0))
    return pl.pallas_call(
        functools.partial(_mixer_a_kernel, tb=tb),
        grid=(n, t // tb),
        in_specs=[
            pl.BlockSpec(memory_space=pltpu.SMEM),
            xspec,
            _const_spec((1, D_MODEL)),
            _const_spec((D_MODEL, a_in)),
            _const_spec((1, LANES)),
            _const_spec((1, LANES)),
            _const_spec((1, LANES)),
            mspec,
            mspec,
            _const_spec((A_Q + X_WIDTH, D_MODEL)),
        ],
        out_specs=[xspec, wspec, wspec],
        out_shape=[
            jax.ShapeDtypeStruct((n, t, D_MODEL), F32),
            jax.ShapeDtypeStruct((n, BLOCK, A_KV), F32),
            jax.ShapeDtypeStruct((n, BLOCK, A_KV), F32),
        ],
        scratch_shapes=[
            pltpu.VMEM((tb + BLOCK, A_KV), BF16),
            pltpu.VMEM((tb + BLOCK, A_KV), BF16),
            pltpu.VMEM((tb, A_Q + X_WIDTH), BF16),
        ],
        compiler_params=_params("arbitrary", "arbitrary"),
        name="mixer_a_prompt",
    )(sinks, x, norm1_g.reshape(1, D_MODEL), w_in_b, jnp.tile(q_norm, 2).reshape(1, LANES),
      jnp.tile(k_norm, 2).reshape(1, LANES), jnp.tile(xq_norm, 2).reshape(1, LANES), mk_all, mv_all, w_out_b)


def _mixer_b_kernel(x_ref, g1_ref, win_ref, vg_ref, ws_ref, bias_ref, xqg_ref, mk_ref, mv_ref, wout_ref,
                    x1_ref, cv_ref, ocat, *, tb):
    x = x_ref[0]
    h = _rms_rows(x, g1_ref[...]).astype(BF16)
    z = _dot(h, win_ref[...])
    uv = _gelu_exact(z[:, :2 * B_WIDTH])
    u = uv[:, :B_WIDTH]
    vn = _rms_rows(uv[:, B_WIDTH:], vg_ref[...])
    cv_ref[0] = vn[tb - CHUNK:, :]
    xq_pairs = _pair_head_norm(z[:, 2 * B_WIDTH:], xqg_ref[...])

    wi = lax.broadcasted_iota(jnp.int32, (CHUNK, CHUNK), 0)
    wj = lax.broadcasted_iota(jnp.int32, (CHUNK, CHUNK), 1)
    causal = wi >= wj
    w_tril = [jnp.where(causal, ws_ref[g], 0.0).astype(BF16) for g in range(B_GROUPS)]
    lo = lax.broadcasted_iota(jnp.int32, (CHUNK, LANES), 1) < HEAD_DIM
    bias = bias_ref[...]
    for c in range(tb // CHUNK):
        vc = vn[c * CHUNK:(c + 1) * CHUNK, :].astype(BF16)
        r = [_dot(w_tril[g], vc[:, c0:c0 + 256]) for g, c0 in enumerate(GATE_WINDOW_START)]
        mixed = jnp.concatenate(
            [r[0][:, :128], jnp.where(lo, r[0][:, 128:], r[1][:, :128]), r[1][:, 128:],
             r[2][:, :128], jnp.where(lo, r[2][:, 128:], r[3][:, :128]), r[3][:, 128:]], axis=1)
        o = u[c * CHUNK:(c + 1) * CHUNK, :] * (mixed + bias)
        ocat[c * CHUNK:(c + 1) * CHUNK, 0:B_WIDTH] = o.astype(BF16)

    _prompt_mem_attention(xq_pairs, mk_ref, mv_ref, ocat, B_WIDTH)
    x1_ref[0] = x + _dot(ocat[...], wout_ref[...])


def _gate_bias_rows(b_s):
    return jnp.repeat(b_s.T, B_GROUP_DIM, axis=1)


def _mixer_b_prompt(x, layer, norm1_g, w_in_b, v_norm, w_s, b_s, xq_norm, mk_all, mv_all, w_out_b):
    n, t, _ = x.shape
    tb = min(PROMPT_ROWS, t)
    b_in = w_in_b.shape[1]
    xspec = pl.BlockSpec((1, tb, D_MODEL), lambda b, i: (b, i, 0))
    mspec = pl.BlockSpec((1, 1, N_MEM, X_WIDTH), lambda b, i: (layer, b, 0, 0))
    return pl.pallas_call(
        functools.partial(_mixer_b_kernel, tb=tb),
        grid=(n, t // tb),
        in_specs=[
            xspec,
            _const_spec((1, D_MODEL)),
            _const_spec((D_MODEL, b_in)),
            _const_spec((1, B_WIDTH)),
            _const_spec((B_GROUPS, CHUNK, CHUNK)),
            _const_spec((CHUNK, B_WIDTH)),
            _const_spec((1, LANES)),
            mspec,
            mspec,
            _const_spec((B_WIDTH + X_WIDTH, D_MODEL)),
        ],
        out_specs=[xspec, pl.BlockSpec((1, CHUNK, B_WIDTH), lambda b, i: (b, 0, 0))],
        out_shape=[
            jax.ShapeDtypeStruct((n, t, D_MODEL), F32),
            jax.ShapeDtypeStruct((n, CHUNK, B_WIDTH), F32),
        ],
        scratch_shapes=[pltpu.VMEM((tb, B_WIDTH + X_WIDTH), BF16)],
        compiler_params=_params("arbitrary", "arbitrary"),
        name="mixer_b_prompt",
    )(x, norm1_g.reshape(1, D_MODEL), w_in_b, v_norm.reshape(1, B_WIDTH), w_s, _gate_bias_rows(b_s),
      jnp.tile(xq_norm, 2).reshape(1, LANES), mk_all, mv_all, w_out_b)


def _ffn_prompt_kernel(x_ref, g2_ref, wfi_ref, cw_ref, cb_ref, wd_ref, y_ref, tail_ref, *, tb):
    t = pl.program_id(1)

    @pl.when(t == 0)
    def _():
        tail_ref[0] = jnp.zeros((8, D_FF), F32)

    x = x_ref[0]
    h = _rms_rows(x, g2_ref[...]).astype(BF16)
    gu = _dot(h, wfi_ref[...])
    a = gu[:, :D_FF]
    row = lax.broadcasted_iota(jnp.int32, (tb, 1), 0)
    prev_last = tail_ref[0, 7:8, :]
    am1 = jnp.where(row == 0, prev_last, pltpu.roll(a, 1, 0))
    am2 = jnp.where(row == 0, tail_ref[0, 6:7, :], jnp.where(row == 1, prev_last, pltpu.roll(a, 2, 0)))
    tail_ref[0] = a[tb - 8:, :]
    y_ref[0] = x + _dot(_conv_gate(a, am1, am2, gu[:, D_FF:], cw_ref, cb_ref), wd_ref[...])


def _ffn_prompt(x, layer, norm2_g, w_ffn_in_b, conv_w, conv_b, w_down_b):
    n, t, _ = x.shape
    tb = min(PROMPT_ROWS, t)
    xspec = pl.BlockSpec((1, tb, D_MODEL), lambda b, i: (b, i, 0))
    y, tail = pl.pallas_call(
        functools.partial(_ffn_prompt_kernel, tb=tb),
        grid=(n, t // tb),
        in_specs=[
            xspec,
            _const_spec((1, D_MODEL)),
            _layer_spec(layer, (D_MODEL, 2 * D_FF)),
            _const_spec((CONV_W, D_FF)),
            _const_spec((1, D_FF)),
            _layer_spec(layer, (D_FF, D_MODEL)),
        ],
        out_specs=[xspec, pl.BlockSpec((1, 8, D_FF), lambda b, i: (b, 0, 0))],
        out_shape=[
            jax.ShapeDtypeStruct((n, t, D_MODEL), F32),
            jax.ShapeDtypeStruct((n, 8, D_FF), F32),
        ],
        compiler_params=_params("arbitrary", "arbitrary"),
        name="ffn_prompt",
    )(x, norm2_g.reshape(1, D_MODEL), w_ffn_in_b, conv_w, conv_b.reshape(1, D_FF), w_down_b)
    return y, tail[:, 8 - (CONV_W - 1):, :]


def _stack_positions(ref, seq):
    return jnp.concatenate([ref[t] for t in range(seq)], axis=0)


def _ffn_sample_kernel(x_ref, oc_ref, wout_ref, g2_ref, wfi_ref, cw_ref, cb_ref, wd_ref, p0_ref, p1_ref,
                       y_ref, atail_ref, *, seq, nb):
    x1 = _stack_positions(x_ref, seq) + _dot(_stack_positions(oc_ref, seq), wout_ref[...])
    h = _rms_rows(x1, g2_ref[...]).astype(BF16)
    gu = _dot(h, wfi_ref[...])
    a = gu[:, :D_FF]
    blocks = [p0_ref[...], p1_ref[...]] + [a[t * nb:(t + 1) * nb] for t in range(seq)]
    am2 = jnp.concatenate(blocks[:seq], axis=0)
    am1 = jnp.concatenate(blocks[1:seq + 1], axis=0)
    y = x1 + _dot(_conv_gate(a, am1, am2, gu[:, D_FF:], cw_ref, cb_ref), wd_ref[...])
    for t in range(seq):
        y_ref[t] = y[t * nb:(t + 1) * nb]
    for r in range(CONV_W - 1):
        atail_ref[r] = blocks[seq + r]


def _ffn_sample(x3, oc3, w_out_e, layer, norm2_g, w_ffn_in_b, conv_w, conv_b, w_down_b, prefix):
    s, n, _ = x3.shape
    kc = oc3.shape[2]
    nb = min(SAMPLE_FFN_SEQS, n)
    assert n % nb == 0 and s >= CONV_W - 1

    def blk(lead, width):
        return pl.BlockSpec((lead, nb, width), lambda i: (0, i, 0))

    pspec = pl.BlockSpec((nb, D_FF), lambda i: (i, 0))
    return pl.pallas_call(
        functools.partial(_ffn_sample_kernel, seq=s, nb=nb),
        grid=(n // nb,),
        in_specs=[
            blk(s, D_MODEL),
            blk(s, kc),
            _const_spec((kc, D_MODEL)),
            _const_spec((1, D_MODEL)),
            _layer_spec(layer, (D_MODEL, 2 * D_FF)),
            _const_spec((CONV_W, D_FF)),
            _const_spec((1, D_FF)),
            _layer_spec(layer, (D_FF, D_MODEL)),
            pspec,
            pspec,
        ],
        out_specs=[blk(s, D_MODEL), blk(CONV_W - 1, D_FF)],
        out_shape=[
            jax.ShapeDtypeStruct((s, n, D_MODEL), F32),
            jax.ShapeDtypeStruct((CONV_W - 1, n, D_FF), F32),
        ],
        compiler_params=_params("arbitrary"),
        name="ffn_sample",
    )(x3, oc3, w_out_e, norm2_g.reshape(1, D_MODEL), w_ffn_in_b, conv_w, conv_b.reshape(1, D_FF), w_down_b,
      prefix[:, 0, :], prefix[:, 1, :])


def _write_slots(z, c0, heads, gain, out_ref):
    for hh in range(heads):
        out_ref[hh] = _slot_head_norm(z[:, c0 + hh * X_WIDTH:c0 + (hh + 1) * X_WIDTH], gain)


def _sample_in_a_kernel(x_ref, g1_ref, win_ref, wkvt_ref, qg_ref, kgc_ref, xqg_ref,
                        qs_ref, v_ref, xs_ref, kt_ref, vt_ref, *, seq):
    h = _rms_rows(_stack_positions(x_ref, seq), g1_ref[...]).astype(BF16)
    z = _dot(h, win_ref[...])
    _write_slots(z, 0, A_HEADS, qg_ref[...], qs_ref)
    c0 = A_HEADS * A_KV
    v_ref[...] = z[:, c0:c0 + A_KV]
    _write_slots(z, c0 + A_KV, X_HEADS, xqg_ref[...], xs_ref)
    kvt = _nt_dot(wkvt_ref[...], h)
    for hk in range(A_KV_HEADS):
        sl = slice(hk * HEAD_DIM, (hk + 1) * HEAD_DIM)
        kh = kvt[sl, :]
        ms = jnp.mean(kh * kh, axis=0, keepdims=True)
        kt_ref[sl, :] = kh * lax.rsqrt(ms + EPS) * kgc_ref[sl, :]
    vt_ref[...] = kvt[A_KV:, :]


def _expand_cols(w, heads, slot_of):
    d = w.shape[0]
    wh = w.reshape(d, heads, 1, HEAD_DIM)
    sel = np.zeros((heads, X_HEADS, 1), np.float32)
    for h in range(heads):
        sel[h, slot_of(h), 0] = 1.0
    return (wh * jnp.asarray(sel, w.dtype)).reshape(d, heads * X_WIDTH)


def _expand_rows(w, heads, slot_of):
    return _expand_cols(w.T, heads, slot_of).T


def _sample_in_a(x3, norm1_g, w_in_b, q_norm, k_norm, xq_norm):
    s, n, _ = x3.shape
    rows = s * n
    wq = _expand_cols(w_in_b[:, :A_Q], A_HEADS, lambda h: h // A_GROUP)
    wx = _expand_cols(w_in_b[:, A_Q + 2 * A_KV:], X_HEADS, lambda h: h)
    w_e = jnp.concatenate([wq, w_in_b[:, A_Q + A_KV:A_Q + 2 * A_KV], wx], axis=1)
    w_kvt = w_in_b[:, A_Q:A_Q + 2 * A_KV].T
    slots = lambda heads: jax.ShapeDtypeStruct((heads, rows, X_WIDTH), F32)
    full = lambda shape: pl.BlockSpec(shape, lambda i: (0,) * len(shape))
    return pl.pallas_call(
        functools.partial(_sample_in_a_kernel, seq=s),
        grid=(1,),
        in_specs=[
            _const_spec((s, n, D_MODEL)),
            _const_spec((1, D_MODEL)),
            _const_spec(w_e.shape),
            _const_spec(w_kvt.shape),
            _const_spec((1, A_KV)),
            _const_spec((A_KV, 1)),
            _const_spec((1, X_WIDTH)),
        ],
        out_specs=[
            full((A_HEADS, rows, A_KV)), full((rows, A_KV)), full((X_HEADS, rows, X_WIDTH)),
            full((A_KV, rows)), full((A_KV, rows)),
        ],
        out_shape=[
            slots(A_HEADS),
            jax.ShapeDtypeStruct((rows, A_KV), F32),
            slots(X_HEADS),
            jax.ShapeDtypeStruct((A_KV, rows), F32),
            jax.ShapeDtypeStruct((A_KV, rows), F32),
        ],
        compiler_params=_params("arbitrary"),
        name="sample_in_a",
    )(x3, norm1_g.reshape(1, D_MODEL), w_e, w_kvt, jnp.tile(q_norm, 4).reshape(1, A_KV),
      jnp.tile(k_norm, 4).reshape(A_KV, 1), jnp.tile(xq_norm, 4).reshape(1, X_WIDTH))


def _sample_in_b_kernel(x_ref, g1_ref, win_ref, vg_ref, coef_ref, gbias_ref, xqg_ref, o_ref, vn_ref, xs_ref,
                        *, seq, n):
    h = _rms_rows(_stack_positions(x_ref, seq), g1_ref[...]).astype(BF16)
    z = _dot(h, win_ref[...])
    uv = _gelu_exact(z[:, :2 * B_WIDTH])
    u = uv[:, :B_WIDTH]
    vn = _rms_rows(uv[:, B_WIDTH:], vg_ref[...])
    vn_ref[...] = vn
    for t in range(seq):
        mixed = gbias_ref[t:t + 1, :] + coef_ref[t * seq:t * seq + 1, :] * vn[0:n]
        for j in range(1, t + 1):
            mixed = mixed + coef_ref[t * seq + j:t * seq + j + 1, :] * vn[j * n:(j + 1) * n]
        o_ref[t * n:(t + 1) * n, :] = (u[t * n:(t + 1) * n] * mixed).astype(o_ref.dtype)
    _write_slots(z, 2 * B_WIDTH, X_HEADS, xqg_ref[...], xs_ref)


def _sample_in_b(x3, norm1_g, w_in_b, v_norm, w_s, b_s, xq_norm):
    s, n, _ = x3.shape
    rows = s * n
    wx = _expand_cols(w_in_b[:, 2 * B_WIDTH:], X_HEADS, lambda h: h)
    w_e = jnp.concatenate([w_in_b[:, :2 * B_WIDTH], wx], axis=1)
    coef = jnp.repeat(w_s[:, :s, :s].transpose(1, 2, 0).reshape(s * s, B_GROUPS), B_GROUP_DIM, axis=1)
    gbias = jnp.repeat(b_s[:, :s].T, B_GROUP_DIM, axis=1)
    full = lambda shape: pl.BlockSpec(shape, lambda i: (0,) * len(shape))
    return pl.pallas_call(
        functools.partial(_sample_in_b_kernel, seq=s, n=n),
        grid=(1,),
        in_specs=[
            _const_spec((s, n, D_MODEL)),
            _const_spec((1, D_MODEL)),
            _const_spec(w_e.shape),
            _const_spec((1, B_WIDTH)),
            _const_spec((s * s, B_WIDTH)),
            _const_spec((s, B_WIDTH)),
            _const_spec((1, X_WIDTH)),
        ],
        out_specs=[full((rows, B_WIDTH)), full((rows, B_WIDTH)), full((X_HEADS, rows, X_WIDTH))],
        out_shape=[
            jax.ShapeDtypeStruct((rows, B_WIDTH), BF16),
            jax.ShapeDtypeStruct((rows, B_WIDTH), F32),
            jax.ShapeDtypeStruct((X_HEADS, rows, X_WIDTH), F32),
        ],
        compiler_params=_params("arbitrary"),
        name="sample_in_b",
    )(x3, norm1_g.reshape(1, D_MODEL), w_e, v_norm.reshape(1, B_WIDTH), coef, gbias,
      jnp.tile(xq_norm, 4).reshape(1, X_WIDTH))


def _sample_attn_win_kernel(qs_ref, ktn_ref, vtn_ref, vn_ref, slope_ref, sink_ref, ck_ref, cv_ref,
                            o_ref, wk_ref, wv_ref, *, sb, seq):
    rows = A_HEADS * seq
    tq = _div_mod(lax.broadcasted_iota(jnp.int32, (rows, WINDOW), 0), seq)[1]
    lane = lax.broadcasted_iota(jnp.int32, (rows, WINDOW), 1)
    slope = slope_ref[...]
    sink = sink_ref[...]
    d1 = tq + WINDOW - lane
    valid1 = d1 < WINDOW
    bias1 = slope * d1.astype(F32)
    new_seq, new_pos = _div_mod(lane, seq)
    d2 = tq - new_pos
    causal2 = d2 >= 0
    bias2 = slope * d2.astype(F32)
    ktn = ktn_ref[...]
    vtn = vtn_ref[...]
    ktn_b = ktn.astype(BF16)
    vn_b = vn_ref[...].astype(BF16)
    keep_old = lax.broadcasted_iota(jnp.int32, (A_KV, WINDOW), 1) < WINDOW - seq

    def body(n, carry):
        q = (qs_ref[n] * SCALE).astype(BF16)
        ckt = ck_ref[n]
        cvt = cv_ref[n]
        s1 = jnp.where(valid1, _dot(q, ckt.astype(BF16)) - bias1, NEG)
        s2 = jnp.where(causal2 & (new_seq == n), _dot(q, ktn_b) - bias2, NEG)
        m = jnp.maximum(jnp.maximum(jnp.max(s1, axis=-1, keepdims=True), jnp.max(s2, axis=-1, keepdims=True)), sink)
        e1 = jnp.exp(s1 - m)
        e2 = jnp.exp(s2 - m)
        den = jnp.sum(e1, axis=-1, keepdims=True) + jnp.sum(e2, axis=-1, keepdims=True) + jnp.exp(sink - m)
        o = _nt_dot(e1.astype(BF16), cvt.astype(BF16)) + _dot(e2.astype(BF16), vn_b)
        o_ref[n] = o / den
        shift_new = WINDOW - seq - n * seq
        wk_ref[n] = jnp.where(keep_old, pltpu.roll(ckt, WINDOW - seq, 1), pltpu.roll(ktn, shift_new, 1))
        wv_ref[n] = jnp.where(keep_old, pltpu.roll(cvt, WINDOW - seq, 1), pltpu.roll(vtn, shift_new, 1))
        return carry

    lax.fori_loop(0, sb, body, 0, unroll=SAMPLE_SEQ_UNROLL)


def _sample_attn_win(qs, kt_new, vt_new, v_new, slope_rows, sink_rows, ckt, cvt, layer, seq):
    n = qs.shape[0]
    rows = A_HEADS * seq
    assert LANES % seq == 0
    sb = LANES // seq
    assert n % sb == 0

    def seqs(*tail):
        return pl.BlockSpec((sb,) + tail, lambda i: (i,) + (0,) * len(tail))

    cache = pl.BlockSpec((None, sb, A_KV, WINDOW), lambda i: (layer, i, 0, 0))
    win = jax.ShapeDtypeStruct((n, A_KV, WINDOW), F32)
    return pl.pallas_call(
        functools.partial(_sample_attn_win_kernel, sb=sb, seq=seq),
        grid=(n // sb,),
        in_specs=[
            seqs(rows, A_KV),
            pl.BlockSpec((A_KV, LANES), lambda i: (0, i)),
            pl.BlockSpec((A_KV, LANES), lambda i: (0, i)),
            pl.BlockSpec((LANES, A_KV), lambda i: (i, 0)),
            _const_spec((rows, 1)),
            _const_spec((rows, 1)),
            cache,
            cache,
        ],
        out_specs=[seqs(rows, A_KV), seqs(A_KV, WINDOW), seqs(A_KV, WINDOW)],
        out_shape=[jax.ShapeDtypeStruct((n, rows, A_KV), F32), win, win],
        compiler_params=_params("arbitrary"),
        name="sample_attn_win",
    )(qs, kt_new, vt_new, v_new, slope_rows, sink_rows, ckt, cvt)


def _sample_attn_mem_kernel(xs_ref, mk_ref, mv_ref, xo_ref, *, sb):
    def body(n, carry):
        q = (xs_ref[n] * SCALE).astype(BF16)
        s = _dot(q, mk_ref[n].astype(BF16))
        m = jnp.max(s, axis=-1, keepdims=True)
        e = jnp.exp(s - m)
        den = jnp.sum(e, axis=-1, keepdims=True)
        xo_ref[n] = _nt_dot(e.astype(BF16), mv_ref[n].astype(BF16)) / den
        return carry

    lax.fori_loop(0, sb, body, 0, unroll=SAMPLE_SEQ_UNROLL)


def _sample_attn_mem(xs, mkt, mvt, layer):
    n, xrows, _ = xs.shape
    sb = min(SAMPLE_MEM_SEQS, n)
    assert n % sb == 0
    qspec = pl.BlockSpec((sb, xrows, X_WIDTH), lambda i: (i, 0, 0))
    cache = pl.BlockSpec((None, sb, X_WIDTH, N_MEM), lambda i: (layer, i, 0, 0))
    return pl.pallas_call(
        functools.partial(_sample_attn_mem_kernel, sb=sb),
        grid=(n // sb,),
        in_specs=[qspec, cache, cache],
        out_specs=qspec,
        out_shape=jax.ShapeDtypeStruct((n, xrows, X_WIDTH), F32),
        compiler_params=_params("arbitrary"),
        name="sample_attn_mem",
    )(xs, mkt, mvt)


def _slots_to_seq(a, seq, n):
    hh, _, w = a.shape
    return a.reshape(hh, seq, n, w).transpose(2, 0, 1, 3).reshape(n, hh * seq, w)


def _seq_to_cols(a, heads, seq):
    n, _, w = a.shape
    return a.reshape(n, heads, seq, w).transpose(2, 0, 1, 3).reshape(seq, n, heads * w)


def _feature_major(cache):
    l, n, p, kv, d = cache.shape
    return cache.transpose(0, 1, 3, 4, 2).reshape(l, n, kv * d, p)


def kernel(x_prompt, x_sample, cache_win_k, cache_win_v, cache_mem_k, cache_mem_v, state_conv, mem_prompt,
           norm1_g, norm2_g, mem_norm_g, w_in_a, q_norm_a, k_norm_a, sinks_a, w_out_a, w_in_b, v_norm_b, w_s_b,
           b_s_b, w_out_b, w_mem_kv, xq_norm, xk_norm, w_ffn_in, conv_w, conv_b, w_down):
    depth = norm1_g.shape[0]
    assert depth == 2 and w_in_a.shape[0] == 1 and w_in_b.shape[0] == 1
    nb = x_prompt.shape[0]
    ns, seq, _ = x_sample.shape
    assert cache_win_k.shape[2] == WINDOW

    w_in_a_b = w_in_a[0].astype(BF16)
    w_in_b_b = w_in_b[0].astype(BF16)
    w_out_a_b = w_out_a[0].astype(BF16)
    w_out_b_b = w_out_b[0].astype(BF16)
    w_mem_kv_b = w_mem_kv.astype(BF16)
    w_ffn_in_b = w_ffn_in.astype(BF16)
    w_down_b = w_down.astype(BF16)

    mk_all, mv_all = _memory_kv(mem_prompt, mem_norm_g, w_mem_kv_b, xk_norm)
    x1, win_k_p, win_v_p = _mixer_a_prompt(x_prompt, 0, norm1_g[0], _reorder_heads(w_in_a_b, Q_LANE_ORDER, 1),
                                           q_norm_a[0], k_norm_a[0], xq_norm[0], sinks_a[0], mk_all, mv_all,
                                           _reorder_heads(w_out_a_b, Q_LANE_ORDER, 0))
    x2, conv_p0 = _ffn_prompt(x1, 0, norm2_g[0], w_ffn_in_b, conv_w[0], conv_b[0], w_down_b)
    x3, chunk_v_p = _mixer_b_prompt(x2, 1, norm1_g[1], w_in_b_b, v_norm_b[0], w_s_b[0], b_s_b[0], xq_norm[1],
                                    mk_all, mv_all, w_out_b_b)
    y_prompt, conv_p1 = _ffn_prompt(x3, 1, norm2_g[1], w_ffn_in_b, conv_w[1], conv_b[1], w_down_b)

    xs3 = x_sample.transpose(1, 0, 2)
    mkt = _feature_major(cache_mem_k)
    mvt = _feature_major(cache_mem_v)

    def by_seq_cols(a):
        return a.reshape(a.shape[0], seq, ns).transpose(0, 2, 1).reshape(a.shape[0], ns * seq)

    qs, v_new, xslots, kt_new, vt_new = _sample_in_a(xs3, norm1_g[0], w_in_a_b, q_norm_a[0], k_norm_a[0], xq_norm[0])
    slope_rows = jnp.repeat(jnp.asarray(SLOPES, F32), seq).reshape(A_HEADS * seq, 1)
    sink_rows = jnp.repeat(sinks_a[0].astype(F32), seq).reshape(A_HEADS * seq, 1)
    o_s, win_kt_s, win_vt_s = _sample_attn_win(
        _slots_to_seq(qs, seq, ns), by_seq_cols(kt_new), by_seq_cols(vt_new),
        v_new.reshape(seq, ns, A_KV).transpose(1, 0, 2).reshape(ns * seq, A_KV), slope_rows, sink_rows,
        _feature_major(cache_win_k), _feature_major(cache_win_v), 0, seq)
    xo_s = _sample_attn_mem(_slots_to_seq(xslots, seq, ns), mkt, mvt, 0)
    oc3 = jnp.concatenate([_seq_to_cols(o_s, A_HEADS, seq), _seq_to_cols(xo_s, X_HEADS, seq)], axis=2).astype(BF16)
    w_out_a_e = jnp.concatenate([_expand_rows(w_out_a_b[:A_Q], A_HEADS, lambda h: h // A_GROUP),
                                 _expand_rows(w_out_a_b[A_Q:], X_HEADS, lambda h: h)], axis=0)
    ys1, conv_s0 = _ffn_sample(xs3, oc3, w_out_a_e, 0, norm2_g[0], w_ffn_in_b, conv_w[0], conv_b[0], w_down_b,
                               state_conv[0])

    o_b, vn_s, xslots = _sample_in_b(ys1, norm1_g[1], w_in_b_b, v_norm_b[0], w_s_b[0], b_s_b[0], xq_norm[1])
    xo_s = _sample_attn_mem(_slots_to_seq(xslots, seq, ns), mkt, mvt, 1)
    oc3 = jnp.concatenate([o_b.reshape(seq, ns, B_WIDTH), _seq_to_cols(xo_s, X_HEADS, seq).astype(BF16)], axis=2)
    w_out_b_e = jnp.concatenate([w_out_b_b[:B_WIDTH], _expand_rows(w_out_b_b[B_WIDTH:], X_HEADS, lambda h: h)], axis=0)
    ys2, conv_s1 = _ffn_sample(ys1, oc3, w_out_b_e, 1, norm2_g[1], w_ffn_in_b, conv_w[1], conv_b[1], w_down_b,
                               state_conv[1])

    kv5 = (A_KV_HEADS, HEAD_DIM)

    def window_out(wt):
        return wt.reshape(1, ns, A_KV_HEADS, HEAD_DIM, WINDOW).transpose(0, 1, 4, 2, 3)

    return (
        y_prompt,
        ys2.transpose(1, 0, 2),
        win_k_p.reshape(1, nb, WINDOW, *kv5),
        win_v_p.reshape(1, nb, WINDOW, *kv5),
        chunk_v_p.reshape(1, nb, CHUNK, B_WIDTH),
        mk_all.reshape(depth, nb, N_MEM, X_HEADS, HEAD_DIM),
        mv_all.reshape(depth, nb, N_MEM, X_HEADS, HEAD_DIM),
        jnp.stack([conv_p0, conv_p1]),
        window_out(win_kt_s),
        window_out(win_vt_s),
        vn_s.reshape(seq, ns, B_WIDTH).transpose(1, 0, 2)[None],
        jnp.stack([conv_s0.transpose(1, 0, 2), conv_s1.transpose(1, 0, 2)]),
    )
```

```python
import functools
import math

import numpy as np
import jax
import jax.numpy as jnp
from jax import lax
from jax.experimental import pallas as pl
from jax.experimental.pallas import tpu as pltpu

F32 = jnp.float32
BF16 = jnp.bfloat16

D_MODEL = 1024
HEAD_DIM = 64
A_HEADS = 12
A_KV_HEADS = 4
A_GROUP = A_HEADS // A_KV_HEADS
WINDOW = 128
BLOCK = 128
B_WIDTH = 768
B_GROUPS = 4
B_GROUP_DIM = B_WIDTH // B_GROUPS
CHUNK = 128
X_HEADS = 4
X_WIDTH = X_HEADS * HEAD_DIM
N_MEM = 256
D_FF = 2816
CONV_W = 3
A_Q = A_HEADS * HEAD_DIM
A_KV = A_KV_HEADS * HEAD_DIM
EPS = 1e-6
NEG = -1e30
SCALE = HEAD_DIM ** -0.5
SQRT_HALF = float(np.sqrt(0.5).astype(np.float32))

LANES = 128
VMEM_LIMIT_BYTES = 56 * 1024 * 1024
GATE_WINDOW_START = (0, 128, 384, 512)
assert B_GROUP_DIM == 192 and B_GROUPS == 4
PROMPT_ROWS = 512
SAMPLE_MEM_SEQS = 16
SAMPLE_FFN_SEQS = 64
SAMPLE_SEQ_UNROLL = 4


def _alibi_slopes(n):
    def pow2(m):
        start = 2.0 ** (-8.0 / m)
        return [start ** (i + 1) for i in range(m)]

    p = 2 ** int(math.floor(math.log2(n)))
    s = pow2(p)
    if p < n:
        s = s + pow2(2 * p)[0::2][: n - p]
    return [float(np.float32(v)) for v in s]


SLOPES = _alibi_slopes(A_HEADS)

Q_LANE_ORDER = (0, 3, 1, 4, 2, 5, 6, 9, 7, 10, 8, 11)
assert all(Q_LANE_ORDER[p] // A_GROUP == 2 * (p // (2 * A_GROUP)) + p % 2 for p in range(A_HEADS))


def _reorder_heads(w, order, axis):
    n = len(order) * HEAD_DIM
    idx = np.concatenate([np.arange(h * HEAD_DIM, (h + 1) * HEAD_DIM) for h in order] + [np.arange(n, w.shape[axis])])
    return jnp.take(w, jnp.asarray(idx), axis=axis)


def _params(*sem):
    return pltpu.CompilerParams(dimension_semantics=sem, vmem_limit_bytes=VMEM_LIMIT_BYTES)


def _const_spec(shape):
    nd = len(shape)
    return pl.BlockSpec(shape, lambda *_: (0,) * nd, pipeline_mode=pl.Buffered(1))


def _layer_spec(layer, shape):
    nd = len(shape)
    return pl.BlockSpec((None,) + tuple(shape), lambda *_: (layer,) + (0,) * nd, pipeline_mode=pl.Buffered(1))


def _rms_rows(x, g):
    ms = jnp.mean(x * x, axis=-1, keepdims=True)
    return x * lax.rsqrt(ms + EPS) * g


def _pair_head_norm(z, g2):
    t, w = z.shape
    lo = lax.broadcasted_iota(jnp.int32, (t, LANES), 1) < HEAD_DIM
    out = []
    for p in range(w // LANES):
        zz = z[:, p * LANES:(p + 1) * LANES]
        sq = zz * zz
        s_lo = jnp.sum(jnp.where(lo, sq, 0.0), axis=-1, keepdims=True)
        s_hi = jnp.sum(jnp.where(lo, 0.0, sq), axis=-1, keepdims=True)
        r_lo = lax.rsqrt(s_lo * (1.0 / HEAD_DIM) + EPS)
        r_hi = lax.rsqrt(s_hi * (1.0 / HEAD_DIM) + EPS)
        out.append(zz * jnp.where(lo, r_lo, r_hi) * g2)
    return out


def _slot_head_norm(z, g4):
    ms = jnp.sum(z * z, axis=-1, keepdims=True) * (1.0 / HEAD_DIM)
    return z * lax.rsqrt(ms + EPS) * g4


def _nt_dot(a, b):
    return lax.dot_general(a, b, (((1,), (1,)), ((), ())), preferred_element_type=F32)


def _dot(a, b):
    return jnp.dot(a, b, preferred_element_type=F32)


def _div_mod(i, n):
    if n & (n - 1) == 0:
        return i >> (n.bit_length() - 1), i & (n - 1)
    return i // n, i % n


def _gelu_exact(x):
    return 0.5 * x * (1.0 + lax.erf(x * SQRT_HALF))


def _conv_gate(a, am1, am2, up, cw_ref, cb_ref):
    c = cb_ref[...] + ((cw_ref[0:1, :] * am2 + cw_ref[1:2, :] * am1) + cw_ref[2:3, :] * a)
    return ((c * (1.0 / (1.0 + jnp.exp(-c)))) * up).astype(BF16)


def _memkv_kernel(mem_ref, g_ref, w_ref, kg_ref, mk_ref, mv_ref):
    m = mem_ref[0]
    h = _rms_rows(m, g_ref[0]).astype(BF16)
    kv = _dot(h, w_ref[0])
    pairs = _pair_head_norm(kv[:, :X_WIDTH], kg_ref[0])
    for p, val in enumerate(pairs):
        mk_ref[0, 0, :, p * LANES:(p + 1) * LANES] = val
    mv_ref[0, 0] = kv[:, X_WIDTH:]


def _memory_kv(mem, mem_norm_g, w_mem_kv_b, xk_norm):
    depth = w_mem_kv_b.shape[0]
    n = mem.shape[0]
    g = mem_norm_g.reshape(depth, 1, D_MODEL)
    kg2 = jnp.tile(xk_norm, (1, 2)).reshape(depth, 1, LANES)
    out = jax.ShapeDtypeStruct((depth, n, N_MEM, X_WIDTH), F32)
    return pl.pallas_call(
        _memkv_kernel,
        grid=(depth, n),
        in_specs=[
            pl.BlockSpec((1, N_MEM, D_MODEL), lambda l, b: (b, 0, 0)),
            pl.BlockSpec((1, 1, D_MODEL), lambda l, b: (l, 0, 0)),
            pl.BlockSpec((1, D_MODEL, 2 * X_WIDTH), lambda l, b: (l, 0, 0)),
            pl.BlockSpec((1, 1, LANES), lambda l, b: (l, 0, 0)),
        ],
        out_specs=[
            pl.BlockSpec((1, 1, N_MEM, X_WIDTH), lambda l, b: (l, b, 0, 0)),
            pl.BlockSpec((1, 1, N_MEM, X_WIDTH), lambda l, b: (l, b, 0, 0)),
        ],
        out_shape=[out, out],
        compiler_params=_params("arbitrary", "arbitrary"),
        name="memory_kv",
    )(mem, g, w_mem_kv_b, kg2)


def _half_masked_rows(pair, lo):
    return jnp.concatenate([jnp.where(lo, pair, 0.0), jnp.where(lo, 0.0, pair)], axis=0).astype(BF16)


def _prompt_mem_attention(xq_pairs, mk_ref, mv_ref, ocat, col0):
    t = xq_pairs[0].shape[0]
    lo = lax.broadcasted_iota(jnp.int32, (t, LANES), 1) < HEAD_DIM
    for j, pair in enumerate(xq_pairs):
        mk = mk_ref[0, 0, :, j * LANES:(j + 1) * LANES].astype(BF16)
        mv = mv_ref[0, 0, :, j * LANES:(j + 1) * LANES].astype(BF16)
        s = _nt_dot(_half_masked_rows(pair * SCALE, lo), mk)
        m = jnp.max(s, axis=-1, keepdims=True)
        e = jnp.exp(s - m)
        den = jnp.sum(e, axis=-1, keepdims=True)
        o = _dot(e.astype(BF16), mv) / den
        ocat[:, col0 + j * LANES:col0 + (j + 1) * LANES] = jnp.where(lo, o[:t], o[t:]).astype(BF16)


def _mixer_a_body(t, x, sinks_ref, g1_ref, win_ref, qg_ref, kg_ref, xqg_ref, mk_ref, mv_ref, wout_ref,
                  wk_ref, wv_ref, kbuf, vbuf, ocat, tb):
    h = _rms_rows(x, g1_ref[...]).astype(BF16)
    z = _dot(h, win_ref[...])
    q_pairs = _pair_head_norm(z[:, :A_Q], qg_ref[...])
    k_pairs = _pair_head_norm(z[:, A_Q:A_Q + A_KV], kg_ref[...])
    v = z[:, A_Q + A_KV:A_Q + 2 * A_KV]
    xq_pairs = _pair_head_norm(z[:, A_Q + 2 * A_KV:], xqg_ref[...])

    for p, kp in enumerate(k_pairs):
        wk_ref[0, :, p * LANES:(p + 1) * LANES] = kp[tb - BLOCK:, :]
        kbuf[BLOCK:BLOCK + tb, p * LANES:(p + 1) * LANES] = kp.astype(BF16)
    wv_ref[0] = v[tb - BLOCK:, :]
    vbuf[BLOCK:BLOCK + tb, :] = v.astype(BF16)

    qi = lax.broadcasted_iota(jnp.int32, (BLOCK, 2 * BLOCK), 0)
    kj = lax.broadcasted_iota(jnp.int32, (BLOCK, 2 * BLOCK), 1)
    d = qi + BLOCK - kj
    in_window = (d >= 0) & (d < WINDOW)
    dist = d.astype(F32)

    lo = lax.broadcasted_iota(jnp.int32, (BLOCK, LANES), 1) < HEAD_DIM
    pairs_per_slot = A_HEADS // A_KV_HEADS
    for s in range(tb // BLOCK):
        rows = slice(s * BLOCK, (s + 1) * BLOCK)
        first_key = jnp.where(t > 0, 0, BLOCK) if s == 0 else 0
        valid = in_window & (kj >= first_key)
        for j in range(A_KV // LANES):
            kw = kbuf[s * BLOCK:(s + 2) * BLOCK, j * LANES:(j + 1) * LANES]
            vw = vbuf[s * BLOCK:(s + 2) * BLOCK, j * LANES:(j + 1) * LANES]
            q_rows = jnp.concatenate(
                [_half_masked_rows(q_pairs[pairs_per_slot * j + c][rows] * SCALE, lo) for c in range(pairs_per_slot)],
                axis=0)
            sc_all = _nt_dot(q_rows, kw)
            e_parts, den_parts = [], []
            for r in range(2 * pairs_per_slot):
                hq = Q_LANE_ORDER[2 * pairs_per_slot * j + r]
                sc = sc_all[r * BLOCK:(r + 1) * BLOCK] - SLOPES[hq] * dist
                sc = jnp.where(valid, sc, NEG)
                sink = sinks_ref[hq]
                m = jnp.maximum(jnp.max(sc, axis=-1, keepdims=True), sink)
                e = jnp.exp(sc - m)
                den_parts.append(jnp.sum(e, axis=-1, keepdims=True) + jnp.exp(sink - m))
                e_parts.append(e.astype(BF16))
            o_all = _dot(jnp.concatenate(e_parts, axis=0), vw)
            for c in range(pairs_per_slot):
                o_lo = o_all[2 * c * BLOCK:(2 * c + 1) * BLOCK] / den_parts[2 * c]
                o_hi = o_all[(2 * c + 1) * BLOCK:(2 * c + 2) * BLOCK] / den_parts[2 * c + 1]
                col = pairs_per_slot * j + c
                ocat[rows, col * LANES:(col + 1) * LANES] = jnp.where(lo, o_lo, o_hi).astype(BF16)

    _prompt_mem_attention(xq_pairs, mk_ref, mv_ref, ocat, A_Q)
    x1 = x + _dot(ocat[...], wout_ref[...])

    kbuf[0:BLOCK, :] = kbuf[tb:tb + BLOCK, :]
    vbuf[0:BLOCK, :] = vbuf[tb:tb + BLOCK, :]
    return x1


def _ffn_body(x, g2_ref, wfi_ref, cw_ref, cb_ref, wd_ref, tail_ref, tb):
    h = _rms_rows(x, g2_ref[...]).astype(BF16)
    gu = _dot(h, wfi_ref[...])
    a = gu[:, :D_FF]
    row = lax.broadcasted_iota(jnp.int32, (tb, 1), 0)
    prev_last = tail_ref[0, 7:8, :]
    am1 = jnp.where(row == 0, prev_last, pltpu.roll(a, 1, 0))
    am2 = jnp.where(row == 0, tail_ref[0, 6:7, :], jnp.where(row == 1, prev_last, pltpu.roll(a, 2, 0)))
    tail_ref[0] = a[tb - 8:, :]
    return x + _dot(_conv_gate(a, am1, am2, gu[:, D_FF:], cw_ref, cb_ref), wd_ref[...])


def _layer_a_kernel(sinks_ref, x_ref, g1_ref, win_ref, qg_ref, kg_ref, xqg_ref, mk_ref, mv_ref, wout_ref,
                    g2_ref, wfi_ref, cw_ref, cb_ref, wd_ref, y_ref, wk_ref, wv_ref, tail_ref, kbuf, vbuf, ocat, *, tb):
    t = pl.program_id(1)

    @pl.when(t == 0)
    def _():
        kbuf[0:BLOCK, :] = jnp.zeros((BLOCK, A_KV), BF16)
        vbuf[0:BLOCK, :] = jnp.zeros((BLOCK, A_KV), BF16)
        tail_ref[0] = jnp.zeros((8, D_FF), F32)

    x1 = _mixer_a_body(t, x_ref[0], sinks_ref, g1_ref, win_ref, qg_ref, kg_ref, xqg_ref, mk_ref, mv_ref, wout_ref,
                       wk_ref, wv_ref, kbuf, vbuf, ocat, tb)
    y_ref[0] = _ffn_body(x1, g2_ref, wfi_ref, cw_ref, cb_ref, wd_ref, tail_ref, tb)


def _layer_specs(tb, layer):
    xspec = pl.BlockSpec((1, tb, D_MODEL), lambda b, i: (b, i, 0))
    mspec = pl.BlockSpec((1, 1, N_MEM, X_WIDTH), lambda b, i: (layer, b, 0, 0))
    per_seq = lambda rows, width: pl.BlockSpec((1, rows, width), lambda b, i: (b, 0, 0))
    return xspec, mspec, per_seq(8, D_FF), per_seq


def _ffn_specs(layer):
    return [
        _const_spec((1, D_MODEL)),
        _layer_spec(layer, (D_MODEL, 2 * D_FF)),
        _const_spec((CONV_W, D_FF)),
        _const_spec((1, D_FF)),
        _layer_spec(layer, (D_FF, D_MODEL)),
    ]


def _layer_a_prompt(x, layer, norm1_g, w_in_b, q_norm, k_norm, xq_norm, sinks, mk_all, mv_all, w_out_b,
                    norm2_g, w_ffn_in_b, conv_w, conv_b, w_down_b):
    n, t, _ = x.shape
    tb = min(PROMPT_ROWS, t)
    a_in = w_in_b.shape[1]
    xspec, mspec, tailspec, per_seq = _layer_specs(tb, layer)
    wspec = per_seq(BLOCK, A_KV)
    y, wk, wv, tail = pl.pallas_call(
        functools.partial(_layer_a_kernel, tb=tb),
        grid=(n, t // tb),
        in_specs=[
            pl.BlockSpec(memory_space=pltpu.SMEM),
            xspec,
            _const_spec((1, D_MODEL)),
            _const_spec((D_MODEL, a_in)),
            _const_spec((1, LANES)),
            _const_spec((1, LANES)),
            _const_spec((1, LANES)),
            mspec,
            mspec,
            _const_spec((A_Q + X_WIDTH, D_MODEL)),
        ] + _ffn_specs(layer),
        out_specs=[xspec, wspec, wspec, tailspec],
        out_shape=[
            jax.ShapeDtypeStruct((n, t, D_MODEL), F32),
            jax.ShapeDtypeStruct((n, BLOCK, A_KV), F32),
            jax.ShapeDtypeStruct((n, BLOCK, A_KV), F32),
            jax.ShapeDtypeStruct((n, 8, D_FF), F32),
        ],
        scratch_shapes=[
            pltpu.VMEM((tb + BLOCK, A_KV), BF16),
            pltpu.VMEM((tb + BLOCK, A_KV), BF16),
            pltpu.VMEM((tb, A_Q + X_WIDTH), BF16),
        ],
        compiler_params=_params("arbitrary", "arbitrary"),
        name="layer_a_prompt",
    )(sinks, x, norm1_g.reshape(1, D_MODEL), w_in_b, jnp.tile(q_norm, 2).reshape(1, LANES),
      jnp.tile(k_norm, 2).reshape(1, LANES), jnp.tile(xq_norm, 2).reshape(1, LANES), mk_all, mv_all, w_out_b,
      norm2_g.reshape(1, D_MODEL), w_ffn_in_b, conv_w, conv_b.reshape(1, D_FF), w_down_b)
    return y, wk, wv, tail[:, 8 - (CONV_W - 1):, :]


def _mixer_b_body(x, g1_ref, win_ref, vg_ref, ws_ref, bias_ref, xqg_ref, mk_ref, mv_ref, wout_ref, cv_ref, ocat, tb):
    h = _rms_rows(x, g1_ref[...]).astype(BF16)
    z = _dot(h, win_ref[...])
    uv = _gelu_exact(z[:, :2 * B_WIDTH])
    u = uv[:, :B_WIDTH]
    vn = _rms_rows(uv[:, B_WIDTH:], vg_ref[...])
    cv_ref[0] = vn[tb - CHUNK:, :]
    xq_pairs = _pair_head_norm(z[:, 2 * B_WIDTH:], xqg_ref[...])

    wi = lax.broadcasted_iota(jnp.int32, (CHUNK, CHUNK), 0)
    wj = lax.broadcasted_iota(jnp.int32, (CHUNK, CHUNK), 1)
    causal = wi >= wj
    w_tril = [jnp.where(causal, ws_ref[g], 0.0).astype(BF16) for g in range(B_GROUPS)]
    lo = lax.broadcasted_iota(jnp.int32, (CHUNK, LANES), 1) < HEAD_DIM
    bias = bias_ref[...]
    for c in range(tb // CHUNK):
        vc = vn[c * CHUNK:(c + 1) * CHUNK, :].astype(BF16)
        r = [_dot(w_tril[g], vc[:, c0:c0 + 256]) for g, c0 in enumerate(GATE_WINDOW_START)]
        mixed = jnp.concatenate(
            [r[0][:, :128], jnp.where(lo, r[0][:, 128:], r[1][:, :128]), r[1][:, 128:],
             r[2][:, :128], jnp.where(lo, r[2][:, 128:], r[3][:, :128]), r[3][:, 128:]], axis=1)
        o = u[c * CHUNK:(c + 1) * CHUNK, :] * (mixed + bias)
        ocat[c * CHUNK:(c + 1) * CHUNK, 0:B_WIDTH] = o.astype(BF16)

    _prompt_mem_attention(xq_pairs, mk_ref, mv_ref, ocat, B_WIDTH)
    return x + _dot(ocat[...], wout_ref[...])


def _layer_b_kernel(x_ref, g1_ref, win_ref, vg_ref, ws_ref, bias_ref, xqg_ref, mk_ref, mv_ref, wout_ref,
                    g2_ref, wfi_ref, cw_ref, cb_ref, wd_ref, y_ref, cv_ref, tail_ref, ocat, *, tb):
    @pl.when(pl.program_id(1) == 0)
    def _():
        tail_ref[0] = jnp.zeros((8, D_FF), F32)

    x1 = _mixer_b_body(x_ref[0], g1_ref, win_ref, vg_ref, ws_ref, bias_ref, xqg_ref, mk_ref, mv_ref, wout_ref,
                       cv_ref, ocat, tb)
    y_ref[0] = _ffn_body(x1, g2_ref, wfi_ref, cw_ref, cb_ref, wd_ref, tail_ref, tb)


def _gate_bias_rows(b_s):
    return jnp.repeat(b_s.T, B_GROUP_DIM, axis=1)


def _layer_b_prompt(x, layer, norm1_g, w_in_b, v_norm, w_s, b_s, xq_norm, mk_all, mv_all, w_out_b,
                    norm2_g, w_ffn_in_b, conv_w, conv_b, w_down_b):
    n, t, _ = x.shape
    tb = min(PROMPT_ROWS, t)
    b_in = w_in_b.shape[1]
    xspec, mspec, tailspec, per_seq = _layer_specs(tb, layer)
    y, cv, tail = pl.pallas_call(
        functools.partial(_layer_b_kernel, tb=tb),
        grid=(n, t // tb),
        in_specs=[
            xspec,
            _const_spec((1, D_MODEL)),
            _const_spec((D_MODEL, b_in)),
            _const_spec((1, B_WIDTH)),
            _const_spec((B_GROUPS, CHUNK, CHUNK)),
            _const_spec((CHUNK, B_WIDTH)),
            _const_spec((1, LANES)),
            mspec,
            mspec,
            _const_spec((B_WIDTH + X_WIDTH, D_MODEL)),
        ] + _ffn_specs(layer),
        out_specs=[xspec, per_seq(CHUNK, B_WIDTH), tailspec],
        out_shape=[
            jax.ShapeDtypeStruct((n, t, D_MODEL), F32),
            jax.ShapeDtypeStruct((n, CHUNK, B_WIDTH), F32),
            jax.ShapeDtypeStruct((n, 8, D_FF), F32),
        ],
        scratch_shapes=[pltpu.VMEM((tb, B_WIDTH + X_WIDTH), BF16)],
        compiler_params=_params("arbitrary", "arbitrary"),
        name="layer_b_prompt",
    )(x, norm1_g.reshape(1, D_MODEL), w_in_b, v_norm.reshape(1, B_WIDTH), w_s, _gate_bias_rows(b_s),
      jnp.tile(xq_norm, 2).reshape(1, LANES), mk_all, mv_all, w_out_b,
      norm2_g.reshape(1, D_MODEL), w_ffn_in_b, conv_w, conv_b.reshape(1, D_FF), w_down_b)
    return y, cv, tail[:, 8 - (CONV_W - 1):, :]


def _stack_positions(ref, seq):
    return jnp.concatenate([ref[t] for t in range(seq)], axis=0)


def _ffn_sample_kernel(x_ref, oc_ref, wout_ref, g2_ref, wfi_ref, cw_ref, cb_ref, wd_ref, p0_ref, p1_ref,
                       y_ref, atail_ref, *, seq, nb):
    x1 = _stack_positions(x_ref, seq) + _dot(_stack_positions(oc_ref, seq), wout_ref[...])
    h = _rms_rows(x1, g2_ref[...]).astype(BF16)
    gu = _dot(h, wfi_ref[...])
    a = gu[:, :D_FF]
    blocks = [p0_ref[...], p1_ref[...]] + [a[t * nb:(t + 1) * nb] for t in range(seq)]
    am2 = jnp.concatenate(blocks[:seq], axis=0)
    am1 = jnp.concatenate(blocks[1:seq + 1], axis=0)
    y = x1 + _dot(_conv_gate(a, am1, am2, gu[:, D_FF:], cw_ref, cb_ref), wd_ref[...])
    for t in range(seq):
        y_ref[t] = y[t * nb:(t + 1) * nb]
    for r in range(CONV_W - 1):
        atail_ref[r] = blocks[seq + r]


def _ffn_sample(x3, oc3, w_out_e, layer, norm2_g, w_ffn_in_b, conv_w, conv_b, w_down_b, prefix):
    s, n, _ = x3.shape
    kc = oc3.shape[2]
    nb = min(SAMPLE_FFN_SEQS, n)
    assert n % nb == 0 and s >= CONV_W - 1

    def blk(lead, width):
        return pl.BlockSpec((lead, nb, width), lambda i: (0, i, 0))

    pspec = pl.BlockSpec((nb, D_FF), lambda i: (i, 0))
    return pl.pallas_call(
        functools.partial(_ffn_sample_kernel, seq=s, nb=nb),
        grid=(n // nb,),
        in_specs=[
            blk(s, D_MODEL),
            blk(s, kc),
            _const_spec((kc, D_MODEL)),
            _const_spec((1, D_MODEL)),
            _layer_spec(layer, (D_MODEL, 2 * D_FF)),
            _const_spec((CONV_W, D_FF)),
            _const_spec((1, D_FF)),
            _layer_spec(layer, (D_FF, D_MODEL)),
            pspec,
            pspec,
        ],
        out_specs=[blk(s, D_MODEL), blk(CONV_W - 1, D_FF)],
        out_shape=[
            jax.ShapeDtypeStruct((s, n, D_MODEL), F32),
            jax.ShapeDtypeStruct((CONV_W - 1, n, D_FF), F32),
        ],
        compiler_params=_params("arbitrary"),
        name="ffn_sample",
    )(x3, oc3, w_out_e, norm2_g.reshape(1, D_MODEL), w_ffn_in_b, conv_w, conv_b.reshape(1, D_FF), w_down_b,
      prefix[:, 0, :], prefix[:, 1, :])


def _write_slots(z, c0, heads, gain, out_ref):
    for hh in range(heads):
        out_ref[hh] = _slot_head_norm(z[:, c0 + hh * X_WIDTH:c0 + (hh + 1) * X_WIDTH], gain)


def _sample_in_a_kernel(x_ref, g1_ref, win_ref, wkvt_ref, qg_ref, kgc_ref, xqg_ref,
                        qs_ref, v_ref, xs_ref, kt_ref, vt_ref, *, seq):
    h = _rms_rows(_stack_positions(x_ref, seq), g1_ref[...]).astype(BF16)
    z = _dot(h, win_ref[...])
    _write_slots(z, 0, A_HEADS, qg_ref[...], qs_ref)
    c0 = A_HEADS * A_KV
    v_ref[...] = z[:, c0:c0 + A_KV]
    _write_slots(z, c0 + A_KV, X_HEADS, xqg_ref[...], xs_ref)
    kvt = _nt_dot(wkvt_ref[...], h)
    for hk in range(A_KV_HEADS):
        sl = slice(hk * HEAD_DIM, (hk + 1) * HEAD_DIM)
        kh = kvt[sl, :]
        ms = jnp.mean(kh * kh, axis=0, keepdims=True)
        kt_ref[sl, :] = kh * lax.rsqrt(ms + EPS) * kgc_ref[sl, :]
    vt_ref[...] = kvt[A_KV:, :]


def _expand_cols(w, heads, slot_of):
    d = w.shape[0]
    wh = w.reshape(d, heads, 1, HEAD_DIM)
    sel = np.zeros((heads, X_HEADS, 1), np.float32)
    for h in range(heads):
        sel[h, slot_of(h), 0] = 1.0
    return (wh * jnp.asarray(sel, w.dtype)).reshape(d, heads * X_WIDTH)


def _expand_rows(w, heads, slot_of):
    return _expand_cols(w.T, heads, slot_of).T


def _sample_in_a(x3, norm1_g, w_in_b, q_norm, k_norm, xq_norm):
    s, n, _ = x3.shape
    rows = s * n
    wq = _expand_cols(w_in_b[:, :A_Q], A_HEADS, lambda h: h // A_GROUP)
    wx = _expand_cols(w_in_b[:, A_Q + 2 * A_KV:], X_HEADS, lambda h: h)
    w_e = jnp.concatenate([wq, w_in_b[:, A_Q + A_KV:A_Q + 2 * A_KV], wx], axis=1)
    w_kvt = w_in_b[:, A_Q:A_Q + 2 * A_KV].T
    slots = lambda heads: jax.ShapeDtypeStruct((heads, rows, X_WIDTH), F32)
    full = lambda shape: pl.BlockSpec(shape, lambda i: (0,) * len(shape))
    return pl.pallas_call(
        functools.partial(_sample_in_a_kernel, seq=s),
        grid=(1,),
        in_specs=[
            _const_spec((s, n, D_MODEL)),
            _const_spec((1, D_MODEL)),
            _const_spec(w_e.shape),
            _const_spec(w_kvt.shape),
            _const_spec((1, A_KV)),
            _const_spec((A_KV, 1)),
            _const_spec((1, X_WIDTH)),
        ],
        out_specs=[
            full((A_HEADS, rows, A_KV)), full((rows, A_KV)), full((X_HEADS, rows, X_WIDTH)),
            full((A_KV, rows)), full((A_KV, rows)),
        ],
        out_shape=[
            slots(A_HEADS),
            jax.ShapeDtypeStruct((rows, A_KV), F32),
            slots(X_HEADS),
            jax.ShapeDtypeStruct((A_KV, rows), F32),
            jax.ShapeDtypeStruct((A_KV, rows), F32),
        ],
        compiler_params=_params("arbitrary"),
        name="sample_in_a",
    )(x3, norm1_g.reshape(1, D_MODEL), w_e, w_kvt, jnp.tile(q_norm, 4).reshape(1, A_KV),
      jnp.tile(k_norm, 4).reshape(A_KV, 1), jnp.tile(xq_norm, 4).reshape(1, X_WIDTH))


def _sample_in_b_kernel(x_ref, g1_ref, win_ref, vg_ref, coef_ref, gbias_ref, xqg_ref, o_ref, vn_ref, xs_ref,
                        *, seq, n):
    h = _rms_rows(_stack_positions(x_ref, seq), g1_ref[...]).astype(BF16)
    z = _dot(h, win_ref[...])
    uv = _gelu_exact(z[:, :2 * B_WIDTH])
    u = uv[:, :B_WIDTH]
    vn = _rms_rows(uv[:, B_WIDTH:], vg_ref[...])
    vn_ref[...] = vn
    for t in range(seq):
        mixed = gbias_ref[t:t + 1, :] + coef_ref[t * seq:t * seq + 1, :] * vn[0:n]
        for j in range(1, t + 1):
            mixed = mixed + coef_ref[t * seq + j:t * seq + j + 1, :] * vn[j * n:(j + 1) * n]
        o_ref[t * n:(t + 1) * n, :] = (u[t * n:(t + 1) * n] * mixed).astype(o_ref.dtype)
    _write_slots(z, 2 * B_WIDTH, X_HEADS, xqg_ref[...], xs_ref)


def _sample_in_b(x3, norm1_g, w_in_b, v_norm, w_s, b_s, xq_norm):
    s, n, _ = x3.shape
    rows = s * n
    wx = _expand_cols(w_in_b[:, 2 * B_WIDTH:], X_HEADS, lambda h: h)
    w_e = jnp.concatenate([w_in_b[:, :2 * B_WIDTH], wx], axis=1)
    coef = jnp.repeat(w_s[:, :s, :s].transpose(1, 2, 0).reshape(s * s, B_GROUPS), B_GROUP_DIM, axis=1)
    gbias = jnp.repeat(b_s[:, :s].T, B_GROUP_DIM, axis=1)
    full = lambda shape: pl.BlockSpec(shape, lambda i: (0,) * len(shape))
    return pl.pallas_call(
        functools.partial(_sample_in_b_kernel, seq=s, n=n),
        grid=(1,),
        in_specs=[
            _const_spec((s, n, D_MODEL)),
            _const_spec((1, D_MODEL)),
            _const_spec(w_e.shape),
            _const_spec((1, B_WIDTH)),
            _const_spec((s * s, B_WIDTH)),
            _const_spec((s, B_WIDTH)),
            _const_spec((1, X_WIDTH)),
        ],
        out_specs=[full((rows, B_WIDTH)), full((rows, B_WIDTH)), full((X_HEADS, rows, X_WIDTH))],
        out_shape=[
            jax.ShapeDtypeStruct((rows, B_WIDTH), BF16),
            jax.ShapeDtypeStruct((rows, B_WIDTH), F32),
            jax.ShapeDtypeStruct((X_HEADS, rows, X_WIDTH), F32),
        ],
        compiler_params=_params("arbitrary"),
        name="sample_in_b",
    )(x3, norm1_g.reshape(1, D_MODEL), w_e, v_norm.reshape(1, B_WIDTH), coef, gbias,
      jnp.tile(xq_norm, 4).reshape(1, X_WIDTH))


def _sample_attn_win_kernel(qs_ref, ktn_ref, vtn_ref, vn_ref, slope_ref, sink_ref, ck_ref, cv_ref,
                            o_ref, wk_ref, wv_ref, *, sb, seq):
    rows = A_HEADS * seq
    tq = _div_mod(lax.broadcasted_iota(jnp.int32, (rows, WINDOW), 0), seq)[1]
    lane = lax.broadcasted_iota(jnp.int32, (rows, WINDOW), 1)
    slope = slope_ref[...]
    sink = sink_ref[...]
    d1 = tq + WINDOW - lane
    valid1 = d1 < WINDOW
    bias1 = slope * d1.astype(F32)
    new_seq, new_pos = _div_mod(lane, seq)
    d2 = tq - new_pos
    causal2 = d2 >= 0
    bias2 = slope * d2.astype(F32)
    ktn = ktn_ref[...]
    vtn = vtn_ref[...]
    ktn_b = ktn.astype(BF16)
    vn_b = vn_ref[...].astype(BF16)
    keep_old = lax.broadcasted_iota(jnp.int32, (A_KV, WINDOW), 1) < WINDOW - seq

    def body(n, carry):
        q = (qs_ref[n] * SCALE).astype(BF16)
        ckt = ck_ref[n]
        cvt = cv_ref[n]
        s1 = jnp.where(valid1, _dot(q, ckt.astype(BF16)) - bias1, NEG)
        s2 = jnp.where(causal2 & (new_seq == n), _dot(q, ktn_b) - bias2, NEG)
        m = jnp.maximum(jnp.maximum(jnp.max(s1, axis=-1, keepdims=True), jnp.max(s2, axis=-1, keepdims=True)), sink)
        e1 = jnp.exp(s1 - m)
        e2 = jnp.exp(s2 - m)
        den = jnp.sum(e1, axis=-1, keepdims=True) + jnp.sum(e2, axis=-1, keepdims=True) + jnp.exp(sink - m)
        o = _nt_dot(e1.astype(BF16), cvt.astype(BF16)) + _dot(e2.astype(BF16), vn_b)
        o_ref[n] = o / den
        shift_new = WINDOW - seq - n * seq
        wk_ref[n] = jnp.where(keep_old, pltpu.roll(ckt, WINDOW - seq, 1), pltpu.roll(ktn, shift_new, 1))
        wv_ref[n] = jnp.where(keep_old, pltpu.roll(cvt, WINDOW - seq, 1), pltpu.roll(vtn, shift_new, 1))
        return carry

    lax.fori_loop(0, sb, body, 0, unroll=SAMPLE_SEQ_UNROLL)


def _sample_attn_win(qs, kt_new, vt_new, v_new, slope_rows, sink_rows, ckt, cvt, layer, seq):
    n = qs.shape[0]
    rows = A_HEADS * seq
    assert LANES % seq == 0
    sb = LANES // seq
    assert n % sb == 0

    def seqs(*tail):
        return pl.BlockSpec((sb,) + tail, lambda i: (i,) + (0,) * len(tail))

    cache = pl.BlockSpec((None, sb, A_KV, WINDOW), lambda i: (layer, i, 0, 0))
    win = jax.ShapeDtypeStruct((n, A_KV, WINDOW), F32)
    return pl.pallas_call(
        functools.partial(_sample_attn_win_kernel, sb=sb, seq=seq),
        grid=(n // sb,),
        in_specs=[
            seqs(rows, A_KV),
            pl.BlockSpec((A_KV, LANES), lambda i: (0, i)),
            pl.BlockSpec((A_KV, LANES), lambda i: (0, i)),
            pl.BlockSpec((LANES, A_KV), lambda i: (i, 0)),
            _const_spec((rows, 1)),
            _const_spec((rows, 1)),
            cache,
            cache,
        ],
        out_specs=[seqs(rows, A_KV), seqs(A_KV, WINDOW), seqs(A_KV, WINDOW)],
        out_shape=[jax.ShapeDtypeStruct((n, rows, A_KV), F32), win, win],
        compiler_params=_params("arbitrary"),
        name="sample_attn_win",
    )(qs, kt_new, vt_new, v_new, slope_rows, sink_rows, ckt, cvt)


def _sample_attn_mem_kernel(xs_ref, mk_ref, mv_ref, xo_ref, *, sb):
    def body(n, carry):
        q = (xs_ref[n] * SCALE).astype(BF16)
        s = _dot(q, mk_ref[n].astype(BF16))
        m = jnp.max(s, axis=-1, keepdims=True)
        e = jnp.exp(s - m)
        den = jnp.sum(e, axis=-1, keepdims=True)
        xo_ref[n] = _nt_dot(e.astype(BF16), mv_ref[n].astype(BF16)) / den
        return carry

    lax.fori_loop(0, sb, body, 0, unroll=SAMPLE_SEQ_UNROLL)


def _sample_attn_mem(xs, mkt, mvt, layer):
    n, xrows, _ = xs.shape
    sb = min(SAMPLE_MEM_SEQS, n)
    assert n % sb == 0
    qspec = pl.BlockSpec((sb, xrows, X_WIDTH), lambda i: (i, 0, 0))
    cache = pl.BlockSpec((None, sb, X_WIDTH, N_MEM), lambda i: (layer, i, 0, 0))
    return pl.pallas_call(
        functools.partial(_sample_attn_mem_kernel, sb=sb),
        grid=(n // sb,),
        in_specs=[qspec, cache, cache],
        out_specs=qspec,
        out_shape=jax.ShapeDtypeStruct((n, xrows, X_WIDTH), F32),
        compiler_params=_params("arbitrary"),
        name="sample_attn_mem",
    )(xs, mkt, mvt)


def _slots_to_seq(a, seq, n):
    hh, _, w = a.shape
    return a.reshape(hh, seq, n, w).transpose(2, 0, 1, 3).reshape(n, hh * seq, w)


def _seq_to_cols(a, heads, seq):
    n, _, w = a.shape
    return a.reshape(n, heads, seq, w).transpose(2, 0, 1, 3).reshape(seq, n, heads * w)


def _feature_major(cache):
    l, n, p, kv, d = cache.shape
    return cache.transpose(0, 1, 3, 4, 2).reshape(l, n, kv * d, p)


def kernel(x_prompt, x_sample, cache_win_k, cache_win_v, cache_mem_k, cache_mem_v, state_conv, mem_prompt,
           norm1_g, norm2_g, mem_norm_g, w_in_a, q_norm_a, k_norm_a, sinks_a, w_out_a, w_in_b, v_norm_b, w_s_b,
           b_s_b, w_out_b, w_mem_kv, xq_norm, xk_norm, w_ffn_in, conv_w, conv_b, w_down):
    depth = norm1_g.shape[0]
    assert depth == 2 and w_in_a.shape[0] == 1 and w_in_b.shape[0] == 1
    nb = x_prompt.shape[0]
    ns, seq, _ = x_sample.shape
    assert cache_win_k.shape[2] == WINDOW

    w_in_a_b = w_in_a[0].astype(BF16)
    w_in_b_b = w_in_b[0].astype(BF16)
    w_out_a_b = w_out_a[0].astype(BF16)
    w_out_b_b = w_out_b[0].astype(BF16)
    w_mem_kv_b = w_mem_kv.astype(BF16)
    w_ffn_in_b = w_ffn_in.astype(BF16)
    w_down_b = w_down.astype(BF16)

    mk_all, mv_all = _memory_kv(mem_prompt, mem_norm_g, w_mem_kv_b, xk_norm)
    x2, win_k_p, win_v_p, conv_p0 = _layer_a_prompt(
        x_prompt, 0, norm1_g[0], _reorder_heads(w_in_a_b, Q_LANE_ORDER, 1), q_norm_a[0], k_norm_a[0], xq_norm[0],
        sinks_a[0], mk_all, mv_all, _reorder_heads(w_out_a_b, Q_LANE_ORDER, 0),
        norm2_g[0], w_ffn_in_b, conv_w[0], conv_b[0], w_down_b)
    y_prompt, chunk_v_p, conv_p1 = _layer_b_prompt(
        x2, 1, norm1_g[1], w_in_b_b, v_norm_b[0], w_s_b[0], b_s_b[0], xq_norm[1], mk_all, mv_all, w_out_b_b,
        norm2_g[1], w_ffn_in_b, conv_w[1], conv_b[1], w_down_b)

    xs3 = x_sample.transpose(1, 0, 2)
    mkt = _feature_major(cache_mem_k)
    mvt = _feature_major(cache_mem_v)

    def by_seq_cols(a):
        return a.reshape(a.shape[0], seq, ns).transpose(0, 2, 1).reshape(a.shape[0], ns * seq)

    qs, v_new, xslots, kt_new, vt_new = _sample_in_a(xs3, norm1_g[0], w_in_a_b, q_norm_a[0], k_norm_a[0], xq_norm[0])
    slope_rows = jnp.repeat(jnp.asarray(SLOPES, F32), seq).reshape(A_HEADS * seq, 1)
    sink_rows = jnp.repeat(sinks_a[0].astype(F32), seq).reshape(A_HEADS * seq, 1)
    o_s, win_kt_s, win_vt_s = _sample_attn_win(
        _slots_to_seq(qs, seq, ns), by_seq_cols(kt_new), by_seq_cols(vt_new),
        v_new.reshape(seq, ns, A_KV).transpose(1, 0, 2).reshape(ns * seq, A_KV), slope_rows, sink_rows,
        _feature_major(cache_win_k), _feature_major(cache_win_v), 0, seq)
    xo_s = _sample_attn_mem(_slots_to_seq(xslots, seq, ns), mkt, mvt, 0)
    oc3 = jnp.concatenate([_seq_to_cols(o_s, A_HEADS, seq), _seq_to_cols(xo_s, X_HEADS, seq)], axis=2).astype(BF16)
    w_out_a_e = jnp.concatenate([_expand_rows(w_out_a_b[:A_Q], A_HEADS, lambda h: h // A_GROUP),
                                 _expand_rows(w_out_a_b[A_Q:], X_HEADS, lambda h: h)], axis=0)
    ys1, conv_s0 = _ffn_sample(xs3, oc3, w_out_a_e, 0, norm2_g[0], w_ffn_in_b, conv_w[0], conv_b[0], w_down_b,
                               state_conv[0])

    o_b, vn_s, xslots = _sample_in_b(ys1, norm1_g[1], w_in_b_b, v_norm_b[0], w_s_b[0], b_s_b[0], xq_norm[1])
    xo_s = _sample_attn_mem(_slots_to_seq(xslots, seq, ns), mkt, mvt, 1)
    oc3 = jnp.concatenate([o_b.reshape(seq, ns, B_WIDTH), _seq_to_cols(xo_s, X_HEADS, seq).astype(BF16)], axis=2)
    w_out_b_e = jnp.concatenate([w_out_b_b[:B_WIDTH], _expand_rows(w_out_b_b[B_WIDTH:], X_HEADS, lambda h: h)], axis=0)
    ys2, conv_s1 = _ffn_sample(ys1, oc3, w_out_b_e, 1, norm2_g[1], w_ffn_in_b, conv_w[1], conv_b[1], w_down_b,
                               state_conv[1])

    kv5 = (A_KV_HEADS, HEAD_DIM)

    def window_out(wt):
        return wt.reshape(1, ns, A_KV_HEADS, HEAD_DIM, WINDOW).transpose(0, 1, 4, 2, 3)

    return (
        y_prompt,
        ys2.transpose(1, 0, 2),
        win_k_p.reshape(1, nb, WINDOW, *kv5),
        win_v_p.reshape(1, nb, WINDOW, *kv5),
        chunk_v_p.reshape(1, nb, CHUNK, B_WIDTH),
        mk_all.reshape(depth, nb, N_MEM, X_HEADS, HEAD_DIM),
        mv_all.reshape(depth, nb, N_MEM, X_HEADS, HEAD_DIM),
        jnp.stack([conv_p0, conv_p1]),
        window_out(win_kt_s),
        window_out(win_vt_s),
        vn_s.reshape(seq, ns, B_WIDTH).transpose(1, 0, 2)[None],
        jnp.stack([conv_s0.transpose(1, 0, 2), conv_s1.transpose(1, 0, 2)]),
    )
```

```python
import functools
import math

import numpy as np
import jax
import jax.numpy as jnp
from jax import lax
from jax.experimental import pallas as pl
from jax.experimental.pallas import tpu as pltpu

F32 = jnp.float32
BF16 = jnp.bfloat16

D_MODEL = 1024
HEAD_DIM = 64
A_HEADS = 12
A_KV_HEADS = 4
A_GROUP = A_HEADS // A_KV_HEADS
WINDOW = 128
BLOCK = 128
B_WIDTH = 768
B_GROUPS = 4
B_GROUP_DIM = B_WIDTH // B_GROUPS
CHUNK = 128
X_HEADS = 4
X_WIDTH = X_HEADS * HEAD_DIM
N_MEM = 256
D_FF = 2816
CONV_W = 3
A_Q = A_HEADS * HEAD_DIM
A_KV = A_KV_HEADS * HEAD_DIM
EPS = 1e-6
NEG = -1e30
SCALE = HEAD_DIM ** -0.5
SQRT_HALF = float(np.sqrt(0.5).astype(np.float32))

LANES = 128
VMEM_LIMIT_BYTES = 56 * 1024 * 1024
GATE_WINDOW_START = (0, 128, 384, 512)
assert B_GROUP_DIM == 192 and B_GROUPS == 4
PROMPT_ROWS = 512
SAMPLE_MEM_SEQS = 16
SAMPLE_FFN_SEQS = 64
SAMPLE_SEQ_UNROLL = 4


def _alibi_slopes(n):
    def pow2(m):
        start = 2.0 ** (-8.0 / m)
        return [start ** (i + 1) for i in range(m)]

    p = 2 ** int(math.floor(math.log2(n)))
    s = pow2(p)
    if p < n:
        s = s + pow2(2 * p)[0::2][: n - p]
    return [float(np.float32(v)) for v in s]


SLOPES = _alibi_slopes(A_HEADS)

Q_LANE_ORDER = (0, 3, 1, 4, 2, 5, 6, 9, 7, 10, 8, 11)
assert all(Q_LANE_ORDER[p] // A_GROUP == 2 * (p // (2 * A_GROUP)) + p % 2 for p in range(A_HEADS))


def _reorder_heads(w, order, axis):
    n = len(order) * HEAD_DIM
    parts = [lax.slice_in_dim(w, h * HEAD_DIM, (h + 1) * HEAD_DIM, axis=axis) for h in order]
    return jnp.concatenate(parts + [lax.slice_in_dim(w, n, w.shape[axis], axis=axis)], axis=axis)


def _params(*sem):
    return pltpu.CompilerParams(dimension_semantics=sem, vmem_limit_bytes=VMEM_LIMIT_BYTES)


def _const_spec(shape):
    nd = len(shape)
    return pl.BlockSpec(shape, lambda *_: (0,) * nd, pipeline_mode=pl.Buffered(1))


def _layer_spec(layer, shape):
    nd = len(shape)
    return pl.BlockSpec((None,) + tuple(shape), lambda *_: (layer,) + (0,) * nd, pipeline_mode=pl.Buffered(1))


def _rms_rows(x, g):
    ms = jnp.mean(x * x, axis=-1, keepdims=True)
    return x * lax.rsqrt(ms + EPS) * g


def _pair_head_norm(z, g2):
    t, w = z.shape
    lo = lax.broadcasted_iota(jnp.int32, (t, LANES), 1) < HEAD_DIM
    out = []
    for p in range(w // LANES):
        zz = z[:, p * LANES:(p + 1) * LANES]
        sq = zz * zz
        s_lo = jnp.sum(jnp.where(lo, sq, 0.0), axis=-1, keepdims=True)
        s_hi = jnp.sum(jnp.where(lo, 0.0, sq), axis=-1, keepdims=True)
        r_lo = lax.rsqrt(s_lo * (1.0 / HEAD_DIM) + EPS)
        r_hi = lax.rsqrt(s_hi * (1.0 / HEAD_DIM) + EPS)
        out.append(zz * jnp.where(lo, r_lo, r_hi) * g2)
    return out


def _slot_head_norm(z, g4):
    ms = jnp.sum(z * z, axis=-1, keepdims=True) * (1.0 / HEAD_DIM)
    return z * lax.rsqrt(ms + EPS) * g4


def _nt_dot(a, b):
    return lax.dot_general(a, b, (((1,), (1,)), ((), ())), preferred_element_type=F32)


def _dot(a, b):
    return jnp.dot(a, b, preferred_element_type=F32)


def _div_mod(i, n):
    if n & (n - 1) == 0:
        return i >> (n.bit_length() - 1), i & (n - 1)
    return i // n, i % n


def _gelu_exact(x):
    return 0.5 * x * (1.0 + lax.erf(x * SQRT_HALF))


def _conv_gate(a, am1, am2, up, cw_ref, cb_ref):
    c = cb_ref[...] + ((cw_ref[0:1, :] * am2 + cw_ref[1:2, :] * am1) + cw_ref[2:3, :] * a)
    return ((c * (1.0 / (1.0 + jnp.exp(-c)))) * up).astype(BF16)


def _memkv_kernel(mem_ref, g_ref, w_ref, kg_ref, mkb_ref, mvb_ref, mkt_ref, mvt_ref):
    m = mem_ref[0]
    h = _rms_rows(m, g_ref[0]).astype(BF16)
    kv = _dot(h, w_ref[0])
    mk = jnp.concatenate(_pair_head_norm(kv[:, :X_WIDTH], kg_ref[0]), axis=1)
    mv = kv[:, X_WIDTH:]
    mkb_ref[0, 0] = mk.astype(BF16)
    mvb_ref[0, 0] = mv.astype(BF16)
    mkt_ref[0, 0] = mk.T
    mvt_ref[0, 0] = mv.T


def _memory_kv(mem, mem_norm_g, w_mem_kv_b, xk_norm):
    depth = w_mem_kv_b.shape[0]
    n = mem.shape[0]
    g = mem_norm_g.reshape(depth, 1, D_MODEL)
    kg2 = jnp.tile(xk_norm, (1, 2)).reshape(depth, 1, LANES)
    spec = lambda rows, cols: pl.BlockSpec((1, 1, rows, cols), lambda l, b: (l, b, 0, 0))
    return pl.pallas_call(
        _memkv_kernel,
        grid=(depth, n),
        in_specs=[
            pl.BlockSpec((1, N_MEM, D_MODEL), lambda l, b: (b, 0, 0)),
            pl.BlockSpec((1, 1, D_MODEL), lambda l, b: (l, 0, 0)),
            pl.BlockSpec((1, D_MODEL, 2 * X_WIDTH), lambda l, b: (l, 0, 0)),
            pl.BlockSpec((1, 1, LANES), lambda l, b: (l, 0, 0)),
        ],
        out_specs=[spec(N_MEM, X_WIDTH), spec(N_MEM, X_WIDTH), spec(X_WIDTH, N_MEM), spec(X_WIDTH, N_MEM)],
        out_shape=[
            jax.ShapeDtypeStruct((depth, n, N_MEM, X_WIDTH), BF16),
            jax.ShapeDtypeStruct((depth, n, N_MEM, X_WIDTH), BF16),
            jax.ShapeDtypeStruct((depth, n, X_WIDTH, N_MEM), F32),
            jax.ShapeDtypeStruct((depth, n, X_WIDTH, N_MEM), F32),
        ],
        compiler_params=_params("arbitrary", "arbitrary"),
        name="memory_kv",
    )(mem, g, w_mem_kv_b, kg2)


def _half_masked_rows(pair, lo):
    return jnp.concatenate([jnp.where(lo, pair, 0.0), jnp.where(lo, 0.0, pair)], axis=0).astype(BF16)


def _prompt_mem_attention(xq_pairs, mk_ref, mv_ref, ocat, col0):
    t = xq_pairs[0].shape[0]
    lo = lax.broadcasted_iota(jnp.int32, (t, LANES), 1) < HEAD_DIM
    for j, pair in enumerate(xq_pairs):
        mk = mk_ref[0, 0, :, j * LANES:(j + 1) * LANES]
        mv = mv_ref[0, 0, :, j * LANES:(j + 1) * LANES]
        s = _nt_dot(_half_masked_rows(pair * SCALE, lo), mk)
        m = jnp.max(s, axis=-1, keepdims=True)
        e = jnp.exp(s - m)
        den = jnp.sum(e, axis=-1, keepdims=True)
        o = _dot(e.astype(BF16), mv) / den
        ocat[:, col0 + j * LANES:col0 + (j + 1) * LANES] = jnp.where(lo, o[:t], o[t:]).astype(BF16)


def _mixer_a_body(t, x, sinks_ref, g1_ref, win_ref, qg_ref, kg_ref, xqg_ref, mk_ref, mv_ref, wout_ref,
                  wk_ref, wv_ref, kbuf, vbuf, ocat, tb):
    h = _rms_rows(x, g1_ref[...]).astype(BF16)
    z = _dot(h, win_ref[...])
    q_pairs = _pair_head_norm(z[:, :A_Q], qg_ref[...])
    k_pairs = _pair_head_norm(z[:, A_Q:A_Q + A_KV], kg_ref[...])
    v = z[:, A_Q + A_KV:A_Q + 2 * A_KV]
    xq_pairs = _pair_head_norm(z[:, A_Q + 2 * A_KV:], xqg_ref[...])

    for p, kp in enumerate(k_pairs):
        wk_ref[0, p * LANES:(p + 1) * LANES, :] = kp[tb - BLOCK:, :].T
        kbuf[BLOCK:BLOCK + tb, p * LANES:(p + 1) * LANES] = kp.astype(BF16)
    wv_ref[0] = v[tb - BLOCK:, :].T
    vbuf[BLOCK:BLOCK + tb, :] = v.astype(BF16)

    qi = lax.broadcasted_iota(jnp.int32, (BLOCK, 2 * BLOCK), 0)
    kj = lax.broadcasted_iota(jnp.int32, (BLOCK, 2 * BLOCK), 1)
    d = qi + BLOCK - kj
    in_window = (d >= 0) & (d < WINDOW)
    dist = d.astype(F32)

    lo = lax.broadcasted_iota(jnp.int32, (BLOCK, LANES), 1) < HEAD_DIM
    pairs_per_slot = A_HEADS // A_KV_HEADS
    for s in range(tb // BLOCK):
        rows = slice(s * BLOCK, (s + 1) * BLOCK)
        first_key = jnp.where(t > 0, 0, BLOCK) if s == 0 else 0
        valid = in_window & (kj >= first_key)
        for j in range(A_KV // LANES):
            kw = kbuf[s * BLOCK:(s + 2) * BLOCK, j * LANES:(j + 1) * LANES]
            vw = vbuf[s * BLOCK:(s + 2) * BLOCK, j * LANES:(j + 1) * LANES]
            q_rows = jnp.concatenate(
                [_half_masked_rows(q_pairs[pairs_per_slot * j + c][rows] * SCALE, lo) for c in range(pairs_per_slot)],
                axis=0)
            sc_all = _nt_dot(q_rows, kw)
            e_parts, den_parts = [], []
            for r in range(2 * pairs_per_slot):
                hq = Q_LANE_ORDER[2 * pairs_per_slot * j + r]
                sc = sc_all[r * BLOCK:(r + 1) * BLOCK] - SLOPES[hq] * dist
                sc = jnp.where(valid, sc, NEG)
                sink = sinks_ref[hq]
                m = jnp.maximum(jnp.max(sc, axis=-1, keepdims=True), sink)
                e = jnp.exp(sc - m)
                den_parts.append(jnp.sum(e, axis=-1, keepdims=True) + jnp.exp(sink - m))
                e_parts.append(e.astype(BF16))
            o_all = _dot(jnp.concatenate(e_parts, axis=0), vw)
            for c in range(pairs_per_slot):
                o_lo = o_all[2 * c * BLOCK:(2 * c + 1) * BLOCK] / den_parts[2 * c]
                o_hi = o_all[(2 * c + 1) * BLOCK:(2 * c + 2) * BLOCK] / den_parts[2 * c + 1]
                col = pairs_per_slot * j + c
                ocat[rows, col * LANES:(col + 1) * LANES] = jnp.where(lo, o_lo, o_hi).astype(BF16)

    _prompt_mem_attention(xq_pairs, mk_ref, mv_ref, ocat, A_Q)
    x1 = x + _dot(ocat[...], wout_ref[...])

    kbuf[0:BLOCK, :] = kbuf[tb:tb + BLOCK, :]
    vbuf[0:BLOCK, :] = vbuf[tb:tb + BLOCK, :]
    return x1


def _ffn_body(x, g2_ref, wfi_ref, cw_ref, cb_ref, wd_ref, tail_ref, tb):
    h = _rms_rows(x, g2_ref[...]).astype(BF16)
    gu = _dot(h, wfi_ref[...])
    a = gu[:, :D_FF]
    row = lax.broadcasted_iota(jnp.int32, (tb, 1), 0)
    prev_last = tail_ref[0, 7:8, :]
    am1 = jnp.where(row == 0, prev_last, pltpu.roll(a, 1, 0))
    am2 = jnp.where(row == 0, tail_ref[0, 6:7, :], jnp.where(row == 1, prev_last, pltpu.roll(a, 2, 0)))
    tail_ref[0] = a[tb - 8:, :]
    return x + _dot(_conv_gate(a, am1, am2, gu[:, D_FF:], cw_ref, cb_ref), wd_ref[...])


def _layer_a_kernel(sinks_ref, x_ref, g1_ref, win_ref, qg_ref, kg_ref, xqg_ref, mk_ref, mv_ref, wout_ref,
                    g2_ref, wfi_ref, cw_ref, cb_ref, wd_ref, y_ref, wk_ref, wv_ref, tail_ref, kbuf, vbuf, ocat, *, tb):
    t = pl.program_id(1)

    @pl.when(t == 0)
    def _():
        kbuf[0:BLOCK, :] = jnp.zeros((BLOCK, A_KV), BF16)
        vbuf[0:BLOCK, :] = jnp.zeros((BLOCK, A_KV), BF16)
        tail_ref[0] = jnp.zeros((8, D_FF), F32)

    x1 = _mixer_a_body(t, x_ref[0], sinks_ref, g1_ref, win_ref, qg_ref, kg_ref, xqg_ref, mk_ref, mv_ref, wout_ref,
                       wk_ref, wv_ref, kbuf, vbuf, ocat, tb)
    y_ref[0] = _ffn_body(x1, g2_ref, wfi_ref, cw_ref, cb_ref, wd_ref, tail_ref, tb)


def _layer_specs(tb, layer):
    xspec = pl.BlockSpec((1, tb, D_MODEL), lambda b, i: (b, i, 0))
    mspec = pl.BlockSpec((1, 1, N_MEM, X_WIDTH), lambda b, i: (layer, b, 0, 0))
    per_seq = lambda rows, width: pl.BlockSpec((1, rows, width), lambda b, i: (b, 0, 0))
    return xspec, mspec, per_seq(8, D_FF), per_seq


def _ffn_specs(layer):
    return [
        _const_spec((1, D_MODEL)),
        _layer_spec(layer, (D_MODEL, 2 * D_FF)),
        _const_spec((CONV_W, D_FF)),
        _const_spec((1, D_FF)),
        _layer_spec(layer, (D_FF, D_MODEL)),
    ]


def _layer_a_prompt(x, layer, norm1_g, w_in_b, q_norm, k_norm, xq_norm, sinks, mk_all, mv_all, w_out_b,
                    norm2_g, w_ffn_in_b, conv_w, conv_b, w_down_b):
    n, t, _ = x.shape
    tb = min(PROMPT_ROWS, t)
    a_in = w_in_b.shape[1]
    xspec, mspec, tailspec, per_seq = _layer_specs(tb, layer)
    wspec = per_seq(A_KV, BLOCK)
    y, wk, wv, tail = pl.pallas_call(
        functools.partial(_layer_a_kernel, tb=tb),
        grid=(n, t // tb),
        in_specs=[
            pl.BlockSpec(memory_space=pltpu.SMEM),
            xspec,
            _const_spec((1, D_MODEL)),
            _const_spec((D_MODEL, a_in)),
            _const_spec((1, LANES)),
            _const_spec((1, LANES)),
            _const_spec((1, LANES)),
            mspec,
            mspec,
            _const_spec((A_Q + X_WIDTH, D_MODEL)),
        ] + _ffn_specs(layer),
        out_specs=[xspec, wspec, wspec, tailspec],
        out_shape=[
            jax.ShapeDtypeStruct((n, t, D_MODEL), F32),
            jax.ShapeDtypeStruct((n, A_KV, BLOCK), F32),
            jax.ShapeDtypeStruct((n, A_KV, BLOCK), F32),
            jax.ShapeDtypeStruct((n, 8, D_FF), F32),
        ],
        scratch_shapes=[
            pltpu.VMEM((tb + BLOCK, A_KV), BF16),
            pltpu.VMEM((tb + BLOCK, A_KV), BF16),
            pltpu.VMEM((tb, A_Q + X_WIDTH), BF16),
        ],
        compiler_params=_params("arbitrary", "arbitrary"),
        name="layer_a_prompt",
    )(sinks, x, norm1_g.reshape(1, D_MODEL), w_in_b, jnp.tile(q_norm, 2).reshape(1, LANES),
      jnp.tile(k_norm, 2).reshape(1, LANES), jnp.tile(xq_norm, 2).reshape(1, LANES), mk_all, mv_all, w_out_b,
      norm2_g.reshape(1, D_MODEL), w_ffn_in_b, conv_w, conv_b.reshape(1, D_FF), w_down_b)
    return y, wk, wv, tail[:, 8 - (CONV_W - 1):, :]


def _mixer_b_body(x, g1_ref, win_ref, vg_ref, ws_ref, bias_ref, xqg_ref, mk_ref, mv_ref, wout_ref, cv_ref, ocat, tb):
    h = _rms_rows(x, g1_ref[...]).astype(BF16)
    z = _dot(h, win_ref[...])
    uv = _gelu_exact(z[:, :2 * B_WIDTH])
    u = uv[:, :B_WIDTH]
    vn = _rms_rows(uv[:, B_WIDTH:], vg_ref[...])
    cv_ref[0] = vn[tb - CHUNK:, :]
    xq_pairs = _pair_head_norm(z[:, 2 * B_WIDTH:], xqg_ref[...])

    wi = lax.broadcasted_iota(jnp.int32, (CHUNK, CHUNK), 0)
    wj = lax.broadcasted_iota(jnp.int32, (CHUNK, CHUNK), 1)
    causal = wi >= wj
    w_tril = [jnp.where(causal, ws_ref[g], 0.0).astype(BF16) for g in range(B_GROUPS)]
    lo = lax.broadcasted_iota(jnp.int32, (CHUNK, LANES), 1) < HEAD_DIM
    bias = bias_ref[...]
    for c in range(tb // CHUNK):
        vc = vn[c * CHUNK:(c + 1) * CHUNK, :].astype(BF16)
        r = [_dot(w_tril[g], vc[:, c0:c0 + 256]) for g, c0 in enumerate(GATE_WINDOW_START)]
        mixed = jnp.concatenate(
            [r[0][:, :128], jnp.where(lo, r[0][:, 128:], r[1][:, :128]), r[1][:, 128:],
             r[2][:, :128], jnp.where(lo, r[2][:, 128:], r[3][:, :128]), r[3][:, 128:]], axis=1)
        o = u[c * CHUNK:(c + 1) * CHUNK, :] * (mixed + bias)
        ocat[c * CHUNK:(c + 1) * CHUNK, 0:B_WIDTH] = o.astype(BF16)

    _prompt_mem_attention(xq_pairs, mk_ref, mv_ref, ocat, B_WIDTH)
    return x + _dot(ocat[...], wout_ref[...])


def _layer_b_kernel(x_ref, g1_ref, win_ref, vg_ref, ws_ref, bias_ref, xqg_ref, mk_ref, mv_ref, wout_ref,
                    g2_ref, wfi_ref, cw_ref, cb_ref, wd_ref, y_ref, cv_ref, tail_ref, ocat, *, tb):
    @pl.when(pl.program_id(1) == 0)
    def _():
        tail_ref[0] = jnp.zeros((8, D_FF), F32)

    x1 = _mixer_b_body(x_ref[0], g1_ref, win_ref, vg_ref, ws_ref, bias_ref, xqg_ref, mk_ref, mv_ref, wout_ref,
                       cv_ref, ocat, tb)
    y_ref[0] = _ffn_body(x1, g2_ref, wfi_ref, cw_ref, cb_ref, wd_ref, tail_ref, tb)


def _gate_bias_rows(b_s):
    return jnp.repeat(b_s.T, B_GROUP_DIM, axis=1)


def _layer_b_prompt(x, layer, norm1_g, w_in_b, v_norm, w_s, b_s, xq_norm, mk_all, mv_all, w_out_b,
                    norm2_g, w_ffn_in_b, conv_w, conv_b, w_down_b):
    n, t, _ = x.shape
    tb = min(PROMPT_ROWS, t)
    b_in = w_in_b.shape[1]
    xspec, mspec, tailspec, per_seq = _layer_specs(tb, layer)
    y, cv, tail = pl.pallas_call(
        functools.partial(_layer_b_kernel, tb=tb),
        grid=(n, t // tb),
        in_specs=[
            xspec,
            _const_spec((1, D_MODEL)),
            _const_spec((D_MODEL, b_in)),
            _const_spec((1, B_WIDTH)),
            _const_spec((B_GROUPS, CHUNK, CHUNK)),
            _const_spec((CHUNK, B_WIDTH)),
            _const_spec((1, LANES)),
            mspec,
            mspec,
            _const_spec((B_WIDTH + X_WIDTH, D_MODEL)),
        ] + _ffn_specs(layer),
        out_specs=[xspec, per_seq(CHUNK, B_WIDTH), tailspec],
        out_shape=[
            jax.ShapeDtypeStruct((n, t, D_MODEL), F32),
            jax.ShapeDtypeStruct((n, CHUNK, B_WIDTH), F32),
            jax.ShapeDtypeStruct((n, 8, D_FF), F32),
        ],
        scratch_shapes=[pltpu.VMEM((tb, B_WIDTH + X_WIDTH), BF16)],
        compiler_params=_params("arbitrary", "arbitrary"),
        name="layer_b_prompt",
    )(x, norm1_g.reshape(1, D_MODEL), w_in_b, v_norm.reshape(1, B_WIDTH), w_s, _gate_bias_rows(b_s),
      jnp.tile(xq_norm, 2).reshape(1, LANES), mk_all, mv_all, w_out_b,
      norm2_g.reshape(1, D_MODEL), w_ffn_in_b, conv_w, conv_b.reshape(1, D_FF), w_down_b)
    return y, cv, tail[:, 8 - (CONV_W - 1):, :]


def _stack_positions(ref, seq):
    return jnp.concatenate([ref[t] for t in range(seq)], axis=0)


def _ffn_sample_kernel(x_ref, om_ref, ox_ref, wom_ref, wox_ref, g2_ref, wfi_ref, cw_ref, cb_ref, wd_ref, p0_ref, p1_ref,
                       y_ref, atail_ref, *, seq, nb):
    x1 = (_stack_positions(x_ref, seq) + _dot(_stack_positions(om_ref, seq), wom_ref[...])
          + _dot(_stack_positions(ox_ref, seq), wox_ref[...]))
    h = _rms_rows(x1, g2_ref[...]).astype(BF16)
    gu = _dot(h, wfi_ref[...])
    a = gu[:, :D_FF]
    blocks = [p0_ref[...], p1_ref[...]] + [a[t * nb:(t + 1) * nb] for t in range(seq)]
    am2 = jnp.concatenate(blocks[:seq], axis=0)
    am1 = jnp.concatenate(blocks[1:seq + 1], axis=0)
    y = x1 + _dot(_conv_gate(a, am1, am2, gu[:, D_FF:], cw_ref, cb_ref), wd_ref[...])
    for t in range(seq):
        y_ref[t] = y[t * nb:(t + 1) * nb]
    for r in range(CONV_W - 1):
        atail_ref[r] = blocks[seq + r]


def _ffn_sample(x3, om3, ox3, w_om, w_ox, layer, norm2_g, w_ffn_in_b, conv_w, conv_b, w_down_b, prefix):
    s, n, _ = x3.shape
    km, kx = om3.shape[2], ox3.shape[2]
    nb = min(SAMPLE_FFN_SEQS, n)
    assert n % nb == 0 and s >= CONV_W - 1

    def blk(lead, width):
        return pl.BlockSpec((lead, nb, width), lambda i: (0, i, 0))

    pspec = pl.BlockSpec((nb, D_FF), lambda i: (i, 0))
    return pl.pallas_call(
        functools.partial(_ffn_sample_kernel, seq=s, nb=nb),
        grid=(n // nb,),
        in_specs=[
            blk(s, D_MODEL),
            blk(s, km),
            blk(s, kx),
            _const_spec((km, D_MODEL)),
            _const_spec((kx, D_MODEL)),
            _const_spec((1, D_MODEL)),
            _layer_spec(layer, (D_MODEL, 2 * D_FF)),
            _const_spec((CONV_W, D_FF)),
            _const_spec((1, D_FF)),
            _layer_spec(layer, (D_FF, D_MODEL)),
            pspec,
            pspec,
        ],
        out_specs=[blk(s, D_MODEL), blk(CONV_W - 1, D_FF)],
        out_shape=[
            jax.ShapeDtypeStruct((s, n, D_MODEL), F32),
            jax.ShapeDtypeStruct((CONV_W - 1, n, D_FF), F32),
        ],
        compiler_params=_params("arbitrary"),
        name="ffn_sample",
    )(x3, om3, ox3, w_om, w_ox, norm2_g.reshape(1, D_MODEL), w_ffn_in_b, conv_w, conv_b.reshape(1, D_FF), w_down_b,
      prefix[:, 0, :], prefix[:, 1, :])


def _write_query_slots(z, heads, gain, out_ref):
    for hh in range(heads):
        q = _slot_head_norm(z[:, hh * X_WIDTH:(hh + 1) * X_WIDTH], gain)
        out_ref[hh] = (q * SCALE).astype(out_ref.dtype)


def _sample_in_a_kernel(x_ref, g1_ref, wq_ref, wv_ref, wx_ref, wkvt_ref, qg_ref, kgc_ref, xqg_ref,
                        qs_ref, v_ref, xs_ref, kt_ref, vt_ref, *, seq):
    h = _rms_rows(_stack_positions(x_ref, seq), g1_ref[...]).astype(BF16)
    _write_query_slots(_dot(h, wq_ref[...]), A_HEADS, qg_ref[...], qs_ref)
    v_ref[...] = _dot(h, wv_ref[...])
    _write_query_slots(_dot(h, wx_ref[...]), X_HEADS, xqg_ref[...], xs_ref)
    kvt = _nt_dot(wkvt_ref[...], h)
    for hk in range(A_KV_HEADS):
        sl = slice(hk * HEAD_DIM, (hk + 1) * HEAD_DIM)
        kh = kvt[sl, :]
        ms = jnp.mean(kh * kh, axis=0, keepdims=True)
        kt_ref[sl, :] = kh * lax.rsqrt(ms + EPS) * kgc_ref[sl, :]
    vt_ref[...] = kvt[A_KV:, :]


def _expand_cols(w, heads, slot_of):
    d = w.shape[0]
    wh = w.reshape(d, heads, 1, HEAD_DIM)
    sel = np.zeros((heads, X_HEADS, 1), np.float32)
    for h in range(heads):
        sel[h, slot_of(h), 0] = 1.0
    return (wh * jnp.asarray(sel, w.dtype)).reshape(d, heads * X_WIDTH)


def _expand_rows(w, heads, slot_of):
    d = w.shape[1]
    wh = w.reshape(heads, 1, HEAD_DIM, d)
    sel = np.zeros((heads, X_HEADS, 1, 1), np.float32)
    for h in range(heads):
        sel[h, slot_of(h), 0, 0] = 1.0
    return (wh * jnp.asarray(sel, w.dtype)).reshape(heads * X_WIDTH, d)


def _sample_in_a(x3, norm1_g, w_in_b, q_norm, k_norm, xq_norm):
    s, n, _ = x3.shape
    rows = s * n
    wq = _expand_cols(w_in_b[:, :A_Q], A_HEADS, lambda h: h // A_GROUP)
    wv = w_in_b[:, A_Q + A_KV:A_Q + 2 * A_KV]
    wx = _expand_cols(w_in_b[:, A_Q + 2 * A_KV:], X_HEADS, lambda h: h)
    w_kvt = w_in_b[:, A_Q:A_Q + 2 * A_KV].T
    slots = lambda heads: jax.ShapeDtypeStruct((heads, rows, X_WIDTH), BF16)
    full = lambda shape: pl.BlockSpec(shape, lambda i: (0,) * len(shape))
    return pl.pallas_call(
        functools.partial(_sample_in_a_kernel, seq=s),
        grid=(1,),
        in_specs=[
            _const_spec((s, n, D_MODEL)),
            _const_spec((1, D_MODEL)),
            _const_spec(wq.shape),
            _const_spec(wv.shape),
            _const_spec(wx.shape),
            _const_spec(w_kvt.shape),
            _const_spec((1, A_KV)),
            _const_spec((A_KV, 1)),
            _const_spec((1, X_WIDTH)),
        ],
        out_specs=[
            full((A_HEADS, rows, A_KV)), full((rows, A_KV)), full((X_HEADS, rows, X_WIDTH)),
            full((A_KV, rows)), full((A_KV, rows)),
        ],
        out_shape=[
            slots(A_HEADS),
            jax.ShapeDtypeStruct((rows, A_KV), F32),
            slots(X_HEADS),
            jax.ShapeDtypeStruct((A_KV, rows), F32),
            jax.ShapeDtypeStruct((A_KV, rows), F32),
        ],
        compiler_params=_params("arbitrary"),
        name="sample_in_a",
    )(x3, norm1_g.reshape(1, D_MODEL), wq, wv, wx, w_kvt, jnp.tile(q_norm, 4).reshape(1, A_KV),
      jnp.tile(k_norm, 4).reshape(A_KV, 1), jnp.tile(xq_norm, 4).reshape(1, X_WIDTH))


def _sample_in_b_kernel(x_ref, g1_ref, wuv_ref, wx_ref, vg_ref, coef_ref, gbias_ref, xqg_ref, o_ref, vn_ref, xs_ref,
                        *, seq, n):
    h = _rms_rows(_stack_positions(x_ref, seq), g1_ref[...]).astype(BF16)
    uv = _gelu_exact(_dot(h, wuv_ref[...]))
    u = uv[:, :B_WIDTH]
    vn = _rms_rows(uv[:, B_WIDTH:], vg_ref[...])
    vn_ref[...] = vn
    for t in range(seq):
        mixed = gbias_ref[t:t + 1, :] + coef_ref[t * seq:t * seq + 1, :] * vn[0:n]
        for j in range(1, t + 1):
            mixed = mixed + coef_ref[t * seq + j:t * seq + j + 1, :] * vn[j * n:(j + 1) * n]
        o_ref[t * n:(t + 1) * n, :] = (u[t * n:(t + 1) * n] * mixed).astype(o_ref.dtype)
    _write_query_slots(_dot(h, wx_ref[...]), X_HEADS, xqg_ref[...], xs_ref)


def _sample_in_b(x3, norm1_g, w_in_b, v_norm, w_s, b_s, xq_norm):
    s, n, _ = x3.shape
    rows = s * n
    wuv = w_in_b[:, :2 * B_WIDTH]
    wx = _expand_cols(w_in_b[:, 2 * B_WIDTH:], X_HEADS, lambda h: h)
    coef = jnp.repeat(w_s[:, :s, :s].transpose(1, 2, 0).reshape(s * s, B_GROUPS), B_GROUP_DIM, axis=1)
    gbias = jnp.repeat(b_s[:, :s].T, B_GROUP_DIM, axis=1)
    full = lambda shape: pl.BlockSpec(shape, lambda i: (0,) * len(shape))
    return pl.pallas_call(
        functools.partial(_sample_in_b_kernel, seq=s, n=n),
        grid=(1,),
        in_specs=[
            _const_spec((s, n, D_MODEL)),
            _const_spec((1, D_MODEL)),
            _const_spec(wuv.shape),
            _const_spec(wx.shape),
            _const_spec((1, B_WIDTH)),
            _const_spec((s * s, B_WIDTH)),
            _const_spec((s, B_WIDTH)),
            _const_spec((1, X_WIDTH)),
        ],
        out_specs=[full((rows, B_WIDTH)), full((rows, B_WIDTH)), full((X_HEADS, rows, X_WIDTH))],
        out_shape=[
            jax.ShapeDtypeStruct((rows, B_WIDTH), BF16),
            jax.ShapeDtypeStruct((rows, B_WIDTH), F32),
            jax.ShapeDtypeStruct((X_HEADS, rows, X_WIDTH), BF16),
        ],
        compiler_params=_params("arbitrary"),
        name="sample_in_b",
    )(x3, norm1_g.reshape(1, D_MODEL), wuv, wx, v_norm.reshape(1, B_WIDTH), coef, gbias,
      jnp.tile(xq_norm, 4).reshape(1, X_WIDTH))


def _sample_attn_win_kernel(qs_ref, ktn_ref, vtn_ref, vn_ref, slope_ref, sink_ref, ck_ref, cv_ref,
                            o_ref, wk_ref, wv_ref, *, sb, seq):
    rows = A_HEADS * seq
    tq = _div_mod(lax.broadcasted_iota(jnp.int32, (rows, WINDOW), 0), seq)[1]
    lane = lax.broadcasted_iota(jnp.int32, (rows, WINDOW), 1)
    slope = slope_ref[...]
    sink = sink_ref[...]
    d1 = tq + WINDOW - lane
    valid1 = d1 < WINDOW
    bias1 = slope * d1.astype(F32)
    new_seq, new_pos = _div_mod(lane, seq)
    d2 = tq - new_pos
    causal2 = d2 >= 0
    bias2 = slope * d2.astype(F32)
    ktn = ktn_ref[...]
    vtn = vtn_ref[...]
    ktn_b = ktn.astype(BF16)
    vn_b = vn_ref[...].astype(BF16)
    keep_old = lax.broadcasted_iota(jnp.int32, (A_KV, WINDOW), 1) < WINDOW - seq

    def body(n, carry):
        q = qs_ref[n]
        ckt = ck_ref[n]
        cvt = cv_ref[n]
        s1 = jnp.where(valid1, _dot(q, ckt.astype(BF16)) - bias1, NEG)
        s2 = jnp.where(causal2 & (new_seq == n), _dot(q, ktn_b) - bias2, NEG)
        m = jnp.maximum(jnp.maximum(jnp.max(s1, axis=-1, keepdims=True), jnp.max(s2, axis=-1, keepdims=True)), sink)
        e1 = jnp.exp(s1 - m)
        e2 = jnp.exp(s2 - m)
        den = jnp.sum(e1, axis=-1, keepdims=True) + jnp.sum(e2, axis=-1, keepdims=True) + jnp.exp(sink - m)
        o = _nt_dot(e1.astype(BF16), cvt.astype(BF16)) + _dot(e2.astype(BF16), vn_b)
        o_ref[n] = (o / den).astype(o_ref.dtype)
        shift_new = WINDOW - seq - n * seq
        wk_ref[n] = jnp.where(keep_old, pltpu.roll(ckt, WINDOW - seq, 1), pltpu.roll(ktn, shift_new, 1))
        wv_ref[n] = jnp.where(keep_old, pltpu.roll(cvt, WINDOW - seq, 1), pltpu.roll(vtn, shift_new, 1))
        return carry

    lax.fori_loop(0, sb, body, 0, unroll=SAMPLE_SEQ_UNROLL)


def _sample_attn_win(qs, kt_new, vt_new, v_new, slope_rows, sink_rows, ckt, cvt, layer, seq):
    n = qs.shape[0]
    rows = A_HEADS * seq
    assert LANES % seq == 0
    sb = LANES // seq
    assert n % sb == 0

    def seqs(*tail):
        return pl.BlockSpec((sb,) + tail, lambda i: (i,) + (0,) * len(tail))

    cache = pl.BlockSpec((None, sb, A_KV, WINDOW), lambda i: (layer, i, 0, 0))
    win = jax.ShapeDtypeStruct((n, A_KV, WINDOW), F32)
    return pl.pallas_call(
        functools.partial(_sample_attn_win_kernel, sb=sb, seq=seq),
        grid=(n // sb,),
        in_specs=[
            seqs(rows, A_KV),
            pl.BlockSpec((A_KV, LANES), lambda i: (0, i)),
            pl.BlockSpec((A_KV, LANES), lambda i: (0, i)),
            pl.BlockSpec((LANES, A_KV), lambda i: (i, 0)),
            _const_spec((rows, 1)),
            _const_spec((rows, 1)),
            cache,
            cache,
        ],
        out_specs=[seqs(rows, A_KV), seqs(A_KV, WINDOW), seqs(A_KV, WINDOW)],
        out_shape=[jax.ShapeDtypeStruct((n, rows, A_KV), BF16), win, win],
        compiler_params=_params("arbitrary"),
        name="sample_attn_win",
    )(qs, kt_new, vt_new, v_new, slope_rows, sink_rows, ckt, cvt)


def _sample_attn_mem_kernel(xs_ref, mk_ref, mv_ref, xo_ref, *, sb):
    def body(n, carry):
        s = _dot(xs_ref[n], mk_ref[n].astype(BF16))
        m = jnp.max(s, axis=-1, keepdims=True)
        e = jnp.exp(s - m)
        den = jnp.sum(e, axis=-1, keepdims=True)
        xo_ref[n] = (_nt_dot(e.astype(BF16), mv_ref[n].astype(BF16)) / den).astype(xo_ref.dtype)
        return carry

    lax.fori_loop(0, sb, body, 0, unroll=SAMPLE_SEQ_UNROLL)


def _sample_attn_mem(xs, mkt, mvt, layer):
    n, xrows, _ = xs.shape
    sb = min(SAMPLE_MEM_SEQS, n)
    assert n % sb == 0
    qspec = pl.BlockSpec((sb, xrows, X_WIDTH), lambda i: (i, 0, 0))
    cache = pl.BlockSpec((None, sb, X_WIDTH, N_MEM), lambda i: (layer, i, 0, 0))
    return pl.pallas_call(
        functools.partial(_sample_attn_mem_kernel, sb=sb),
        grid=(n // sb,),
        in_specs=[qspec, cache, cache],
        out_specs=qspec,
        out_shape=jax.ShapeDtypeStruct((n, xrows, X_WIDTH), BF16),
        compiler_params=_params("arbitrary"),
        name="sample_attn_mem",
    )(xs, mkt, mvt)


def _slots_to_seq(a, seq, n):
    hh, _, w = a.shape
    return a.reshape(hh, seq, n, w).transpose(2, 0, 1, 3).reshape(n, hh * seq, w)


def _seq_to_cols(a, heads, seq):
    n, _, w = a.shape
    return a.reshape(n, heads, seq, w).transpose(2, 0, 1, 3).reshape(seq, n, heads * w)


def _feature_major(cache):
    l, n, p, kv, d = cache.shape
    return cache.transpose(0, 1, 3, 4, 2).reshape(l, n, kv * d, p)


def kernel(x_prompt, x_sample, cache_win_k, cache_win_v, cache_mem_k, cache_mem_v, state_conv, mem_prompt,
           norm1_g, norm2_g, mem_norm_g, w_in_a, q_norm_a, k_norm_a, sinks_a, w_out_a, w_in_b, v_norm_b, w_s_b,
           b_s_b, w_out_b, w_mem_kv, xq_norm, xk_norm, w_ffn_in, conv_w, conv_b, w_down):
    depth = norm1_g.shape[0]
    assert depth == 2 and w_in_a.shape[0] == 1 and w_in_b.shape[0] == 1
    nb = x_prompt.shape[0]
    ns, seq, _ = x_sample.shape
    assert cache_win_k.shape[2] == WINDOW

    w_in_a_b = w_in_a[0].astype(BF16)
    w_in_b_b = w_in_b[0].astype(BF16)
    w_out_a_b = w_out_a[0].astype(BF16)
    w_out_b_b = w_out_b[0].astype(BF16)
    w_mem_kv_b = w_mem_kv.astype(BF16)
    w_ffn_in_b = w_ffn_in.astype(BF16)
    w_down_b = w_down.astype(BF16)

    mk_all, mv_all, mem_kt_p, mem_vt_p = _memory_kv(mem_prompt, mem_norm_g, w_mem_kv_b, xk_norm)
    x2, win_k_p, win_v_p, conv_p0 = _layer_a_prompt(
        x_prompt, 0, norm1_g[0], _reorder_heads(w_in_a_b, Q_LANE_ORDER, 1), q_norm_a[0], k_norm_a[0], xq_norm[0],
        sinks_a[0], mk_all, mv_all, _reorder_heads(w_out_a_b, Q_LANE_ORDER, 0),
        norm2_g[0], w_ffn_in_b, conv_w[0], conv_b[0], w_down_b)
    y_prompt, chunk_v_p, conv_p1 = _layer_b_prompt(
        x2, 1, norm1_g[1], w_in_b_b, v_norm_b[0], w_s_b[0], b_s_b[0], xq_norm[1], mk_all, mv_all, w_out_b_b,
        norm2_g[1], w_ffn_in_b, conv_w[1], conv_b[1], w_down_b)

    xs3 = x_sample.transpose(1, 0, 2)
    mkt = _feature_major(cache_mem_k)
    mvt = _feature_major(cache_mem_v)

    def by_seq_cols(a):
        return a.reshape(a.shape[0], seq, ns).transpose(0, 2, 1).reshape(a.shape[0], ns * seq)

    qs, v_new, xslots, kt_new, vt_new = _sample_in_a(xs3, norm1_g[0], w_in_a_b, q_norm_a[0], k_norm_a[0], xq_norm[0])
    slope_rows = jnp.repeat(jnp.asarray(SLOPES, F32), seq).reshape(A_HEADS * seq, 1)
    sink_rows = jnp.repeat(sinks_a[0].astype(F32), seq).reshape(A_HEADS * seq, 1)
    o_s, win_kt_s, win_vt_s = _sample_attn_win(
        _slots_to_seq(qs, seq, ns), by_seq_cols(kt_new), by_seq_cols(vt_new),
        v_new.reshape(seq, ns, A_KV).transpose(1, 0, 2).reshape(ns * seq, A_KV), slope_rows, sink_rows,
        _feature_major(cache_win_k), _feature_major(cache_win_v), 0, seq)
    xo_s = _sample_attn_mem(_slots_to_seq(xslots, seq, ns), mkt, mvt, 0)
    ys1, conv_s0 = _ffn_sample(
        xs3, _seq_to_cols(o_s, A_HEADS, seq), _seq_to_cols(xo_s, X_HEADS, seq),
        _expand_rows(w_out_a_b[:A_Q], A_HEADS, lambda h: h // A_GROUP), _expand_rows(w_out_a_b[A_Q:], X_HEADS, lambda h: h),
        0, norm2_g[0], w_ffn_in_b, conv_w[0], conv_b[0], w_down_b, state_conv[0])

    o_b, vn_s, xslots = _sample_in_b(ys1, norm1_g[1], w_in_b_b, v_norm_b[0], w_s_b[0], b_s_b[0], xq_norm[1])
    xo_s = _sample_attn_mem(_slots_to_seq(xslots, seq, ns), mkt, mvt, 1)
    ys2, conv_s1 = _ffn_sample(
        ys1, o_b.reshape(seq, ns, B_WIDTH), _seq_to_cols(xo_s, X_HEADS, seq),
        w_out_b_b[:B_WIDTH], _expand_rows(w_out_b_b[B_WIDTH:], X_HEADS, lambda h: h),
        1, norm2_g[1], w_ffn_in_b, conv_w[1], conv_b[1], w_down_b, state_conv[1])

    def position_major(ft):
        lead = ft.shape[:-2]
        nd = len(lead)
        return ft.reshape(*lead, A_KV_HEADS, HEAD_DIM, ft.shape[-1]).transpose(*range(nd), nd + 2, nd, nd + 1)

    return (
        y_prompt,
        ys2.transpose(1, 0, 2),
        position_major(win_k_p)[None],
        position_major(win_v_p)[None],
        chunk_v_p.reshape(1, nb, CHUNK, B_WIDTH),
        position_major(mem_kt_p),
        position_major(mem_vt_p),
        jnp.stack([conv_p0, conv_p1]),
        position_major(win_kt_s)[None],
        position_major(win_vt_s)[None],
        vn_s.reshape(seq, ns, B_WIDTH).transpose(1, 0, 2)[None],
        jnp.stack([conv_s0.transpose(1, 0, 2), conv_s1.transpose(1, 0, 2)]),
    )
```

```python
import functools
import math

import numpy as np
import jax
import jax.numpy as jnp
from jax import lax
from jax.experimental import pallas as pl
from jax.experimental.pallas import tpu as pltpu

F32 = jnp.float32
BF16 = jnp.bfloat16

D_MODEL = 1024
HEAD_DIM = 64
A_HEADS = 12
A_KV_HEADS = 4
A_GROUP = A_HEADS // A_KV_HEADS
WINDOW = 128
BLOCK = 128
B_WIDTH = 768
B_GROUPS = 4
B_GROUP_DIM = B_WIDTH // B_GROUPS
CHUNK = 128
X_HEADS = 4
X_WIDTH = X_HEADS * HEAD_DIM
N_MEM = 256
D_FF = 2816
CONV_W = 3
A_Q = A_HEADS * HEAD_DIM
A_KV = A_KV_HEADS * HEAD_DIM
EPS = 1e-6
NEG = -1e30
SCALE = HEAD_DIM ** -0.5
LOG2E = float(np.float32(1.0 / np.log(2.0)))
SQRT_HALF = float(np.sqrt(0.5).astype(np.float32))

LANES = 128
VMEM_LIMIT_BYTES = 56 * 1024 * 1024
GATE_WINDOW_START = (0, 128, 384, 512)
assert B_GROUP_DIM == 192 and B_GROUPS == 4
PROMPT_ROWS = 512
SAMPLE_MEM_SEQS = 16
SAMPLE_FFN_SEQS = 64
SAMPLE_SEQ_UNROLL = 4


def _alibi_slopes(n):
    def pow2(m):
        start = 2.0 ** (-8.0 / m)
        return [start ** (i + 1) for i in range(m)]

    p = 2 ** int(math.floor(math.log2(n)))
    s = pow2(p)
    if p < n:
        s = s + pow2(2 * p)[0::2][: n - p]
    return [float(np.float32(v)) for v in s]


SLOPES = _alibi_slopes(A_HEADS)

Q_LANE_ORDER = (0, 3, 1, 4, 2, 5, 6, 9, 7, 10, 8, 11)
assert all(Q_LANE_ORDER[p] // A_GROUP == 2 * (p // (2 * A_GROUP)) + p % 2 for p in range(A_HEADS))


_SLOTS = A_KV_HEADS // 2
assert Q_LANE_ORDER == tuple(np.arange(A_HEADS).reshape(_SLOTS, 2, A_GROUP).swapaxes(1, 2).reshape(-1).tolist())


def _reorder_heads(w, axis):
    lead, tail = w.shape[:axis], w.shape[axis + 1:]
    q = lax.slice_in_dim(w, 0, A_Q, axis=axis).reshape(*lead, _SLOTS, 2, A_GROUP, HEAD_DIM, *tail)
    q = jnp.swapaxes(q, axis + 1, axis + 2).reshape(*lead, A_Q, *tail)
    return jnp.concatenate([q, lax.slice_in_dim(w, A_Q, w.shape[axis], axis=axis)], axis=axis)


def _params(*sem):
    return pltpu.CompilerParams(dimension_semantics=sem, vmem_limit_bytes=VMEM_LIMIT_BYTES)


def _const_spec(shape):
    nd = len(shape)
    return pl.BlockSpec(shape, lambda *_: (0,) * nd, pipeline_mode=pl.Buffered(1))


def _layer_spec(layer, shape):
    nd = len(shape)
    return pl.BlockSpec((None,) + tuple(shape), lambda *_: (layer,) + (0,) * nd, pipeline_mode=pl.Buffered(1))


def _rms_rows(x, g):
    ms = jnp.mean(x * x, axis=-1, keepdims=True)
    return x * lax.rsqrt(ms + EPS) * g


def _pair_head_norm(z, g2):
    t, w = z.shape
    lo = lax.broadcasted_iota(jnp.int32, (t, LANES), 1) < HEAD_DIM
    out = []
    for p in range(w // LANES):
        zz = z[:, p * LANES:(p + 1) * LANES]
        sq = zz * zz
        s_lo = jnp.sum(jnp.where(lo, sq, 0.0), axis=-1, keepdims=True)
        s_hi = jnp.sum(jnp.where(lo, 0.0, sq), axis=-1, keepdims=True)
        r_lo = lax.rsqrt(s_lo * (1.0 / HEAD_DIM) + EPS)
        r_hi = lax.rsqrt(s_hi * (1.0 / HEAD_DIM) + EPS)
        out.append(zz * jnp.where(lo, r_lo, r_hi) * g2)
    return out


def _slot_head_norm(z, g4):
    ms = jnp.sum(z * z, axis=-1, keepdims=True) * (1.0 / HEAD_DIM)
    return z * lax.rsqrt(ms + EPS) * g4


def _nt_dot(a, b):
    return lax.dot_general(a, b, (((1,), (1,)), ((), ())), preferred_element_type=F32)


def _dot(a, b):
    return jnp.dot(a, b, preferred_element_type=F32)


def _div_mod(i, n):
    if n & (n - 1) == 0:
        return i >> (n.bit_length() - 1), i & (n - 1)
    return i // n, i % n


def _gelu_exact(x):
    return 0.5 * x * (1.0 + lax.erf(x * SQRT_HALF))


def _conv_gate(a, am1, am2, up, cw_ref, cb_ref):
    c = cb_ref[...] + ((cw_ref[0:1, :] * am2 + cw_ref[1:2, :] * am1) + cw_ref[2:3, :] * a)
    return ((c * (1.0 / (1.0 + jnp.exp(-c)))) * up).astype(BF16)


def _memkv_kernel(mem_ref, g_ref, w_ref, kg_ref, mkb_ref, mvb_ref, mkt_ref, mvt_ref):
    m = mem_ref[0]
    h = _rms_rows(m, g_ref[0]).astype(BF16)
    kv = _dot(h, w_ref[0])
    mk = jnp.concatenate(_pair_head_norm(kv[:, :X_WIDTH], kg_ref[0]), axis=1)
    mv = kv[:, X_WIDTH:]
    mkb_ref[0, 0] = mk.astype(BF16)
    mvb_ref[0, 0] = mv.astype(BF16)
    mkt_ref[0, 0] = mk.T
    mvt_ref[0, 0] = mv.T


def _memory_kv(mem, mem_norm_g, w_mem_kv_b, xk_norm):
    depth = w_mem_kv_b.shape[0]
    n = mem.shape[0]
    g = mem_norm_g.reshape(depth, 1, D_MODEL)
    kg2 = jnp.tile(xk_norm, (1, 2)).reshape(depth, 1, LANES)
    spec = lambda rows, cols: pl.BlockSpec((1, 1, rows, cols), lambda l, b: (l, b, 0, 0))
    return pl.pallas_call(
        _memkv_kernel,
        grid=(depth, n),
        in_specs=[
            pl.BlockSpec((1, N_MEM, D_MODEL), lambda l, b: (b, 0, 0)),
            pl.BlockSpec((1, 1, D_MODEL), lambda l, b: (l, 0, 0)),
            pl.BlockSpec((1, D_MODEL, 2 * X_WIDTH), lambda l, b: (l, 0, 0)),
            pl.BlockSpec((1, 1, LANES), lambda l, b: (l, 0, 0)),
        ],
        out_specs=[spec(N_MEM, X_WIDTH), spec(N_MEM, X_WIDTH), spec(X_WIDTH, N_MEM), spec(X_WIDTH, N_MEM)],
        out_shape=[
            jax.ShapeDtypeStruct((depth, n, N_MEM, X_WIDTH), BF16),
            jax.ShapeDtypeStruct((depth, n, N_MEM, X_WIDTH), BF16),
            jax.ShapeDtypeStruct((depth, n, X_WIDTH, N_MEM), F32),
            jax.ShapeDtypeStruct((depth, n, X_WIDTH, N_MEM), F32),
        ],
        compiler_params=_params("arbitrary", "arbitrary"),
        name="memory_kv",
    )(mem, g, w_mem_kv_b, kg2)


def _half_masked_rows(pair, lo):
    return jnp.concatenate([jnp.where(lo, pair, 0.0), jnp.where(lo, 0.0, pair)], axis=0).astype(BF16)


def _prompt_mem_attention(xq_pairs, mk_ref, mv_ref, ocat, col0):
    t = xq_pairs[0].shape[0]
    lo = lax.broadcasted_iota(jnp.int32, (t, LANES), 1) < HEAD_DIM
    for j, pair in enumerate(xq_pairs):
        mk = mk_ref[0, 0, :, j * LANES:(j + 1) * LANES]
        mv = mv_ref[0, 0, :, j * LANES:(j + 1) * LANES]
        s = _nt_dot(_half_masked_rows(pair * (SCALE * LOG2E), lo), mk)
        m = jnp.max(s, axis=-1, keepdims=True)
        e = jnp.exp2(s - m)
        den = jnp.sum(e, axis=-1, keepdims=True)
        o = _dot(e.astype(BF16), mv) / den
        ocat[:, col0 + j * LANES:col0 + (j + 1) * LANES] = jnp.where(lo, o[:t], o[t:]).astype(BF16)


def _mixer_a_body(t, x, sinks_ref, g1_ref, win_ref, qg_ref, kg_ref, xqg_ref, mk_ref, mv_ref, wout_ref,
                  wk_ref, wv_ref, kbuf, vbuf, ocat, tb):
    h = _rms_rows(x, g1_ref[...]).astype(BF16)
    z = _dot(h, win_ref[...])
    q_pairs = _pair_head_norm(z[:, :A_Q], qg_ref[...])
    k_pairs = _pair_head_norm(z[:, A_Q:A_Q + A_KV], kg_ref[...])
    v = z[:, A_Q + A_KV:A_Q + 2 * A_KV]
    xq_pairs = _pair_head_norm(z[:, A_Q + 2 * A_KV:], xqg_ref[...])

    for p, kp in enumerate(k_pairs):
        wk_ref[0, p * LANES:(p + 1) * LANES, :] = kp[tb - BLOCK:, :].T
        kbuf[BLOCK:BLOCK + tb, p * LANES:(p + 1) * LANES] = kp.astype(BF16)
    wv_ref[0] = v[tb - BLOCK:, :].T
    vbuf[BLOCK:BLOCK + tb, :] = v.astype(BF16)

    qi = lax.broadcasted_iota(jnp.int32, (BLOCK, 2 * BLOCK), 0)
    kj = lax.broadcasted_iota(jnp.int32, (BLOCK, 2 * BLOCK), 1)
    d = qi + BLOCK - kj
    in_window = (d >= 0) & (d < WINDOW)
    key_pos = lax.broadcasted_iota(jnp.int32, (1, 2 * BLOCK), 1).astype(F32)
    query_pos = (lax.broadcasted_iota(jnp.int32, (BLOCK, 1), 0) + BLOCK).astype(F32)

    lo = lax.broadcasted_iota(jnp.int32, (BLOCK, LANES), 1) < HEAD_DIM
    pairs_per_slot = A_HEADS // A_KV_HEADS
    for s in range(tb // BLOCK):
        rows = slice(s * BLOCK, (s + 1) * BLOCK)
        first_key = jnp.where(t > 0, 0, BLOCK) if s == 0 else 0
        valid = in_window & (kj >= first_key)
        for j in range(A_KV // LANES):
            kw = kbuf[s * BLOCK:(s + 2) * BLOCK, j * LANES:(j + 1) * LANES]
            vw = vbuf[s * BLOCK:(s + 2) * BLOCK, j * LANES:(j + 1) * LANES]
            q_rows = jnp.concatenate(
                [_half_masked_rows(q_pairs[pairs_per_slot * j + c][rows] * (SCALE * LOG2E), lo)
                 for c in range(pairs_per_slot)], axis=0)
            sc_all = _nt_dot(q_rows, kw)
            e_parts, den_parts = [], []
            for r in range(2 * pairs_per_slot):
                hq = Q_LANE_ORDER[2 * pairs_per_slot * j + r]
                slope2 = SLOPES[hq] * LOG2E
                sc = jnp.where(valid, sc_all[r * BLOCK:(r + 1) * BLOCK] + slope2 * key_pos, NEG)
                sink = sinks_ref[hq] * LOG2E + slope2 * query_pos
                m = jnp.maximum(jnp.max(sc, axis=-1, keepdims=True), sink)
                e = jnp.exp2(sc - m)
                den_parts.append(jnp.sum(e, axis=-1, keepdims=True) + jnp.exp2(sink - m))
                e_parts.append(e.astype(BF16))
            o_all = _dot(jnp.concatenate(e_parts, axis=0), vw)
            for c in range(pairs_per_slot):
                o_lo = o_all[2 * c * BLOCK:(2 * c + 1) * BLOCK] / den_parts[2 * c]
                o_hi = o_all[(2 * c + 1) * BLOCK:(2 * c + 2) * BLOCK] / den_parts[2 * c + 1]
                col = pairs_per_slot * j + c
                ocat[rows, col * LANES:(col + 1) * LANES] = jnp.where(lo, o_lo, o_hi).astype(BF16)

    _prompt_mem_attention(xq_pairs, mk_ref, mv_ref, ocat, A_Q)
    x1 = x + _dot(ocat[...], wout_ref[...])

    kbuf[0:BLOCK, :] = kbuf[tb:tb + BLOCK, :]
    vbuf[0:BLOCK, :] = vbuf[tb:tb + BLOCK, :]
    return x1


def _ffn_body(x, g2_ref, wfi_ref, cw_ref, cb_ref, wd_ref, tail_ref, tb):
    h = _rms_rows(x, g2_ref[...]).astype(BF16)
    gu = _dot(h, wfi_ref[...])
    a = gu[:, :D_FF]
    row = lax.broadcasted_iota(jnp.int32, (tb, 1), 0)
    prev_last = tail_ref[0, 7:8, :]
    am1 = jnp.where(row == 0, prev_last, pltpu.roll(a, 1, 0))
    am2 = jnp.where(row == 0, tail_ref[0, 6:7, :], jnp.where(row == 1, prev_last, pltpu.roll(a, 2, 0)))
    tail_ref[0] = a[tb - 8:, :]
    return x + _dot(_conv_gate(a, am1, am2, gu[:, D_FF:], cw_ref, cb_ref), wd_ref[...])


def _layer_a_kernel(sinks_ref, x_ref, g1_ref, win_ref, qg_ref, kg_ref, xqg_ref, mk_ref, mv_ref, wout_ref,
                    g2_ref, wfi_ref, cw_ref, cb_ref, wd_ref, y_ref, wk_ref, wv_ref, tail_ref, kbuf, vbuf, ocat, *, tb):
    t = pl.program_id(1)

    @pl.when(t == 0)
    def _():
        kbuf[0:BLOCK, :] = jnp.zeros((BLOCK, A_KV), BF16)
        vbuf[0:BLOCK, :] = jnp.zeros((BLOCK, A_KV), BF16)
        tail_ref[0] = jnp.zeros((8, D_FF), F32)

    x1 = _mixer_a_body(t, x_ref[0], sinks_ref, g1_ref, win_ref, qg_ref, kg_ref, xqg_ref, mk_ref, mv_ref, wout_ref,
                       wk_ref, wv_ref, kbuf, vbuf, ocat, tb)
    y_ref[0] = _ffn_body(x1, g2_ref, wfi_ref, cw_ref, cb_ref, wd_ref, tail_ref, tb)


def _layer_specs(tb, layer):
    xspec = pl.BlockSpec((1, tb, D_MODEL), lambda b, i: (b, i, 0))
    mspec = pl.BlockSpec((1, 1, N_MEM, X_WIDTH), lambda b, i: (layer, b, 0, 0))
    per_seq = lambda rows, width: pl.BlockSpec((1, rows, width), lambda b, i: (b, 0, 0))
    return xspec, mspec, per_seq(8, D_FF), per_seq


def _ffn_specs(layer):
    return [
        _const_spec((1, D_MODEL)),
        _layer_spec(layer, (D_MODEL, 2 * D_FF)),
        _const_spec((CONV_W, D_FF)),
        _const_spec((1, D_FF)),
        _layer_spec(layer, (D_FF, D_MODEL)),
    ]


def _layer_a_prompt(x, layer, norm1_g, w_in_b, q_norm, k_norm, xq_norm, sinks, mk_all, mv_all, w_out_b,
                    norm2_g, w_ffn_in_b, conv_w, conv_b, w_down_b):
    n, t, _ = x.shape
    tb = min(PROMPT_ROWS, t)
    a_in = w_in_b.shape[1]
    xspec, mspec, tailspec, per_seq = _layer_specs(tb, layer)
    wspec = per_seq(A_KV, BLOCK)
    y, wk, wv, tail = pl.pallas_call(
        functools.partial(_layer_a_kernel, tb=tb),
        grid=(n, t // tb),
        in_specs=[
            pl.BlockSpec(memory_space=pltpu.SMEM),
            xspec,
            _const_spec((1, D_MODEL)),
            _const_spec((D_MODEL, a_in)),
            _const_spec((1, LANES)),
            _const_spec((1, LANES)),
            _const_spec((1, LANES)),
            mspec,
            mspec,
            _const_spec((A_Q + X_WIDTH, D_MODEL)),
        ] + _ffn_specs(layer),
        out_specs=[xspec, wspec, wspec, tailspec],
        out_shape=[
            jax.ShapeDtypeStruct((n, t, D_MODEL), F32),
            jax.ShapeDtypeStruct((n, A_KV, BLOCK), F32),
            jax.ShapeDtypeStruct((n, A_KV, BLOCK), F32),
            jax.ShapeDtypeStruct((n, 8, D_FF), F32),
        ],
        scratch_shapes=[
            pltpu.VMEM((tb + BLOCK, A_KV), BF16),
            pltpu.VMEM((tb + BLOCK, A_KV), BF16),
            pltpu.VMEM((tb, A_Q + X_WIDTH), BF16),
        ],
        compiler_params=_params("arbitrary", "arbitrary"),
        name="layer_a_prompt",
    )(sinks, x, norm1_g.reshape(1, D_MODEL), w_in_b, jnp.tile(q_norm, 2).reshape(1, LANES),
      jnp.tile(k_norm, 2).reshape(1, LANES), jnp.tile(xq_norm, 2).reshape(1, LANES), mk_all, mv_all, w_out_b,
      norm2_g.reshape(1, D_MODEL), w_ffn_in_b, conv_w, conv_b.reshape(1, D_FF), w_down_b)
    return y, wk, wv, tail[:, 8 - (CONV_W - 1):, :]


def _mixer_b_body(x, g1_ref, win_ref, vg_ref, ws_ref, bias_ref, xqg_ref, mk_ref, mv_ref, wout_ref, cv_ref, ocat, tb):
    h = _rms_rows(x, g1_ref[...]).astype(BF16)
    z = _dot(h, win_ref[...])
    uv = _gelu_exact(z[:, :2 * B_WIDTH])
    u = uv[:, :B_WIDTH]
    vn = _rms_rows(uv[:, B_WIDTH:], vg_ref[...])
    cv_ref[0] = vn[tb - CHUNK:, :]
    xq_pairs = _pair_head_norm(z[:, 2 * B_WIDTH:], xqg_ref[...])

    wi = lax.broadcasted_iota(jnp.int32, (CHUNK, CHUNK), 0)
    wj = lax.broadcasted_iota(jnp.int32, (CHUNK, CHUNK), 1)
    causal = wi >= wj
    w_tril = [jnp.where(causal, ws_ref[g], 0.0).astype(BF16) for g in range(B_GROUPS)]
    lo = lax.broadcasted_iota(jnp.int32, (CHUNK, LANES), 1) < HEAD_DIM
    bias = bias_ref[...]
    for c in range(tb // CHUNK):
        vc = vn[c * CHUNK:(c + 1) * CHUNK, :].astype(BF16)
        r = [_dot(w_tril[g], vc[:, c0:c0 + 256]) for g, c0 in enumerate(GATE_WINDOW_START)]
        mixed = jnp.concatenate(
            [r[0][:, :128], jnp.where(lo, r[0][:, 128:], r[1][:, :128]), r[1][:, 128:],
             r[2][:, :128], jnp.where(lo, r[2][:, 128:], r[3][:, :128]), r[3][:, 128:]], axis=1)
        o = u[c * CHUNK:(c + 1) * CHUNK, :] * (mixed + bias)
        ocat[c * CHUNK:(c + 1) * CHUNK, 0:B_WIDTH] = o.astype(BF16)

    _prompt_mem_attention(xq_pairs, mk_ref, mv_ref, ocat, B_WIDTH)
    return x + _dot(ocat[...], wout_ref[...])


def _layer_b_kernel(x_ref, g1_ref, win_ref, vg_ref, ws_ref, bias_ref, xqg_ref, mk_ref, mv_ref, wout_ref,
                    g2_ref, wfi_ref, cw_ref, cb_ref, wd_ref, y_ref, cv_ref, tail_ref, ocat, *, tb):
    @pl.when(pl.program_id(1) == 0)
    def _():
        tail_ref[0] = jnp.zeros((8, D_FF), F32)

    x1 = _mixer_b_body(x_ref[0], g1_ref, win_ref, vg_ref, ws_ref, bias_ref, xqg_ref, mk_ref, mv_ref, wout_ref,
                       cv_ref, ocat, tb)
    y_ref[0] = _ffn_body(x1, g2_ref, wfi_ref, cw_ref, cb_ref, wd_ref, tail_ref, tb)


def _gate_bias_rows(b_s):
    return jnp.repeat(b_s.T, B_GROUP_DIM, axis=1)


def _layer_b_prompt(x, layer, norm1_g, w_in_b, v_norm, w_s, b_s, xq_norm, mk_all, mv_all, w_out_b,
                    norm2_g, w_ffn_in_b, conv_w, conv_b, w_down_b):
    n, t, _ = x.shape
    tb = min(PROMPT_ROWS, t)
    b_in = w_in_b.shape[1]
    xspec, mspec, tailspec, per_seq = _layer_specs(tb, layer)
    y, cv, tail = pl.pallas_call(
        functools.partial(_layer_b_kernel, tb=tb),
        grid=(n, t // tb),
        in_specs=[
            xspec,
            _const_spec((1, D_MODEL)),
            _const_spec((D_MODEL, b_in)),
            _const_spec((1, B_WIDTH)),
            _const_spec((B_GROUPS, CHUNK, CHUNK)),
            _const_spec((CHUNK, B_WIDTH)),
            _const_spec((1, LANES)),
            mspec,
            mspec,
            _const_spec((B_WIDTH + X_WIDTH, D_MODEL)),
        ] + _ffn_specs(layer),
        out_specs=[xspec, per_seq(CHUNK, B_WIDTH), tailspec],
        out_shape=[
            jax.ShapeDtypeStruct((n, t, D_MODEL), F32),
            jax.ShapeDtypeStruct((n, CHUNK, B_WIDTH), F32),
            jax.ShapeDtypeStruct((n, 8, D_FF), F32),
        ],
        scratch_shapes=[pltpu.VMEM((tb, B_WIDTH + X_WIDTH), BF16)],
        compiler_params=_params("arbitrary", "arbitrary"),
        name="layer_b_prompt",
    )(x, norm1_g.reshape(1, D_MODEL), w_in_b, v_norm.reshape(1, B_WIDTH), w_s, _gate_bias_rows(b_s),
      jnp.tile(xq_norm, 2).reshape(1, LANES), mk_all, mv_all, w_out_b,
      norm2_g.reshape(1, D_MODEL), w_ffn_in_b, conv_w, conv_b.reshape(1, D_FF), w_down_b)
    return y, cv, tail[:, 8 - (CONV_W - 1):, :]


def _stack_positions(ref, seq):
    return jnp.concatenate([ref[t] for t in range(seq)], axis=0)


def _ffn_sample_kernel(x_ref, om_ref, ox_ref, wom_ref, wox_ref, g2_ref, wfi_ref, cw_ref, cb_ref, wd_ref, p0_ref, p1_ref,
                       y_ref, atail_ref, *, seq, nb):
    x1 = (_stack_positions(x_ref, seq) + _dot(_stack_positions(om_ref, seq), wom_ref[...])
          + _dot(_stack_positions(ox_ref, seq), wox_ref[...]))
    h = _rms_rows(x1, g2_ref[...]).astype(BF16)
    gu = _dot(h, wfi_ref[...])
    a = gu[:, :D_FF]
    blocks = [p0_ref[...], p1_ref[...]] + [a[t * nb:(t + 1) * nb] for t in range(seq)]
    am2 = jnp.concatenate(blocks[:seq], axis=0)
    am1 = jnp.concatenate(blocks[1:seq + 1], axis=0)
    y = x1 + _dot(_conv_gate(a, am1, am2, gu[:, D_FF:], cw_ref, cb_ref), wd_ref[...])
    for t in range(seq):
        y_ref[t] = y[t * nb:(t + 1) * nb]
    for r in range(CONV_W - 1):
        atail_ref[r] = blocks[seq + r]


def _ffn_sample(x3, om3, ox3, w_om, w_ox, layer, norm2_g, w_ffn_in_b, conv_w, conv_b, w_down_b, prefix):
    s, n, _ = x3.shape
    km, kx = om3.shape[2], ox3.shape[2]
    nb = min(SAMPLE_FFN_SEQS, n)
    assert n % nb == 0 and s >= CONV_W - 1

    def blk(lead, width):
        return pl.BlockSpec((lead, nb, width), lambda i: (0, i, 0))

    pspec = pl.BlockSpec((nb, D_FF), lambda i: (i, 0))
    return pl.pallas_call(
        functools.partial(_ffn_sample_kernel, seq=s, nb=nb),
        grid=(n // nb,),
        in_specs=[
            blk(s, D_MODEL),
            blk(s, km),
            blk(s, kx),
            _const_spec((km, D_MODEL)),
            _const_spec((kx, D_MODEL)),
            _const_spec((1, D_MODEL)),
            _layer_spec(layer, (D_MODEL, 2 * D_FF)),
            _const_spec((CONV_W, D_FF)),
            _const_spec((1, D_FF)),
            _layer_spec(layer, (D_FF, D_MODEL)),
            pspec,
            pspec,
        ],
        out_specs=[blk(s, D_MODEL), blk(CONV_W - 1, D_FF)],
        out_shape=[
            jax.ShapeDtypeStruct((s, n, D_MODEL), F32),
            jax.ShapeDtypeStruct((CONV_W - 1, n, D_FF), F32),
        ],
        compiler_params=_params("arbitrary"),
        name="ffn_sample",
    )(x3, om3, ox3, w_om, w_ox, norm2_g.reshape(1, D_MODEL), w_ffn_in_b, conv_w, conv_b.reshape(1, D_FF), w_down_b,
      prefix[:, 0, :], prefix[:, 1, :])


def _write_query_slots(z, heads, gain, out_ref):
    for hh in range(heads):
        q = _slot_head_norm(z[:, hh * X_WIDTH:(hh + 1) * X_WIDTH], gain)
        out_ref[hh] = (q * SCALE).astype(out_ref.dtype)


def _sample_in_a_kernel(x_ref, g1_ref, wq_ref, wv_ref, wx_ref, wkvt_ref, qg_ref, kgc_ref, xqg_ref,
                        qs_ref, v_ref, xs_ref, kt_ref, vt_ref, *, seq):
    h = _rms_rows(_stack_positions(x_ref, seq), g1_ref[...]).astype(BF16)
    _write_query_slots(_dot(h, wq_ref[...]), A_HEADS, qg_ref[...], qs_ref)
    v_ref[...] = _dot(h, wv_ref[...])
    _write_query_slots(_dot(h, wx_ref[...]), X_HEADS, xqg_ref[...], xs_ref)
    kvt = _nt_dot(wkvt_ref[...], h)
    for hk in range(A_KV_HEADS):
        sl = slice(hk * HEAD_DIM, (hk + 1) * HEAD_DIM)
        kh = kvt[sl, :]
        ms = jnp.mean(kh * kh, axis=0, keepdims=True)
        kt_ref[sl, :] = kh * lax.rsqrt(ms + EPS) * kgc_ref[sl, :]
    vt_ref[...] = kvt[A_KV:, :]


def _expand_cols(w, heads, slot_of):
    d = w.shape[0]
    wh = w.reshape(d, heads, 1, HEAD_DIM)
    sel = np.zeros((heads, X_HEADS, 1), np.float32)
    for h in range(heads):
        sel[h, slot_of(h), 0] = 1.0
    return (wh * jnp.asarray(sel, w.dtype)).reshape(d, heads * X_WIDTH)


def _expand_rows(w, heads, slot_of):
    d = w.shape[1]
    wh = w.reshape(heads, 1, HEAD_DIM, d)
    sel = np.zeros((heads, X_HEADS, 1, 1), np.float32)
    for h in range(heads):
        sel[h, slot_of(h), 0, 0] = 1.0
    return (wh * jnp.asarray(sel, w.dtype)).reshape(heads * X_WIDTH, d)


def _sample_in_a(x3, norm1_g, w_in_b, q_norm, k_norm, xq_norm):
    s, n, _ = x3.shape
    rows = s * n
    wq = _expand_cols(w_in_b[:, :A_Q], A_HEADS, lambda h: h // A_GROUP)
    wv = w_in_b[:, A_Q + A_KV:A_Q + 2 * A_KV]
    wx = _expand_cols(w_in_b[:, A_Q + 2 * A_KV:], X_HEADS, lambda h: h)
    w_kvt = w_in_b[:, A_Q:A_Q + 2 * A_KV].T
    slots = lambda heads: jax.ShapeDtypeStruct((heads, rows, X_WIDTH), BF16)
    full = lambda shape: pl.BlockSpec(shape, lambda i: (0,) * len(shape))
    return pl.pallas_call(
        functools.partial(_sample_in_a_kernel, seq=s),
        grid=(1,),
        in_specs=[
            _const_spec((s, n, D_MODEL)),
            _const_spec((1, D_MODEL)),
            _const_spec(wq.shape),
            _const_spec(wv.shape),
            _const_spec(wx.shape),
            _const_spec(w_kvt.shape),
            _const_spec((1, A_KV)),
            _const_spec((A_KV, 1)),
            _const_spec((1, X_WIDTH)),
        ],
        out_specs=[
            full((A_HEADS, rows, A_KV)), full((rows, A_KV)), full((X_HEADS, rows, X_WIDTH)),
            full((A_KV, rows)), full((A_KV, rows)),
        ],
        out_shape=[
            slots(A_HEADS),
            jax.ShapeDtypeStruct((rows, A_KV), F32),
            slots(X_HEADS),
            jax.ShapeDtypeStruct((A_KV, rows), F32),
            jax.ShapeDtypeStruct((A_KV, rows), F32),
        ],
        compiler_params=_params("arbitrary"),
        name="sample_in_a",
    )(x3, norm1_g.reshape(1, D_MODEL), wq, wv, wx, w_kvt, jnp.tile(q_norm, 4).reshape(1, A_KV),
      jnp.tile(k_norm, 4).reshape(A_KV, 1), jnp.tile(xq_norm, 4).reshape(1, X_WIDTH))


def _sample_in_b_kernel(x_ref, g1_ref, wuv_ref, wx_ref, vg_ref, coef_ref, gbias_ref, xqg_ref, o_ref, vn_ref, xs_ref,
                        *, seq, n):
    h = _rms_rows(_stack_positions(x_ref, seq), g1_ref[...]).astype(BF16)
    uv = _gelu_exact(_dot(h, wuv_ref[...]))
    u = uv[:, :B_WIDTH]
    vn = _rms_rows(uv[:, B_WIDTH:], vg_ref[...])
    vn_ref[...] = vn
    for t in range(seq):
        mixed = gbias_ref[t:t + 1, :] + coef_ref[t * seq:t * seq + 1, :] * vn[0:n]
        for j in range(1, t + 1):
            mixed = mixed + coef_ref[t * seq + j:t * seq + j + 1, :] * vn[j * n:(j + 1) * n]
        o_ref[t * n:(t + 1) * n, :] = (u[t * n:(t + 1) * n] * mixed).astype(o_ref.dtype)
    _write_query_slots(_dot(h, wx_ref[...]), X_HEADS, xqg_ref[...], xs_ref)


def _sample_in_b(x3, norm1_g, w_in_b, v_norm, w_s, b_s, xq_norm):
    s, n, _ = x3.shape
    rows = s * n
    wuv = w_in_b[:, :2 * B_WIDTH]
    wx = _expand_cols(w_in_b[:, 2 * B_WIDTH:], X_HEADS, lambda h: h)
    coef = jnp.repeat(w_s[:, :s, :s].transpose(1, 2, 0).reshape(s * s, B_GROUPS), B_GROUP_DIM, axis=1)
    gbias = jnp.repeat(b_s[:, :s].T, B_GROUP_DIM, axis=1)
    full = lambda shape: pl.BlockSpec(shape, lambda i: (0,) * len(shape))
    return pl.pallas_call(
        functools.partial(_sample_in_b_kernel, seq=s, n=n),
        grid=(1,),
        in_specs=[
            _const_spec((s, n, D_MODEL)),
            _const_spec((1, D_MODEL)),
            _const_spec(wuv.shape),
            _const_spec(wx.shape),
            _const_spec((1, B_WIDTH)),
            _const_spec((s * s, B_WIDTH)),
            _const_spec((s, B_WIDTH)),
            _const_spec((1, X_WIDTH)),
        ],
        out_specs=[full((rows, B_WIDTH)), full((rows, B_WIDTH)), full((X_HEADS, rows, X_WIDTH))],
        out_shape=[
            jax.ShapeDtypeStruct((rows, B_WIDTH), BF16),
            jax.ShapeDtypeStruct((rows, B_WIDTH), F32),
            jax.ShapeDtypeStruct((X_HEADS, rows, X_WIDTH), BF16),
        ],
        compiler_params=_params("arbitrary"),
        name="sample_in_b",
    )(x3, norm1_g.reshape(1, D_MODEL), wuv, wx, v_norm.reshape(1, B_WIDTH), coef, gbias,
      jnp.tile(xq_norm, 4).reshape(1, X_WIDTH))


def _sample_attn_win_kernel(qs_ref, ktn_ref, vtn_ref, vn_ref, slope_ref, sink_ref, ck_ref, cv_ref,
                            o_ref, wk_ref, wv_ref, *, sb, seq):
    rows = A_HEADS * seq
    tq = _div_mod(lax.broadcasted_iota(jnp.int32, (rows, WINDOW), 0), seq)[1]
    lane = lax.broadcasted_iota(jnp.int32, (rows, WINDOW), 1)
    slope = slope_ref[...]
    sink = sink_ref[...]
    d1 = tq + WINDOW - lane
    valid1 = d1 < WINDOW
    bias1 = slope * d1.astype(F32)
    new_seq, new_pos = _div_mod(lane, seq)
    d2 = tq - new_pos
    causal2 = d2 >= 0
    bias2 = slope * d2.astype(F32)
    ktn = ktn_ref[...]
    vtn = vtn_ref[...]
    ktn_b = ktn.astype(BF16)
    vn_b = vn_ref[...].astype(BF16)
    keep_old = lax.broadcasted_iota(jnp.int32, (A_KV, WINDOW), 1) < WINDOW - seq

    def body(n, carry):
        q = qs_ref[n]
        ckt = ck_ref[n]
        cvt = cv_ref[n]
        s1 = jnp.where(valid1, _dot(q, ckt.astype(BF16)) - bias1, NEG)
        s2 = jnp.where(causal2 & (new_seq == n), _dot(q, ktn_b) - bias2, NEG)
        m = jnp.maximum(jnp.maximum(jnp.max(s1, axis=-1, keepdims=True), jnp.max(s2, axis=-1, keepdims=True)), sink)
        e1 = jnp.exp(s1 - m)
        e2 = jnp.exp(s2 - m)
        den = jnp.sum(e1, axis=-1, keepdims=True) + jnp.sum(e2, axis=-1, keepdims=True) + jnp.exp(sink - m)
        o = _nt_dot(e1.astype(BF16), cvt.astype(BF16)) + _dot(e2.astype(BF16), vn_b)
        o_ref[n] = (o / den).astype(o_ref.dtype)
        shift_new = WINDOW - seq - n * seq
        wk_ref[n] = jnp.where(keep_old, pltpu.roll(ckt, WINDOW - seq, 1), pltpu.roll(ktn, shift_new, 1))
        wv_ref[n] = jnp.where(keep_old, pltpu.roll(cvt, WINDOW - seq, 1), pltpu.roll(vtn, shift_new, 1))
        return carry

    lax.fori_loop(0, sb, body, 0, unroll=SAMPLE_SEQ_UNROLL)


def _sample_attn_win(qs, kt_new, vt_new, v_new, slope_rows, sink_rows, ckt, cvt, layer, seq):
    n = qs.shape[0]
    rows = A_HEADS * seq
    assert LANES % seq == 0
    sb = LANES // seq
    assert n % sb == 0

    def seqs(*tail):
        return pl.BlockSpec((sb,) + tail, lambda i: (i,) + (0,) * len(tail))

    cache = pl.BlockSpec((None, sb, A_KV, WINDOW), lambda i: (layer, i, 0, 0))
    win = jax.ShapeDtypeStruct((n, A_KV, WINDOW), F32)
    return pl.pallas_call(
        functools.partial(_sample_attn_win_kernel, sb=sb, seq=seq),
        grid=(n // sb,),
        in_specs=[
            seqs(rows, A_KV),
            pl.BlockSpec((A_KV, LANES), lambda i: (0, i)),
            pl.BlockSpec((A_KV, LANES), lambda i: (0, i)),
            pl.BlockSpec((LANES, A_KV), lambda i: (i, 0)),
            _const_spec((rows, 1)),
            _const_spec((rows, 1)),
            cache,
            cache,
        ],
        out_specs=[seqs(rows, A_KV), seqs(A_KV, WINDOW), seqs(A_KV, WINDOW)],
        out_shape=[jax.ShapeDtypeStruct((n, rows, A_KV), BF16), win, win],
        compiler_params=_params("arbitrary"),
        name="sample_attn_win",
    )(qs, kt_new, vt_new, v_new, slope_rows, sink_rows, ckt, cvt)


def _sample_attn_mem_kernel(xs_ref, mk_ref, mv_ref, xo_ref, *, sb):
    def body(n, carry):
        s = _dot(xs_ref[n], mk_ref[n].astype(BF16))
        m = jnp.max(s, axis=-1, keepdims=True)
        e = jnp.exp(s - m)
        den = jnp.sum(e, axis=-1, keepdims=True)
        xo_ref[n] = (_nt_dot(e.astype(BF16), mv_ref[n].astype(BF16)) / den).astype(xo_ref.dtype)
        return carry

    lax.fori_loop(0, sb, body, 0, unroll=SAMPLE_SEQ_UNROLL)


def _sample_attn_mem(xs, mkt, mvt, layer):
    n, xrows, _ = xs.shape
    sb = min(SAMPLE_MEM_SEQS, n)
    assert n % sb == 0
    qspec = pl.BlockSpec((sb, xrows, X_WIDTH), lambda i: (i, 0, 0))
    cache = pl.BlockSpec((None, sb, X_WIDTH, N_MEM), lambda i: (layer, i, 0, 0))
    return pl.pallas_call(
        functools.partial(_sample_attn_mem_kernel, sb=sb),
        grid=(n // sb,),
        in_specs=[qspec, cache, cache],
        out_specs=qspec,
        out_shape=jax.ShapeDtypeStruct((n, xrows, X_WIDTH), BF16),
        compiler_params=_params("arbitrary"),
        name="sample_attn_mem",
    )(xs, mkt, mvt)


def _slots_to_seq(a, seq, n):
    hh, _, w = a.shape
    return a.reshape(hh, seq, n, w).transpose(2, 0, 1, 3).reshape(n, hh * seq, w)


def _seq_to_cols(a, heads, seq):
    n, _, w = a.shape
    return a.reshape(n, heads, seq, w).transpose(2, 0, 1, 3).reshape(seq, n, heads * w)


def _feature_major(cache):
    l, n, p, kv, d = cache.shape
    return cache.transpose(0, 1, 3, 4, 2).reshape(l, n, kv * d, p)


def kernel(x_prompt, x_sample, cache_win_k, cache_win_v, cache_mem_k, cache_mem_v, state_conv, mem_prompt,
           norm1_g, norm2_g, mem_norm_g, w_in_a, q_norm_a, k_norm_a, sinks_a, w_out_a, w_in_b, v_norm_b, w_s_b,
           b_s_b, w_out_b, w_mem_kv, xq_norm, xk_norm, w_ffn_in, conv_w, conv_b, w_down):
    depth = norm1_g.shape[0]
    assert depth == 2 and w_in_a.shape[0] == 1 and w_in_b.shape[0] == 1
    nb = x_prompt.shape[0]
    ns, seq, _ = x_sample.shape
    assert cache_win_k.shape[2] == WINDOW

    w_in_a_b = w_in_a[0].astype(BF16)
    w_in_b_b = w_in_b[0].astype(BF16)
    w_out_a_b = w_out_a[0].astype(BF16)
    w_out_b_b = w_out_b[0].astype(BF16)
    w_mem_kv_b = w_mem_kv.astype(BF16)
    w_ffn_in_b = w_ffn_in.astype(BF16)
    w_down_b = w_down.astype(BF16)

    mk_all, mv_all, mem_kt_p, mem_vt_p = _memory_kv(mem_prompt, mem_norm_g, w_mem_kv_b, xk_norm)
    x2, win_k_p, win_v_p, conv_p0 = _layer_a_prompt(
        x_prompt, 0, norm1_g[0], _reorder_heads(w_in_a_b, 1), q_norm_a[0], k_norm_a[0], xq_norm[0],
        sinks_a[0], mk_all, mv_all, _reorder_heads(w_out_a_b, 0),
        norm2_g[0], w_ffn_in_b, conv_w[0], conv_b[0], w_down_b)
    y_prompt, chunk_v_p, conv_p1 = _layer_b_prompt(
        x2, 1, norm1_g[1], w_in_b_b, v_norm_b[0], w_s_b[0], b_s_b[0], xq_norm[1], mk_all, mv_all, w_out_b_b,
        norm2_g[1], w_ffn_in_b, conv_w[1], conv_b[1], w_down_b)

    xs3 = x_sample.transpose(1, 0, 2)
    mkt = _feature_major(cache_mem_k)
    mvt = _feature_major(cache_mem_v)

    def by_seq_cols(a):
        return a.reshape(a.shape[0], seq, ns).transpose(0, 2, 1).reshape(a.shape[0], ns * seq)

    qs, v_new, xslots, kt_new, vt_new = _sample_in_a(xs3, norm1_g[0], w_in_a_b, q_norm_a[0], k_norm_a[0], xq_norm[0])
    slope_rows = jnp.repeat(jnp.asarray(SLOPES, F32), seq).reshape(A_HEADS * seq, 1)
    sink_rows = jnp.repeat(sinks_a[0].astype(F32), seq).reshape(A_HEADS * seq, 1)
    o_s, win_kt_s, win_vt_s = _sample_attn_win(
        _slots_to_seq(qs, seq, ns), by_seq_cols(kt_new), by_seq_cols(vt_new),
        v_new.reshape(seq, ns, A_KV).transpose(1, 0, 2).reshape(ns * seq, A_KV), slope_rows, sink_rows,
        _feature_major(cache_win_k), _feature_major(cache_win_v), 0, seq)
    xo_s = _sample_attn_mem(_slots_to_seq(xslots, seq, ns), mkt, mvt, 0)
    ys1, conv_s0 = _ffn_sample(
        xs3, _seq_to_cols(o_s, A_HEADS, seq), _seq_to_cols(xo_s, X_HEADS, seq),
        _expand_rows(w_out_a_b[:A_Q], A_HEADS, lambda h: h // A_GROUP), _expand_rows(w_out_a_b[A_Q:], X_HEADS, lambda h: h),
        0, norm2_g[0], w_ffn_in_b, conv_w[0], conv_b[0], w_down_b, state_conv[0])

    o_b, vn_s, xslots = _sample_in_b(ys1, norm1_g[1], w_in_b_b, v_norm_b[0], w_s_b[0], b_s_b[0], xq_norm[1])
    xo_s = _sample_attn_mem(_slots_to_seq(xslots, seq, ns), mkt, mvt, 1)
    ys2, conv_s1 = _ffn_sample(
        ys1, o_b.reshape(seq, ns, B_WIDTH), _seq_to_cols(xo_s, X_HEADS, seq),
        w_out_b_b[:B_WIDTH], _expand_rows(w_out_b_b[B_WIDTH:], X_HEADS, lambda h: h),
        1, norm2_g[1], w_ffn_in_b, conv_w[1], conv_b[1], w_down_b, state_conv[1])

    def position_major(ft):
        lead = ft.shape[:-2]
        nd = len(lead)
        return ft.reshape(*lead, A_KV_HEADS, HEAD_DIM, ft.shape[-1]).transpose(*range(nd), nd + 2, nd, nd + 1)

    return (
        y_prompt,
        ys2.transpose(1, 0, 2),
        position_major(win_k_p)[None],
        position_major(win_v_p)[None],
        chunk_v_p.reshape(1, nb, CHUNK, B_WIDTH),
        position_major(mem_kt_p),
        position_major(mem_vt_p),
        jnp.stack([conv_p0, conv_p1]),
        position_major(win_kt_s)[None],
        position_major(win_vt_s)[None],
        vn_s.reshape(seq, ns, B_WIDTH).transpose(1, 0, 2)[None],
        jnp.stack([conv_s0.transpose(1, 0, 2), conv_s1.transpose(1, 0, 2)]),
    )
```

```python
import functools
import math

import numpy as np
import jax
import jax.numpy as jnp
from jax import lax
from jax.experimental import pallas as pl
from jax.experimental.pallas import tpu as pltpu

F32 = jnp.float32
BF16 = jnp.bfloat16

D_MODEL = 1024
HEAD_DIM = 64
A_HEADS = 12
A_KV_HEADS = 4
A_GROUP = A_HEADS // A_KV_HEADS
WINDOW = 128
BLOCK = 128
B_WIDTH = 768
B_GROUPS = 4
B_GROUP_DIM = B_WIDTH // B_GROUPS
CHUNK = 128
X_HEADS = 4
X_WIDTH = X_HEADS * HEAD_DIM
N_MEM = 256
D_FF = 2816
CONV_W = 3
A_Q = A_HEADS * HEAD_DIM
A_KV = A_KV_HEADS * HEAD_DIM
EPS = 1e-6
NEG = -1e30
SCALE = HEAD_DIM ** -0.5
LOG2E = float(np.float32(1.0 / np.log(2.0)))
SQRT_HALF = float(np.sqrt(0.5).astype(np.float32))

LANES = 128
VMEM_LIMIT_BYTES = 56 * 1024 * 1024
GATE_WINDOW_START = (0, 128, 384, 512)
assert B_GROUP_DIM == 192 and B_GROUPS == 4
PROMPT_ROWS = 512
SAMPLE_MEM_SEQS = 16
SAMPLE_FFN_SEQS = 64
SAMPLE_SEQ_UNROLL = 4


def _alibi_slopes(n):
    def pow2(m):
        start = 2.0 ** (-8.0 / m)
        return [start ** (i + 1) for i in range(m)]

    p = 2 ** int(math.floor(math.log2(n)))
    s = pow2(p)
    if p < n:
        s = s + pow2(2 * p)[0::2][: n - p]
    return [float(np.float32(v)) for v in s]


SLOPES = _alibi_slopes(A_HEADS)

Q_LANE_ORDER = (0, 3, 1, 4, 2, 5, 6, 9, 7, 10, 8, 11)
assert all(Q_LANE_ORDER[p] // A_GROUP == 2 * (p // (2 * A_GROUP)) + p % 2 for p in range(A_HEADS))


_SLOTS = A_KV_HEADS // 2
assert Q_LANE_ORDER == tuple(np.arange(A_HEADS).reshape(_SLOTS, 2, A_GROUP).swapaxes(1, 2).reshape(-1).tolist())


def _reorder_heads(w, axis):
    lead, tail = w.shape[:axis], w.shape[axis + 1:]
    q = lax.slice_in_dim(w, 0, A_Q, axis=axis).reshape(*lead, _SLOTS, 2, A_GROUP, HEAD_DIM, *tail)
    q = jnp.swapaxes(q, axis + 1, axis + 2).reshape(*lead, A_Q, *tail)
    return jnp.concatenate([q, lax.slice_in_dim(w, A_Q, w.shape[axis], axis=axis)], axis=axis)


def _params(*sem):
    return pltpu.CompilerParams(dimension_semantics=sem, vmem_limit_bytes=VMEM_LIMIT_BYTES)


def _const_spec(shape):
    nd = len(shape)
    return pl.BlockSpec(shape, lambda *_: (0,) * nd, pipeline_mode=pl.Buffered(1))


def _layer_spec(layer, shape):
    nd = len(shape)
    return pl.BlockSpec((None,) + tuple(shape), lambda *_: (layer,) + (0,) * nd, pipeline_mode=pl.Buffered(1))


def _rms_rows(x, g):
    ms = jnp.mean(x * x, axis=-1, keepdims=True)
    return x * lax.rsqrt(ms + EPS) * g


def _pair_head_norm(z, g2):
    t, w = z.shape
    lo = lax.broadcasted_iota(jnp.int32, (t, LANES), 1) < HEAD_DIM
    out = []
    for p in range(w // LANES):
        zz = z[:, p * LANES:(p + 1) * LANES]
        sq = zz * zz
        s_lo = jnp.sum(jnp.where(lo, sq, 0.0), axis=-1, keepdims=True)
        s_hi = jnp.sum(jnp.where(lo, 0.0, sq), axis=-1, keepdims=True)
        r_lo = lax.rsqrt(s_lo * (1.0 / HEAD_DIM) + EPS)
        r_hi = lax.rsqrt(s_hi * (1.0 / HEAD_DIM) + EPS)
        out.append(zz * jnp.where(lo, r_lo, r_hi) * g2)
    return out


def _nt_dot(a, b):
    return lax.dot_general(a, b, (((1,), (1,)), ((), ())), preferred_element_type=F32)


def _dot(a, b):
    return jnp.dot(a, b, preferred_element_type=F32)


def _div_mod(i, n):
    if n & (n - 1) == 0:
        return i >> (n.bit_length() - 1), i & (n - 1)
    return i // n, i % n


def _gelu_exact(x):
    return 0.5 * x * (1.0 + lax.erf(x * SQRT_HALF))


def _conv_gate(a, am1, am2, up, cw_ref, cb_ref):
    c = cb_ref[...] + ((cw_ref[0:1, :] * am2 + cw_ref[1:2, :] * am1) + cw_ref[2:3, :] * a)
    return ((c * (1.0 / (1.0 + jnp.exp(-c)))) * up).astype(BF16)


def _memkv_kernel(mem_ref, g_ref, w_ref, kg_ref, mkb_ref, mvb_ref, mkt_ref, mvt_ref):
    m = mem_ref[0]
    h = _rms_rows(m, g_ref[0]).astype(BF16)
    kv = _dot(h, w_ref[0])
    mk = jnp.concatenate(_pair_head_norm(kv[:, :X_WIDTH], kg_ref[0]), axis=1)
    mv = kv[:, X_WIDTH:]
    mkb_ref[0, 0] = mk.astype(BF16)
    mvb_ref[0, 0] = mv.astype(BF16)
    mkt_ref[0, 0] = mk.T
    mvt_ref[0, 0] = mv.T


def _memory_kv(mem, mem_norm_g, w_mem_kv_b, xk_norm):
    depth = w_mem_kv_b.shape[0]
    n = mem.shape[0]
    g = mem_norm_g.reshape(depth, 1, D_MODEL)
    kg2 = jnp.tile(xk_norm, (1, 2)).reshape(depth, 1, LANES)
    spec = lambda rows, cols: pl.BlockSpec((1, 1, rows, cols), lambda l, b: (l, b, 0, 0))
    return pl.pallas_call(
        _memkv_kernel,
        grid=(depth, n),
        in_specs=[
            pl.BlockSpec((1, N_MEM, D_MODEL), lambda l, b: (b, 0, 0)),
            pl.BlockSpec((1, 1, D_MODEL), lambda l, b: (l, 0, 0)),
            pl.BlockSpec((1, D_MODEL, 2 * X_WIDTH), lambda l, b: (l, 0, 0)),
            pl.BlockSpec((1, 1, LANES), lambda l, b: (l, 0, 0)),
        ],
        out_specs=[spec(N_MEM, X_WIDTH), spec(N_MEM, X_WIDTH), spec(X_WIDTH, N_MEM), spec(X_WIDTH, N_MEM)],
        out_shape=[
            jax.ShapeDtypeStruct((depth, n, N_MEM, X_WIDTH), BF16),
            jax.ShapeDtypeStruct((depth, n, N_MEM, X_WIDTH), BF16),
            jax.ShapeDtypeStruct((depth, n, X_WIDTH, N_MEM), F32),
            jax.ShapeDtypeStruct((depth, n, X_WIDTH, N_MEM), F32),
        ],
        compiler_params=_params("arbitrary", "arbitrary"),
        name="memory_kv",
    )(mem, g, w_mem_kv_b, kg2)


def _half_masked_rows(pair, lo):
    return jnp.concatenate([jnp.where(lo, pair, 0.0), jnp.where(lo, 0.0, pair)], axis=0).astype(BF16)


def _prompt_mem_attention(xq_pairs, mk_ref, mv_ref, ocat, col0):
    t = xq_pairs[0].shape[0]
    lo = lax.broadcasted_iota(jnp.int32, (t, LANES), 1) < HEAD_DIM
    for j, pair in enumerate(xq_pairs):
        mk = mk_ref[0, 0, :, j * LANES:(j + 1) * LANES]
        mv = mv_ref[0, 0, :, j * LANES:(j + 1) * LANES]
        s = _nt_dot(_half_masked_rows(pair * (SCALE * LOG2E), lo), mk)
        m = jnp.max(s, axis=-1, keepdims=True)
        e = jnp.exp2(s - m)
        den = jnp.sum(e, axis=-1, keepdims=True)
        o = _dot(e.astype(BF16), mv) / den
        ocat[:, col0 + j * LANES:col0 + (j + 1) * LANES] = jnp.where(lo, o[:t], o[t:]).astype(BF16)


def _mixer_a_body(t, x, sinks_ref, g1_ref, win_ref, qg_ref, kg_ref, xqg_ref, mk_ref, mv_ref, wout_ref,
                  wk_ref, wv_ref, kbuf, vbuf, ocat, tb):
    h = _rms_rows(x, g1_ref[...]).astype(BF16)
    z = _dot(h, win_ref[...])
    q_pairs = _pair_head_norm(z[:, :A_Q], qg_ref[...])
    k_pairs = _pair_head_norm(z[:, A_Q:A_Q + A_KV], kg_ref[...])
    v = z[:, A_Q + A_KV:A_Q + 2 * A_KV]
    xq_pairs = _pair_head_norm(z[:, A_Q + 2 * A_KV:], xqg_ref[...])

    for p, kp in enumerate(k_pairs):
        wk_ref[0, p * LANES:(p + 1) * LANES, :] = kp[tb - BLOCK:, :].T
        kbuf[BLOCK:BLOCK + tb, p * LANES:(p + 1) * LANES] = kp.astype(BF16)
    wv_ref[0] = v[tb - BLOCK:, :].T
    vbuf[BLOCK:BLOCK + tb, :] = v.astype(BF16)

    qi = lax.broadcasted_iota(jnp.int32, (BLOCK, 2 * BLOCK), 0)
    kj = lax.broadcasted_iota(jnp.int32, (BLOCK, 2 * BLOCK), 1)
    d = qi + BLOCK - kj
    in_window = (d >= 0) & (d < WINDOW)
    key_pos = lax.broadcasted_iota(jnp.int32, (1, 2 * BLOCK), 1).astype(F32)
    query_pos = (lax.broadcasted_iota(jnp.int32, (BLOCK, 1), 0) + BLOCK).astype(F32)

    lo = lax.broadcasted_iota(jnp.int32, (BLOCK, LANES), 1) < HEAD_DIM
    pairs_per_slot = A_HEADS // A_KV_HEADS
    for s in range(tb // BLOCK):
        rows = slice(s * BLOCK, (s + 1) * BLOCK)
        first_key = jnp.where(t > 0, 0, BLOCK) if s == 0 else 0
        valid = in_window & (kj >= first_key)
        for j in range(A_KV // LANES):
            kw = kbuf[s * BLOCK:(s + 2) * BLOCK, j * LANES:(j + 1) * LANES]
            vw = vbuf[s * BLOCK:(s + 2) * BLOCK, j * LANES:(j + 1) * LANES]
            q_rows = jnp.concatenate(
                [_half_masked_rows(q_pairs[pairs_per_slot * j + c][rows] * (SCALE * LOG2E), lo)
                 for c in range(pairs_per_slot)], axis=0)
            sc_all = _nt_dot(q_rows, kw)
            e_parts, den_parts = [], []
            for r in range(2 * pairs_per_slot):
                hq = Q_LANE_ORDER[2 * pairs_per_slot * j + r]
                slope2 = SLOPES[hq] * LOG2E
                sc = jnp.where(valid, sc_all[r * BLOCK:(r + 1) * BLOCK] + slope2 * key_pos, NEG)
                sink = sinks_ref[hq] * LOG2E + slope2 * query_pos
                m = jnp.maximum(jnp.max(sc, axis=-1, keepdims=True), sink)
                e = jnp.exp2(sc - m)
                den_parts.append(jnp.sum(e, axis=-1, keepdims=True) + jnp.exp2(sink - m))
                e_parts.append(e.astype(BF16))
            o_all = _dot(jnp.concatenate(e_parts, axis=0), vw)
            for c in range(pairs_per_slot):
                o_lo = o_all[2 * c * BLOCK:(2 * c + 1) * BLOCK] / den_parts[2 * c]
                o_hi = o_all[(2 * c + 1) * BLOCK:(2 * c + 2) * BLOCK] / den_parts[2 * c + 1]
                col = pairs_per_slot * j + c
                ocat[rows, col * LANES:(col + 1) * LANES] = jnp.where(lo, o_lo, o_hi).astype(BF16)

    _prompt_mem_attention(xq_pairs, mk_ref, mv_ref, ocat, A_Q)
    x1 = x + _dot(ocat[...], wout_ref[...])

    kbuf[0:BLOCK, :] = kbuf[tb:tb + BLOCK, :]
    vbuf[0:BLOCK, :] = vbuf[tb:tb + BLOCK, :]
    return x1


def _ffn_body(x, g2_ref, wfi_ref, cw_ref, cb_ref, wd_ref, tail_ref, tb):
    h = _rms_rows(x, g2_ref[...]).astype(BF16)
    gu = _dot(h, wfi_ref[...])
    a = gu[:, :D_FF]
    row = lax.broadcasted_iota(jnp.int32, (tb, 1), 0)
    prev_last = tail_ref[0, 7:8, :]
    am1 = jnp.where(row == 0, prev_last, pltpu.roll(a, 1, 0))
    am2 = jnp.where(row == 0, tail_ref[0, 6:7, :], jnp.where(row == 1, prev_last, pltpu.roll(a, 2, 0)))
    tail_ref[0] = a[tb - 8:, :]
    return x + _dot(_conv_gate(a, am1, am2, gu[:, D_FF:], cw_ref, cb_ref), wd_ref[...])


def _layer_a_kernel(sinks_ref, x_ref, g1_ref, win_ref, qg_ref, kg_ref, xqg_ref, mk_ref, mv_ref, wout_ref,
                    g2_ref, wfi_ref, cw_ref, cb_ref, wd_ref, y_ref, wk_ref, wv_ref, tail_ref, kbuf, vbuf, ocat, *, tb):
    t = pl.program_id(1)

    @pl.when(t == 0)
    def _():
        kbuf[0:BLOCK, :] = jnp.zeros((BLOCK, A_KV), BF16)
        vbuf[0:BLOCK, :] = jnp.zeros((BLOCK, A_KV), BF16)
        tail_ref[0] = jnp.zeros((8, D_FF), F32)

    x1 = _mixer_a_body(t, x_ref[0], sinks_ref, g1_ref, win_ref, qg_ref, kg_ref, xqg_ref, mk_ref, mv_ref, wout_ref,
                       wk_ref, wv_ref, kbuf, vbuf, ocat, tb)
    y_ref[0] = _ffn_body(x1, g2_ref, wfi_ref, cw_ref, cb_ref, wd_ref, tail_ref, tb)


def _layer_specs(tb, layer):
    xspec = pl.BlockSpec((1, tb, D_MODEL), lambda b, i: (b, i, 0))
    mspec = pl.BlockSpec((1, 1, N_MEM, X_WIDTH), lambda b, i: (layer, b, 0, 0))
    per_seq = lambda rows, width: pl.BlockSpec((1, rows, width), lambda b, i: (b, 0, 0))
    return xspec, mspec, per_seq(8, D_FF), per_seq


def _ffn_specs(layer):
    return [
        _const_spec((1, D_MODEL)),
        _layer_spec(layer, (D_MODEL, 2 * D_FF)),
        _const_spec((CONV_W, D_FF)),
        _const_spec((1, D_FF)),
        _layer_spec(layer, (D_FF, D_MODEL)),
    ]


def _layer_a_prompt(x, layer, norm1_g, w_in_b, q_norm, k_norm, xq_norm, sinks, mk_all, mv_all, w_out_b,
                    norm2_g, w_ffn_in_b, conv_w, conv_b, w_down_b):
    n, t, _ = x.shape
    tb = min(PROMPT_ROWS, t)
    a_in = w_in_b.shape[1]
    xspec, mspec, tailspec, per_seq = _layer_specs(tb, layer)
    wspec = per_seq(A_KV, BLOCK)
    y, wk, wv, tail = pl.pallas_call(
        functools.partial(_layer_a_kernel, tb=tb),
        grid=(n, t // tb),
        in_specs=[
            pl.BlockSpec(memory_space=pltpu.SMEM),
            xspec,
            _const_spec((1, D_MODEL)),
            _const_spec((D_MODEL, a_in)),
            _const_spec((1, LANES)),
            _const_spec((1, LANES)),
            _const_spec((1, LANES)),
            mspec,
            mspec,
            _const_spec((A_Q + X_WIDTH, D_MODEL)),
        ] + _ffn_specs(layer),
        out_specs=[xspec, wspec, wspec, tailspec],
        out_shape=[
            jax.ShapeDtypeStruct((n, t, D_MODEL), F32),
            jax.ShapeDtypeStruct((n, A_KV, BLOCK), F32),
            jax.ShapeDtypeStruct((n, A_KV, BLOCK), F32),
            jax.ShapeDtypeStruct((n, 8, D_FF), F32),
        ],
        scratch_shapes=[
            pltpu.VMEM((tb + BLOCK, A_KV), BF16),
            pltpu.VMEM((tb + BLOCK, A_KV), BF16),
            pltpu.VMEM((tb, A_Q + X_WIDTH), BF16),
        ],
        compiler_params=_params("arbitrary", "arbitrary"),
        name="layer_a_prompt",
    )(sinks, x, norm1_g.reshape(1, D_MODEL), w_in_b, jnp.tile(q_norm, 2).reshape(1, LANES),
      jnp.tile(k_norm, 2).reshape(1, LANES), jnp.tile(xq_norm, 2).reshape(1, LANES), mk_all, mv_all, w_out_b,
      norm2_g.reshape(1, D_MODEL), w_ffn_in_b, conv_w, conv_b.reshape(1, D_FF), w_down_b)
    return y, wk, wv, tail[:, 8 - (CONV_W - 1):, :]


def _mixer_b_body(x, g1_ref, win_ref, vg_ref, ws_ref, bias_ref, xqg_ref, mk_ref, mv_ref, wout_ref, cv_ref, ocat, tb):
    h = _rms_rows(x, g1_ref[...]).astype(BF16)
    z = _dot(h, win_ref[...])
    uv = _gelu_exact(z[:, :2 * B_WIDTH])
    u = uv[:, :B_WIDTH]
    vn = _rms_rows(uv[:, B_WIDTH:], vg_ref[...])
    cv_ref[0] = vn[tb - CHUNK:, :]
    xq_pairs = _pair_head_norm(z[:, 2 * B_WIDTH:], xqg_ref[...])

    wi = lax.broadcasted_iota(jnp.int32, (CHUNK, CHUNK), 0)
    wj = lax.broadcasted_iota(jnp.int32, (CHUNK, CHUNK), 1)
    causal = wi >= wj
    w_tril = [jnp.where(causal, ws_ref[g], 0.0).astype(BF16) for g in range(B_GROUPS)]
    lo = lax.broadcasted_iota(jnp.int32, (CHUNK, LANES), 1) < HEAD_DIM
    bias = bias_ref[...]
    for c in range(tb // CHUNK):
        vc = vn[c * CHUNK:(c + 1) * CHUNK, :].astype(BF16)
        r = [_dot(w_tril[g], vc[:, c0:c0 + 256]) for g, c0 in enumerate(GATE_WINDOW_START)]
        mixed = jnp.concatenate(
            [r[0][:, :128], jnp.where(lo, r[0][:, 128:], r[1][:, :128]), r[1][:, 128:],
             r[2][:, :128], jnp.where(lo, r[2][:, 128:], r[3][:, :128]), r[3][:, 128:]], axis=1)
        o = u[c * CHUNK:(c + 1) * CHUNK, :] * (mixed + bias)
        ocat[c * CHUNK:(c + 1) * CHUNK, 0:B_WIDTH] = o.astype(BF16)

    _prompt_mem_attention(xq_pairs, mk_ref, mv_ref, ocat, B_WIDTH)
    return x + _dot(ocat[...], wout_ref[...])


def _layer_b_kernel(x_ref, g1_ref, win_ref, vg_ref, ws_ref, bias_ref, xqg_ref, mk_ref, mv_ref, wout_ref,
                    g2_ref, wfi_ref, cw_ref, cb_ref, wd_ref, y_ref, cv_ref, tail_ref, ocat, *, tb):
    @pl.when(pl.program_id(1) == 0)
    def _():
        tail_ref[0] = jnp.zeros((8, D_FF), F32)

    x1 = _mixer_b_body(x_ref[0], g1_ref, win_ref, vg_ref, ws_ref, bias_ref, xqg_ref, mk_ref, mv_ref, wout_ref,
                       cv_ref, ocat, tb)
    y_ref[0] = _ffn_body(x1, g2_ref, wfi_ref, cw_ref, cb_ref, wd_ref, tail_ref, tb)


def _gate_bias_rows(b_s):
    return jnp.repeat(b_s.T, B_GROUP_DIM, axis=1)


def _layer_b_prompt(x, layer, norm1_g, w_in_b, v_norm, w_s, b_s, xq_norm, mk_all, mv_all, w_out_b,
                    norm2_g, w_ffn_in_b, conv_w, conv_b, w_down_b):
    n, t, _ = x.shape
    tb = min(PROMPT_ROWS, t)
    b_in = w_in_b.shape[1]
    xspec, mspec, tailspec, per_seq = _layer_specs(tb, layer)
    y, cv, tail = pl.pallas_call(
        functools.partial(_layer_b_kernel, tb=tb),
        grid=(n, t // tb),
        in_specs=[
            xspec,
            _const_spec((1, D_MODEL)),
            _const_spec((D_MODEL, b_in)),
            _const_spec((1, B_WIDTH)),
            _const_spec((B_GROUPS, CHUNK, CHUNK)),
            _const_spec((CHUNK, B_WIDTH)),
            _const_spec((1, LANES)),
            mspec,
            mspec,
            _const_spec((B_WIDTH + X_WIDTH, D_MODEL)),
        ] + _ffn_specs(layer),
        out_specs=[xspec, per_seq(CHUNK, B_WIDTH), tailspec],
        out_shape=[
            jax.ShapeDtypeStruct((n, t, D_MODEL), F32),
            jax.ShapeDtypeStruct((n, CHUNK, B_WIDTH), F32),
            jax.ShapeDtypeStruct((n, 8, D_FF), F32),
        ],
        scratch_shapes=[pltpu.VMEM((tb, B_WIDTH + X_WIDTH), BF16)],
        compiler_params=_params("arbitrary", "arbitrary"),
        name="layer_b_prompt",
    )(x, norm1_g.reshape(1, D_MODEL), w_in_b, v_norm.reshape(1, B_WIDTH), w_s, _gate_bias_rows(b_s),
      jnp.tile(xq_norm, 2).reshape(1, LANES), mk_all, mv_all, w_out_b,
      norm2_g.reshape(1, D_MODEL), w_ffn_in_b, conv_w, conv_b.reshape(1, D_FF), w_down_b)
    return y, cv, tail[:, 8 - (CONV_W - 1):, :]


def _stack_positions(ref, seq):
    return jnp.concatenate([ref[t] for t in range(seq)], axis=0)


def _ffn_sample_kernel(x_ref, om_ref, ox_ref, wom_ref, wox_ref, g2_ref, wfi_ref, cw_ref, cb_ref, wd_ref, p0_ref, p1_ref,
                       y_ref, atail_ref, *, seq, nb, mixer_slots):
    om = _stack_positions(om_ref, seq)
    if mixer_slots:
        om = _compact_slots(om, Q_SLOT_PAIRS)
    ox = _compact_slots(_stack_positions(ox_ref, seq), X_SLOT_PAIRS)
    x1 = _stack_positions(x_ref, seq) + _dot(om, wom_ref[...]) + _dot(ox, wox_ref[...])
    h = _rms_rows(x1, g2_ref[...]).astype(BF16)
    gu = _dot(h, wfi_ref[...])
    a = gu[:, :D_FF]
    blocks = [p0_ref[...], p1_ref[...]] + [a[t * nb:(t + 1) * nb] for t in range(seq)]
    am2 = jnp.concatenate(blocks[:seq], axis=0)
    am1 = jnp.concatenate(blocks[1:seq + 1], axis=0)
    y = x1 + _dot(_conv_gate(a, am1, am2, gu[:, D_FF:], cw_ref, cb_ref), wd_ref[...])
    for t in range(seq):
        y_ref[t] = y[t * nb:(t + 1) * nb]
    for r in range(CONV_W - 1):
        atail_ref[r] = blocks[seq + r]


def _ffn_sample(x3, om3, ox3, w_om, w_ox, layer, norm2_g, w_ffn_in_b, conv_w, conv_b, w_down_b, prefix):
    s, n, _ = x3.shape
    km, kx = om3.shape[2], ox3.shape[2]
    nb = min(SAMPLE_FFN_SEQS, n)
    assert n % nb == 0 and s >= CONV_W - 1

    def blk(lead, width):
        return pl.BlockSpec((lead, nb, width), lambda i: (0, i, 0))

    pspec = pl.BlockSpec((nb, D_FF), lambda i: (i, 0))
    mixer_slots = km != w_om.shape[0]
    assert km == (A_HEADS * X_WIDTH if mixer_slots else w_om.shape[0]) and kx == X_HEADS * X_WIDTH
    return pl.pallas_call(
        functools.partial(_ffn_sample_kernel, seq=s, nb=nb, mixer_slots=mixer_slots),
        grid=(n // nb,),
        in_specs=[
            blk(s, D_MODEL),
            blk(s, km),
            blk(s, kx),
            _const_spec(w_om.shape),
            _const_spec(w_ox.shape),
            _const_spec((1, D_MODEL)),
            _layer_spec(layer, (D_MODEL, 2 * D_FF)),
            _const_spec((CONV_W, D_FF)),
            _const_spec((1, D_FF)),
            _layer_spec(layer, (D_FF, D_MODEL)),
            pspec,
            pspec,
        ],
        out_specs=[blk(s, D_MODEL), blk(CONV_W - 1, D_FF)],
        out_shape=[
            jax.ShapeDtypeStruct((s, n, D_MODEL), F32),
            jax.ShapeDtypeStruct((CONV_W - 1, n, D_FF), F32),
        ],
        compiler_params=_params("arbitrary"),
        name="ffn_sample",
    )(x3, om3, ox3, w_om, w_ox, norm2_g.reshape(1, D_MODEL), w_ffn_in_b, conv_w, conv_b.reshape(1, D_FF), w_down_b,
      prefix[:, 0, :], prefix[:, 1, :])


Q_SLOT_PAIRS = tuple((Q_LANE_ORDER[2 * c], Q_LANE_ORDER[2 * c + 1], c // A_GROUP) for c in range(A_HEADS // 2))
X_SLOT_PAIRS = tuple((2 * c, 2 * c + 1, c) for c in range(X_HEADS // 2))


def _write_query_slots(pairs, slot_pairs, out_ref):
    t = pairs[0].shape[0]
    lo = lax.broadcasted_iota(jnp.int32, (t, LANES), 1) < HEAD_DIM
    zeros = jnp.zeros((t, LANES), out_ref.dtype)
    for pair, (h_lo, h_hi, col) in zip(pairs, slot_pairs):
        scaled = pair * SCALE
        for head, own in ((h_lo, jnp.where(lo, scaled, 0.0)), (h_hi, jnp.where(lo, 0.0, scaled))):
            own = own.astype(out_ref.dtype)
            out_ref[head] = jnp.concatenate([own, zeros] if col == 0 else [zeros, own], axis=1)


def _compact_slots(o, slot_pairs):
    lo = lax.broadcasted_iota(jnp.int32, (o.shape[0], LANES), 1) < HEAD_DIM
    cols = []
    for h_lo, h_hi, col in slot_pairs:
        a = o[:, h_lo * X_WIDTH + col * LANES:h_lo * X_WIDTH + (col + 1) * LANES]
        b = o[:, h_hi * X_WIDTH + col * LANES:h_hi * X_WIDTH + (col + 1) * LANES]
        cols.append(jnp.where(lo, a, b))
    return jnp.concatenate(cols, axis=1)


def _sample_in_a_kernel(x_ref, g1_ref, win_ref, wkvt_ref, qg_ref, kgc_ref, xqg_ref,
                        qs_ref, v_ref, xs_ref, kt_ref, vt_ref, *, seq):
    h = _rms_rows(_stack_positions(x_ref, seq), g1_ref[...]).astype(BF16)
    z = _dot(h, win_ref[...])
    _write_query_slots(_pair_head_norm(z[:, :A_Q], qg_ref[...]), Q_SLOT_PAIRS, qs_ref)
    v_ref[...] = z[:, A_Q + A_KV:A_Q + 2 * A_KV]
    _write_query_slots(_pair_head_norm(z[:, A_Q + 2 * A_KV:], xqg_ref[...]), X_SLOT_PAIRS, xs_ref)
    kvt = _nt_dot(wkvt_ref[...], h)
    for hk in range(A_KV_HEADS):
        sl = slice(hk * HEAD_DIM, (hk + 1) * HEAD_DIM)
        kh = kvt[sl, :]
        ms = jnp.mean(kh * kh, axis=0, keepdims=True)
        kt_ref[sl, :] = kh * lax.rsqrt(ms + EPS) * kgc_ref[sl, :]
    vt_ref[...] = kvt[A_KV:, :]


def _sample_in_a(x3, norm1_g, w_in_r, q_norm, k_norm, xq_norm):
    s, n, _ = x3.shape
    rows = s * n
    w_kvt = w_in_r[:, A_Q:A_Q + 2 * A_KV].T
    slots = lambda heads: jax.ShapeDtypeStruct((heads, rows, X_WIDTH), BF16)
    full = lambda shape: pl.BlockSpec(shape, lambda i: (0,) * len(shape))
    return pl.pallas_call(
        functools.partial(_sample_in_a_kernel, seq=s),
        grid=(1,),
        in_specs=[
            _const_spec((s, n, D_MODEL)),
            _const_spec((1, D_MODEL)),
            _const_spec(w_in_r.shape),
            _const_spec(w_kvt.shape),
            _const_spec((1, LANES)),
            _const_spec((A_KV, 1)),
            _const_spec((1, LANES)),
        ],
        out_specs=[
            full((A_HEADS, rows, A_KV)), full((rows, A_KV)), full((X_HEADS, rows, X_WIDTH)),
            full((A_KV, rows)), full((A_KV, rows)),
        ],
        out_shape=[
            slots(A_HEADS),
            jax.ShapeDtypeStruct((rows, A_KV), F32),
            slots(X_HEADS),
            jax.ShapeDtypeStruct((A_KV, rows), F32),
            jax.ShapeDtypeStruct((A_KV, rows), F32),
        ],
        compiler_params=_params("arbitrary"),
        name="sample_in_a",
    )(x3, norm1_g.reshape(1, D_MODEL), w_in_r, w_kvt, jnp.tile(q_norm, 2).reshape(1, LANES),
      jnp.tile(k_norm, 4).reshape(A_KV, 1), jnp.tile(xq_norm, 2).reshape(1, LANES))


def _sample_in_b_kernel(x_ref, g1_ref, win_ref, vg_ref, coef_ref, gbias_ref, xqg_ref, o_ref, vn_ref, xs_ref,
                        *, seq, n):
    h = _rms_rows(_stack_positions(x_ref, seq), g1_ref[...]).astype(BF16)
    z = _dot(h, win_ref[...])
    uv = _gelu_exact(z[:, :2 * B_WIDTH])
    u = uv[:, :B_WIDTH]
    vn = _rms_rows(uv[:, B_WIDTH:], vg_ref[...])
    vn_ref[...] = vn
    for t in range(seq):
        mixed = gbias_ref[t:t + 1, :] + coef_ref[t * seq:t * seq + 1, :] * vn[0:n]
        for j in range(1, t + 1):
            mixed = mixed + coef_ref[t * seq + j:t * seq + j + 1, :] * vn[j * n:(j + 1) * n]
        o_ref[t * n:(t + 1) * n, :] = (u[t * n:(t + 1) * n] * mixed).astype(o_ref.dtype)
    _write_query_slots(_pair_head_norm(z[:, 2 * B_WIDTH:], xqg_ref[...]), X_SLOT_PAIRS, xs_ref)


def _sample_in_b(x3, norm1_g, w_in_b, v_norm, w_s, b_s, xq_norm):
    s, n, _ = x3.shape
    rows = s * n
    coef = jnp.repeat(w_s[:, :s, :s].transpose(1, 2, 0).reshape(s * s, B_GROUPS), B_GROUP_DIM, axis=1)
    gbias = jnp.repeat(b_s[:, :s].T, B_GROUP_DIM, axis=1)
    full = lambda shape: pl.BlockSpec(shape, lambda i: (0,) * len(shape))
    return pl.pallas_call(
        functools.partial(_sample_in_b_kernel, seq=s, n=n),
        grid=(1,),
        in_specs=[
            _const_spec((s, n, D_MODEL)),
            _const_spec((1, D_MODEL)),
            _const_spec(w_in_b.shape),
            _const_spec((1, B_WIDTH)),
            _const_spec((s * s, B_WIDTH)),
            _const_spec((s, B_WIDTH)),
            _const_spec((1, LANES)),
        ],
        out_specs=[full((rows, B_WIDTH)), full((rows, B_WIDTH)), full((X_HEADS, rows, X_WIDTH))],
        out_shape=[
            jax.ShapeDtypeStruct((rows, B_WIDTH), BF16),
            jax.ShapeDtypeStruct((rows, B_WIDTH), F32),
            jax.ShapeDtypeStruct((X_HEADS, rows, X_WIDTH), BF16),
        ],
        compiler_params=_params("arbitrary"),
        name="sample_in_b",
    )(x3, norm1_g.reshape(1, D_MODEL), w_in_b, v_norm.reshape(1, B_WIDTH), coef, gbias,
      jnp.tile(xq_norm, 2).reshape(1, LANES))


def _sample_attn_win_kernel(qs_ref, ktn_ref, vtn_ref, vn_ref, slope_ref, sink_ref, ck_ref, cv_ref,
                            o_ref, wk_ref, wv_ref, *, sb, seq):
    rows = A_HEADS * seq
    tq = _div_mod(lax.broadcasted_iota(jnp.int32, (rows, WINDOW), 0), seq)[1]
    lane = lax.broadcasted_iota(jnp.int32, (rows, WINDOW), 1)
    slope = slope_ref[...]
    sink = sink_ref[...]
    d1 = tq + WINDOW - lane
    valid1 = d1 < WINDOW
    bias1 = slope * d1.astype(F32)
    new_seq, new_pos = _div_mod(lane, seq)
    d2 = tq - new_pos
    causal2 = d2 >= 0
    bias2 = slope * d2.astype(F32)
    ktn = ktn_ref[...]
    vtn = vtn_ref[...]
    ktn_b = ktn.astype(BF16)
    vn_b = vn_ref[...].astype(BF16)
    keep_old = lax.broadcasted_iota(jnp.int32, (A_KV, WINDOW), 1) < WINDOW - seq

    def body(n, carry):
        q = qs_ref[n]
        ckt = ck_ref[n]
        cvt = cv_ref[n]
        s1 = jnp.where(valid1, _dot(q, ckt.astype(BF16)) - bias1, NEG)
        s2 = jnp.where(causal2 & (new_seq == n), _dot(q, ktn_b) - bias2, NEG)
        m = jnp.maximum(jnp.maximum(jnp.max(s1, axis=-1, keepdims=True), jnp.max(s2, axis=-1, keepdims=True)), sink)
        e1 = jnp.exp(s1 - m)
        e2 = jnp.exp(s2 - m)
        den = jnp.sum(e1, axis=-1, keepdims=True) + jnp.sum(e2, axis=-1, keepdims=True) + jnp.exp(sink - m)
        o = _nt_dot(e1.astype(BF16), cvt.astype(BF16)) + _dot(e2.astype(BF16), vn_b)
        o_ref[n] = (o / den).astype(o_ref.dtype)
        shift_new = WINDOW - seq - n * seq
        wk_ref[n] = jnp.where(keep_old, pltpu.roll(ckt, WINDOW - seq, 1), pltpu.roll(ktn, shift_new, 1))
        wv_ref[n] = jnp.where(keep_old, pltpu.roll(cvt, WINDOW - seq, 1), pltpu.roll(vtn, shift_new, 1))
        return carry

    lax.fori_loop(0, sb, body, 0, unroll=SAMPLE_SEQ_UNROLL)


def _sample_attn_win(qs, kt_new, vt_new, v_new, slope_rows, sink_rows, ckt, cvt, layer, seq):
    n = qs.shape[0]
    rows = A_HEADS * seq
    assert LANES % seq == 0
    sb = LANES // seq
    assert n % sb == 0

    def seqs(*tail):
        return pl.BlockSpec((sb,) + tail, lambda i: (i,) + (0,) * len(tail))

    cache = pl.BlockSpec((None, sb, A_KV, WINDOW), lambda i: (layer, i, 0, 0))
    win = jax.ShapeDtypeStruct((n, A_KV, WINDOW), F32)
    return pl.pallas_call(
        functools.partial(_sample_attn_win_kernel, sb=sb, seq=seq),
        grid=(n // sb,),
        in_specs=[
            seqs(rows, A_KV),
            pl.BlockSpec((A_KV, LANES), lambda i: (0, i)),
            pl.BlockSpec((A_KV, LANES), lambda i: (0, i)),
            pl.BlockSpec((LANES, A_KV), lambda i: (i, 0)),
            _const_spec((rows, 1)),
            _const_spec((rows, 1)),
            cache,
            cache,
        ],
        out_specs=[seqs(rows, A_KV), seqs(A_KV, WINDOW), seqs(A_KV, WINDOW)],
        out_shape=[jax.ShapeDtypeStruct((n, rows, A_KV), BF16), win, win],
        compiler_params=_params("arbitrary"),
        name="sample_attn_win",
    )(qs, kt_new, vt_new, v_new, slope_rows, sink_rows, ckt, cvt)


def _sample_attn_mem_kernel(xs_ref, mk_ref, mv_ref, xo_ref, *, sb):
    def body(n, carry):
        s = _dot(xs_ref[n], mk_ref[n].astype(BF16))
        m = jnp.max(s, axis=-1, keepdims=True)
        e = jnp.exp(s - m)
        den = jnp.sum(e, axis=-1, keepdims=True)
        xo_ref[n] = (_nt_dot(e.astype(BF16), mv_ref[n].astype(BF16)) / den).astype(xo_ref.dtype)
        return carry

    lax.fori_loop(0, sb, body, 0, unroll=SAMPLE_SEQ_UNROLL)


def _sample_attn_mem(xs, mkt, mvt, layer):
    n, xrows, _ = xs.shape
    sb = min(SAMPLE_MEM_SEQS, n)
    assert n % sb == 0
    qspec = pl.BlockSpec((sb, xrows, X_WIDTH), lambda i: (i, 0, 0))
    cache = pl.BlockSpec((None, sb, X_WIDTH, N_MEM), lambda i: (layer, i, 0, 0))
    return pl.pallas_call(
        functools.partial(_sample_attn_mem_kernel, sb=sb),
        grid=(n // sb,),
        in_specs=[qspec, cache, cache],
        out_specs=qspec,
        out_shape=jax.ShapeDtypeStruct((n, xrows, X_WIDTH), BF16),
        compiler_params=_params("arbitrary"),
        name="sample_attn_mem",
    )(xs, mkt, mvt)


def _slots_to_seq(a, seq, n):
    hh, _, w = a.shape
    return a.reshape(hh, seq, n, w).transpose(2, 0, 1, 3).reshape(n, hh * seq, w)


def _seq_to_cols(a, heads, seq):
    n, _, w = a.shape
    return a.reshape(n, heads, seq, w).transpose(2, 0, 1, 3).reshape(seq, n, heads * w)


def _feature_major(cache):
    l, n, p, kv, d = cache.shape
    return cache.transpose(0, 1, 3, 4, 2).reshape(l, n, kv * d, p)


def kernel(x_prompt, x_sample, cache_win_k, cache_win_v, cache_mem_k, cache_mem_v, state_conv, mem_prompt,
           norm1_g, norm2_g, mem_norm_g, w_in_a, q_norm_a, k_norm_a, sinks_a, w_out_a, w_in_b, v_norm_b, w_s_b,
           b_s_b, w_out_b, w_mem_kv, xq_norm, xk_norm, w_ffn_in, conv_w, conv_b, w_down):
    depth = norm1_g.shape[0]
    assert depth == 2 and w_in_a.shape[0] == 1 and w_in_b.shape[0] == 1
    nb = x_prompt.shape[0]
    ns, seq, _ = x_sample.shape
    assert cache_win_k.shape[2] == WINDOW

    w_in_a_b = w_in_a[0].astype(BF16)
    w_in_b_b = w_in_b[0].astype(BF16)
    w_out_a_b = w_out_a[0].astype(BF16)
    w_out_b_b = w_out_b[0].astype(BF16)
    w_mem_kv_b = w_mem_kv.astype(BF16)
    w_ffn_in_b = w_ffn_in.astype(BF16)
    w_down_b = w_down.astype(BF16)

    mk_all, mv_all, mem_kt_p, mem_vt_p = _memory_kv(mem_prompt, mem_norm_g, w_mem_kv_b, xk_norm)
    w_in_a_r = _reorder_heads(w_in_a_b, 1)
    w_out_a_r = _reorder_heads(w_out_a_b, 0)
    x2, win_k_p, win_v_p, conv_p0 = _layer_a_prompt(
        x_prompt, 0, norm1_g[0], w_in_a_r, q_norm_a[0], k_norm_a[0], xq_norm[0], sinks_a[0], mk_all, mv_all, w_out_a_r,
        norm2_g[0], w_ffn_in_b, conv_w[0], conv_b[0], w_down_b)
    y_prompt, chunk_v_p, conv_p1 = _layer_b_prompt(
        x2, 1, norm1_g[1], w_in_b_b, v_norm_b[0], w_s_b[0], b_s_b[0], xq_norm[1], mk_all, mv_all, w_out_b_b,
        norm2_g[1], w_ffn_in_b, conv_w[1], conv_b[1], w_down_b)

    xs3 = x_sample.transpose(1, 0, 2)
    mkt = _feature_major(cache_mem_k)
    mvt = _feature_major(cache_mem_v)

    def by_seq_cols(a):
        return a.reshape(a.shape[0], seq, ns).transpose(0, 2, 1).reshape(a.shape[0], ns * seq)

    qs, v_new, xslots, kt_new, vt_new = _sample_in_a(xs3, norm1_g[0], w_in_a_r, q_norm_a[0], k_norm_a[0], xq_norm[0])
    slope_rows = jnp.repeat(jnp.asarray(SLOPES, F32), seq).reshape(A_HEADS * seq, 1)
    sink_rows = jnp.repeat(sinks_a[0].astype(F32), seq).reshape(A_HEADS * seq, 1)
    o_s, win_kt_s, win_vt_s = _sample_attn_win(
        _slots_to_seq(qs, seq, ns), by_seq_cols(kt_new), by_seq_cols(vt_new),
        v_new.reshape(seq, ns, A_KV).transpose(1, 0, 2).reshape(ns * seq, A_KV), slope_rows, sink_rows,
        _feature_major(cache_win_k), _feature_major(cache_win_v), 0, seq)
    xo_s = _sample_attn_mem(_slots_to_seq(xslots, seq, ns), mkt, mvt, 0)
    ys1, conv_s0 = _ffn_sample(
        xs3, _seq_to_cols(o_s, A_HEADS, seq), _seq_to_cols(xo_s, X_HEADS, seq),
        w_out_a_r[:A_Q], w_out_a_r[A_Q:], 0, norm2_g[0], w_ffn_in_b, conv_w[0], conv_b[0], w_down_b, state_conv[0])

    o_b, vn_s, xslots = _sample_in_b(ys1, norm1_g[1], w_in_b_b, v_norm_b[0], w_s_b[0], b_s_b[0], xq_norm[1])
    xo_s = _sample_attn_mem(_slots_to_seq(xslots, seq, ns), mkt, mvt, 1)
    ys2, conv_s1 = _ffn_sample(
        ys1, o_b.reshape(seq, ns, B_WIDTH), _seq_to_cols(xo_s, X_HEADS, seq),
        w_out_b_b[:B_WIDTH], w_out_b_b[B_WIDTH:], 1, norm2_g[1], w_ffn_in_b, conv_w[1], conv_b[1], w_down_b, state_conv[1])

    def position_major(ft):
        lead = ft.shape[:-2]
        nd = len(lead)
        return ft.reshape(*lead, A_KV_HEADS, HEAD_DIM, ft.shape[-1]).transpose(*range(nd), nd + 2, nd, nd + 1)

    return (
        y_prompt,
        ys2.transpose(1, 0, 2),
        position_major(win_k_p)[None],
        position_major(win_v_p)[None],
        chunk_v_p.reshape(1, nb, CHUNK, B_WIDTH),
        position_major(mem_kt_p),
        position_major(mem_vt_p),
        jnp.stack([conv_p0, conv_p1]),
        position_major(win_kt_s)[None],
        position_major(win_vt_s)[None],
        vn_s.reshape(seq, ns, B_WIDTH).transpose(1, 0, 2)[None],
        jnp.stack([conv_s0.transpose(1, 0, 2), conv_s1.transpose(1, 0, 2)]),
    )
```

```python
import functools
import math

import numpy as np
import jax
import jax.numpy as jnp
from jax import lax
from jax.experimental import pallas as pl
from jax.experimental.pallas import tpu as pltpu

F32 = jnp.float32
BF16 = jnp.bfloat16

D_MODEL = 1024
HEAD_DIM = 64
A_HEADS = 12
A_KV_HEADS = 4
A_GROUP = A_HEADS // A_KV_HEADS
WINDOW = 128
BLOCK = 128
B_WIDTH = 768
B_GROUPS = 4
B_GROUP_DIM = B_WIDTH // B_GROUPS
CHUNK = 128
X_HEADS = 4
X_WIDTH = X_HEADS * HEAD_DIM
N_MEM = 256
D_FF = 2816
CONV_W = 3
A_Q = A_HEADS * HEAD_DIM
A_KV = A_KV_HEADS * HEAD_DIM
EPS = 1e-6
NEG = -1e30
SCALE = HEAD_DIM ** -0.5
LOG2E = float(np.float32(1.0 / np.log(2.0)))
SQRT_HALF = float(np.sqrt(0.5).astype(np.float32))

LANES = 128
VMEM_LIMIT_BYTES = 56 * 1024 * 1024
GATE_WINDOW_START = (0, 128, 384, 512)
assert B_GROUP_DIM == 192 and B_GROUPS == 4
PROMPT_ROWS = 512
SAMPLE_MEM_SEQS = 16
SAMPLE_FFN_SEQS = 128
SAMPLE_SEQ_UNROLL = 4


def _alibi_slopes(n):
    def pow2(m):
        start = 2.0 ** (-8.0 / m)
        return [start ** (i + 1) for i in range(m)]

    p = 2 ** int(math.floor(math.log2(n)))
    s = pow2(p)
    if p < n:
        s = s + pow2(2 * p)[0::2][: n - p]
    return [float(np.float32(v)) for v in s]


SLOPES = _alibi_slopes(A_HEADS)

Q_LANE_ORDER = (0, 3, 1, 4, 2, 5, 6, 9, 7, 10, 8, 11)
assert all(Q_LANE_ORDER[p] // A_GROUP == 2 * (p // (2 * A_GROUP)) + p % 2 for p in range(A_HEADS))


_SLOTS = A_KV_HEADS // 2
assert Q_LANE_ORDER == tuple(np.arange(A_HEADS).reshape(_SLOTS, 2, A_GROUP).swapaxes(1, 2).reshape(-1).tolist())


def _reorder_heads(w, axis):
    lead, tail = w.shape[:axis], w.shape[axis + 1:]
    q = lax.slice_in_dim(w, 0, A_Q, axis=axis).reshape(*lead, _SLOTS, 2, A_GROUP, HEAD_DIM, *tail)
    q = jnp.swapaxes(q, axis + 1, axis + 2).reshape(*lead, A_Q, *tail)
    return jnp.concatenate([q, lax.slice_in_dim(w, A_Q, w.shape[axis], axis=axis)], axis=axis)


def _params(*sem):
    return pltpu.CompilerParams(dimension_semantics=sem, vmem_limit_bytes=VMEM_LIMIT_BYTES)


def _const_spec(shape):
    nd = len(shape)
    return pl.BlockSpec(shape, lambda *_: (0,) * nd, pipeline_mode=pl.Buffered(1))


def _layer_spec(layer, shape):
    nd = len(shape)
    return pl.BlockSpec((None,) + tuple(shape), lambda *_: (layer,) + (0,) * nd, pipeline_mode=pl.Buffered(1))


def _rms_rows(x, g):
    ms = jnp.mean(x * x, axis=-1, keepdims=True)
    return x * lax.rsqrt(ms + EPS) * g


def _pair_head_norm(z, g2):
    t, w = z.shape
    lo = lax.broadcasted_iota(jnp.int32, (t, LANES), 1) < HEAD_DIM
    out = []
    for p in range(w // LANES):
        zz = z[:, p * LANES:(p + 1) * LANES]
        sq = zz * zz
        s_lo = jnp.sum(jnp.where(lo, sq, 0.0), axis=-1, keepdims=True)
        s_hi = jnp.sum(jnp.where(lo, 0.0, sq), axis=-1, keepdims=True)
        r_lo = lax.rsqrt(s_lo * (1.0 / HEAD_DIM) + EPS)
        r_hi = lax.rsqrt(s_hi * (1.0 / HEAD_DIM) + EPS)
        out.append(zz * jnp.where(lo, r_lo, r_hi) * g2)
    return out


def _nt_dot(a, b):
    return lax.dot_general(a, b, (((1,), (1,)), ((), ())), preferred_element_type=F32)


def _dot(a, b):
    return jnp.dot(a, b, preferred_element_type=F32)


def _div_mod(i, n):
    if n & (n - 1) == 0:
        return i >> (n.bit_length() - 1), i & (n - 1)
    return i // n, i % n


def _gelu_exact(x):
    return 0.5 * x * (1.0 + lax.erf(x * SQRT_HALF))


def _conv_gate(a, am1, am2, up, cw_ref, cb_ref):
    c = cb_ref[...] + ((cw_ref[0:1, :] * am2 + cw_ref[1:2, :] * am1) + cw_ref[2:3, :] * a)
    return ((c * (1.0 / (1.0 + jnp.exp(-c)))) * up).astype(BF16)


def _memkv_kernel(mem_ref, g_ref, w_ref, kg_ref, mkb_ref, mvb_ref, mkt_ref, mvt_ref, *, depth):
    m = mem_ref[0]
    for layer in range(depth):
        h = _rms_rows(m, g_ref[layer]).astype(BF16)
        kv = _dot(h, w_ref[layer])
        mk = jnp.concatenate(_pair_head_norm(kv[:, :X_WIDTH], kg_ref[layer]), axis=1)
        mv = kv[:, X_WIDTH:]
        mkb_ref[layer, 0] = mk.astype(BF16)
        mvb_ref[layer, 0] = mv.astype(BF16)
        mkt_ref[layer, 0] = mk.T
        mvt_ref[layer, 0] = mv.T


def _memory_kv(mem, mem_norm_g, w_mem_kv_b, xk_norm):
    depth = w_mem_kv_b.shape[0]
    n = mem.shape[0]
    g = mem_norm_g.reshape(depth, 1, D_MODEL)
    kg2 = jnp.tile(xk_norm, (1, 2)).reshape(depth, 1, LANES)
    spec = lambda rows, cols: pl.BlockSpec((depth, 1, rows, cols), lambda b: (0, b, 0, 0))
    return pl.pallas_call(
        functools.partial(_memkv_kernel, depth=depth),
        grid=(n,),
        in_specs=[
            pl.BlockSpec((1, N_MEM, D_MODEL), lambda b: (b, 0, 0)),
            _const_spec((depth, 1, D_MODEL)),
            _const_spec((depth, D_MODEL, 2 * X_WIDTH)),
            _const_spec((depth, 1, LANES)),
        ],
        out_specs=[spec(N_MEM, X_WIDTH), spec(N_MEM, X_WIDTH), spec(X_WIDTH, N_MEM), spec(X_WIDTH, N_MEM)],
        out_shape=[
            jax.ShapeDtypeStruct((depth, n, N_MEM, X_WIDTH), BF16),
            jax.ShapeDtypeStruct((depth, n, N_MEM, X_WIDTH), BF16),
            jax.ShapeDtypeStruct((depth, n, X_WIDTH, N_MEM), F32),
            jax.ShapeDtypeStruct((depth, n, X_WIDTH, N_MEM), F32),
        ],
        compiler_params=_params("arbitrary"),
        name="memory_kv",
    )(mem, g, w_mem_kv_b, kg2)


def _half_masked_rows(pair, lo):
    return jnp.concatenate([jnp.where(lo, pair, 0.0), jnp.where(lo, 0.0, pair)], axis=0).astype(BF16)


def _prompt_mem_attention(xq_pairs, mk_ref, mv_ref, ocat, col0):
    t = xq_pairs[0].shape[0]
    lo = lax.broadcasted_iota(jnp.int32, (t, LANES), 1) < HEAD_DIM
    for j, pair in enumerate(xq_pairs):
        mk = mk_ref[0, 0, :, j * LANES:(j + 1) * LANES]
        mv = mv_ref[0, 0, :, j * LANES:(j + 1) * LANES]
        s = _nt_dot(_half_masked_rows(pair * (SCALE * LOG2E), lo), mk)
        m = jnp.max(s, axis=-1, keepdims=True)
        e = jnp.exp2(s - m)
        den = jnp.sum(e, axis=-1, keepdims=True)
        o = _dot(e.astype(BF16), mv) / den
        ocat[:, col0 + j * LANES:col0 + (j + 1) * LANES] = jnp.where(lo, o[:t], o[t:]).astype(BF16)


def _mixer_a_body(t, x, sinks_ref, g1_ref, win_ref, qg_ref, kg_ref, xqg_ref, mk_ref, mv_ref, wout_ref,
                  wk_ref, wv_ref, kbuf, vbuf, ocat, tb):
    h = _rms_rows(x, g1_ref[...]).astype(BF16)
    z = _dot(h, win_ref[...])
    q_pairs = _pair_head_norm(z[:, :A_Q], qg_ref[...])
    k_pairs = _pair_head_norm(z[:, A_Q:A_Q + A_KV], kg_ref[...])
    v = z[:, A_Q + A_KV:A_Q + 2 * A_KV]
    xq_pairs = _pair_head_norm(z[:, A_Q + 2 * A_KV:], xqg_ref[...])

    for p, kp in enumerate(k_pairs):
        wk_ref[0, p * LANES:(p + 1) * LANES, :] = kp[tb - BLOCK:, :].T
        kbuf[BLOCK:BLOCK + tb, p * LANES:(p + 1) * LANES] = kp.astype(BF16)
    wv_ref[0] = v[tb - BLOCK:, :].T
    vbuf[BLOCK:BLOCK + tb, :] = v.astype(BF16)

    qi = lax.broadcasted_iota(jnp.int32, (BLOCK, 2 * BLOCK), 0)
    kj = lax.broadcasted_iota(jnp.int32, (BLOCK, 2 * BLOCK), 1)
    d = qi + BLOCK - kj
    in_window = (d >= 0) & (d < WINDOW)
    key_pos = lax.broadcasted_iota(jnp.int32, (1, 2 * BLOCK), 1).astype(F32)
    query_pos = (lax.broadcasted_iota(jnp.int32, (BLOCK, 1), 0) + BLOCK).astype(F32)

    lo = lax.broadcasted_iota(jnp.int32, (BLOCK, LANES), 1) < HEAD_DIM
    pairs_per_slot = A_HEADS // A_KV_HEADS
    for s in range(tb // BLOCK):
        rows = slice(s * BLOCK, (s + 1) * BLOCK)
        first_key = jnp.where(t > 0, 0, BLOCK) if s == 0 else 0
        valid = in_window & (kj >= first_key)
        for j in range(A_KV // LANES):
            kw = kbuf[s * BLOCK:(s + 2) * BLOCK, j * LANES:(j + 1) * LANES]
            vw = vbuf[s * BLOCK:(s + 2) * BLOCK, j * LANES:(j + 1) * LANES]
            q_rows = jnp.concatenate(
                [_half_masked_rows(q_pairs[pairs_per_slot * j + c][rows] * (SCALE * LOG2E), lo)
                 for c in range(pairs_per_slot)], axis=0)
            sc_all = _nt_dot(q_rows, kw)
            e_parts, den_parts = [], []
            for r in range(2 * pairs_per_slot):
                hq = Q_LANE_ORDER[2 * pairs_per_slot * j + r]
                slope2 = SLOPES[hq] * LOG2E
                sc = jnp.where(valid, sc_all[r * BLOCK:(r + 1) * BLOCK] + slope2 * key_pos, NEG)
                sink = sinks_ref[hq] * LOG2E + slope2 * query_pos
                m = jnp.maximum(jnp.max(sc, axis=-1, keepdims=True), sink)
                e = jnp.exp2(sc - m)
                den_parts.append(jnp.sum(e, axis=-1, keepdims=True) + jnp.exp2(sink - m))
                e_parts.append(e.astype(BF16))
            o_all = _dot(jnp.concatenate(e_parts, axis=0), vw)
            for c in range(pairs_per_slot):
                o_lo = o_all[2 * c * BLOCK:(2 * c + 1) * BLOCK] / den_parts[2 * c]
                o_hi = o_all[(2 * c + 1) * BLOCK:(2 * c + 2) * BLOCK] / den_parts[2 * c + 1]
                col = pairs_per_slot * j + c
                ocat[rows, col * LANES:(col + 1) * LANES] = jnp.where(lo, o_lo, o_hi).astype(BF16)

    _prompt_mem_attention(xq_pairs, mk_ref, mv_ref, ocat, A_Q)
    x1 = x + _dot(ocat[...], wout_ref[...])

    kbuf[0:BLOCK, :] = kbuf[tb:tb + BLOCK, :]
    vbuf[0:BLOCK, :] = vbuf[tb:tb + BLOCK, :]
    return x1


def _ffn_body(x, g2_ref, wfi_ref, cw_ref, cb_ref, wd_ref, tail_ref, tb):
    h = _rms_rows(x, g2_ref[...]).astype(BF16)
    gu = _dot(h, wfi_ref[...])
    a = gu[:, :D_FF]
    row = lax.broadcasted_iota(jnp.int32, (tb, 1), 0)
    prev_last = tail_ref[0, 7:8, :]
    am1 = jnp.where(row == 0, prev_last, pltpu.roll(a, 1, 0))
    am2 = jnp.where(row == 0, tail_ref[0, 6:7, :], jnp.where(row == 1, prev_last, pltpu.roll(a, 2, 0)))
    tail_ref[0] = a[tb - 8:, :]
    return x + _dot(_conv_gate(a, am1, am2, gu[:, D_FF:], cw_ref, cb_ref), wd_ref[...])


def _layer_a_kernel(sinks_ref, x_ref, g1_ref, win_ref, qg_ref, kg_ref, xqg_ref, mk_ref, mv_ref, wout_ref,
                    g2_ref, wfi_ref, cw_ref, cb_ref, wd_ref, y_ref, wk_ref, wv_ref, tail_ref, kbuf, vbuf, ocat, *, tb):
    t = pl.program_id(1)

    @pl.when(t == 0)
    def _():
        kbuf[0:BLOCK, :] = jnp.zeros((BLOCK, A_KV), BF16)
        vbuf[0:BLOCK, :] = jnp.zeros((BLOCK, A_KV), BF16)
        tail_ref[0] = jnp.zeros((8, D_FF), F32)

    x1 = _mixer_a_body(t, x_ref[0], sinks_ref, g1_ref, win_ref, qg_ref, kg_ref, xqg_ref, mk_ref, mv_ref, wout_ref,
                       wk_ref, wv_ref, kbuf, vbuf, ocat, tb)
    y_ref[0] = _ffn_body(x1, g2_ref, wfi_ref, cw_ref, cb_ref, wd_ref, tail_ref, tb)


def _layer_specs(tb, layer):
    xspec = pl.BlockSpec((1, tb, D_MODEL), lambda b, i: (b, i, 0))
    mspec = pl.BlockSpec((1, 1, N_MEM, X_WIDTH), lambda b, i: (layer, b, 0, 0))
    per_seq = lambda rows, width: pl.BlockSpec((1, rows, width), lambda b, i: (b, 0, 0))
    return xspec, mspec, per_seq(8, D_FF), per_seq


def _ffn_specs(layer):
    return [
        _const_spec((1, D_MODEL)),
        _layer_spec(layer, (D_MODEL, 2 * D_FF)),
        _const_spec((CONV_W, D_FF)),
        _const_spec((1, D_FF)),
        _layer_spec(layer, (D_FF, D_MODEL)),
    ]


def _layer_a_prompt(x, layer, norm1_g, w_in_b, q_norm, k_norm, xq_norm, sinks, mk_all, mv_all, w_out_b,
                    norm2_g, w_ffn_in_b, conv_w, conv_b, w_down_b):
    n, t, _ = x.shape
    tb = min(PROMPT_ROWS, t)
    a_in = w_in_b.shape[1]
    xspec, mspec, tailspec, per_seq = _layer_specs(tb, layer)
    wspec = per_seq(A_KV, BLOCK)
    y, wk, wv, tail = pl.pallas_call(
        functools.partial(_layer_a_kernel, tb=tb),
        grid=(n, t // tb),
        in_specs=[
            pl.BlockSpec(memory_space=pltpu.SMEM),
            xspec,
            _const_spec((1, D_MODEL)),
            _const_spec((D_MODEL, a_in)),
            _const_spec((1, LANES)),
            _const_spec((1, LANES)),
            _const_spec((1, LANES)),
            mspec,
            mspec,
            _const_spec((A_Q + X_WIDTH, D_MODEL)),
        ] + _ffn_specs(layer),
        out_specs=[xspec, wspec, wspec, tailspec],
        out_shape=[
            jax.ShapeDtypeStruct((n, t, D_MODEL), F32),
            jax.ShapeDtypeStruct((n, A_KV, BLOCK), F32),
            jax.ShapeDtypeStruct((n, A_KV, BLOCK), F32),
            jax.ShapeDtypeStruct((n, 8, D_FF), F32),
        ],
        scratch_shapes=[
            pltpu.VMEM((tb + BLOCK, A_KV), BF16),
            pltpu.VMEM((tb + BLOCK, A_KV), BF16),
            pltpu.VMEM((tb, A_Q + X_WIDTH), BF16),
        ],
        compiler_params=_params("arbitrary", "arbitrary"),
        name="layer_a_prompt",
    )(sinks, x, norm1_g.reshape(1, D_MODEL), w_in_b, jnp.tile(q_norm, 2).reshape(1, LANES),
      jnp.tile(k_norm, 2).reshape(1, LANES), jnp.tile(xq_norm, 2).reshape(1, LANES), mk_all, mv_all, w_out_b,
      norm2_g.reshape(1, D_MODEL), w_ffn_in_b, conv_w, conv_b.reshape(1, D_FF), w_down_b)
    return y, wk, wv, tail[:, 8 - (CONV_W - 1):, :]


def _mixer_b_body(x, g1_ref, win_ref, vg_ref, ws_ref, bias_ref, xqg_ref, mk_ref, mv_ref, wout_ref, cv_ref, ocat, tb):
    h = _rms_rows(x, g1_ref[...]).astype(BF16)
    z = _dot(h, win_ref[...])
    uv = _gelu_exact(z[:, :2 * B_WIDTH])
    u = uv[:, :B_WIDTH]
    vn = _rms_rows(uv[:, B_WIDTH:], vg_ref[...])
    cv_ref[0] = vn[tb - CHUNK:, :]
    xq_pairs = _pair_head_norm(z[:, 2 * B_WIDTH:], xqg_ref[...])

    wi = lax.broadcasted_iota(jnp.int32, (CHUNK, CHUNK), 0)
    wj = lax.broadcasted_iota(jnp.int32, (CHUNK, CHUNK), 1)
    causal = wi >= wj
    w_tril = [jnp.where(causal, ws_ref[g], 0.0).astype(BF16) for g in range(B_GROUPS)]
    lo = lax.broadcasted_iota(jnp.int32, (CHUNK, LANES), 1) < HEAD_DIM
    bias = bias_ref[...]
    for c in range(tb // CHUNK):
        vc = vn[c * CHUNK:(c + 1) * CHUNK, :].astype(BF16)
        r = [_dot(w_tril[g], vc[:, c0:c0 + 256]) for g, c0 in enumerate(GATE_WINDOW_START)]
        mixed = jnp.concatenate(
            [r[0][:, :128], jnp.where(lo, r[0][:, 128:], r[1][:, :128]), r[1][:, 128:],
             r[2][:, :128], jnp.where(lo, r[2][:, 128:], r[3][:, :128]), r[3][:, 128:]], axis=1)
        o = u[c * CHUNK:(c + 1) * CHUNK, :] * (mixed + bias)
        ocat[c * CHUNK:(c + 1) * CHUNK, 0:B_WIDTH] = o.astype(BF16)

    _prompt_mem_attention(xq_pairs, mk_ref, mv_ref, ocat, B_WIDTH)
    return x + _dot(ocat[...], wout_ref[...])


def _layer_b_kernel(x_ref, g1_ref, win_ref, vg_ref, ws_ref, bias_ref, xqg_ref, mk_ref, mv_ref, wout_ref,
                    g2_ref, wfi_ref, cw_ref, cb_ref, wd_ref, y_ref, cv_ref, tail_ref, ocat, *, tb):
    @pl.when(pl.program_id(1) == 0)
    def _():
        tail_ref[0] = jnp.zeros((8, D_FF), F32)

    x1 = _mixer_b_body(x_ref[0], g1_ref, win_ref, vg_ref, ws_ref, bias_ref, xqg_ref, mk_ref, mv_ref, wout_ref,
                       cv_ref, ocat, tb)
    y_ref[0] = _ffn_body(x1, g2_ref, wfi_ref, cw_ref, cb_ref, wd_ref, tail_ref, tb)


def _gate_bias_rows(b_s):
    return jnp.repeat(b_s.T, B_GROUP_DIM, axis=1)


def _layer_b_prompt(x, layer, norm1_g, w_in_b, v_norm, w_s, b_s, xq_norm, mk_all, mv_all, w_out_b,
                    norm2_g, w_ffn_in_b, conv_w, conv_b, w_down_b):
    n, t, _ = x.shape
    tb = min(PROMPT_ROWS, t)
    b_in = w_in_b.shape[1]
    xspec, mspec, tailspec, per_seq = _layer_specs(tb, layer)
    y, cv, tail = pl.pallas_call(
        functools.partial(_layer_b_kernel, tb=tb),
        grid=(n, t // tb),
        in_specs=[
            xspec,
            _const_spec((1, D_MODEL)),
            _const_spec((D_MODEL, b_in)),
            _const_spec((1, B_WIDTH)),
            _const_spec((B_GROUPS, CHUNK, CHUNK)),
            _const_spec((CHUNK, B_WIDTH)),
            _const_spec((1, LANES)),
            mspec,
            mspec,
            _const_spec((B_WIDTH + X_WIDTH, D_MODEL)),
        ] + _ffn_specs(layer),
        out_specs=[xspec, per_seq(CHUNK, B_WIDTH), tailspec],
        out_shape=[
            jax.ShapeDtypeStruct((n, t, D_MODEL), F32),
            jax.ShapeDtypeStruct((n, CHUNK, B_WIDTH), F32),
            jax.ShapeDtypeStruct((n, 8, D_FF), F32),
        ],
        scratch_shapes=[pltpu.VMEM((tb, B_WIDTH + X_WIDTH), BF16)],
        compiler_params=_params("arbitrary", "arbitrary"),
        name="layer_b_prompt",
    )(x, norm1_g.reshape(1, D_MODEL), w_in_b, v_norm.reshape(1, B_WIDTH), w_s, _gate_bias_rows(b_s),
      jnp.tile(xq_norm, 2).reshape(1, LANES), mk_all, mv_all, w_out_b,
      norm2_g.reshape(1, D_MODEL), w_ffn_in_b, conv_w, conv_b.reshape(1, D_FF), w_down_b)
    return y, cv, tail[:, 8 - (CONV_W - 1):, :]


def _stack_positions(ref, seq):
    return jnp.concatenate([ref[t] for t in range(seq)], axis=0)


def _ffn_sample_kernel(x_ref, om_ref, ox_ref, wom_ref, wox_ref, g2_ref, wfi_ref, cw_ref, cb_ref, wd_ref, p0_ref, p1_ref,
                       y_ref, atail_ref, *, seq, nb, mixer_slots):
    om = _stack_positions(om_ref, seq)
    if mixer_slots:
        om = _compact_slots(om, Q_SLOT_PAIRS)
    ox = _compact_slots(_stack_positions(ox_ref, seq), X_SLOT_PAIRS)
    x1 = _stack_positions(x_ref, seq) + _dot(om, wom_ref[...]) + _dot(ox, wox_ref[...])
    h = _rms_rows(x1, g2_ref[...]).astype(BF16)
    gu = _dot(h, wfi_ref[...])
    a = gu[:, :D_FF]
    blocks = [p0_ref[...], p1_ref[...]] + [a[t * nb:(t + 1) * nb] for t in range(seq)]
    am2 = jnp.concatenate(blocks[:seq], axis=0)
    am1 = jnp.concatenate(blocks[1:seq + 1], axis=0)
    y = x1 + _dot(_conv_gate(a, am1, am2, gu[:, D_FF:], cw_ref, cb_ref), wd_ref[...])
    for t in range(seq):
        y_ref[t] = y[t * nb:(t + 1) * nb]
    for r in range(CONV_W - 1):
        atail_ref[r] = blocks[seq + r]


def _ffn_sample(x3, om3, ox3, w_om, w_ox, layer, norm2_g, w_ffn_in_b, conv_w, conv_b, w_down_b, prefix):
    s, n, _ = x3.shape
    km, kx = om3.shape[2], ox3.shape[2]
    nb = min(SAMPLE_FFN_SEQS, n)
    assert n % nb == 0 and s >= CONV_W - 1

    def blk(lead, width):
        return pl.BlockSpec((lead, nb, width), lambda i: (0, i, 0))

    pspec = pl.BlockSpec((nb, D_FF), lambda i: (i, 0))
    mixer_slots = km != w_om.shape[0]
    assert km == (A_HEADS * X_WIDTH if mixer_slots else w_om.shape[0]) and kx == X_HEADS * X_WIDTH
    return pl.pallas_call(
        functools.partial(_ffn_sample_kernel, seq=s, nb=nb, mixer_slots=mixer_slots),
        grid=(n // nb,),
        in_specs=[
            blk(s, D_MODEL),
            blk(s, km),
            blk(s, kx),
            _const_spec(w_om.shape),
            _const_spec(w_ox.shape),
            _const_spec((1, D_MODEL)),
            _layer_spec(layer, (D_MODEL, 2 * D_FF)),
            _const_spec((CONV_W, D_FF)),
            _const_spec((1, D_FF)),
            _layer_spec(layer, (D_FF, D_MODEL)),
            pspec,
            pspec,
        ],
        out_specs=[blk(s, D_MODEL), blk(CONV_W - 1, D_FF)],
        out_shape=[
            jax.ShapeDtypeStruct((s, n, D_MODEL), F32),
            jax.ShapeDtypeStruct((CONV_W - 1, n, D_FF), F32),
        ],
        compiler_params=_params("arbitrary"),
        name="ffn_sample",
    )(x3, om3, ox3, w_om, w_ox, norm2_g.reshape(1, D_MODEL), w_ffn_in_b, conv_w, conv_b.reshape(1, D_FF), w_down_b,
      prefix[:, 0, :], prefix[:, 1, :])


Q_SLOT_PAIRS = tuple((Q_LANE_ORDER[2 * c], Q_LANE_ORDER[2 * c + 1], c // A_GROUP) for c in range(A_HEADS // 2))
X_SLOT_PAIRS = tuple((2 * c, 2 * c + 1, c) for c in range(X_HEADS // 2))


def _write_query_slots(pairs, slot_pairs, out_ref):
    t = pairs[0].shape[0]
    lo = lax.broadcasted_iota(jnp.int32, (t, LANES), 1) < HEAD_DIM
    zeros = jnp.zeros((t, LANES), out_ref.dtype)
    for pair, (h_lo, h_hi, col) in zip(pairs, slot_pairs):
        scaled = pair * SCALE
        for head, own in ((h_lo, jnp.where(lo, scaled, 0.0)), (h_hi, jnp.where(lo, 0.0, scaled))):
            own = own.astype(out_ref.dtype)
            out_ref[head] = jnp.concatenate([own, zeros] if col == 0 else [zeros, own], axis=1)


def _compact_slots(o, slot_pairs):
    lo = lax.broadcasted_iota(jnp.int32, (o.shape[0], LANES), 1) < HEAD_DIM
    cols = []
    for h_lo, h_hi, col in slot_pairs:
        a = o[:, h_lo * X_WIDTH + col * LANES:h_lo * X_WIDTH + (col + 1) * LANES]
        b = o[:, h_hi * X_WIDTH + col * LANES:h_hi * X_WIDTH + (col + 1) * LANES]
        cols.append(jnp.where(lo, a, b))
    return jnp.concatenate(cols, axis=1)


def _sample_in_a_kernel(x_ref, g1_ref, win_ref, wkvt_ref, qg_ref, kgc_ref, xqg_ref,
                        qs_ref, v_ref, xs_ref, kt_ref, vt_ref, *, seq):
    h = _rms_rows(_stack_positions(x_ref, seq), g1_ref[...]).astype(BF16)
    z = _dot(h, win_ref[...])
    _write_query_slots(_pair_head_norm(z[:, :A_Q], qg_ref[...]), Q_SLOT_PAIRS, qs_ref)
    v_ref[...] = z[:, A_Q + A_KV:A_Q + 2 * A_KV]
    _write_query_slots(_pair_head_norm(z[:, A_Q + 2 * A_KV:], xqg_ref[...]), X_SLOT_PAIRS, xs_ref)
    kvt = _nt_dot(wkvt_ref[...], h)
    for hk in range(A_KV_HEADS):
        sl = slice(hk * HEAD_DIM, (hk + 1) * HEAD_DIM)
        kh = kvt[sl, :]
        ms = jnp.mean(kh * kh, axis=0, keepdims=True)
        kt_ref[sl, :] = kh * lax.rsqrt(ms + EPS) * kgc_ref[sl, :]
    vt_ref[...] = kvt[A_KV:, :]


def _sample_in_a(x3, norm1_g, w_in_r, q_norm, k_norm, xq_norm):
    s, n, _ = x3.shape
    rows = s * n
    w_kvt = w_in_r[:, A_Q:A_Q + 2 * A_KV].T
    slots = lambda heads: jax.ShapeDtypeStruct((heads, rows, X_WIDTH), BF16)
    full = lambda shape: pl.BlockSpec(shape, lambda i: (0,) * len(shape))
    return pl.pallas_call(
        functools.partial(_sample_in_a_kernel, seq=s),
        grid=(1,),
        in_specs=[
            _const_spec((s, n, D_MODEL)),
            _const_spec((1, D_MODEL)),
            _const_spec(w_in_r.shape),
            _const_spec(w_kvt.shape),
            _const_spec((1, LANES)),
            _const_spec((A_KV, 1)),
            _const_spec((1, LANES)),
        ],
        out_specs=[
            full((A_HEADS, rows, A_KV)), full((rows, A_KV)), full((X_HEADS, rows, X_WIDTH)),
            full((A_KV, rows)), full((A_KV, rows)),
        ],
        out_shape=[
            slots(A_HEADS),
            jax.ShapeDtypeStruct((rows, A_KV), F32),
            slots(X_HEADS),
            jax.ShapeDtypeStruct((A_KV, rows), F32),
            jax.ShapeDtypeStruct((A_KV, rows), F32),
        ],
        compiler_params=_params("arbitrary"),
        name="sample_in_a",
    )(x3, norm1_g.reshape(1, D_MODEL), w_in_r, w_kvt, jnp.tile(q_norm, 2).reshape(1, LANES),
      jnp.tile(k_norm, 4).reshape(A_KV, 1), jnp.tile(xq_norm, 2).reshape(1, LANES))


def _sample_in_b_kernel(x_ref, g1_ref, win_ref, vg_ref, coef_ref, gbias_ref, xqg_ref, o_ref, vn_ref, xs_ref,
                        *, seq, n):
    h = _rms_rows(_stack_positions(x_ref, seq), g1_ref[...]).astype(BF16)
    z = _dot(h, win_ref[...])
    uv = _gelu_exact(z[:, :2 * B_WIDTH])
    u = uv[:, :B_WIDTH]
    vn = _rms_rows(uv[:, B_WIDTH:], vg_ref[...])
    vn_ref[...] = vn
    for t in range(seq):
        mixed = gbias_ref[t:t + 1, :] + coef_ref[t * seq:t * seq + 1, :] * vn[0:n]
        for j in range(1, t + 1):
            mixed = mixed + coef_ref[t * seq + j:t * seq + j + 1, :] * vn[j * n:(j + 1) * n]
        o_ref[t * n:(t + 1) * n, :] = (u[t * n:(t + 1) * n] * mixed).astype(o_ref.dtype)
    _write_query_slots(_pair_head_norm(z[:, 2 * B_WIDTH:], xqg_ref[...]), X_SLOT_PAIRS, xs_ref)


def _sample_in_b(x3, norm1_g, w_in_b, v_norm, w_s, b_s, xq_norm):
    s, n, _ = x3.shape
    rows = s * n
    coef = jnp.repeat(w_s[:, :s, :s].transpose(1, 2, 0).reshape(s * s, B_GROUPS), B_GROUP_DIM, axis=1)
    gbias = jnp.repeat(b_s[:, :s].T, B_GROUP_DIM, axis=1)
    full = lambda shape: pl.BlockSpec(shape, lambda i: (0,) * len(shape))
    return pl.pallas_call(
        functools.partial(_sample_in_b_kernel, seq=s, n=n),
        grid=(1,),
        in_specs=[
            _const_spec((s, n, D_MODEL)),
            _const_spec((1, D_MODEL)),
            _const_spec(w_in_b.shape),
            _const_spec((1, B_WIDTH)),
            _const_spec((s * s, B_WIDTH)),
            _const_spec((s, B_WIDTH)),
            _const_spec((1, LANES)),
        ],
        out_specs=[full((rows, B_WIDTH)), full((rows, B_WIDTH)), full((X_HEADS, rows, X_WIDTH))],
        out_shape=[
            jax.ShapeDtypeStruct((rows, B_WIDTH), BF16),
            jax.ShapeDtypeStruct((rows, B_WIDTH), F32),
            jax.ShapeDtypeStruct((X_HEADS, rows, X_WIDTH), BF16),
        ],
        compiler_params=_params("arbitrary"),
        name="sample_in_b",
    )(x3, norm1_g.reshape(1, D_MODEL), w_in_b, v_norm.reshape(1, B_WIDTH), coef, gbias,
      jnp.tile(xq_norm, 2).reshape(1, LANES))


def _sample_attn_win_kernel(qs_ref, ktn_ref, vtn_ref, vn_ref, slope_ref, sink_ref, ck_ref, cv_ref,
                            o_ref, wk_ref, wv_ref, *, sb, seq):
    rows = A_HEADS * seq
    tq = _div_mod(lax.broadcasted_iota(jnp.int32, (rows, WINDOW), 0), seq)[1]
    lane = lax.broadcasted_iota(jnp.int32, (rows, WINDOW), 1)
    slope = slope_ref[...]
    sink = sink_ref[...]
    d1 = tq + WINDOW - lane
    valid1 = d1 < WINDOW
    bias1 = slope * d1.astype(F32)
    new_seq, new_pos = _div_mod(lane, seq)
    d2 = tq - new_pos
    causal2 = d2 >= 0
    bias2 = slope * d2.astype(F32)
    ktn = ktn_ref[...]
    vtn = vtn_ref[...]
    ktn_b = ktn.astype(BF16)
    vn_b = vn_ref[...].astype(BF16)
    keep_old = lax.broadcasted_iota(jnp.int32, (A_KV, WINDOW), 1) < WINDOW - seq

    def body(n, carry):
        q = qs_ref[n]
        ckt = ck_ref[n]
        cvt = cv_ref[n]
        s1 = jnp.where(valid1, _dot(q, ckt.astype(BF16)) - bias1, NEG)
        s2 = jnp.where(causal2 & (new_seq == n), _dot(q, ktn_b) - bias2, NEG)
        m = jnp.maximum(jnp.maximum(jnp.max(s1, axis=-1, keepdims=True), jnp.max(s2, axis=-1, keepdims=True)), sink)
        e1 = jnp.exp(s1 - m)
        e2 = jnp.exp(s2 - m)
        den = jnp.sum(e1, axis=-1, keepdims=True) + jnp.sum(e2, axis=-1, keepdims=True) + jnp.exp(sink - m)
        o = _nt_dot(e1.astype(BF16), cvt.astype(BF16)) + _dot(e2.astype(BF16), vn_b)
        o_ref[n] = (o / den).astype(o_ref.dtype)
        shift_new = WINDOW - seq - n * seq
        wk_ref[n] = jnp.where(keep_old, pltpu.roll(ckt, WINDOW - seq, 1), pltpu.roll(ktn, shift_new, 1))
        wv_ref[n] = jnp.where(keep_old, pltpu.roll(cvt, WINDOW - seq, 1), pltpu.roll(vtn, shift_new, 1))
        return carry

    lax.fori_loop(0, sb, body, 0, unroll=SAMPLE_SEQ_UNROLL)


def _sample_attn_win(qs, kt_new, vt_new, v_new, slope_rows, sink_rows, ckt, cvt, layer, seq):
    n = qs.shape[0]
    rows = A_HEADS * seq
    assert LANES % seq == 0
    sb = LANES // seq
    assert n % sb == 0

    def seqs(*tail):
        return pl.BlockSpec((sb,) + tail, lambda i: (i,) + (0,) * len(tail))

    cache = pl.BlockSpec((None, sb, A_KV, WINDOW), lambda i: (layer, i, 0, 0))
    win = jax.ShapeDtypeStruct((n, A_KV, WINDOW), F32)
    return pl.pallas_call(
        functools.partial(_sample_attn_win_kernel, sb=sb, seq=seq),
        grid=(n // sb,),
        in_specs=[
            seqs(rows, A_KV),
            pl.BlockSpec((A_KV, LANES), lambda i: (0, i)),
            pl.BlockSpec((A_KV, LANES), lambda i: (0, i)),
            pl.BlockSpec((LANES, A_KV), lambda i: (i, 0)),
            _const_spec((rows, 1)),
            _const_spec((rows, 1)),
            cache,
            cache,
        ],
        out_specs=[seqs(rows, A_KV), seqs(A_KV, WINDOW), seqs(A_KV, WINDOW)],
        out_shape=[jax.ShapeDtypeStruct((n, rows, A_KV), BF16), win, win],
        compiler_params=_params("arbitrary"),
        name="sample_attn_win",
    )(qs, kt_new, vt_new, v_new, slope_rows, sink_rows, ckt, cvt)


def _sample_attn_mem_kernel(xs_ref, mk_ref, mv_ref, xo_ref, *, sb):
    def body(n, carry):
        s = _dot(xs_ref[n], mk_ref[n].astype(BF16))
        m = jnp.max(s, axis=-1, keepdims=True)
        e = jnp.exp(s - m)
        den = jnp.sum(e, axis=-1, keepdims=True)
        xo_ref[n] = (_nt_dot(e.astype(BF16), mv_ref[n].astype(BF16)) / den).astype(xo_ref.dtype)
        return carry

    lax.fori_loop(0, sb, body, 0, unroll=SAMPLE_SEQ_UNROLL)


def _sample_attn_mem(xs, mkt, mvt, layer):
    n, xrows, _ = xs.shape
    sb = min(SAMPLE_MEM_SEQS, n)
    assert n % sb == 0
    qspec = pl.BlockSpec((sb, xrows, X_WIDTH), lambda i: (i, 0, 0))
    cache = pl.BlockSpec((None, sb, X_WIDTH, N_MEM), lambda i: (layer, i, 0, 0))
    return pl.pallas_call(
        functools.partial(_sample_attn_mem_kernel, sb=sb),
        grid=(n // sb,),
        in_specs=[qspec, cache, cache],
        out_specs=qspec,
        out_shape=jax.ShapeDtypeStruct((n, xrows, X_WIDTH), BF16),
        compiler_params=_params("arbitrary"),
        name="sample_attn_mem",
    )(xs, mkt, mvt)


def _slots_to_seq(a, seq, n):
    hh, _, w = a.shape
    return a.reshape(hh, seq, n, w).transpose(2, 0, 1, 3).reshape(n, hh * seq, w)


def _seq_to_cols(a, heads, seq):
    n, _, w = a.shape
    return a.reshape(n, heads, seq, w).transpose(2, 0, 1, 3).reshape(seq, n, heads * w)


def _feature_major(cache):
    l, n, p, kv, d = cache.shape
    return cache.transpose(0, 1, 3, 4, 2).reshape(l, n, kv * d, p)


def kernel(x_prompt, x_sample, cache_win_k, cache_win_v, cache_mem_k, cache_mem_v, state_conv, mem_prompt,
           norm1_g, norm2_g, mem_norm_g, w_in_a, q_norm_a, k_norm_a, sinks_a, w_out_a, w_in_b, v_norm_b, w_s_b,
           b_s_b, w_out_b, w_mem_kv, xq_norm, xk_norm, w_ffn_in, conv_w, conv_b, w_down):
    depth = norm1_g.shape[0]
    assert depth == 2 and w_in_a.shape[0] == 1 and w_in_b.shape[0] == 1
    nb = x_prompt.shape[0]
    ns, seq, _ = x_sample.shape
    assert cache_win_k.shape[2] == WINDOW

    w_in_a_b = w_in_a[0].astype(BF16)
    w_in_b_b = w_in_b[0].astype(BF16)
    w_out_a_b = w_out_a[0].astype(BF16)
    w_out_b_b = w_out_b[0].astype(BF16)
    w_mem_kv_b = w_mem_kv.astype(BF16)
    w_ffn_in_b = w_ffn_in.astype(BF16)
    w_down_b = w_down.astype(BF16)

    mk_all, mv_all, mem_kt_p, mem_vt_p = _memory_kv(mem_prompt, mem_norm_g, w_mem_kv_b, xk_norm)
    w_in_a_r = _reorder_heads(w_in_a_b, 1)
    w_out_a_r = _reorder_heads(w_out_a_b, 0)
    x2, win_k_p, win_v_p, conv_p0 = _layer_a_prompt(
        x_prompt, 0, norm1_g[0], w_in_a_r, q_norm_a[0], k_norm_a[0], xq_norm[0], sinks_a[0], mk_all, mv_all, w_out_a_r,
        norm2_g[0], w_ffn_in_b, conv_w[0], conv_b[0], w_down_b)
    y_prompt, chunk_v_p, conv_p1 = _layer_b_prompt(
        x2, 1, norm1_g[1], w_in_b_b, v_norm_b[0], w_s_b[0], b_s_b[0], xq_norm[1], mk_all, mv_all, w_out_b_b,
        norm2_g[1], w_ffn_in_b, conv_w[1], conv_b[1], w_down_b)

    xs3 = x_sample.transpose(1, 0, 2)
    mkt = _feature_major(cache_mem_k)
    mvt = _feature_major(cache_mem_v)

    def by_seq_cols(a):
        return a.reshape(a.shape[0], seq, ns).transpose(0, 2, 1).reshape(a.shape[0], ns * seq)

    qs, v_new, xslots, kt_new, vt_new = _sample_in_a(xs3, norm1_g[0], w_in_a_r, q_norm_a[0], k_norm_a[0], xq_norm[0])
    slope_rows = jnp.repeat(jnp.asarray(SLOPES, F32), seq).reshape(A_HEADS * seq, 1)
    sink_rows = jnp.repeat(sinks_a[0].astype(F32), seq).reshape(A_HEADS * seq, 1)
    o_s, win_kt_s, win_vt_s = _sample_attn_win(
        _slots_to_seq(qs, seq, ns), by_seq_cols(kt_new), by_seq_cols(vt_new),
        v_new.reshape(seq, ns, A_KV).transpose(1, 0, 2).reshape(ns * seq, A_KV), slope_rows, sink_rows,
        _feature_major(cache_win_k), _feature_major(cache_win_v), 0, seq)
    xo_s = _sample_attn_mem(_slots_to_seq(xslots, seq, ns), mkt, mvt, 0)
    ys1, conv_s0 = _ffn_sample(
        xs3, _seq_to_cols(o_s, A_HEADS, seq), _seq_to_cols(xo_s, X_HEADS, seq),
        w_out_a_r[:A_Q], w_out_a_r[A_Q:], 0, norm2_g[0], w_ffn_in_b, conv_w[0], conv_b[0], w_down_b, state_conv[0])

    o_b, vn_s, xslots = _sample_in_b(ys1, norm1_g[1], w_in_b_b, v_norm_b[0], w_s_b[0], b_s_b[0], xq_norm[1])
    xo_s = _sample_attn_mem(_slots_to_seq(xslots, seq, ns), mkt, mvt, 1)
    ys2, conv_s1 = _ffn_sample(
        ys1, o_b.reshape(seq, ns, B_WIDTH), _seq_to_cols(xo_s, X_HEADS, seq),
        w_out_b_b[:B_WIDTH], w_out_b_b[B_WIDTH:], 1, norm2_g[1], w_ffn_in_b, conv_w[1], conv_b[1], w_down_b, state_conv[1])

    def position_major(ft):
        lead = ft.shape[:-2]
        nd = len(lead)
        return ft.reshape(*lead, A_KV_HEADS, HEAD_DIM, ft.shape[-1]).transpose(*range(nd), nd + 2, nd, nd + 1)

    return (
        y_prompt,
        ys2.transpose(1, 0, 2),
        position_major(win_k_p)[None],
        position_major(win_v_p)[None],
        chunk_v_p.reshape(1, nb, CHUNK, B_WIDTH),
        position_major(mem_kt_p),
        position_major(mem_vt_p),
        jnp.stack([conv_p0, conv_p1]),
        position_major(win_kt_s)[None],
        position_major(win_vt_s)[None],
        vn_s.reshape(seq, ns, B_WIDTH).transpose(1, 0, 2)[None],
        jnp.stack([conv_s0.transpose(1, 0, 2), conv_s1.transpose(1, 0, 2)]),
    )
```

```python
import functools
import math

import numpy as np
import jax
import jax.numpy as jnp
from jax import lax
from jax.experimental import pallas as pl
from jax.experimental.pallas import tpu as pltpu

F32 = jnp.float32
BF16 = jnp.bfloat16

D_MODEL = 1024
HEAD_DIM = 64
A_HEADS = 12
A_KV_HEADS = 4
A_GROUP = A_HEADS // A_KV_HEADS
WINDOW = 128
BLOCK = 128
B_WIDTH = 768
B_GROUPS = 4
B_GROUP_DIM = B_WIDTH // B_GROUPS
CHUNK = 128
X_HEADS = 4
X_WIDTH = X_HEADS * HEAD_DIM
N_MEM = 256
D_FF = 2816
CONV_W = 3
A_Q = A_HEADS * HEAD_DIM
A_KV = A_KV_HEADS * HEAD_DIM
EPS = 1e-6
NEG = -1e30
SCALE = HEAD_DIM ** -0.5
LOG2E = float(np.float32(1.0 / np.log(2.0)))
SQRT_HALF = float(np.sqrt(0.5).astype(np.float32))

LANES = 128
VMEM_LIMIT_BYTES = 56 * 1024 * 1024
GATE_WINDOW_START = (0, 128, 384, 512)
assert B_GROUP_DIM == 192 and B_GROUPS == 4
PROMPT_ROWS = 512
SAMPLE_MEM_SEQS = 16
SAMPLE_FFN_SEQS = 128
SAMPLE_SEQ_UNROLL = 4


def _alibi_slopes(n):
    def pow2(m):
        start = 2.0 ** (-8.0 / m)
        return [start ** (i + 1) for i in range(m)]

    p = 2 ** int(math.floor(math.log2(n)))
    s = pow2(p)
    if p < n:
        s = s + pow2(2 * p)[0::2][: n - p]
    return [float(np.float32(v)) for v in s]


SLOPES = _alibi_slopes(A_HEADS)

Q_LANE_ORDER = (0, 3, 1, 4, 2, 5, 6, 9, 7, 10, 8, 11)
assert all(Q_LANE_ORDER[p] // A_GROUP == 2 * (p // (2 * A_GROUP)) + p % 2 for p in range(A_HEADS))


_SLOTS = A_KV_HEADS // 2
assert Q_LANE_ORDER == tuple(np.arange(A_HEADS).reshape(_SLOTS, 2, A_GROUP).swapaxes(1, 2).reshape(-1).tolist())


def _reorder_heads(w, axis):
    lead, tail = w.shape[:axis], w.shape[axis + 1:]
    q = lax.slice_in_dim(w, 0, A_Q, axis=axis).reshape(*lead, _SLOTS, 2, A_GROUP, HEAD_DIM, *tail)
    q = jnp.swapaxes(q, axis + 1, axis + 2).reshape(*lead, A_Q, *tail)
    return jnp.concatenate([q, lax.slice_in_dim(w, A_Q, w.shape[axis], axis=axis)], axis=axis)


def _params(*sem):
    return pltpu.CompilerParams(dimension_semantics=sem, vmem_limit_bytes=VMEM_LIMIT_BYTES)


def _const_spec(shape):
    nd = len(shape)
    return pl.BlockSpec(shape, lambda *_: (0,) * nd, pipeline_mode=pl.Buffered(1))


def _rms_rows(x, g):
    ms = jnp.mean(x * x, axis=-1, keepdims=True)
    return x * lax.rsqrt(ms + EPS) * g


def _pair_head_norm(z, g2):
    t, w = z.shape
    lo = lax.broadcasted_iota(jnp.int32, (t, LANES), 1) < HEAD_DIM
    out = []
    for p in range(w // LANES):
        zz = z[:, p * LANES:(p + 1) * LANES]
        sq = zz * zz
        s_lo = jnp.sum(jnp.where(lo, sq, 0.0), axis=-1, keepdims=True)
        s_hi = jnp.sum(jnp.where(lo, 0.0, sq), axis=-1, keepdims=True)
        r_lo = lax.rsqrt(s_lo * (1.0 / HEAD_DIM) + EPS)
        r_hi = lax.rsqrt(s_hi * (1.0 / HEAD_DIM) + EPS)
        out.append(zz * jnp.where(lo, r_lo, r_hi) * g2)
    return out


def _nt_dot(a, b):
    return lax.dot_general(a, b, (((1,), (1,)), ((), ())), preferred_element_type=F32)


def _dot(a, b):
    return jnp.dot(a, b, preferred_element_type=F32)


def _div_mod(i, n):
    if n & (n - 1) == 0:
        return i >> (n.bit_length() - 1), i & (n - 1)
    return i // n, i % n


def _gelu_exact(x):
    return 0.5 * x * (1.0 + lax.erf(x * SQRT_HALF))


def _conv_gate(a, am1, am2, up, cw_ref, cb_ref):
    c = cb_ref[...] + ((cw_ref[0:1, :] * am2 + cw_ref[1:2, :] * am1) + cw_ref[2:3, :] * a)
    return ((c * (1.0 / (1.0 + jnp.exp(-c)))) * up).astype(BF16)


def _memkv_kernel(mem_ref, g_ref, w_ref, kg_ref, mkb_ref, mvb_ref, mkt_ref, mvt_ref, *, depth):
    m = mem_ref[0]
    for layer in range(depth):
        h = _rms_rows(m, g_ref[layer]).astype(BF16)
        kv = _dot(h, w_ref[layer])
        mk = jnp.concatenate(_pair_head_norm(kv[:, :X_WIDTH], kg_ref[layer]), axis=1)
        mv = kv[:, X_WIDTH:]
        mkb_ref[layer, 0] = mk.astype(BF16)
        mvb_ref[layer, 0] = mv.astype(BF16)
        mkt_ref[layer, 0] = mk.T
        mvt_ref[layer, 0] = mv.T


def _memory_kv(mem, mem_norm_g, w_mem_kv_b, xk_norm):
    depth = w_mem_kv_b.shape[0]
    n = mem.shape[0]
    g = mem_norm_g.reshape(depth, 1, D_MODEL)
    kg2 = jnp.tile(xk_norm, (1, 2)).reshape(depth, 1, LANES)
    spec = lambda rows, cols: pl.BlockSpec((depth, 1, rows, cols), lambda b: (0, b, 0, 0))
    return pl.pallas_call(
        functools.partial(_memkv_kernel, depth=depth),
        grid=(n,),
        in_specs=[
            pl.BlockSpec((1, N_MEM, D_MODEL), lambda b: (b, 0, 0)),
            _const_spec((depth, 1, D_MODEL)),
            _const_spec((depth, D_MODEL, 2 * X_WIDTH)),
            _const_spec((depth, 1, LANES)),
        ],
        out_specs=[spec(N_MEM, X_WIDTH), spec(N_MEM, X_WIDTH), spec(X_WIDTH, N_MEM), spec(X_WIDTH, N_MEM)],
        out_shape=[
            jax.ShapeDtypeStruct((depth, n, N_MEM, X_WIDTH), BF16),
            jax.ShapeDtypeStruct((depth, n, N_MEM, X_WIDTH), BF16),
            jax.ShapeDtypeStruct((depth, n, X_WIDTH, N_MEM), F32),
            jax.ShapeDtypeStruct((depth, n, X_WIDTH, N_MEM), F32),
        ],
        compiler_params=_params("arbitrary"),
        name="memory_kv",
    )(mem, g, w_mem_kv_b, kg2)


def _half_masked_rows(pair, lo):
    return jnp.concatenate([jnp.where(lo, pair, 0.0), jnp.where(lo, 0.0, pair)], axis=0).astype(BF16)


def _prompt_mem_attention(xq_pairs, mk_ref, mv_ref, ocat, col0):
    t = xq_pairs[0].shape[0]
    lo = lax.broadcasted_iota(jnp.int32, (t, LANES), 1) < HEAD_DIM
    for j, pair in enumerate(xq_pairs):
        mk = mk_ref[0, 0, :, j * LANES:(j + 1) * LANES]
        mv = mv_ref[0, 0, :, j * LANES:(j + 1) * LANES]
        s = _nt_dot(_half_masked_rows(pair * (SCALE * LOG2E), lo), mk)
        m = jnp.max(s, axis=-1, keepdims=True)
        e = jnp.exp2(s - m)
        den = jnp.sum(e, axis=-1, keepdims=True)
        o = _dot(e.astype(BF16), mv) / den
        ocat[:, col0 + j * LANES:col0 + (j + 1) * LANES] = jnp.where(lo, o[:t], o[t:]).astype(BF16)


def _mixer_a_body(t, x, sinks_ref, g1_ref, win_ref, qg_ref, kg_ref, xqg_ref, mk_ref, mv_ref, wout_ref,
                  wk_ref, wv_ref, kbuf, vbuf, ocat, tb):
    h = _rms_rows(x, g1_ref[...]).astype(BF16)
    z = _dot(h, win_ref[...])
    q_pairs = _pair_head_norm(z[:, :A_Q], qg_ref[...])
    k_pairs = _pair_head_norm(z[:, A_Q:A_Q + A_KV], kg_ref[...])
    v = z[:, A_Q + A_KV:A_Q + 2 * A_KV]
    xq_pairs = _pair_head_norm(z[:, A_Q + 2 * A_KV:], xqg_ref[...])

    for p, kp in enumerate(k_pairs):
        wk_ref[0, p * LANES:(p + 1) * LANES, :] = kp[tb - BLOCK:, :].T
        kbuf[BLOCK:BLOCK + tb, p * LANES:(p + 1) * LANES] = kp.astype(BF16)
    wv_ref[0] = v[tb - BLOCK:, :].T
    vbuf[BLOCK:BLOCK + tb, :] = v.astype(BF16)

    qi = lax.broadcasted_iota(jnp.int32, (BLOCK, 2 * BLOCK), 0)
    kj = lax.broadcasted_iota(jnp.int32, (BLOCK, 2 * BLOCK), 1)
    d = qi + BLOCK - kj
    in_window = (d >= 0) & (d < WINDOW)
    key_pos = lax.broadcasted_iota(jnp.int32, (1, 2 * BLOCK), 1).astype(F32)
    query_pos = (lax.broadcasted_iota(jnp.int32, (BLOCK, 1), 0) + BLOCK).astype(F32)

    lo = lax.broadcasted_iota(jnp.int32, (BLOCK, LANES), 1) < HEAD_DIM
    pairs_per_slot = A_HEADS // A_KV_HEADS
    for s in range(tb // BLOCK):
        rows = slice(s * BLOCK, (s + 1) * BLOCK)
        first_key = jnp.where(t > 0, 0, BLOCK) if s == 0 else 0
        valid = in_window & (kj >= first_key)
        for j in range(A_KV // LANES):
            kw = kbuf[s * BLOCK:(s + 2) * BLOCK, j * LANES:(j + 1) * LANES]
            vw = vbuf[s * BLOCK:(s + 2) * BLOCK, j * LANES:(j + 1) * LANES]
            q_rows = jnp.concatenate(
                [_half_masked_rows(q_pairs[pairs_per_slot * j + c][rows] * (SCALE * LOG2E), lo)
                 for c in range(pairs_per_slot)], axis=0)
            sc_all = _nt_dot(q_rows, kw)
            e_parts, den_parts = [], []
            for r in range(2 * pairs_per_slot):
                hq = Q_LANE_ORDER[2 * pairs_per_slot * j + r]
                slope2 = SLOPES[hq] * LOG2E
                sc = jnp.where(valid, sc_all[r * BLOCK:(r + 1) * BLOCK] + slope2 * key_pos, NEG)
                sink = sinks_ref[hq] * LOG2E + slope2 * query_pos
                m = jnp.maximum(jnp.max(sc, axis=-1, keepdims=True), sink)
                e = jnp.exp2(sc - m)
                den_parts.append(jnp.sum(e, axis=-1, keepdims=True) + jnp.exp2(sink - m))
                e_parts.append(e.astype(BF16))
            o_all = _dot(jnp.concatenate(e_parts, axis=0), vw)
            for c in range(pairs_per_slot):
                o_lo = o_all[2 * c * BLOCK:(2 * c + 1) * BLOCK] / den_parts[2 * c]
                o_hi = o_all[(2 * c + 1) * BLOCK:(2 * c + 2) * BLOCK] / den_parts[2 * c + 1]
                col = pairs_per_slot * j + c
                ocat[rows, col * LANES:(col + 1) * LANES] = jnp.where(lo, o_lo, o_hi).astype(BF16)

    _prompt_mem_attention(xq_pairs, mk_ref, mv_ref, ocat, A_Q)
    x1 = x + _dot(ocat[...], wout_ref[...])

    kbuf[0:BLOCK, :] = kbuf[tb:tb + BLOCK, :]
    vbuf[0:BLOCK, :] = vbuf[tb:tb + BLOCK, :]
    return x1


def _ffn_body(x, g2_ref, wfi_ref, cw_ref, cb_ref, wd_ref, tail_ref, tb):
    h = _rms_rows(x, g2_ref[...]).astype(BF16)
    gu = _dot(h, wfi_ref[...])
    a = gu[:, :D_FF]
    row = lax.broadcasted_iota(jnp.int32, (tb, 1), 0)
    prev_last = tail_ref[0, 7:8, :]
    am1 = jnp.where(row == 0, prev_last, pltpu.roll(a, 1, 0))
    am2 = jnp.where(row == 0, tail_ref[0, 6:7, :], jnp.where(row == 1, prev_last, pltpu.roll(a, 2, 0)))
    tail_ref[0] = a[tb - 8:, :]
    return x + _dot(_conv_gate(a, am1, am2, gu[:, D_FF:], cw_ref, cb_ref), wd_ref[...])


FFN_CAST_CHUNKS = 16


def _stream_cast(src, dst, stage, sem):
    rows = stage.shape[1]

    def chunk_copy(c):
        return pltpu.make_async_copy(src.at[pl.ds(c * rows, rows), :], stage.at[c % 2], sem.at[c % 2])

    chunk_copy(0).start()
    for c in range(FFN_CAST_CHUNKS):
        if c + 1 < FFN_CAST_CHUNKS:
            chunk_copy(c + 1).start()
        chunk_copy(c).wait()
        dst[c * rows:(c + 1) * rows, :] = stage[c % 2].astype(BF16)


def _ffn_weights(layer, wfi_hbm, wd_hbm, wfi_out, wd_out, wfi_v, wd_v, stage_i, stage_d, sem_in, sem_out):
    step = pl.program_id(0) * pl.num_programs(1) + pl.program_id(1)
    writebacks = (pltpu.make_async_copy(wfi_v, wfi_out, sem_out.at[0]),
                  pltpu.make_async_copy(wd_v, wd_out, sem_out.at[1]))

    @pl.when(step == 0)
    def _():
        _stream_cast(wfi_hbm.at[layer], wfi_v, stage_i, sem_in)
        _stream_cast(wd_hbm.at[layer], wd_v, stage_d, sem_in)
        for copy in writebacks:
            copy.start()

    @pl.when(step == pl.num_programs(0) * pl.num_programs(1) - 1)
    def _():
        for copy in writebacks:
            copy.wait()


def _layer_a_kernel(sinks_ref, x_ref, g1_ref, win_ref, qg_ref, kg_ref, xqg_ref, mk_ref, mv_ref, wout_ref,
                    g2_ref, cw_ref, cb_ref, wfi_hbm, wd_hbm, y_ref, wk_ref, wv_ref, tail_ref, wfi_out, wd_out,
                    kbuf, vbuf, ocat, wfi_v, wd_v, stage_i, stage_d, sem_in, sem_out, *, tb, layer):
    t = pl.program_id(1)

    @pl.when(t == 0)
    def _():
        kbuf[0:BLOCK, :] = jnp.zeros((BLOCK, A_KV), BF16)
        vbuf[0:BLOCK, :] = jnp.zeros((BLOCK, A_KV), BF16)
        tail_ref[0] = jnp.zeros((8, D_FF), F32)

    x1 = _mixer_a_body(t, x_ref[0], sinks_ref, g1_ref, win_ref, qg_ref, kg_ref, xqg_ref, mk_ref, mv_ref, wout_ref,
                       wk_ref, wv_ref, kbuf, vbuf, ocat, tb)
    _ffn_weights(layer, wfi_hbm, wd_hbm, wfi_out, wd_out, wfi_v, wd_v, stage_i, stage_d, sem_in, sem_out)
    y_ref[0] = _ffn_body(x1, g2_ref, wfi_v, cw_ref, cb_ref, wd_v, tail_ref, tb)


def _layer_specs(tb, layer):
    xspec = pl.BlockSpec((1, tb, D_MODEL), lambda b, i: (b, i, 0))
    mspec = pl.BlockSpec((1, 1, N_MEM, X_WIDTH), lambda b, i: (layer, b, 0, 0))
    per_seq = lambda rows, width: pl.BlockSpec((1, rows, width), lambda b, i: (b, 0, 0))
    return xspec, mspec, per_seq(8, D_FF), per_seq


def _ffn_in_specs():
    hbm = pl.BlockSpec(memory_space=pl.ANY)
    return [_const_spec((1, D_MODEL)), _const_spec((CONV_W, D_FF)), _const_spec((1, D_FF)), hbm, hbm]


def _ffn_weight_outputs():
    hbm = pl.BlockSpec(memory_space=pl.ANY)
    shapes = [jax.ShapeDtypeStruct((D_MODEL, 2 * D_FF), BF16), jax.ShapeDtypeStruct((D_FF, D_MODEL), BF16)]
    assert D_MODEL % FFN_CAST_CHUNKS == 0 and D_FF % FFN_CAST_CHUNKS == 0
    scratch = [
        pltpu.VMEM((D_MODEL, 2 * D_FF), BF16),
        pltpu.VMEM((D_FF, D_MODEL), BF16),
        pltpu.VMEM((2, D_MODEL // FFN_CAST_CHUNKS, 2 * D_FF), F32),
        pltpu.VMEM((2, D_FF // FFN_CAST_CHUNKS, D_MODEL), F32),
        pltpu.SemaphoreType.DMA((2,)),
        pltpu.SemaphoreType.DMA((2,)),
    ]
    return [hbm, hbm], shapes, scratch


def _layer_a_prompt(x, layer, norm1_g, w_in_b, q_norm, k_norm, xq_norm, sinks, mk_all, mv_all, w_out_b,
                    norm2_g, w_ffn_in, conv_w, conv_b, w_down):
    n, t, _ = x.shape
    tb = min(PROMPT_ROWS, t)
    a_in = w_in_b.shape[1]
    xspec, mspec, tailspec, per_seq = _layer_specs(tb, layer)
    wspec = per_seq(A_KV, BLOCK)
    w_specs, w_shapes, w_scratch = _ffn_weight_outputs()
    y, wk, wv, tail, wfi_b, wd_b = pl.pallas_call(
        functools.partial(_layer_a_kernel, tb=tb, layer=layer),
        grid=(n, t // tb),
        in_specs=[
            pl.BlockSpec(memory_space=pltpu.SMEM),
            xspec,
            _const_spec((1, D_MODEL)),
            _const_spec((D_MODEL, a_in)),
            _const_spec((1, LANES)),
            _const_spec((1, LANES)),
            _const_spec((1, LANES)),
            mspec,
            mspec,
            _const_spec((A_Q + X_WIDTH, D_MODEL)),
        ] + _ffn_in_specs(),
        out_specs=[xspec, wspec, wspec, tailspec] + w_specs,
        out_shape=[
            jax.ShapeDtypeStruct((n, t, D_MODEL), F32),
            jax.ShapeDtypeStruct((n, A_KV, BLOCK), F32),
            jax.ShapeDtypeStruct((n, A_KV, BLOCK), F32),
            jax.ShapeDtypeStruct((n, 8, D_FF), F32),
        ] + w_shapes,
        scratch_shapes=[
            pltpu.VMEM((tb + BLOCK, A_KV), BF16),
            pltpu.VMEM((tb + BLOCK, A_KV), BF16),
            pltpu.VMEM((tb, A_Q + X_WIDTH), BF16),
        ] + w_scratch,
        compiler_params=_params("arbitrary", "arbitrary"),
        name="layer_a_prompt",
    )(sinks, x, norm1_g.reshape(1, D_MODEL), w_in_b, jnp.tile(q_norm, 2).reshape(1, LANES),
      jnp.tile(k_norm, 2).reshape(1, LANES), jnp.tile(xq_norm, 2).reshape(1, LANES), mk_all, mv_all, w_out_b,
      norm2_g.reshape(1, D_MODEL), conv_w, conv_b.reshape(1, D_FF), w_ffn_in, w_down)
    return y, wk, wv, tail[:, 8 - (CONV_W - 1):, :], wfi_b, wd_b


def _mixer_b_body(x, g1_ref, win_ref, vg_ref, ws_ref, bias_ref, xqg_ref, mk_ref, mv_ref, wout_ref, cv_ref, ocat, tb):
    h = _rms_rows(x, g1_ref[...]).astype(BF16)
    z = _dot(h, win_ref[...])
    uv = _gelu_exact(z[:, :2 * B_WIDTH])
    u = uv[:, :B_WIDTH]
    vn = _rms_rows(uv[:, B_WIDTH:], vg_ref[...])
    cv_ref[0] = vn[tb - CHUNK:, :]
    xq_pairs = _pair_head_norm(z[:, 2 * B_WIDTH:], xqg_ref[...])

    wi = lax.broadcasted_iota(jnp.int32, (CHUNK, CHUNK), 0)
    wj = lax.broadcasted_iota(jnp.int32, (CHUNK, CHUNK), 1)
    causal = wi >= wj
    w_tril = [jnp.where(causal, ws_ref[g], 0.0).astype(BF16) for g in range(B_GROUPS)]
    lo = lax.broadcasted_iota(jnp.int32, (CHUNK, LANES), 1) < HEAD_DIM
    bias = bias_ref[...]
    for c in range(tb // CHUNK):
        vc = vn[c * CHUNK:(c + 1) * CHUNK, :].astype(BF16)
        r = [_dot(w_tril[g], vc[:, c0:c0 + 256]) for g, c0 in enumerate(GATE_WINDOW_START)]
        mixed = jnp.concatenate(
            [r[0][:, :128], jnp.where(lo, r[0][:, 128:], r[1][:, :128]), r[1][:, 128:],
             r[2][:, :128], jnp.where(lo, r[2][:, 128:], r[3][:, :128]), r[3][:, 128:]], axis=1)
        o = u[c * CHUNK:(c + 1) * CHUNK, :] * (mixed + bias)
        ocat[c * CHUNK:(c + 1) * CHUNK, 0:B_WIDTH] = o.astype(BF16)

    _prompt_mem_attention(xq_pairs, mk_ref, mv_ref, ocat, B_WIDTH)
    return x + _dot(ocat[...], wout_ref[...])


def _layer_b_kernel(x_ref, g1_ref, win_ref, vg_ref, ws_ref, bias_ref, xqg_ref, mk_ref, mv_ref, wout_ref,
                    g2_ref, cw_ref, cb_ref, wfi_hbm, wd_hbm, y_ref, cv_ref, tail_ref, wfi_out, wd_out,
                    ocat, wfi_v, wd_v, stage_i, stage_d, sem_in, sem_out, *, tb, layer):
    @pl.when(pl.program_id(1) == 0)
    def _():
        tail_ref[0] = jnp.zeros((8, D_FF), F32)

    x1 = _mixer_b_body(x_ref[0], g1_ref, win_ref, vg_ref, ws_ref, bias_ref, xqg_ref, mk_ref, mv_ref, wout_ref,
                       cv_ref, ocat, tb)
    _ffn_weights(layer, wfi_hbm, wd_hbm, wfi_out, wd_out, wfi_v, wd_v, stage_i, stage_d, sem_in, sem_out)
    y_ref[0] = _ffn_body(x1, g2_ref, wfi_v, cw_ref, cb_ref, wd_v, tail_ref, tb)


def _gate_bias_rows(b_s):
    return jnp.repeat(b_s.T, B_GROUP_DIM, axis=1)


def _layer_b_prompt(x, layer, norm1_g, w_in_b, v_norm, w_s, b_s, xq_norm, mk_all, mv_all, w_out_b,
                    norm2_g, w_ffn_in, conv_w, conv_b, w_down):
    n, t, _ = x.shape
    tb = min(PROMPT_ROWS, t)
    b_in = w_in_b.shape[1]
    xspec, mspec, tailspec, per_seq = _layer_specs(tb, layer)
    w_specs, w_shapes, w_scratch = _ffn_weight_outputs()
    y, cv, tail, wfi_b, wd_b = pl.pallas_call(
        functools.partial(_layer_b_kernel, tb=tb, layer=layer),
        grid=(n, t // tb),
        in_specs=[
            xspec,
            _const_spec((1, D_MODEL)),
            _const_spec((D_MODEL, b_in)),
            _const_spec((1, B_WIDTH)),
            _const_spec((B_GROUPS, CHUNK, CHUNK)),
            _const_spec((CHUNK, B_WIDTH)),
            _const_spec((1, LANES)),
            mspec,
            mspec,
            _const_spec((B_WIDTH + X_WIDTH, D_MODEL)),
        ] + _ffn_in_specs(),
        out_specs=[xspec, per_seq(CHUNK, B_WIDTH), tailspec] + w_specs,
        out_shape=[
            jax.ShapeDtypeStruct((n, t, D_MODEL), F32),
            jax.ShapeDtypeStruct((n, CHUNK, B_WIDTH), F32),
            jax.ShapeDtypeStruct((n, 8, D_FF), F32),
        ] + w_shapes,
        scratch_shapes=[pltpu.VMEM((tb, B_WIDTH + X_WIDTH), BF16)] + w_scratch,
        compiler_params=_params("arbitrary", "arbitrary"),
        name="layer_b_prompt",
    )(x, norm1_g.reshape(1, D_MODEL), w_in_b, v_norm.reshape(1, B_WIDTH), w_s, _gate_bias_rows(b_s),
      jnp.tile(xq_norm, 2).reshape(1, LANES), mk_all, mv_all, w_out_b,
      norm2_g.reshape(1, D_MODEL), conv_w, conv_b.reshape(1, D_FF), w_ffn_in, w_down)
    return y, cv, tail[:, 8 - (CONV_W - 1):, :], wfi_b, wd_b


def _stack_positions(ref, seq):
    return jnp.concatenate([ref[t] for t in range(seq)], axis=0)


def _ffn_sample_kernel(x_ref, om_ref, ox_ref, wom_ref, wox_ref, g2_ref, wfi_ref, cw_ref, cb_ref, wd_ref, p0_ref, p1_ref,
                       y_ref, atail_ref, *, seq, nb, mixer_slots):
    om = _stack_positions(om_ref, seq)
    if mixer_slots:
        om = _compact_slots(om, Q_SLOT_PAIRS)
    ox = _compact_slots(_stack_positions(ox_ref, seq), X_SLOT_PAIRS)
    x1 = _stack_positions(x_ref, seq) + _dot(om, wom_ref[...]) + _dot(ox, wox_ref[...])
    h = _rms_rows(x1, g2_ref[...]).astype(BF16)
    gu = _dot(h, wfi_ref[...])
    a = gu[:, :D_FF]
    blocks = [p0_ref[...], p1_ref[...]] + [a[t * nb:(t + 1) * nb] for t in range(seq)]
    am2 = jnp.concatenate(blocks[:seq], axis=0)
    am1 = jnp.concatenate(blocks[1:seq + 1], axis=0)
    y = x1 + _dot(_conv_gate(a, am1, am2, gu[:, D_FF:], cw_ref, cb_ref), wd_ref[...])
    for t in range(seq):
        y_ref[t] = y[t * nb:(t + 1) * nb]
    for r in range(CONV_W - 1):
        atail_ref[r] = blocks[seq + r]


def _ffn_sample(x3, om3, ox3, w_om, w_ox, norm2_g, w_ffn_in_b, conv_w, conv_b, w_down_b, prefix):
    s, n, _ = x3.shape
    km, kx = om3.shape[2], ox3.shape[2]
    nb = min(SAMPLE_FFN_SEQS, n)
    assert n % nb == 0 and s >= CONV_W - 1

    def blk(lead, width):
        return pl.BlockSpec((lead, nb, width), lambda i: (0, i, 0))

    pspec = pl.BlockSpec((nb, D_FF), lambda i: (i, 0))
    mixer_slots = km != w_om.shape[0]
    assert km == (A_HEADS * X_WIDTH if mixer_slots else w_om.shape[0]) and kx == X_HEADS * X_WIDTH
    return pl.pallas_call(
        functools.partial(_ffn_sample_kernel, seq=s, nb=nb, mixer_slots=mixer_slots),
        grid=(n // nb,),
        in_specs=[
            blk(s, D_MODEL),
            blk(s, km),
            blk(s, kx),
            _const_spec(w_om.shape),
            _const_spec(w_ox.shape),
            _const_spec((1, D_MODEL)),
            _const_spec((D_MODEL, 2 * D_FF)),
            _const_spec((CONV_W, D_FF)),
            _const_spec((1, D_FF)),
            _const_spec((D_FF, D_MODEL)),
            pspec,
            pspec,
        ],
        out_specs=[blk(s, D_MODEL), blk(CONV_W - 1, D_FF)],
        out_shape=[
            jax.ShapeDtypeStruct((s, n, D_MODEL), F32),
            jax.ShapeDtypeStruct((CONV_W - 1, n, D_FF), F32),
        ],
        compiler_params=_params("arbitrary"),
        name="ffn_sample",
    )(x3, om3, ox3, w_om, w_ox, norm2_g.reshape(1, D_MODEL), w_ffn_in_b, conv_w, conv_b.reshape(1, D_FF), w_down_b,
      prefix[:, 0, :], prefix[:, 1, :])


Q_SLOT_PAIRS = tuple((Q_LANE_ORDER[2 * c], Q_LANE_ORDER[2 * c + 1], c // A_GROUP) for c in range(A_HEADS // 2))
X_SLOT_PAIRS = tuple((2 * c, 2 * c + 1, c) for c in range(X_HEADS // 2))


def _write_query_slots(pairs, slot_pairs, out_ref):
    t = pairs[0].shape[0]
    lo = lax.broadcasted_iota(jnp.int32, (t, LANES), 1) < HEAD_DIM
    zeros = jnp.zeros((t, LANES), out_ref.dtype)
    for pair, (h_lo, h_hi, col) in zip(pairs, slot_pairs):
        scaled = pair * SCALE
        for head, own in ((h_lo, jnp.where(lo, scaled, 0.0)), (h_hi, jnp.where(lo, 0.0, scaled))):
            own = own.astype(out_ref.dtype)
            out_ref[head] = jnp.concatenate([own, zeros] if col == 0 else [zeros, own], axis=1)


def _compact_slots(o, slot_pairs):
    lo = lax.broadcasted_iota(jnp.int32, (o.shape[0], LANES), 1) < HEAD_DIM
    cols = []
    for h_lo, h_hi, col in slot_pairs:
        a = o[:, h_lo * X_WIDTH + col * LANES:h_lo * X_WIDTH + (col + 1) * LANES]
        b = o[:, h_hi * X_WIDTH + col * LANES:h_hi * X_WIDTH + (col + 1) * LANES]
        cols.append(jnp.where(lo, a, b))
    return jnp.concatenate(cols, axis=1)


def _sample_in_a_kernel(x_ref, g1_ref, win_ref, wkvt_ref, qg_ref, kgc_ref, xqg_ref,
                        qs_ref, v_ref, xs_ref, kt_ref, vt_ref, *, seq):
    h = _rms_rows(_stack_positions(x_ref, seq), g1_ref[...]).astype(BF16)
    z = _dot(h, win_ref[...])
    _write_query_slots(_pair_head_norm(z[:, :A_Q], qg_ref[...]), Q_SLOT_PAIRS, qs_ref)
    v_ref[...] = z[:, A_Q + A_KV:A_Q + 2 * A_KV]
    _write_query_slots(_pair_head_norm(z[:, A_Q + 2 * A_KV:], xqg_ref[...]), X_SLOT_PAIRS, xs_ref)
    kvt = _nt_dot(wkvt_ref[...], h)
    for hk in range(A_KV_HEADS):
        sl = slice(hk * HEAD_DIM, (hk + 1) * HEAD_DIM)
        kh = kvt[sl, :]
        ms = jnp.mean(kh * kh, axis=0, keepdims=True)
        kt_ref[sl, :] = kh * lax.rsqrt(ms + EPS) * kgc_ref[sl, :]
    vt_ref[...] = kvt[A_KV:, :]


def _sample_in_a(x3, norm1_g, w_in_r, q_norm, k_norm, xq_norm):
    s, n, _ = x3.shape
    rows = s * n
    w_kvt = w_in_r[:, A_Q:A_Q + 2 * A_KV].T
    slots = lambda heads: jax.ShapeDtypeStruct((heads, rows, X_WIDTH), BF16)
    full = lambda shape: pl.BlockSpec(shape, lambda i: (0,) * len(shape))
    return pl.pallas_call(
        functools.partial(_sample_in_a_kernel, seq=s),
        grid=(1,),
        in_specs=[
            _const_spec((s, n, D_MODEL)),
            _const_spec((1, D_MODEL)),
            _const_spec(w_in_r.shape),
            _const_spec(w_kvt.shape),
            _const_spec((1, LANES)),
            _const_spec((A_KV, 1)),
            _const_spec((1, LANES)),
        ],
        out_specs=[
            full((A_HEADS, rows, A_KV)), full((rows, A_KV)), full((X_HEADS, rows, X_WIDTH)),
            full((A_KV, rows)), full((A_KV, rows)),
        ],
        out_shape=[
            slots(A_HEADS),
            jax.ShapeDtypeStruct((rows, A_KV), F32),
            slots(X_HEADS),
            jax.ShapeDtypeStruct((A_KV, rows), F32),
            jax.ShapeDtypeStruct((A_KV, rows), F32),
        ],
        compiler_params=_params("arbitrary"),
        name="sample_in_a",
    )(x3, norm1_g.reshape(1, D_MODEL), w_in_r, w_kvt, jnp.tile(q_norm, 2).reshape(1, LANES),
      jnp.tile(k_norm, 4).reshape(A_KV, 1), jnp.tile(xq_norm, 2).reshape(1, LANES))


def _sample_in_b_kernel(x_ref, g1_ref, win_ref, vg_ref, coef_ref, gbias_ref, xqg_ref, o_ref, vn_ref, xs_ref,
                        *, seq, n):
    h = _rms_rows(_stack_positions(x_ref, seq), g1_ref[...]).astype(BF16)
    z = _dot(h, win_ref[...])
    uv = _gelu_exact(z[:, :2 * B_WIDTH])
    u = uv[:, :B_WIDTH]
    vn = _rms_rows(uv[:, B_WIDTH:], vg_ref[...])
    vn_ref[...] = vn
    for t in range(seq):
        mixed = gbias_ref[t:t + 1, :] + coef_ref[t * seq:t * seq + 1, :] * vn[0:n]
        for j in range(1, t + 1):
            mixed = mixed + coef_ref[t * seq + j:t * seq + j + 1, :] * vn[j * n:(j + 1) * n]
        o_ref[t * n:(t + 1) * n, :] = (u[t * n:(t + 1) * n] * mixed).astype(o_ref.dtype)
    _write_query_slots(_pair_head_norm(z[:, 2 * B_WIDTH:], xqg_ref[...]), X_SLOT_PAIRS, xs_ref)


def _sample_in_b(x3, norm1_g, w_in_b, v_norm, w_s, b_s, xq_norm):
    s, n, _ = x3.shape
    rows = s * n
    coef = jnp.repeat(w_s[:, :s, :s].transpose(1, 2, 0).reshape(s * s, B_GROUPS), B_GROUP_DIM, axis=1)
    gbias = jnp.repeat(b_s[:, :s].T, B_GROUP_DIM, axis=1)
    full = lambda shape: pl.BlockSpec(shape, lambda i: (0,) * len(shape))
    return pl.pallas_call(
        functools.partial(_sample_in_b_kernel, seq=s, n=n),
        grid=(1,),
        in_specs=[
            _const_spec((s, n, D_MODEL)),
            _const_spec((1, D_MODEL)),
            _const_spec(w_in_b.shape),
            _const_spec((1, B_WIDTH)),
            _const_spec((s * s, B_WIDTH)),
            _const_spec((s, B_WIDTH)),
            _const_spec((1, LANES)),
        ],
        out_specs=[full((rows, B_WIDTH)), full((rows, B_WIDTH)), full((X_HEADS, rows, X_WIDTH))],
        out_shape=[
            jax.ShapeDtypeStruct((rows, B_WIDTH), BF16),
            jax.ShapeDtypeStruct((rows, B_WIDTH), F32),
            jax.ShapeDtypeStruct((X_HEADS, rows, X_WIDTH), BF16),
        ],
        compiler_params=_params("arbitrary"),
        name="sample_in_b",
    )(x3, norm1_g.reshape(1, D_MODEL), w_in_b, v_norm.reshape(1, B_WIDTH), coef, gbias,
      jnp.tile(xq_norm, 2).reshape(1, LANES))


def _sample_attn_win_kernel(qs_ref, ktn_ref, vtn_ref, vn_ref, slope_ref, sink_ref, ck_ref, cv_ref,
                            o_ref, wk_ref, wv_ref, *, sb, seq):
    rows = A_HEADS * seq
    tq = _div_mod(lax.broadcasted_iota(jnp.int32, (rows, WINDOW), 0), seq)[1]
    lane = lax.broadcasted_iota(jnp.int32, (rows, WINDOW), 1)
    slope = slope_ref[...]
    sink = sink_ref[...]
    d1 = tq + WINDOW - lane
    valid1 = d1 < WINDOW
    bias1 = slope * d1.astype(F32)
    new_seq, new_pos = _div_mod(lane, seq)
    d2 = tq - new_pos
    causal2 = d2 >= 0
    bias2 = slope * d2.astype(F32)
    ktn = ktn_ref[...]
    vtn = vtn_ref[...]
    ktn_b = ktn.astype(BF16)
    vn_b = vn_ref[...].astype(BF16)
    keep_old = lax.broadcasted_iota(jnp.int32, (A_KV, WINDOW), 1) < WINDOW - seq

    def body(n, carry):
        q = qs_ref[n]
        ckt = ck_ref[n]
        cvt = cv_ref[n]
        s1 = jnp.where(valid1, _dot(q, ckt.astype(BF16)) - bias1, NEG)
        s2 = jnp.where(causal2 & (new_seq == n), _dot(q, ktn_b) - bias2, NEG)
        m = jnp.maximum(jnp.maximum(jnp.max(s1, axis=-1, keepdims=True), jnp.max(s2, axis=-1, keepdims=True)), sink)
        e1 = jnp.exp(s1 - m)
        e2 = jnp.exp(s2 - m)
        den = jnp.sum(e1, axis=-1, keepdims=True) + jnp.sum(e2, axis=-1, keepdims=True) + jnp.exp(sink - m)
        o = _nt_dot(e1.astype(BF16), cvt.astype(BF16)) + _dot(e2.astype(BF16), vn_b)
        o_ref[n] = (o / den).astype(o_ref.dtype)
        shift_new = WINDOW - seq - n * seq
        wk_ref[n] = jnp.where(keep_old, pltpu.roll(ckt, WINDOW - seq, 1), pltpu.roll(ktn, shift_new, 1))
        wv_ref[n] = jnp.where(keep_old, pltpu.roll(cvt, WINDOW - seq, 1), pltpu.roll(vtn, shift_new, 1))
        return carry

    lax.fori_loop(0, sb, body, 0, unroll=SAMPLE_SEQ_UNROLL)


def _sample_attn_win(qs, kt_new, vt_new, v_new, slope_rows, sink_rows, ckt, cvt, layer, seq):
    n = qs.shape[0]
    rows = A_HEADS * seq
    assert LANES % seq == 0
    sb = LANES // seq
    assert n % sb == 0

    def seqs(*tail):
        return pl.BlockSpec((sb,) + tail, lambda i: (i,) + (0,) * len(tail))

    cache = pl.BlockSpec((None, sb, A_KV, WINDOW), lambda i: (layer, i, 0, 0))
    win = jax.ShapeDtypeStruct((n, A_KV, WINDOW), F32)
    return pl.pallas_call(
        functools.partial(_sample_attn_win_kernel, sb=sb, seq=seq),
        grid=(n // sb,),
        in_specs=[
            seqs(rows, A_KV),
            pl.BlockSpec((A_KV, LANES), lambda i: (0, i)),
            pl.BlockSpec((A_KV, LANES), lambda i: (0, i)),
            pl.BlockSpec((LANES, A_KV), lambda i: (i, 0)),
            _const_spec((rows, 1)),
            _const_spec((rows, 1)),
            cache,
            cache,
        ],
        out_specs=[seqs(rows, A_KV), seqs(A_KV, WINDOW), seqs(A_KV, WINDOW)],
        out_shape=[jax.ShapeDtypeStruct((n, rows, A_KV), BF16), win, win],
        compiler_params=_params("arbitrary"),
        name="sample_attn_win",
    )(qs, kt_new, vt_new, v_new, slope_rows, sink_rows, ckt, cvt)


def _sample_attn_mem_kernel(xs_ref, mk_ref, mv_ref, xo_ref, *, sb):
    def body(n, carry):
        s = _dot(xs_ref[n], mk_ref[n].astype(BF16))
        m = jnp.max(s, axis=-1, keepdims=True)
        e = jnp.exp(s - m)
        den = jnp.sum(e, axis=-1, keepdims=True)
        xo_ref[n] = (_nt_dot(e.astype(BF16), mv_ref[n].astype(BF16)) / den).astype(xo_ref.dtype)
        return carry

    lax.fori_loop(0, sb, body, 0, unroll=SAMPLE_SEQ_UNROLL)


def _sample_attn_mem(xs, mkt, mvt, layer):
    n, xrows, _ = xs.shape
    sb = min(SAMPLE_MEM_SEQS, n)
    assert n % sb == 0
    qspec = pl.BlockSpec((sb, xrows, X_WIDTH), lambda i: (i, 0, 0))
    cache = pl.BlockSpec((None, sb, X_WIDTH, N_MEM), lambda i: (layer, i, 0, 0))
    return pl.pallas_call(
        functools.partial(_sample_attn_mem_kernel, sb=sb),
        grid=(n // sb,),
        in_specs=[qspec, cache, cache],
        out_specs=qspec,
        out_shape=jax.ShapeDtypeStruct((n, xrows, X_WIDTH), BF16),
        compiler_params=_params("arbitrary"),
        name="sample_attn_mem",
    )(xs, mkt, mvt)


def _slots_to_seq(a, seq, n):
    hh, _, w = a.shape
    return a.reshape(hh, seq, n, w).transpose(2, 0, 1, 3).reshape(n, hh * seq, w)


def _seq_to_cols(a, heads, seq):
    n, _, w = a.shape
    return a.reshape(n, heads, seq, w).transpose(2, 0, 1, 3).reshape(seq, n, heads * w)


def _feature_major(cache):
    l, n, p, kv, d = cache.shape
    return cache.transpose(0, 1, 3, 4, 2).reshape(l, n, kv * d, p)


def kernel(x_prompt, x_sample, cache_win_k, cache_win_v, cache_mem_k, cache_mem_v, state_conv, mem_prompt,
           norm1_g, norm2_g, mem_norm_g, w_in_a, q_norm_a, k_norm_a, sinks_a, w_out_a, w_in_b, v_norm_b, w_s_b,
           b_s_b, w_out_b, w_mem_kv, xq_norm, xk_norm, w_ffn_in, conv_w, conv_b, w_down):
    depth = norm1_g.shape[0]
    assert depth == 2 and w_in_a.shape[0] == 1 and w_in_b.shape[0] == 1
    nb = x_prompt.shape[0]
    ns, seq, _ = x_sample.shape
    assert cache_win_k.shape[2] == WINDOW

    w_in_a_b = w_in_a[0].astype(BF16)
    w_in_b_b = w_in_b[0].astype(BF16)
    w_out_a_b = w_out_a[0].astype(BF16)
    w_out_b_b = w_out_b[0].astype(BF16)
    w_mem_kv_b = w_mem_kv.astype(BF16)

    mk_all, mv_all, mem_kt_p, mem_vt_p = _memory_kv(mem_prompt, mem_norm_g, w_mem_kv_b, xk_norm)
    w_in_a_r = _reorder_heads(w_in_a_b, 1)
    w_out_a_r = _reorder_heads(w_out_a_b, 0)
    x2, win_k_p, win_v_p, conv_p0, wfi0_b, wd0_b = _layer_a_prompt(
        x_prompt, 0, norm1_g[0], w_in_a_r, q_norm_a[0], k_norm_a[0], xq_norm[0], sinks_a[0], mk_all, mv_all, w_out_a_r,
        norm2_g[0], w_ffn_in, conv_w[0], conv_b[0], w_down)
    y_prompt, chunk_v_p, conv_p1, wfi1_b, wd1_b = _layer_b_prompt(
        x2, 1, norm1_g[1], w_in_b_b, v_norm_b[0], w_s_b[0], b_s_b[0], xq_norm[1], mk_all, mv_all, w_out_b_b,
        norm2_g[1], w_ffn_in, conv_w[1], conv_b[1], w_down)

    xs3 = x_sample.transpose(1, 0, 2)
    mkt = _feature_major(cache_mem_k)
    mvt = _feature_major(cache_mem_v)

    def by_seq_cols(a):
        return a.reshape(a.shape[0], seq, ns).transpose(0, 2, 1).reshape(a.shape[0], ns * seq)

    qs, v_new, xslots, kt_new, vt_new = _sample_in_a(xs3, norm1_g[0], w_in_a_r, q_norm_a[0], k_norm_a[0], xq_norm[0])
    slope_rows = jnp.repeat(jnp.asarray(SLOPES, F32), seq).reshape(A_HEADS * seq, 1)
    sink_rows = jnp.repeat(sinks_a[0].astype(F32), seq).reshape(A_HEADS * seq, 1)
    o_s, win_kt_s, win_vt_s = _sample_attn_win(
        _slots_to_seq(qs, seq, ns), by_seq_cols(kt_new), by_seq_cols(vt_new),
        v_new.reshape(seq, ns, A_KV).transpose(1, 0, 2).reshape(ns * seq, A_KV), slope_rows, sink_rows,
        _feature_major(cache_win_k), _feature_major(cache_win_v), 0, seq)
    xo_s = _sample_attn_mem(_slots_to_seq(xslots, seq, ns), mkt, mvt, 0)
    ys1, conv_s0 = _ffn_sample(
        xs3, _seq_to_cols(o_s, A_HEADS, seq), _seq_to_cols(xo_s, X_HEADS, seq),
        w_out_a_r[:A_Q], w_out_a_r[A_Q:], norm2_g[0], wfi0_b, conv_w[0], conv_b[0], wd0_b, state_conv[0])

    o_b, vn_s, xslots = _sample_in_b(ys1, norm1_g[1], w_in_b_b, v_norm_b[0], w_s_b[0], b_s_b[0], xq_norm[1])
    xo_s = _sample_attn_mem(_slots_to_seq(xslots, seq, ns), mkt, mvt, 1)
    ys2, conv_s1 = _ffn_sample(
        ys1, o_b.reshape(seq, ns, B_WIDTH), _seq_to_cols(xo_s, X_HEADS, seq),
        w_out_b_b[:B_WIDTH], w_out_b_b[B_WIDTH:], norm2_g[1], wfi1_b, conv_w[1], conv_b[1], wd1_b, state_conv[1])

    def position_major(ft):
        lead = ft.shape[:-2]
        nd = len(lead)
        return ft.reshape(*lead, A_KV_HEADS, HEAD_DIM, ft.shape[-1]).transpose(*range(nd), nd + 2, nd, nd + 1)

    return (
        y_prompt,
        ys2.transpose(1, 0, 2),
        position_major(win_k_p)[None],
        position_major(win_v_p)[None],
        chunk_v_p.reshape(1, nb, CHUNK, B_WIDTH),
        position_major(mem_kt_p),
        position_major(mem_vt_p),
        jnp.stack([conv_p0, conv_p1]),
        position_major(win_kt_s)[None],
        position_major(win_vt_s)[None],
        vn_s.reshape(seq, ns, B_WIDTH).transpose(1, 0, 2)[None],
        jnp.stack([conv_s0.transpose(1, 0, 2), conv_s1.transpose(1, 0, 2)]),
    )
```

```python
import functools
import math

import numpy as np
import jax
import jax.numpy as jnp
from jax import lax
from jax.experimental import pallas as pl
from jax.experimental.pallas import tpu as pltpu

F32 = jnp.float32
BF16 = jnp.bfloat16

D_MODEL = 1024
HEAD_DIM = 64
A_HEADS = 12
A_KV_HEADS = 4
A_GROUP = A_HEADS // A_KV_HEADS
WINDOW = 128
BLOCK = 128
B_WIDTH = 768
B_GROUPS = 4
B_GROUP_DIM = B_WIDTH // B_GROUPS
CHUNK = 128
X_HEADS = 4
X_WIDTH = X_HEADS * HEAD_DIM
N_MEM = 256
D_FF = 2816
CONV_W = 3
A_Q = A_HEADS * HEAD_DIM
A_KV = A_KV_HEADS * HEAD_DIM
EPS = 1e-6
NEG = -1e30
SCALE = HEAD_DIM ** -0.5
LOG2E = float(np.float32(1.0 / np.log(2.0)))
SQRT_HALF = float(np.sqrt(0.5).astype(np.float32))

LANES = 128
VMEM_LIMIT_BYTES = 56 * 1024 * 1024
GATE_WINDOW_START = (0, 128, 384, 512)
assert B_GROUP_DIM == 192 and B_GROUPS == 4
PROMPT_ROWS = 512
SAMPLE_MEM_SEQS = 16
SAMPLE_FFN_SEQS = 128
SAMPLE_SEQ_UNROLL = 4


def _alibi_slopes(n):
    def pow2(m):
        start = 2.0 ** (-8.0 / m)
        return [start ** (i + 1) for i in range(m)]

    p = 2 ** int(math.floor(math.log2(n)))
    s = pow2(p)
    if p < n:
        s = s + pow2(2 * p)[0::2][: n - p]
    return [float(np.float32(v)) for v in s]


SLOPES = _alibi_slopes(A_HEADS)

Q_LANE_ORDER = (0, 3, 1, 4, 2, 5, 6, 9, 7, 10, 8, 11)
assert all(Q_LANE_ORDER[p] // A_GROUP == 2 * (p // (2 * A_GROUP)) + p % 2 for p in range(A_HEADS))


_SLOTS = A_KV_HEADS // 2
assert Q_LANE_ORDER == tuple(np.arange(A_HEADS).reshape(_SLOTS, 2, A_GROUP).swapaxes(1, 2).reshape(-1).tolist())


def _reorder_heads(w, axis):
    lead, tail = w.shape[:axis], w.shape[axis + 1:]
    q = lax.slice_in_dim(w, 0, A_Q, axis=axis).reshape(*lead, _SLOTS, 2, A_GROUP, HEAD_DIM, *tail)
    q = jnp.swapaxes(q, axis + 1, axis + 2).reshape(*lead, A_Q, *tail)
    return jnp.concatenate([q, lax.slice_in_dim(w, A_Q, w.shape[axis], axis=axis)], axis=axis)


def _params(*sem):
    return pltpu.CompilerParams(dimension_semantics=sem, vmem_limit_bytes=VMEM_LIMIT_BYTES)


def _const_spec(shape):
    nd = len(shape)
    return pl.BlockSpec(shape, lambda *_: (0,) * nd, pipeline_mode=pl.Buffered(1))


def _layer_spec(layer, shape):
    nd = len(shape)
    return pl.BlockSpec((None,) + tuple(shape), lambda *_: (layer,) + (0,) * nd, pipeline_mode=pl.Buffered(1))


def _rms_rows(x, g):
    ms = jnp.mean(x * x, axis=-1, keepdims=True)
    return x * lax.rsqrt(ms + EPS) * g


def _pair_head_norm(z, g2):
    t, w = z.shape
    lo = lax.broadcasted_iota(jnp.int32, (t, LANES), 1) < HEAD_DIM
    out = []
    for p in range(w // LANES):
        zz = z[:, p * LANES:(p + 1) * LANES]
        sq = zz * zz
        s_lo = jnp.sum(jnp.where(lo, sq, 0.0), axis=-1, keepdims=True)
        s_hi = jnp.sum(jnp.where(lo, 0.0, sq), axis=-1, keepdims=True)
        r_lo = lax.rsqrt(s_lo * (1.0 / HEAD_DIM) + EPS)
        r_hi = lax.rsqrt(s_hi * (1.0 / HEAD_DIM) + EPS)
        out.append(zz * jnp.where(lo, r_lo, r_hi) * g2)
    return out


def _nt_dot(a, b):
    return lax.dot_general(a, b, (((1,), (1,)), ((), ())), preferred_element_type=F32)


def _dot(a, b):
    return jnp.dot(a, b, preferred_element_type=F32)


def _div_mod(i, n):
    if n & (n - 1) == 0:
        return i >> (n.bit_length() - 1), i & (n - 1)
    return i // n, i % n


def _gelu_exact(x):
    return 0.5 * x * (1.0 + lax.erf(x * SQRT_HALF))


def _conv_gate(a, am1, am2, up, cw_ref, cb_ref):
    c = cb_ref[...] + ((cw_ref[0:1, :] * am2 + cw_ref[1:2, :] * am1) + cw_ref[2:3, :] * a)
    return ((c * (1.0 / (1.0 + jnp.exp(-c)))) * up).astype(BF16)


def _memkv_kernel(mem_ref, g_ref, w_ref, kg_ref, mkb_ref, mvb_ref, mkt_ref, mvt_ref, *, depth):
    m = mem_ref[0]
    for layer in range(depth):
        h = _rms_rows(m, g_ref[layer]).astype(BF16)
        kv = _dot(h, w_ref[layer])
        mk = jnp.concatenate(_pair_head_norm(kv[:, :X_WIDTH], kg_ref[layer]), axis=1)
        mv = kv[:, X_WIDTH:]
        mkb_ref[layer, 0] = mk.astype(BF16)
        mvb_ref[layer, 0] = mv.astype(BF16)
        mkt_ref[layer, 0] = mk.T
        mvt_ref[layer, 0] = mv.T


def _memory_kv(mem, mem_norm_g, w_mem_kv_b, xk_norm):
    depth = w_mem_kv_b.shape[0]
    n = mem.shape[0]
    g = mem_norm_g.reshape(depth, 1, D_MODEL)
    kg2 = jnp.tile(xk_norm, (1, 2)).reshape(depth, 1, LANES)
    spec = lambda rows, cols: pl.BlockSpec((depth, 1, rows, cols), lambda b: (0, b, 0, 0))
    return pl.pallas_call(
        functools.partial(_memkv_kernel, depth=depth),
        grid=(n,),
        in_specs=[
            pl.BlockSpec((1, N_MEM, D_MODEL), lambda b: (b, 0, 0)),
            _const_spec((depth, 1, D_MODEL)),
            _const_spec((depth, D_MODEL, 2 * X_WIDTH)),
            _const_spec((depth, 1, LANES)),
        ],
        out_specs=[spec(N_MEM, X_WIDTH), spec(N_MEM, X_WIDTH), spec(X_WIDTH, N_MEM), spec(X_WIDTH, N_MEM)],
        out_shape=[
            jax.ShapeDtypeStruct((depth, n, N_MEM, X_WIDTH), BF16),
            jax.ShapeDtypeStruct((depth, n, N_MEM, X_WIDTH), BF16),
            jax.ShapeDtypeStruct((depth, n, X_WIDTH, N_MEM), F32),
            jax.ShapeDtypeStruct((depth, n, X_WIDTH, N_MEM), F32),
        ],
        compiler_params=_params("arbitrary"),
        name="memory_kv",
    )(mem, g, w_mem_kv_b, kg2)


def _half_masked_rows(pair, lo):
    return jnp.concatenate([jnp.where(lo, pair, 0.0), jnp.where(lo, 0.0, pair)], axis=0).astype(BF16)


def _prompt_mem_attention(xq_pairs, mk_ref, mv_ref, ocat, col0):
    t = xq_pairs[0].shape[0]
    lo = lax.broadcasted_iota(jnp.int32, (t, LANES), 1) < HEAD_DIM
    for j, pair in enumerate(xq_pairs):
        mk = mk_ref[0, 0, :, j * LANES:(j + 1) * LANES]
        mv = mv_ref[0, 0, :, j * LANES:(j + 1) * LANES]
        s = _nt_dot(_half_masked_rows(pair * (SCALE * LOG2E), lo), mk)
        m = jnp.max(s, axis=-1, keepdims=True)
        e = jnp.exp2(s - m)
        den = jnp.sum(e, axis=-1, keepdims=True)
        o = _dot(e.astype(BF16), mv) / den
        ocat[:, col0 + j * LANES:col0 + (j + 1) * LANES] = jnp.where(lo, o[:t], o[t:]).astype(BF16)


def _mixer_a_body(t, x, sinks_ref, g1_ref, win_ref, qg_ref, kg_ref, xqg_ref, mk_ref, mv_ref, wout_ref,
                  wk_ref, wv_ref, kbuf, vbuf, ocat, tb):
    h = _rms_rows(x, g1_ref[...]).astype(BF16)
    z = _dot(h, win_ref[...])
    q_pairs = _pair_head_norm(z[:, :A_Q], qg_ref[...])
    k_pairs = _pair_head_norm(z[:, A_Q:A_Q + A_KV], kg_ref[...])
    v = z[:, A_Q + A_KV:A_Q + 2 * A_KV]
    xq_pairs = _pair_head_norm(z[:, A_Q + 2 * A_KV:], xqg_ref[...])

    for p, kp in enumerate(k_pairs):
        wk_ref[0, p * LANES:(p + 1) * LANES, :] = kp[tb - BLOCK:, :].T
        kbuf[BLOCK:BLOCK + tb, p * LANES:(p + 1) * LANES] = kp.astype(BF16)
    wv_ref[0] = v[tb - BLOCK:, :].T
    vbuf[BLOCK:BLOCK + tb, :] = v.astype(BF16)

    qi = lax.broadcasted_iota(jnp.int32, (BLOCK, BLOCK), 0)
    kj = lax.broadcasted_iota(jnp.int32, (BLOCK, BLOCK), 1)
    from_prev = kj > qi
    key_pos = jnp.where(from_prev, kj, kj + BLOCK).astype(F32)
    query_pos = (lax.broadcasted_iota(jnp.int32, (BLOCK, 1), 0) + BLOCK).astype(F32)

    lo = lax.broadcasted_iota(jnp.int32, (BLOCK, LANES), 1) < HEAD_DIM
    pairs_per_slot = A_HEADS // A_KV_HEADS
    for s in range(tb // BLOCK):
        rows = slice(s * BLOCK, (s + 1) * BLOCK)
        prev_keys = (jnp.where(t > 0, BLOCK, 0) if s == 0 else BLOCK)
        valid = jnp.logical_not(from_prev) | (kj < prev_keys)
        for j in range(A_KV // LANES):
            kw = kbuf[s * BLOCK:(s + 2) * BLOCK, j * LANES:(j + 1) * LANES]
            vw = vbuf[s * BLOCK:(s + 2) * BLOCK, j * LANES:(j + 1) * LANES]
            q_rows = jnp.concatenate(
                [_half_masked_rows(q_pairs[pairs_per_slot * j + c][rows] * (SCALE * LOG2E), lo)
                 for c in range(pairs_per_slot)], axis=0)
            sc_all = _nt_dot(q_rows, kw)
            e_parts, den_parts = [], []
            for r in range(2 * pairs_per_slot):
                hq = Q_LANE_ORDER[2 * pairs_per_slot * j + r]
                slope2 = SLOPES[hq] * LOG2E
                both = sc_all[r * BLOCK:(r + 1) * BLOCK]
                sc = jnp.where(from_prev, both[:, :BLOCK], both[:, BLOCK:]) + slope2 * key_pos
                sc = jnp.where(valid, sc, NEG)
                sink = sinks_ref[hq] * LOG2E + slope2 * query_pos
                m = jnp.maximum(jnp.max(sc, axis=-1, keepdims=True), sink)
                e = jnp.exp2(sc - m)
                den_parts.append(jnp.sum(e, axis=-1, keepdims=True) + jnp.exp2(sink - m))
                e_parts.append(jnp.concatenate([jnp.where(from_prev, e, 0.0), jnp.where(from_prev, 0.0, e)],
                                               axis=1).astype(BF16))
            o_all = _dot(jnp.concatenate(e_parts, axis=0), vw)
            for c in range(pairs_per_slot):
                o_lo = o_all[2 * c * BLOCK:(2 * c + 1) * BLOCK] / den_parts[2 * c]
                o_hi = o_all[(2 * c + 1) * BLOCK:(2 * c + 2) * BLOCK] / den_parts[2 * c + 1]
                col = pairs_per_slot * j + c
                ocat[rows, col * LANES:(col + 1) * LANES] = jnp.where(lo, o_lo, o_hi).astype(BF16)

    _prompt_mem_attention(xq_pairs, mk_ref, mv_ref, ocat, A_Q)
    x1 = x + _dot(ocat[...], wout_ref[...])

    kbuf[0:BLOCK, :] = kbuf[tb:tb + BLOCK, :]
    vbuf[0:BLOCK, :] = vbuf[tb:tb + BLOCK, :]
    return x1


def _ffn_body(x, g2_ref, wfi_ref, cw_ref, cb_ref, wd_ref, tail_ref, tb):
    h = _rms_rows(x, g2_ref[...]).astype(BF16)
    gu = _dot(h, wfi_ref[...])
    a = gu[:, :D_FF]
    row = lax.broadcasted_iota(jnp.int32, (tb, 1), 0)
    prev_last = tail_ref[0, 7:8, :]
    am1 = jnp.where(row == 0, prev_last, pltpu.roll(a, 1, 0))
    am2 = jnp.where(row == 0, tail_ref[0, 6:7, :], jnp.where(row == 1, prev_last, pltpu.roll(a, 2, 0)))
    tail_ref[0] = a[tb - 8:, :]
    return x + _dot(_conv_gate(a, am1, am2, gu[:, D_FF:], cw_ref, cb_ref), wd_ref[...])


def _layer_a_kernel(sinks_ref, x_ref, g1_ref, win_ref, qg_ref, kg_ref, xqg_ref, mk_ref, mv_ref, wout_ref,
                    g2_ref, wfi_ref, cw_ref, cb_ref, wd_ref, y_ref, wk_ref, wv_ref, tail_ref, kbuf, vbuf, ocat, *, tb):
    t = pl.program_id(1)

    @pl.when(t == 0)
    def _():
        kbuf[0:BLOCK, :] = jnp.zeros((BLOCK, A_KV), BF16)
        vbuf[0:BLOCK, :] = jnp.zeros((BLOCK, A_KV), BF16)
        tail_ref[0] = jnp.zeros((8, D_FF), F32)

    x1 = _mixer_a_body(t, x_ref[0], sinks_ref, g1_ref, win_ref, qg_ref, kg_ref, xqg_ref, mk_ref, mv_ref, wout_ref,
                       wk_ref, wv_ref, kbuf, vbuf, ocat, tb)
    y_ref[0] = _ffn_body(x1, g2_ref, wfi_ref, cw_ref, cb_ref, wd_ref, tail_ref, tb)


def _layer_specs(tb, layer):
    xspec = pl.BlockSpec((1, tb, D_MODEL), lambda b, i: (b, i, 0))
    mspec = pl.BlockSpec((1, 1, N_MEM, X_WIDTH), lambda b, i: (layer, b, 0, 0))
    per_seq = lambda rows, width: pl.BlockSpec((1, rows, width), lambda b, i: (b, 0, 0))
    return xspec, mspec, per_seq(8, D_FF), per_seq


def _ffn_specs(layer):
    return [
        _const_spec((1, D_MODEL)),
        _layer_spec(layer, (D_MODEL, 2 * D_FF)),
        _const_spec((CONV_W, D_FF)),
        _const_spec((1, D_FF)),
        _layer_spec(layer, (D_FF, D_MODEL)),
    ]


def _layer_a_prompt(x, layer, norm1_g, w_in_b, q_norm, k_norm, xq_norm, sinks, mk_all, mv_all, w_out_b,
                    norm2_g, w_ffn_in_b, conv_w, conv_b, w_down_b):
    n, t, _ = x.shape
    tb = min(PROMPT_ROWS, t)
    a_in = w_in_b.shape[1]
    xspec, mspec, tailspec, per_seq = _layer_specs(tb, layer)
    wspec = per_seq(A_KV, BLOCK)
    y, wk, wv, tail = pl.pallas_call(
        functools.partial(_layer_a_kernel, tb=tb),
        grid=(n, t // tb),
        in_specs=[
            pl.BlockSpec(memory_space=pltpu.SMEM),
            xspec,
            _const_spec((1, D_MODEL)),
            _const_spec((D_MODEL, a_in)),
            _const_spec((1, LANES)),
            _const_spec((1, LANES)),
            _const_spec((1, LANES)),
            mspec,
            mspec,
            _const_spec((A_Q + X_WIDTH, D_MODEL)),
        ] + _ffn_specs(layer),
        out_specs=[xspec, wspec, wspec, tailspec],
        out_shape=[
            jax.ShapeDtypeStruct((n, t, D_MODEL), F32),
            jax.ShapeDtypeStruct((n, A_KV, BLOCK), F32),
            jax.ShapeDtypeStruct((n, A_KV, BLOCK), F32),
            jax.ShapeDtypeStruct((n, 8, D_FF), F32),
        ],
        scratch_shapes=[
            pltpu.VMEM((tb + BLOCK, A_KV), BF16),
            pltpu.VMEM((tb + BLOCK, A_KV), BF16),
            pltpu.VMEM((tb, A_Q + X_WIDTH), BF16),
        ],
        compiler_params=_params("arbitrary", "arbitrary"),
        name="layer_a_prompt",
    )(sinks, x, norm1_g.reshape(1, D_MODEL), w_in_b, jnp.tile(q_norm, 2).reshape(1, LANES),
      jnp.tile(k_norm, 2).reshape(1, LANES), jnp.tile(xq_norm, 2).reshape(1, LANES), mk_all, mv_all, w_out_b,
      norm2_g.reshape(1, D_MODEL), w_ffn_in_b, conv_w, conv_b.reshape(1, D_FF), w_down_b)
    return y, wk, wv, tail[:, 8 - (CONV_W - 1):, :]


def _mixer_b_body(x, g1_ref, win_ref, vg_ref, ws_ref, bias_ref, xqg_ref, mk_ref, mv_ref, wout_ref, cv_ref, ocat, tb):
    h = _rms_rows(x, g1_ref[...]).astype(BF16)
    z = _dot(h, win_ref[...])
    uv = _gelu_exact(z[:, :2 * B_WIDTH])
    u = uv[:, :B_WIDTH]
    vn = _rms_rows(uv[:, B_WIDTH:], vg_ref[...])
    cv_ref[0] = vn[tb - CHUNK:, :]
    xq_pairs = _pair_head_norm(z[:, 2 * B_WIDTH:], xqg_ref[...])

    wi = lax.broadcasted_iota(jnp.int32, (CHUNK, CHUNK), 0)
    wj = lax.broadcasted_iota(jnp.int32, (CHUNK, CHUNK), 1)
    causal = wi >= wj
    w_tril = [jnp.where(causal, ws_ref[g], 0.0).astype(BF16) for g in range(B_GROUPS)]
    lo = lax.broadcasted_iota(jnp.int32, (CHUNK, LANES), 1) < HEAD_DIM
    bias = bias_ref[...]
    for c in range(tb // CHUNK):
        vc = vn[c * CHUNK:(c + 1) * CHUNK, :].astype(BF16)
        r = [_dot(w_tril[g], vc[:, c0:c0 + 256]) for g, c0 in enumerate(GATE_WINDOW_START)]
        mixed = jnp.concatenate(
            [r[0][:, :128], jnp.where(lo, r[0][:, 128:], r[1][:, :128]), r[1][:, 128:],
             r[2][:, :128], jnp.where(lo, r[2][:, 128:], r[3][:, :128]), r[3][:, 128:]], axis=1)
        o = u[c * CHUNK:(c + 1) * CHUNK, :] * (mixed + bias)
        ocat[c * CHUNK:(c + 1) * CHUNK, 0:B_WIDTH] = o.astype(BF16)

    _prompt_mem_attention(xq_pairs, mk_ref, mv_ref, ocat, B_WIDTH)
    return x + _dot(ocat[...], wout_ref[...])


def _layer_b_kernel(x_ref, g1_ref, win_ref, vg_ref, ws_ref, bias_ref, xqg_ref, mk_ref, mv_ref, wout_ref,
                    g2_ref, wfi_ref, cw_ref, cb_ref, wd_ref, y_ref, cv_ref, tail_ref, ocat, *, tb):
    @pl.when(pl.program_id(1) == 0)
    def _():
        tail_ref[0] = jnp.zeros((8, D_FF), F32)

    x1 = _mixer_b_body(x_ref[0], g1_ref, win_ref, vg_ref, ws_ref, bias_ref, xqg_ref, mk_ref, mv_ref, wout_ref,
                       cv_ref, ocat, tb)
    y_ref[0] = _ffn_body(x1, g2_ref, wfi_ref, cw_ref, cb_ref, wd_ref, tail_ref, tb)


def _gate_bias_rows(b_s):
    return jnp.repeat(b_s.T, B_GROUP_DIM, axis=1)


def _layer_b_prompt(x, layer, norm1_g, w_in_b, v_norm, w_s, b_s, xq_norm, mk_all, mv_all, w_out_b,
                    norm2_g, w_ffn_in_b, conv_w, conv_b, w_down_b):
    n, t, _ = x.shape
    tb = min(PROMPT_ROWS, t)
    b_in = w_in_b.shape[1]
    xspec, mspec, tailspec, per_seq = _layer_specs(tb, layer)
    y, cv, tail = pl.pallas_call(
        functools.partial(_layer_b_kernel, tb=tb),
        grid=(n, t // tb),
        in_specs=[
            xspec,
            _const_spec((1, D_MODEL)),
            _const_spec((D_MODEL, b_in)),
            _const_spec((1, B_WIDTH)),
            _const_spec((B_GROUPS, CHUNK, CHUNK)),
            _const_spec((CHUNK, B_WIDTH)),
            _const_spec((1, LANES)),
            mspec,
            mspec,
            _const_spec((B_WIDTH + X_WIDTH, D_MODEL)),
        ] + _ffn_specs(layer),
        out_specs=[xspec, per_seq(CHUNK, B_WIDTH), tailspec],
        out_shape=[
            jax.ShapeDtypeStruct((n, t, D_MODEL), F32),
            jax.ShapeDtypeStruct((n, CHUNK, B_WIDTH), F32),
            jax.ShapeDtypeStruct((n, 8, D_FF), F32),
        ],
        scratch_shapes=[pltpu.VMEM((tb, B_WIDTH + X_WIDTH), BF16)],
        compiler_params=_params("arbitrary", "arbitrary"),
        name="layer_b_prompt",
    )(x, norm1_g.reshape(1, D_MODEL), w_in_b, v_norm.reshape(1, B_WIDTH), w_s, _gate_bias_rows(b_s),
      jnp.tile(xq_norm, 2).reshape(1, LANES), mk_all, mv_all, w_out_b,
      norm2_g.reshape(1, D_MODEL), w_ffn_in_b, conv_w, conv_b.reshape(1, D_FF), w_down_b)
    return y, cv, tail[:, 8 - (CONV_W - 1):, :]


def _stack_positions(ref, seq):
    return jnp.concatenate([ref[t] for t in range(seq)], axis=0)


def _ffn_sample_kernel(x_ref, om_ref, ox_ref, wom_ref, wox_ref, g2_ref, wfi_ref, cw_ref, cb_ref, wd_ref, p0_ref, p1_ref,
                       y_ref, atail_ref, *, seq, nb, mixer_slots):
    om = _stack_positions(om_ref, seq)
    if mixer_slots:
        om = _compact_slots(om, Q_SLOT_PAIRS)
    ox = _compact_slots(_stack_positions(ox_ref, seq), X_SLOT_PAIRS)
    x1 = _stack_positions(x_ref, seq) + _dot(om, wom_ref[...]) + _dot(ox, wox_ref[...])
    h = _rms_rows(x1, g2_ref[...]).astype(BF16)
    gu = _dot(h, wfi_ref[...])
    a = gu[:, :D_FF]
    blocks = [p0_ref[...], p1_ref[...]] + [a[t * nb:(t + 1) * nb] for t in range(seq)]
    am2 = jnp.concatenate(blocks[:seq], axis=0)
    am1 = jnp.concatenate(blocks[1:seq + 1], axis=0)
    y = x1 + _dot(_conv_gate(a, am1, am2, gu[:, D_FF:], cw_ref, cb_ref), wd_ref[...])
    for t in range(seq):
        y_ref[t] = y[t * nb:(t + 1) * nb]
    for r in range(CONV_W - 1):
        atail_ref[r] = blocks[seq + r]


def _ffn_sample(x3, om3, ox3, w_om, w_ox, layer, norm2_g, w_ffn_in_b, conv_w, conv_b, w_down_b, prefix):
    s, n, _ = x3.shape
    km, kx = om3.shape[2], ox3.shape[2]
    nb = min(SAMPLE_FFN_SEQS, n)
    assert n % nb == 0 and s >= CONV_W - 1

    def blk(lead, width):
        return pl.BlockSpec((lead, nb, width), lambda i: (0, i, 0))

    pspec = pl.BlockSpec((nb, D_FF), lambda i: (i, 0))
    mixer_slots = km != w_om.shape[0]
    assert km == (A_HEADS * X_WIDTH if mixer_slots else w_om.shape[0]) and kx == X_HEADS * X_WIDTH
    return pl.pallas_call(
        functools.partial(_ffn_sample_kernel, seq=s, nb=nb, mixer_slots=mixer_slots),
        grid=(n // nb,),
        in_specs=[
            blk(s, D_MODEL),
            blk(s, km),
            blk(s, kx),
            _const_spec(w_om.shape),
            _const_spec(w_ox.shape),
            _const_spec((1, D_MODEL)),
            _layer_spec(layer, (D_MODEL, 2 * D_FF)),
            _const_spec((CONV_W, D_FF)),
            _const_spec((1, D_FF)),
            _layer_spec(layer, (D_FF, D_MODEL)),
            pspec,
            pspec,
        ],
        out_specs=[blk(s, D_MODEL), blk(CONV_W - 1, D_FF)],
        out_shape=[
            jax.ShapeDtypeStruct((s, n, D_MODEL), F32),
            jax.ShapeDtypeStruct((CONV_W - 1, n, D_FF), F32),
        ],
        compiler_params=_params("arbitrary"),
        name="ffn_sample",
    )(x3, om3, ox3, w_om, w_ox, norm2_g.reshape(1, D_MODEL), w_ffn_in_b, conv_w, conv_b.reshape(1, D_FF), w_down_b,
      prefix[:, 0, :], prefix[:, 1, :])


Q_SLOT_PAIRS = tuple((Q_LANE_ORDER[2 * c], Q_LANE_ORDER[2 * c + 1], c // A_GROUP) for c in range(A_HEADS // 2))
X_SLOT_PAIRS = tuple((2 * c, 2 * c + 1, c) for c in range(X_HEADS // 2))


def _write_query_slots(pairs, slot_pairs, out_ref):
    t = pairs[0].shape[0]
    lo = lax.broadcasted_iota(jnp.int32, (t, LANES), 1) < HEAD_DIM
    zeros = jnp.zeros((t, LANES), out_ref.dtype)
    for pair, (h_lo, h_hi, col) in zip(pairs, slot_pairs):
        scaled = pair * SCALE
        for head, own in ((h_lo, jnp.where(lo, scaled, 0.0)), (h_hi, jnp.where(lo, 0.0, scaled))):
            own = own.astype(out_ref.dtype)
            out_ref[head] = jnp.concatenate([own, zeros] if col == 0 else [zeros, own], axis=1)


def _compact_slots(o, slot_pairs):
    lo = lax.broadcasted_iota(jnp.int32, (o.shape[0], LANES), 1) < HEAD_DIM
    cols = []
    for h_lo, h_hi, col in slot_pairs:
        a = o[:, h_lo * X_WIDTH + col * LANES:h_lo * X_WIDTH + (col + 1) * LANES]
        b = o[:, h_hi * X_WIDTH + col * LANES:h_hi * X_WIDTH + (col + 1) * LANES]
        cols.append(jnp.where(lo, a, b))
    return jnp.concatenate(cols, axis=1)


def _sample_in_a_kernel(x_ref, g1_ref, win_ref, wkvt_ref, qg_ref, kgc_ref, xqg_ref,
                        qs_ref, v_ref, xs_ref, kt_ref, vt_ref, *, seq):
    h = _rms_rows(_stack_positions(x_ref, seq), g1_ref[...]).astype(BF16)
    z = _dot(h, win_ref[...])
    _write_query_slots(_pair_head_norm(z[:, :A_Q], qg_ref[...]), Q_SLOT_PAIRS, qs_ref)
    v_ref[...] = z[:, A_Q + A_KV:A_Q + 2 * A_KV]
    _write_query_slots(_pair_head_norm(z[:, A_Q + 2 * A_KV:], xqg_ref[...]), X_SLOT_PAIRS, xs_ref)
    kvt = _nt_dot(wkvt_ref[...], h)
    for hk in range(A_KV_HEADS):
        sl = slice(hk * HEAD_DIM, (hk + 1) * HEAD_DIM)
        kh = kvt[sl, :]
        ms = jnp.mean(kh * kh, axis=0, keepdims=True)
        kt_ref[sl, :] = kh * lax.rsqrt(ms + EPS) * kgc_ref[sl, :]
    vt_ref[...] = kvt[A_KV:, :]


def _sample_in_a(x3, norm1_g, w_in_r, q_norm, k_norm, xq_norm):
    s, n, _ = x3.shape
    rows = s * n
    w_kvt = w_in_r[:, A_Q:A_Q + 2 * A_KV].T
    slots = lambda heads: jax.ShapeDtypeStruct((heads, rows, X_WIDTH), BF16)
    full = lambda shape: pl.BlockSpec(shape, lambda i: (0,) * len(shape))
    return pl.pallas_call(
        functools.partial(_sample_in_a_kernel, seq=s),
        grid=(1,),
        in_specs=[
            _const_spec((s, n, D_MODEL)),
            _const_spec((1, D_MODEL)),
            _const_spec(w_in_r.shape),
            _const_spec(w_kvt.shape),
            _const_spec((1, LANES)),
            _const_spec((A_KV, 1)),
            _const_spec((1, LANES)),
        ],
        out_specs=[
            full((A_HEADS, rows, A_KV)), full((rows, A_KV)), full((X_HEADS, rows, X_WIDTH)),
            full((A_KV, rows)), full((A_KV, rows)),
        ],
        out_shape=[
            slots(A_HEADS),
            jax.ShapeDtypeStruct((rows, A_KV), F32),
            slots(X_HEADS),
            jax.ShapeDtypeStruct((A_KV, rows), F32),
            jax.ShapeDtypeStruct((A_KV, rows), F32),
        ],
        compiler_params=_params("arbitrary"),
        name="sample_in_a",
    )(x3, norm1_g.reshape(1, D_MODEL), w_in_r, w_kvt, jnp.tile(q_norm, 2).reshape(1, LANES),
      jnp.tile(k_norm, 4).reshape(A_KV, 1), jnp.tile(xq_norm, 2).reshape(1, LANES))


def _sample_in_b_kernel(x_ref, g1_ref, win_ref, vg_ref, coef_ref, gbias_ref, xqg_ref, o_ref, vn_ref, xs_ref,
                        *, seq, n):
    h = _rms_rows(_stack_positions(x_ref, seq), g1_ref[...]).astype(BF16)
    z = _dot(h, win_ref[...])
    uv = _gelu_exact(z[:, :2 * B_WIDTH])
    u = uv[:, :B_WIDTH]
    vn = _rms_rows(uv[:, B_WIDTH:], vg_ref[...])
    vn_ref[...] = vn
    for t in range(seq):
        mixed = gbias_ref[t:t + 1, :] + coef_ref[t * seq:t * seq + 1, :] * vn[0:n]
        for j in range(1, t + 1):
            mixed = mixed + coef_ref[t * seq + j:t * seq + j + 1, :] * vn[j * n:(j + 1) * n]
        o_ref[t * n:(t + 1) * n, :] = (u[t * n:(t + 1) * n] * mixed).astype(o_ref.dtype)
    _write_query_slots(_pair_head_norm(z[:, 2 * B_WIDTH:], xqg_ref[...]), X_SLOT_PAIRS, xs_ref)


def _sample_in_b(x3, norm1_g, w_in_b, v_norm, w_s, b_s, xq_norm):
    s, n, _ = x3.shape
    rows = s * n
    coef = jnp.repeat(w_s[:, :s, :s].transpose(1, 2, 0).reshape(s * s, B_GROUPS), B_GROUP_DIM, axis=1)
    gbias = jnp.repeat(b_s[:, :s].T, B_GROUP_DIM, axis=1)
    full = lambda shape: pl.BlockSpec(shape, lambda i: (0,) * len(shape))
    return pl.pallas_call(
        functools.partial(_sample_in_b_kernel, seq=s, n=n),
        grid=(1,),
        in_specs=[
            _const_spec((s, n, D_MODEL)),
            _const_spec((1, D_MODEL)),
            _const_spec(w_in_b.shape),
            _const_spec((1, B_WIDTH)),
            _const_spec((s * s, B_WIDTH)),
            _const_spec((s, B_WIDTH)),
            _const_spec((1, LANES)),
        ],
        out_specs=[full((rows, B_WIDTH)), full((rows, B_WIDTH)), full((X_HEADS, rows, X_WIDTH))],
        out_shape=[
            jax.ShapeDtypeStruct((rows, B_WIDTH), BF16),
            jax.ShapeDtypeStruct((rows, B_WIDTH), F32),
            jax.ShapeDtypeStruct((X_HEADS, rows, X_WIDTH), BF16),
        ],
        compiler_params=_params("arbitrary"),
        name="sample_in_b",
    )(x3, norm1_g.reshape(1, D_MODEL), w_in_b, v_norm.reshape(1, B_WIDTH), coef, gbias,
      jnp.tile(xq_norm, 2).reshape(1, LANES))


def _sample_attn_win_kernel(qs_ref, ktn_ref, vtn_ref, vn_ref, slope_ref, sink_ref, ck_ref, cv_ref,
                            o_ref, wk_ref, wv_ref, *, sb, seq):
    rows = A_HEADS * seq
    tq = _div_mod(lax.broadcasted_iota(jnp.int32, (rows, WINDOW), 0), seq)[1]
    lane = lax.broadcasted_iota(jnp.int32, (rows, WINDOW), 1)
    slope = slope_ref[...]
    sink = sink_ref[...]
    d1 = tq + WINDOW - lane
    valid1 = d1 < WINDOW
    bias1 = slope * d1.astype(F32)
    new_seq, new_pos = _div_mod(lane, seq)
    d2 = tq - new_pos
    causal2 = d2 >= 0
    bias2 = slope * d2.astype(F32)
    ktn = ktn_ref[...]
    vtn = vtn_ref[...]
    ktn_b = ktn.astype(BF16)
    vn_b = vn_ref[...].astype(BF16)
    keep_old = lax.broadcasted_iota(jnp.int32, (A_KV, WINDOW), 1) < WINDOW - seq

    def body(n, carry):
        q = qs_ref[n]
        ckt = ck_ref[n]
        cvt = cv_ref[n]
        s1 = jnp.where(valid1, _dot(q, ckt.astype(BF16)) - bias1, NEG)
        s2 = jnp.where(causal2 & (new_seq == n), _dot(q, ktn_b) - bias2, NEG)
        m = jnp.maximum(jnp.maximum(jnp.max(s1, axis=-1, keepdims=True), jnp.max(s2, axis=-1, keepdims=True)), sink)
        e1 = jnp.exp(s1 - m)
        e2 = jnp.exp(s2 - m)
        den = jnp.sum(e1, axis=-1, keepdims=True) + jnp.sum(e2, axis=-1, keepdims=True) + jnp.exp(sink - m)
        o = _nt_dot(e1.astype(BF16), cvt.astype(BF16)) + _dot(e2.astype(BF16), vn_b)
        o_ref[n] = (o / den).astype(o_ref.dtype)
        shift_new = WINDOW - seq - n * seq
        wk_ref[n] = jnp.where(keep_old, pltpu.roll(ckt, WINDOW - seq, 1), pltpu.roll(ktn, shift_new, 1))
        wv_ref[n] = jnp.where(keep_old, pltpu.roll(cvt, WINDOW - seq, 1), pltpu.roll(vtn, shift_new, 1))
        return carry

    lax.fori_loop(0, sb, body, 0, unroll=SAMPLE_SEQ_UNROLL)


def _sample_attn_win(qs, kt_new, vt_new, v_new, slope_rows, sink_rows, ckt, cvt, layer, seq):
    n = qs.shape[0]
    rows = A_HEADS * seq
    assert LANES % seq == 0
    sb = LANES // seq
    assert n % sb == 0

    def seqs(*tail):
        return pl.BlockSpec((sb,) + tail, lambda i: (i,) + (0,) * len(tail))

    cache = pl.BlockSpec((None, sb, A_KV, WINDOW), lambda i: (layer, i, 0, 0))
    win = jax.ShapeDtypeStruct((n, A_KV, WINDOW), F32)
    return pl.pallas_call(
        functools.partial(_sample_attn_win_kernel, sb=sb, seq=seq),
        grid=(n // sb,),
        in_specs=[
            seqs(rows, A_KV),
            pl.BlockSpec((A_KV, LANES), lambda i: (0, i)),
            pl.BlockSpec((A_KV, LANES), lambda i: (0, i)),
            pl.BlockSpec((LANES, A_KV), lambda i: (i, 0)),
            _const_spec((rows, 1)),
            _const_spec((rows, 1)),
            cache,
            cache,
        ],
        out_specs=[seqs(rows, A_KV), seqs(A_KV, WINDOW), seqs(A_KV, WINDOW)],
        out_shape=[jax.ShapeDtypeStruct((n, rows, A_KV), BF16), win, win],
        compiler_params=_params("arbitrary"),
        name="sample_attn_win",
    )(qs, kt_new, vt_new, v_new, slope_rows, sink_rows, ckt, cvt)


def _sample_attn_mem_kernel(xs_ref, mk_ref, mv_ref, xo_ref, *, sb):
    def body(n, carry):
        s = _dot(xs_ref[n], mk_ref[n].astype(BF16))
        m = jnp.max(s, axis=-1, keepdims=True)
        e = jnp.exp(s - m)
        den = jnp.sum(e, axis=-1, keepdims=True)
        xo_ref[n] = (_nt_dot(e.astype(BF16), mv_ref[n].astype(BF16)) / den).astype(xo_ref.dtype)
        return carry

    lax.fori_loop(0, sb, body, 0, unroll=SAMPLE_SEQ_UNROLL)


def _sample_attn_mem(xs, mkt, mvt, layer):
    n, xrows, _ = xs.shape
    sb = min(SAMPLE_MEM_SEQS, n)
    assert n % sb == 0
    qspec = pl.BlockSpec((sb, xrows, X_WIDTH), lambda i: (i, 0, 0))
    cache = pl.BlockSpec((None, sb, X_WIDTH, N_MEM), lambda i: (layer, i, 0, 0))
    return pl.pallas_call(
        functools.partial(_sample_attn_mem_kernel, sb=sb),
        grid=(n // sb,),
        in_specs=[qspec, cache, cache],
        out_specs=qspec,
        out_shape=jax.ShapeDtypeStruct((n, xrows, X_WIDTH), BF16),
        compiler_params=_params("arbitrary"),
        name="sample_attn_mem",
    )(xs, mkt, mvt)


def _slots_to_seq(a, seq, n):
    hh, _, w = a.shape
    return a.reshape(hh, seq, n, w).transpose(2, 0, 1, 3).reshape(n, hh * seq, w)


def _seq_to_cols(a, heads, seq):
    n, _, w = a.shape
    return a.reshape(n, heads, seq, w).transpose(2, 0, 1, 3).reshape(seq, n, heads * w)


def _feature_major(cache):
    l, n, p, kv, d = cache.shape
    return cache.transpose(0, 1, 3, 4, 2).reshape(l, n, kv * d, p)


def kernel(x_prompt, x_sample, cache_win_k, cache_win_v, cache_mem_k, cache_mem_v, state_conv, mem_prompt,
           norm1_g, norm2_g, mem_norm_g, w_in_a, q_norm_a, k_norm_a, sinks_a, w_out_a, w_in_b, v_norm_b, w_s_b,
           b_s_b, w_out_b, w_mem_kv, xq_norm, xk_norm, w_ffn_in, conv_w, conv_b, w_down):
    depth = norm1_g.shape[0]
    assert depth == 2 and w_in_a.shape[0] == 1 and w_in_b.shape[0] == 1
    nb = x_prompt.shape[0]
    ns, seq, _ = x_sample.shape
    assert cache_win_k.shape[2] == WINDOW

    w_in_a_b = w_in_a[0].astype(BF16)
    w_in_b_b = w_in_b[0].astype(BF16)
    w_out_a_b = w_out_a[0].astype(BF16)
    w_out_b_b = w_out_b[0].astype(BF16)
    w_mem_kv_b = w_mem_kv.astype(BF16)
    w_ffn_in_b = w_ffn_in.astype(BF16)
    w_down_b = w_down.astype(BF16)

    mk_all, mv_all, mem_kt_p, mem_vt_p = _memory_kv(mem_prompt, mem_norm_g, w_mem_kv_b, xk_norm)
    w_in_a_r = _reorder_heads(w_in_a_b, 1)
    w_out_a_r = _reorder_heads(w_out_a_b, 0)
    x2, win_k_p, win_v_p, conv_p0 = _layer_a_prompt(
        x_prompt, 0, norm1_g[0], w_in_a_r, q_norm_a[0], k_norm_a[0], xq_norm[0], sinks_a[0], mk_all, mv_all, w_out_a_r,
        norm2_g[0], w_ffn_in_b, conv_w[0], conv_b[0], w_down_b)
    y_prompt, chunk_v_p, conv_p1 = _layer_b_prompt(
        x2, 1, norm1_g[1], w_in_b_b, v_norm_b[0], w_s_b[0], b_s_b[0], xq_norm[1], mk_all, mv_all, w_out_b_b,
        norm2_g[1], w_ffn_in_b, conv_w[1], conv_b[1], w_down_b)

    xs3 = x_sample.transpose(1, 0, 2)
    mkt = _feature_major(cache_mem_k)
    mvt = _feature_major(cache_mem_v)

    def by_seq_cols(a):
        return a.reshape(a.shape[0], seq, ns).transpose(0, 2, 1).reshape(a.shape[0], ns * seq)

    qs, v_new, xslots, kt_new, vt_new = _sample_in_a(xs3, norm1_g[0], w_in_a_r, q_norm_a[0], k_norm_a[0], xq_norm[0])
    slope_rows = jnp.repeat(jnp.asarray(SLOPES, F32), seq).reshape(A_HEADS * seq, 1)
    sink_rows = jnp.repeat(sinks_a[0].astype(F32), seq).reshape(A_HEADS * seq, 1)
    o_s, win_kt_s, win_vt_s = _sample_attn_win(
        _slots_to_seq(qs, seq, ns), by_seq_cols(kt_new), by_seq_cols(vt_new),
        v_new.reshape(seq, ns, A_KV).transpose(1, 0, 2).reshape(ns * seq, A_KV), slope_rows, sink_rows,
        _feature_major(cache_win_k), _feature_major(cache_win_v), 0, seq)
    xo_s = _sample_attn_mem(_slots_to_seq(xslots, seq, ns), mkt, mvt, 0)
    ys1, conv_s0 = _ffn_sample(
        xs3, _seq_to_cols(o_s, A_HEADS, seq), _seq_to_cols(xo_s, X_HEADS, seq),
        w_out_a_r[:A_Q], w_out_a_r[A_Q:], 0, norm2_g[0], w_ffn_in_b, conv_w[0], conv_b[0], w_down_b, state_conv[0])

    o_b, vn_s, xslots = _sample_in_b(ys1, norm1_g[1], w_in_b_b, v_norm_b[0], w_s_b[0], b_s_b[0], xq_norm[1])
    xo_s = _sample_attn_mem(_slots_to_seq(xslots, seq, ns), mkt, mvt, 1)
    ys2, conv_s1 = _ffn_sample(
        ys1, o_b.reshape(seq, ns, B_WIDTH), _seq_to_cols(xo_s, X_HEADS, seq),
        w_out_b_b[:B_WIDTH], w_out_b_b[B_WIDTH:], 1, norm2_g[1], w_ffn_in_b, conv_w[1], conv_b[1], w_down_b, state_conv[1])

    def position_major(ft):
        lead = ft.shape[:-2]
        nd = len(lead)
        return ft.reshape(*lead, A_KV_HEADS, HEAD_DIM, ft.shape[-1]).transpose(*range(nd), nd + 2, nd, nd + 1)

    return (
        y_prompt,
        ys2.transpose(1, 0, 2),
        position_major(win_k_p)[None],
        position_major(win_v_p)[None],
        chunk_v_p.reshape(1, nb, CHUNK, B_WIDTH),
        position_major(mem_kt_p),
        position_major(mem_vt_p),
        jnp.stack([conv_p0, conv_p1]),
        position_major(win_kt_s)[None],
        position_major(win_vt_s)[None],
        vn_s.reshape(seq, ns, B_WIDTH).transpose(1, 0, 2)[None],
        jnp.stack([conv_s0.transpose(1, 0, 2), conv_s1.transpose(1, 0, 2)]),
    )
```

```python
import functools
import math

import numpy as np
import jax
import jax.numpy as jnp
from jax import lax
from jax.experimental import pallas as pl
from jax.experimental.pallas import tpu as pltpu

F32 = jnp.float32
BF16 = jnp.bfloat16

D_MODEL = 1024
HEAD_DIM = 64
A_HEADS = 12
A_KV_HEADS = 4
A_GROUP = A_HEADS // A_KV_HEADS
WINDOW = 128
BLOCK = 128
B_WIDTH = 768
B_GROUPS = 4
B_GROUP_DIM = B_WIDTH // B_GROUPS
CHUNK = 128
X_HEADS = 4
X_WIDTH = X_HEADS * HEAD_DIM
N_MEM = 256
D_FF = 2816
CONV_W = 3
A_Q = A_HEADS * HEAD_DIM
A_KV = A_KV_HEADS * HEAD_DIM
EPS = 1e-6
NEG = -1e30
SCALE = HEAD_DIM ** -0.5
LOG2E = float(np.float32(1.0 / np.log(2.0)))
SQRT_HALF = float(np.sqrt(0.5).astype(np.float32))

LANES = 128
VMEM_LIMIT_BYTES = 56 * 1024 * 1024
GATE_WINDOW_START = (0, 128, 384, 512)
assert B_GROUP_DIM == 192 and B_GROUPS == 4
PROMPT_ROWS = 512
SAMPLE_MEM_SEQS = 16
SAMPLE_FFN_SEQS = 128
SAMPLE_SEQ_UNROLL = 4


def _alibi_slopes(n):
    def pow2(m):
        start = 2.0 ** (-8.0 / m)
        return [start ** (i + 1) for i in range(m)]

    p = 2 ** int(math.floor(math.log2(n)))
    s = pow2(p)
    if p < n:
        s = s + pow2(2 * p)[0::2][: n - p]
    return [float(np.float32(v)) for v in s]


SLOPES = _alibi_slopes(A_HEADS)

Q_LANE_ORDER = (0, 3, 1, 4, 2, 5, 6, 9, 7, 10, 8, 11)
assert all(Q_LANE_ORDER[p] // A_GROUP == 2 * (p // (2 * A_GROUP)) + p % 2 for p in range(A_HEADS))


_SLOTS = A_KV_HEADS // 2
assert Q_LANE_ORDER == tuple(np.arange(A_HEADS).reshape(_SLOTS, 2, A_GROUP).swapaxes(1, 2).reshape(-1).tolist())


def _reorder_heads(w, axis):
    lead, tail = w.shape[:axis], w.shape[axis + 1:]
    q = lax.slice_in_dim(w, 0, A_Q, axis=axis).reshape(*lead, _SLOTS, 2, A_GROUP, HEAD_DIM, *tail)
    q = jnp.swapaxes(q, axis + 1, axis + 2).reshape(*lead, A_Q, *tail)
    return jnp.concatenate([q, lax.slice_in_dim(w, A_Q, w.shape[axis], axis=axis)], axis=axis)


def _params(*sem):
    return pltpu.CompilerParams(dimension_semantics=sem, vmem_limit_bytes=VMEM_LIMIT_BYTES)


def _const_spec(shape):
    nd = len(shape)
    return pl.BlockSpec(shape, lambda *_: (0,) * nd, pipeline_mode=pl.Buffered(1))


def _layer_spec(layer, shape):
    nd = len(shape)
    return pl.BlockSpec((None,) + tuple(shape), lambda *_: (layer,) + (0,) * nd, pipeline_mode=pl.Buffered(1))


def _rms_rows(x, g):
    ms = jnp.mean(x * x, axis=-1, keepdims=True)
    return x * lax.rsqrt(ms + EPS) * g


def _pair_head_norm(z, g2):
    t, w = z.shape
    lo = lax.broadcasted_iota(jnp.int32, (t, LANES), 1) < HEAD_DIM
    out = []
    for p in range(w // LANES):
        zz = z[:, p * LANES:(p + 1) * LANES]
        sq = zz * zz
        s_lo = jnp.sum(jnp.where(lo, sq, 0.0), axis=-1, keepdims=True)
        s_hi = jnp.sum(jnp.where(lo, 0.0, sq), axis=-1, keepdims=True)
        r_lo = lax.rsqrt(s_lo * (1.0 / HEAD_DIM) + EPS)
        r_hi = lax.rsqrt(s_hi * (1.0 / HEAD_DIM) + EPS)
        out.append(zz * jnp.where(lo, r_lo, r_hi) * g2)
    return out


def _nt_dot(a, b):
    return lax.dot_general(a, b, (((1,), (1,)), ((), ())), preferred_element_type=F32)


def _dot(a, b):
    return jnp.dot(a, b, preferred_element_type=F32)


def _div_mod(i, n):
    if n & (n - 1) == 0:
        return i >> (n.bit_length() - 1), i & (n - 1)
    return i // n, i % n


def _gelu_exact(x):
    return 0.5 * x * (1.0 + lax.erf(x * SQRT_HALF))


def _conv_gate(a, am1, am2, up, cw_ref, cb_ref):
    c = cb_ref[...] + ((cw_ref[0:1, :] * am2 + cw_ref[1:2, :] * am1) + cw_ref[2:3, :] * a)
    return ((c * (1.0 / (1.0 + jnp.exp(-c)))) * up).astype(BF16)


def _memkv_kernel(mem_ref, g_ref, w_ref, kg_ref, mkb_ref, mvb_ref, mkt_ref, mvt_ref, *, depth):
    m = mem_ref[0]
    for layer in range(depth):
        h = _rms_rows(m, g_ref[layer]).astype(BF16)
        kv = _dot(h, w_ref[layer])
        mk = jnp.concatenate(_pair_head_norm(kv[:, :X_WIDTH], kg_ref[layer]), axis=1)
        mv = kv[:, X_WIDTH:]
        mkb_ref[layer, 0] = mk.astype(BF16)
        mvb_ref[layer, 0] = mv.astype(BF16)
        mkt_ref[layer, 0] = mk.T
        mvt_ref[layer, 0] = mv.T


def _memory_kv(mem, mem_norm_g, w_mem_kv_b, xk_norm):
    depth = w_mem_kv_b.shape[0]
    n = mem.shape[0]
    g = mem_norm_g.reshape(depth, 1, D_MODEL)
    kg2 = jnp.tile(xk_norm, (1, 2)).reshape(depth, 1, LANES)
    spec = lambda rows, cols: pl.BlockSpec((depth, 1, rows, cols), lambda b: (0, b, 0, 0))
    return pl.pallas_call(
        functools.partial(_memkv_kernel, depth=depth),
        grid=(n,),
        in_specs=[
            pl.BlockSpec((1, N_MEM, D_MODEL), lambda b: (b, 0, 0)),
            _const_spec((depth, 1, D_MODEL)),
            _const_spec((depth, D_MODEL, 2 * X_WIDTH)),
            _const_spec((depth, 1, LANES)),
        ],
        out_specs=[spec(N_MEM, X_WIDTH), spec(N_MEM, X_WIDTH), spec(X_WIDTH, N_MEM), spec(X_WIDTH, N_MEM)],
        out_shape=[
            jax.ShapeDtypeStruct((depth, n, N_MEM, X_WIDTH), BF16),
            jax.ShapeDtypeStruct((depth, n, N_MEM, X_WIDTH), BF16),
            jax.ShapeDtypeStruct((depth, n, X_WIDTH, N_MEM), F32),
            jax.ShapeDtypeStruct((depth, n, X_WIDTH, N_MEM), F32),
        ],
        compiler_params=_params("arbitrary"),
        name="memory_kv",
    )(mem, g, w_mem_kv_b, kg2)


def _half_masked_rows(pair, lo):
    return jnp.concatenate([jnp.where(lo, pair, 0.0), jnp.where(lo, 0.0, pair)], axis=0).astype(BF16)


def _prompt_mem_attention(xq_pairs, mk_ref, mv_ref, ocat, col0):
    t = xq_pairs[0].shape[0]
    lo = lax.broadcasted_iota(jnp.int32, (t, LANES), 1) < HEAD_DIM
    for j, pair in enumerate(xq_pairs):
        mk = mk_ref[0, 0, :, j * LANES:(j + 1) * LANES]
        mv = mv_ref[0, 0, :, j * LANES:(j + 1) * LANES]
        s = _nt_dot(_half_masked_rows(pair * (SCALE * LOG2E), lo), mk)
        m = jnp.max(s, axis=-1, keepdims=True)
        e = jnp.exp2(s - m)
        den = jnp.sum(e, axis=-1, keepdims=True)
        o = _dot(e.astype(BF16), mv) / den
        ocat[:, col0 + j * LANES:col0 + (j + 1) * LANES] = jnp.where(lo, o[:t], o[t:]).astype(BF16)


def _mixer_a_body(t, x, sinks_ref, g1_ref, win_ref, qg_ref, kg_ref, xqg_ref, mk_ref, mv_ref, wout_ref,
                  wk_ref, wv_ref, kbuf, vbuf, ocat, tb):
    h = _rms_rows(x, g1_ref[...]).astype(BF16)
    z = _dot(h, win_ref[...])
    q_pairs = _pair_head_norm(z[:, :A_Q], qg_ref[...])
    k_pairs = _pair_head_norm(z[:, A_Q:A_Q + A_KV], kg_ref[...])
    v = z[:, A_Q + A_KV:A_Q + 2 * A_KV]
    xq_pairs = _pair_head_norm(z[:, A_Q + 2 * A_KV:], xqg_ref[...])

    for p, kp in enumerate(k_pairs):
        wk_ref[0, p * LANES:(p + 1) * LANES, :] = kp[tb - BLOCK:, :].T
        kbuf[BLOCK:BLOCK + tb, p * LANES:(p + 1) * LANES] = kp.astype(BF16)
    wv_ref[0] = v[tb - BLOCK:, :].T
    vbuf[BLOCK:BLOCK + tb, :] = v.astype(BF16)

    qi = lax.broadcasted_iota(jnp.int32, (BLOCK, BLOCK), 0)
    kj = lax.broadcasted_iota(jnp.int32, (BLOCK, BLOCK), 1)
    from_prev = kj > qi
    key_pos = jnp.where(from_prev, kj, kj + BLOCK).astype(F32)
    query_pos = (lax.broadcasted_iota(jnp.int32, (BLOCK, 1), 0) + BLOCK).astype(F32)

    lo = lax.broadcasted_iota(jnp.int32, (BLOCK, LANES), 1) < HEAD_DIM
    pairs_per_slot = A_HEADS // A_KV_HEADS
    for s in range(tb // BLOCK):
        rows = slice(s * BLOCK, (s + 1) * BLOCK)
        prev_keys = (jnp.where(t > 0, BLOCK, 0) if s == 0 else BLOCK)
        valid = jnp.logical_not(from_prev) | (kj < prev_keys)
        for j in range(A_KV // LANES):
            kw = kbuf[s * BLOCK:(s + 2) * BLOCK, j * LANES:(j + 1) * LANES]
            vw = vbuf[s * BLOCK:(s + 2) * BLOCK, j * LANES:(j + 1) * LANES]
            q_rows = jnp.concatenate(
                [_half_masked_rows(q_pairs[pairs_per_slot * j + c][rows] * (SCALE * LOG2E), lo)
                 for c in range(pairs_per_slot)], axis=0)
            sc_all = _nt_dot(q_rows, kw)
            e_parts, den_parts = [], []
            for r in range(2 * pairs_per_slot):
                hq = Q_LANE_ORDER[2 * pairs_per_slot * j + r]
                slope2 = SLOPES[hq] * LOG2E
                both = sc_all[r * BLOCK:(r + 1) * BLOCK]
                sc = jnp.where(from_prev, both[:, :BLOCK], both[:, BLOCK:]) + slope2 * key_pos
                sc = jnp.where(valid, sc, NEG)
                sink = sinks_ref[hq] * LOG2E + slope2 * query_pos
                m = jnp.maximum(jnp.max(sc, axis=-1, keepdims=True), sink)
                e = jnp.exp2(sc - m)
                den_parts.append(jnp.sum(e, axis=-1, keepdims=True) + jnp.exp2(sink - m))
                e_parts.append(jnp.concatenate([jnp.where(from_prev, e, 0.0), jnp.where(from_prev, 0.0, e)],
                                               axis=1).astype(BF16))
            o_all = _dot(jnp.concatenate(e_parts, axis=0), vw)
            for c in range(pairs_per_slot):
                o_lo = o_all[2 * c * BLOCK:(2 * c + 1) * BLOCK] / den_parts[2 * c]
                o_hi = o_all[(2 * c + 1) * BLOCK:(2 * c + 2) * BLOCK] / den_parts[2 * c + 1]
                col = pairs_per_slot * j + c
                ocat[rows, col * LANES:(col + 1) * LANES] = jnp.where(lo, o_lo, o_hi).astype(BF16)

    _prompt_mem_attention(xq_pairs, mk_ref, mv_ref, ocat, A_Q)
    x1 = x + _dot(ocat[...], wout_ref[...])

    kbuf[0:BLOCK, :] = kbuf[tb:tb + BLOCK, :]
    vbuf[0:BLOCK, :] = vbuf[tb:tb + BLOCK, :]
    return x1


def _ffn_body(x, g2_ref, wfi_ref, cw_ref, cb_ref, wd_ref, tail_ref, tb):
    h = _rms_rows(x, g2_ref[...]).astype(BF16)
    gu = _dot(h, wfi_ref[...])
    a = gu[:, :D_FF]
    row = lax.broadcasted_iota(jnp.int32, (tb, 1), 0)
    prev_last = tail_ref[0, 7:8, :]
    am1 = jnp.where(row == 0, prev_last, pltpu.roll(a, 1, 0))
    am2 = jnp.where(row == 0, tail_ref[0, 6:7, :], jnp.where(row == 1, prev_last, pltpu.roll(a, 2, 0)))
    tail_ref[0] = a[tb - 8:, :]
    return x + _dot(_conv_gate(a, am1, am2, gu[:, D_FF:], cw_ref, cb_ref), wd_ref[...])


def _layer_a_kernel(sinks_ref, x_ref, g1_ref, win_ref, qg_ref, kg_ref, xqg_ref, mk_ref, mv_ref, wout_ref,
                    g2_ref, wfi_ref, cw_ref, cb_ref, wd_ref, y_ref, wk_ref, wv_ref, tail_ref, kbuf, vbuf, ocat, *, tb):
    t = pl.program_id(1)

    @pl.when(t == 0)
    def _():
        kbuf[0:BLOCK, :] = jnp.zeros((BLOCK, A_KV), BF16)
        vbuf[0:BLOCK, :] = jnp.zeros((BLOCK, A_KV), BF16)
        tail_ref[0] = jnp.zeros((8, D_FF), F32)

    x1 = _mixer_a_body(t, x_ref[0], sinks_ref, g1_ref, win_ref, qg_ref, kg_ref, xqg_ref, mk_ref, mv_ref, wout_ref,
                       wk_ref, wv_ref, kbuf, vbuf, ocat, tb)
    y_ref[0] = _ffn_body(x1, g2_ref, wfi_ref, cw_ref, cb_ref, wd_ref, tail_ref, tb)


def _layer_specs(tb, layer):
    xspec = pl.BlockSpec((1, tb, D_MODEL), lambda b, i: (b, i, 0))
    mspec = pl.BlockSpec((1, 1, N_MEM, X_WIDTH), lambda b, i: (layer, b, 0, 0))
    per_seq = lambda rows, width: pl.BlockSpec((1, rows, width), lambda b, i: (b, 0, 0))
    return xspec, mspec, per_seq(8, D_FF), per_seq


def _ffn_specs(layer):
    return [
        _const_spec((1, D_MODEL)),
        _layer_spec(layer, (D_MODEL, 2 * D_FF)),
        _const_spec((CONV_W, D_FF)),
        _const_spec((1, D_FF)),
        _layer_spec(layer, (D_FF, D_MODEL)),
    ]


def _layer_a_prompt(x, layer, norm1_g, w_in_b, q_norm, k_norm, xq_norm, sinks, mk_all, mv_all, w_out_b,
                    norm2_g, w_ffn_in_b, conv_w, conv_b, w_down_b):
    n, t, _ = x.shape
    tb = min(PROMPT_ROWS, t)
    a_in = w_in_b.shape[1]
    xspec, mspec, tailspec, per_seq = _layer_specs(tb, layer)
    wspec = per_seq(A_KV, BLOCK)
    y, wk, wv, tail = pl.pallas_call(
        functools.partial(_layer_a_kernel, tb=tb),
        grid=(n, t // tb),
        in_specs=[
            pl.BlockSpec(memory_space=pltpu.SMEM),
            xspec,
            _const_spec((1, D_MODEL)),
            _const_spec((D_MODEL, a_in)),
            _const_spec((1, LANES)),
            _const_spec((1, LANES)),
            _const_spec((1, LANES)),
            mspec,
            mspec,
            _const_spec((A_Q + X_WIDTH, D_MODEL)),
        ] + _ffn_specs(layer),
        out_specs=[xspec, wspec, wspec, tailspec],
        out_shape=[
            jax.ShapeDtypeStruct((n, t, D_MODEL), F32),
            jax.ShapeDtypeStruct((n, A_KV, BLOCK), F32),
            jax.ShapeDtypeStruct((n, A_KV, BLOCK), F32),
            jax.ShapeDtypeStruct((n, 8, D_FF), F32),
        ],
        scratch_shapes=[
            pltpu.VMEM((tb + BLOCK, A_KV), BF16),
            pltpu.VMEM((tb + BLOCK, A_KV), BF16),
            pltpu.VMEM((tb, A_Q + X_WIDTH), BF16),
        ],
        compiler_params=_params("arbitrary", "arbitrary"),
        name="layer_a_prompt",
    )(sinks, x, norm1_g.reshape(1, D_MODEL), w_in_b, jnp.tile(q_norm, 2).reshape(1, LANES),
      jnp.tile(k_norm, 2).reshape(1, LANES), jnp.tile(xq_norm, 2).reshape(1, LANES), mk_all, mv_all, w_out_b,
      norm2_g.reshape(1, D_MODEL), w_ffn_in_b, conv_w, conv_b.reshape(1, D_FF), w_down_b)
    return y, wk, wv, tail[:, 8 - (CONV_W - 1):, :]


def _mixer_b_body(x, g1_ref, win_ref, vg_ref, ws_ref, bias_ref, xqg_ref, mk_ref, mv_ref, wout_ref, cv_ref, ocat, tb):
    h = _rms_rows(x, g1_ref[...]).astype(BF16)
    z = _dot(h, win_ref[...])
    uv = _gelu_exact(z[:, :2 * B_WIDTH])
    u = uv[:, :B_WIDTH]
    vn = _rms_rows(uv[:, B_WIDTH:], vg_ref[...])
    cv_ref[0] = vn[tb - CHUNK:, :]
    xq_pairs = _pair_head_norm(z[:, 2 * B_WIDTH:], xqg_ref[...])

    wi = lax.broadcasted_iota(jnp.int32, (CHUNK, CHUNK), 0)
    wj = lax.broadcasted_iota(jnp.int32, (CHUNK, CHUNK), 1)
    causal = wi >= wj
    w_tril = [jnp.where(causal, ws_ref[g], 0.0).astype(BF16) for g in range(B_GROUPS)]
    lo = lax.broadcasted_iota(jnp.int32, (CHUNK, LANES), 1) < HEAD_DIM
    bias = bias_ref[...]
    for c in range(tb // CHUNK):
        vc = vn[c * CHUNK:(c + 1) * CHUNK, :].astype(BF16)
        r = [_dot(w_tril[g], vc[:, c0:c0 + 256]) for g, c0 in enumerate(GATE_WINDOW_START)]
        mixed = jnp.concatenate(
            [r[0][:, :128], jnp.where(lo, r[0][:, 128:], r[1][:, :128]), r[1][:, 128:],
             r[2][:, :128], jnp.where(lo, r[2][:, 128:], r[3][:, :128]), r[3][:, 128:]], axis=1)
        o = u[c * CHUNK:(c + 1) * CHUNK, :] * (mixed + bias)
        ocat[c * CHUNK:(c + 1) * CHUNK, 0:B_WIDTH] = o.astype(BF16)

    _prompt_mem_attention(xq_pairs, mk_ref, mv_ref, ocat, B_WIDTH)
    return x + _dot(ocat[...], wout_ref[...])


def _layer_b_kernel(x_ref, g1_ref, win_ref, vg_ref, ws_ref, bias_ref, xqg_ref, mk_ref, mv_ref, wout_ref,
                    g2_ref, wfi_ref, cw_ref, cb_ref, wd_ref, y_ref, cv_ref, tail_ref, ocat, *, tb):
    @pl.when(pl.program_id(1) == 0)
    def _():
        tail_ref[0] = jnp.zeros((8, D_FF), F32)

    x1 = _mixer_b_body(x_ref[0], g1_ref, win_ref, vg_ref, ws_ref, bias_ref, xqg_ref, mk_ref, mv_ref, wout_ref,
                       cv_ref, ocat, tb)
    y_ref[0] = _ffn_body(x1, g2_ref, wfi_ref, cw_ref, cb_ref, wd_ref, tail_ref, tb)


def _gate_bias_rows(b_s):
    return jnp.repeat(b_s.T, B_GROUP_DIM, axis=1)


def _layer_b_prompt(x, layer, norm1_g, w_in_b, v_norm, w_s, b_s, xq_norm, mk_all, mv_all, w_out_b,
                    norm2_g, w_ffn_in_b, conv_w, conv_b, w_down_b):
    n, t, _ = x.shape
    tb = min(PROMPT_ROWS, t)
    b_in = w_in_b.shape[1]
    xspec, mspec, tailspec, per_seq = _layer_specs(tb, layer)
    y, cv, tail = pl.pallas_call(
        functools.partial(_layer_b_kernel, tb=tb),
        grid=(n, t // tb),
        in_specs=[
            xspec,
            _const_spec((1, D_MODEL)),
            _const_spec((D_MODEL, b_in)),
            _const_spec((1, B_WIDTH)),
            _const_spec((B_GROUPS, CHUNK, CHUNK)),
            _const_spec((CHUNK, B_WIDTH)),
            _const_spec((1, LANES)),
            mspec,
            mspec,
            _const_spec((B_WIDTH + X_WIDTH, D_MODEL)),
        ] + _ffn_specs(layer),
        out_specs=[xspec, per_seq(CHUNK, B_WIDTH), tailspec],
        out_shape=[
            jax.ShapeDtypeStruct((n, t, D_MODEL), F32),
            jax.ShapeDtypeStruct((n, CHUNK, B_WIDTH), F32),
            jax.ShapeDtypeStruct((n, 8, D_FF), F32),
        ],
        scratch_shapes=[pltpu.VMEM((tb, B_WIDTH + X_WIDTH), BF16)],
        compiler_params=_params("arbitrary", "arbitrary"),
        name="layer_b_prompt",
    )(x, norm1_g.reshape(1, D_MODEL), w_in_b, v_norm.reshape(1, B_WIDTH), w_s, _gate_bias_rows(b_s),
      jnp.tile(xq_norm, 2).reshape(1, LANES), mk_all, mv_all, w_out_b,
      norm2_g.reshape(1, D_MODEL), w_ffn_in_b, conv_w, conv_b.reshape(1, D_FF), w_down_b)
    return y, cv, tail[:, 8 - (CONV_W - 1):, :]


def _stack_positions(ref, seq):
    return jnp.concatenate([ref[t] for t in range(seq)], axis=0)


def _ffn_sample_kernel(x_ref, om_ref, ox_ref, wom_ref, wox_ref, g2_ref, wfi_ref, cw_ref, cb_ref, wd_ref, p0_ref, p1_ref,
                       y_ref, atail_ref, *, seq, nb, mixer_slots):
    om = _compact_slots(om_ref, Q_SLOT_PAIRS, seq, nb) if mixer_slots else _stack_positions(om_ref, seq)
    ox = _compact_slots(ox_ref, X_SLOT_PAIRS, seq, nb)
    x1 = _stack_positions(x_ref, seq) + _dot(om, wom_ref[...]) + _dot(ox, wox_ref[...])
    h = _rms_rows(x1, g2_ref[...]).astype(BF16)
    gu = _dot(h, wfi_ref[...])
    a = gu[:, :D_FF]
    blocks = [p0_ref[...], p1_ref[...]] + [a[t * nb:(t + 1) * nb] for t in range(seq)]
    am2 = jnp.concatenate(blocks[:seq], axis=0)
    am1 = jnp.concatenate(blocks[1:seq + 1], axis=0)
    y = x1 + _dot(_conv_gate(a, am1, am2, gu[:, D_FF:], cw_ref, cb_ref), wd_ref[...])
    for t in range(seq):
        y_ref[t] = y[t * nb:(t + 1) * nb]
    for r in range(CONV_W - 1):
        atail_ref[r] = blocks[seq + r]


def _ffn_sample(x3, om3, ox3, w_om, w_ox, layer, norm2_g, w_ffn_in_b, conv_w, conv_b, w_down_b, prefix):
    s, n, _ = x3.shape
    nb = min(SAMPLE_FFN_SEQS, n)
    assert n % nb == 0 and s >= CONV_W - 1

    def blk(lead, width):
        return pl.BlockSpec((lead, nb, width), lambda i: (0, i, 0))

    def slots(heads):
        return pl.BlockSpec((2, nb * heads * s, LANES), lambda i: (0, i, 0))

    pspec = pl.BlockSpec((nb, D_FF), lambda i: (i, 0))
    mixer_slots = om3.shape[0] == 2 and om3.shape[2] == LANES
    assert ox3.shape == (2, n * X_HEADS * s, LANES)
    return pl.pallas_call(
        functools.partial(_ffn_sample_kernel, seq=s, nb=nb, mixer_slots=mixer_slots),
        grid=(n // nb,),
        in_specs=[
            blk(s, D_MODEL),
            slots(A_HEADS) if mixer_slots else blk(s, om3.shape[2]),
            slots(X_HEADS),
            _const_spec(w_om.shape),
            _const_spec(w_ox.shape),
            _const_spec((1, D_MODEL)),
            _layer_spec(layer, (D_MODEL, 2 * D_FF)),
            _const_spec((CONV_W, D_FF)),
            _const_spec((1, D_FF)),
            _layer_spec(layer, (D_FF, D_MODEL)),
            pspec,
            pspec,
        ],
        out_specs=[blk(s, D_MODEL), blk(CONV_W - 1, D_FF)],
        out_shape=[
            jax.ShapeDtypeStruct((s, n, D_MODEL), F32),
            jax.ShapeDtypeStruct((CONV_W - 1, n, D_FF), F32),
        ],
        compiler_params=_params("arbitrary"),
        name="ffn_sample",
    )(x3, om3, ox3, w_om, w_ox, norm2_g.reshape(1, D_MODEL), w_ffn_in_b, conv_w, conv_b.reshape(1, D_FF), w_down_b,
      prefix[:, 0, :], prefix[:, 1, :])


Q_SLOT_PAIRS = tuple((Q_LANE_ORDER[2 * c], Q_LANE_ORDER[2 * c + 1], c // A_GROUP) for c in range(A_HEADS // 2))
X_SLOT_PAIRS = tuple((2 * c, 2 * c + 1, c) for c in range(X_HEADS // 2))


def _write_query_slots(pairs, slot_pairs, out_ref, seq, n):
    per_seq = 2 * len(slot_pairs) * seq
    lo = lax.broadcasted_iota(jnp.int32, (seq * n, LANES), 1) < HEAD_DIM
    zeros = jnp.zeros((n, LANES), F32)
    for pair, (h_lo, h_hi, col) in zip(pairs, slot_pairs):
        scaled = pair * SCALE
        for head, own in ((h_lo, jnp.where(lo, scaled, 0.0)), (h_hi, jnp.where(lo, 0.0, scaled))):
            for t in range(seq):
                rows = pl.ds(head * seq + t, n, stride=per_seq)
                out_ref[col, rows, :] = own[t * n:(t + 1) * n]
                out_ref[1 - col, rows, :] = zeros


def _read_seq_slots(ref, n, per_seq):
    rows = pl.ds(pl.multiple_of(n * per_seq, 8), per_seq)
    return jnp.concatenate([ref[0, rows, :], ref[1, rows, :]], axis=1)


def _write_seq_slots(ref, n, per_seq, val):
    rows = pl.ds(pl.multiple_of(n * per_seq, 8), per_seq)
    ref[0, rows, :] = val[:, :LANES]
    ref[1, rows, :] = val[:, LANES:]


def _compact_slots(o_ref, slot_pairs, seq, n):
    per_seq = 2 * len(slot_pairs) * seq
    lo = lax.broadcasted_iota(jnp.int32, (n, LANES), 1) < HEAD_DIM
    cols = []
    for h_lo, h_hi, col in slot_pairs:
        blocks = [jnp.where(lo, o_ref[col, pl.ds(h_lo * seq + t, n, stride=per_seq), :],
                            o_ref[col, pl.ds(h_hi * seq + t, n, stride=per_seq), :]) for t in range(seq)]
        cols.append(jnp.concatenate(blocks, axis=0))
    return jnp.concatenate(cols, axis=1).astype(BF16)


def _sample_in_a_kernel(x_ref, g1_ref, win_ref, wkvt_ref, qg_ref, kgc_ref, xqg_ref,
                        qs_ref, v_ref, xs_ref, kt_ref, vt_ref, *, seq, n):
    h = _rms_rows(_stack_positions(x_ref, seq), g1_ref[...]).astype(BF16)
    z = _dot(h, win_ref[...])
    _write_query_slots(_pair_head_norm(z[:, :A_Q], qg_ref[...]), Q_SLOT_PAIRS, qs_ref, seq, n)
    v_ref[...] = z[:, A_Q + A_KV:A_Q + 2 * A_KV]
    _write_query_slots(_pair_head_norm(z[:, A_Q + 2 * A_KV:], xqg_ref[...]), X_SLOT_PAIRS, xs_ref, seq, n)
    kvt = _nt_dot(wkvt_ref[...], h)
    for hk in range(A_KV_HEADS):
        sl = slice(hk * HEAD_DIM, (hk + 1) * HEAD_DIM)
        kh = kvt[sl, :]
        ms = jnp.mean(kh * kh, axis=0, keepdims=True)
        kt_ref[sl, :] = kh * lax.rsqrt(ms + EPS) * kgc_ref[sl, :]
    vt_ref[...] = kvt[A_KV:, :]


def _sample_in_a(x3, norm1_g, w_in_r, q_norm, k_norm, xq_norm):
    s, n, _ = x3.shape
    rows = s * n
    w_kvt = w_in_r[:, A_Q:A_Q + 2 * A_KV].T
    slots = lambda heads: jax.ShapeDtypeStruct((2, n * heads * s, LANES), F32)
    full = lambda shape: pl.BlockSpec(shape, lambda i: (0,) * len(shape))
    return pl.pallas_call(
        functools.partial(_sample_in_a_kernel, seq=s, n=n),
        grid=(1,),
        in_specs=[
            _const_spec((s, n, D_MODEL)),
            _const_spec((1, D_MODEL)),
            _const_spec(w_in_r.shape),
            _const_spec(w_kvt.shape),
            _const_spec((1, LANES)),
            _const_spec((A_KV, 1)),
            _const_spec((1, LANES)),
        ],
        out_specs=[
            full((2, n * A_HEADS * s, LANES)), full((rows, A_KV)), full((2, n * X_HEADS * s, LANES)),
            full((A_KV, rows)), full((A_KV, rows)),
        ],
        out_shape=[
            slots(A_HEADS),
            jax.ShapeDtypeStruct((rows, A_KV), F32),
            slots(X_HEADS),
            jax.ShapeDtypeStruct((A_KV, rows), F32),
            jax.ShapeDtypeStruct((A_KV, rows), F32),
        ],
        compiler_params=_params("arbitrary"),
        name="sample_in_a",
    )(x3, norm1_g.reshape(1, D_MODEL), w_in_r, w_kvt, jnp.tile(q_norm, 2).reshape(1, LANES),
      jnp.tile(k_norm, 4).reshape(A_KV, 1), jnp.tile(xq_norm, 2).reshape(1, LANES))


def _sample_in_b_kernel(x_ref, g1_ref, win_ref, vg_ref, coef_ref, gbias_ref, xqg_ref, o_ref, vn_ref, xs_ref,
                        *, seq, n):
    h = _rms_rows(_stack_positions(x_ref, seq), g1_ref[...]).astype(BF16)
    z = _dot(h, win_ref[...])
    uv = _gelu_exact(z[:, :2 * B_WIDTH])
    u = uv[:, :B_WIDTH]
    vn = _rms_rows(uv[:, B_WIDTH:], vg_ref[...])
    vn_ref[...] = vn
    for t in range(seq):
        mixed = gbias_ref[t:t + 1, :] + coef_ref[t * seq:t * seq + 1, :] * vn[0:n]
        for j in range(1, t + 1):
            mixed = mixed + coef_ref[t * seq + j:t * seq + j + 1, :] * vn[j * n:(j + 1) * n]
        o_ref[t * n:(t + 1) * n, :] = (u[t * n:(t + 1) * n] * mixed).astype(o_ref.dtype)
    _write_query_slots(_pair_head_norm(z[:, 2 * B_WIDTH:], xqg_ref[...]), X_SLOT_PAIRS, xs_ref, seq, n)


def _sample_in_b(x3, norm1_g, w_in_b, v_norm, w_s, b_s, xq_norm):
    s, n, _ = x3.shape
    rows = s * n
    coef = jnp.repeat(w_s[:, :s, :s].transpose(1, 2, 0).reshape(s * s, B_GROUPS), B_GROUP_DIM, axis=1)
    gbias = jnp.repeat(b_s[:, :s].T, B_GROUP_DIM, axis=1)
    full = lambda shape: pl.BlockSpec(shape, lambda i: (0,) * len(shape))
    return pl.pallas_call(
        functools.partial(_sample_in_b_kernel, seq=s, n=n),
        grid=(1,),
        in_specs=[
            _const_spec((s, n, D_MODEL)),
            _const_spec((1, D_MODEL)),
            _const_spec(w_in_b.shape),
            _const_spec((1, B_WIDTH)),
            _const_spec((s * s, B_WIDTH)),
            _const_spec((s, B_WIDTH)),
            _const_spec((1, LANES)),
        ],
        out_specs=[full((rows, B_WIDTH)), full((rows, B_WIDTH)), full((2, n * X_HEADS * s, LANES))],
        out_shape=[
            jax.ShapeDtypeStruct((rows, B_WIDTH), BF16),
            jax.ShapeDtypeStruct((rows, B_WIDTH), F32),
            jax.ShapeDtypeStruct((2, n * X_HEADS * s, LANES), F32),
        ],
        compiler_params=_params("arbitrary"),
        name="sample_in_b",
    )(x3, norm1_g.reshape(1, D_MODEL), w_in_b, v_norm.reshape(1, B_WIDTH), coef, gbias,
      jnp.tile(xq_norm, 2).reshape(1, LANES))


def _sample_attn_win_kernel(qs_ref, ktn_ref, vtn_ref, vn_ref, slope_ref, sink_ref, ck_ref, cv_ref,
                            o_ref, wk_ref, wv_ref, *, sb, seq):
    rows = A_HEADS * seq
    tq = _div_mod(lax.broadcasted_iota(jnp.int32, (rows, WINDOW), 0), seq)[1]
    lane = lax.broadcasted_iota(jnp.int32, (rows, WINDOW), 1)
    slope = slope_ref[...]
    sink = sink_ref[...]
    d1 = tq + WINDOW - lane
    valid1 = d1 < WINDOW
    bias1 = slope * d1.astype(F32)
    new_seq, new_pos = _div_mod(lane, seq)
    d2 = tq - new_pos
    causal2 = d2 >= 0
    bias2 = slope * d2.astype(F32)
    ktn = ktn_ref[...]
    vtn = vtn_ref[...]
    ktn_b = ktn.astype(BF16)
    vn_b = vn_ref[...].astype(BF16)
    keep_old = lax.broadcasted_iota(jnp.int32, (A_KV, WINDOW), 1) < WINDOW - seq

    def body(n, carry):
        q = _read_seq_slots(qs_ref, n, rows).astype(BF16)
        ckt = ck_ref[n]
        cvt = cv_ref[n]
        s1 = jnp.where(valid1, _dot(q, ckt.astype(BF16)) - bias1, NEG)
        s2 = jnp.where(causal2 & (new_seq == n), _dot(q, ktn_b) - bias2, NEG)
        m = jnp.maximum(jnp.maximum(jnp.max(s1, axis=-1, keepdims=True), jnp.max(s2, axis=-1, keepdims=True)), sink)
        e1 = jnp.exp(s1 - m)
        e2 = jnp.exp(s2 - m)
        den = jnp.sum(e1, axis=-1, keepdims=True) + jnp.sum(e2, axis=-1, keepdims=True) + jnp.exp(sink - m)
        o = _nt_dot(e1.astype(BF16), cvt.astype(BF16)) + _dot(e2.astype(BF16), vn_b)
        _write_seq_slots(o_ref, n, rows, o / den)
        shift_new = WINDOW - seq - n * seq
        wk_ref[n] = jnp.where(keep_old, pltpu.roll(ckt, WINDOW - seq, 1), pltpu.roll(ktn, shift_new, 1))
        wv_ref[n] = jnp.where(keep_old, pltpu.roll(cvt, WINDOW - seq, 1), pltpu.roll(vtn, shift_new, 1))
        return carry

    lax.fori_loop(0, sb, body, 0, unroll=SAMPLE_SEQ_UNROLL)


def _sample_attn_win(qs, kt_new, vt_new, v_new, slope_rows, sink_rows, ckt, cvt, layer, seq):
    rows = A_HEADS * seq
    n = qs.shape[1] // rows
    assert LANES % seq == 0
    sb = LANES // seq
    assert n % sb == 0

    def seqs(*tail):
        return pl.BlockSpec((sb,) + tail, lambda i: (i,) + (0,) * len(tail))

    slot_rows = pl.BlockSpec((2, sb * rows, LANES), lambda i: (0, i, 0))
    cache = pl.BlockSpec((None, sb, A_KV, WINDOW), lambda i: (layer, i, 0, 0))
    win = jax.ShapeDtypeStruct((n, A_KV, WINDOW), F32)
    return pl.pallas_call(
        functools.partial(_sample_attn_win_kernel, sb=sb, seq=seq),
        grid=(n // sb,),
        in_specs=[
            slot_rows,
            pl.BlockSpec((A_KV, LANES), lambda i: (0, i)),
            pl.BlockSpec((A_KV, LANES), lambda i: (0, i)),
            pl.BlockSpec((LANES, A_KV), lambda i: (i, 0)),
            _const_spec((rows, 1)),
            _const_spec((rows, 1)),
            cache,
            cache,
        ],
        out_specs=[slot_rows, seqs(A_KV, WINDOW), seqs(A_KV, WINDOW)],
        out_shape=[jax.ShapeDtypeStruct((2, n * rows, LANES), F32), win, win],
        compiler_params=_params("arbitrary"),
        name="sample_attn_win",
    )(qs, kt_new, vt_new, v_new, slope_rows, sink_rows, ckt, cvt)


def _sample_attn_mem_kernel(xs_ref, mk_ref, mv_ref, xo_ref, *, sb, xrows):
    def body(n, carry):
        q = _read_seq_slots(xs_ref, n, xrows).astype(BF16)
        s = _dot(q, mk_ref[n].astype(BF16))
        m = jnp.max(s, axis=-1, keepdims=True)
        e = jnp.exp(s - m)
        den = jnp.sum(e, axis=-1, keepdims=True)
        _write_seq_slots(xo_ref, n, xrows, _nt_dot(e.astype(BF16), mv_ref[n].astype(BF16)) / den)
        return carry

    lax.fori_loop(0, sb, body, 0, unroll=SAMPLE_SEQ_UNROLL)


def _sample_attn_mem(xs, mkt, mvt, layer, seq):
    xrows = X_HEADS * seq
    n = xs.shape[1] // xrows
    sb = min(SAMPLE_MEM_SEQS, n)
    assert n % sb == 0
    qspec = pl.BlockSpec((2, sb * xrows, LANES), lambda i: (0, i, 0))
    cache = pl.BlockSpec((None, sb, X_WIDTH, N_MEM), lambda i: (layer, i, 0, 0))
    return pl.pallas_call(
        functools.partial(_sample_attn_mem_kernel, sb=sb, xrows=xrows),
        grid=(n // sb,),
        in_specs=[qspec, cache, cache],
        out_specs=qspec,
        out_shape=jax.ShapeDtypeStruct((2, n * xrows, LANES), F32),
        compiler_params=_params("arbitrary"),
        name="sample_attn_mem",
    )(xs, mkt, mvt)


def _feature_major(cache):
    l, n, p, kv, d = cache.shape
    return cache.transpose(0, 1, 3, 4, 2).reshape(l, n, kv * d, p)


def kernel(x_prompt, x_sample, cache_win_k, cache_win_v, cache_mem_k, cache_mem_v, state_conv, mem_prompt,
           norm1_g, norm2_g, mem_norm_g, w_in_a, q_norm_a, k_norm_a, sinks_a, w_out_a, w_in_b, v_norm_b, w_s_b,
           b_s_b, w_out_b, w_mem_kv, xq_norm, xk_norm, w_ffn_in, conv_w, conv_b, w_down):
    depth = norm1_g.shape[0]
    assert depth == 2 and w_in_a.shape[0] == 1 and w_in_b.shape[0] == 1
    nb = x_prompt.shape[0]
    ns, seq, _ = x_sample.shape
    assert cache_win_k.shape[2] == WINDOW

    w_in_a_b = w_in_a[0].astype(BF16)
    w_in_b_b = w_in_b[0].astype(BF16)
    w_out_a_b = w_out_a[0].astype(BF16)
    w_out_b_b = w_out_b[0].astype(BF16)
    w_mem_kv_b = w_mem_kv.astype(BF16)
    w_ffn_in_b = w_ffn_in.astype(BF16)
    w_down_b = w_down.astype(BF16)

    mk_all, mv_all, mem_kt_p, mem_vt_p = _memory_kv(mem_prompt, mem_norm_g, w_mem_kv_b, xk_norm)
    w_in_a_r = _reorder_heads(w_in_a_b, 1)
    w_out_a_r = _reorder_heads(w_out_a_b, 0)
    x2, win_k_p, win_v_p, conv_p0 = _layer_a_prompt(
        x_prompt, 0, norm1_g[0], w_in_a_r, q_norm_a[0], k_norm_a[0], xq_norm[0], sinks_a[0], mk_all, mv_all, w_out_a_r,
        norm2_g[0], w_ffn_in_b, conv_w[0], conv_b[0], w_down_b)
    y_prompt, chunk_v_p, conv_p1 = _layer_b_prompt(
        x2, 1, norm1_g[1], w_in_b_b, v_norm_b[0], w_s_b[0], b_s_b[0], xq_norm[1], mk_all, mv_all, w_out_b_b,
        norm2_g[1], w_ffn_in_b, conv_w[1], conv_b[1], w_down_b)

    xs3 = x_sample.transpose(1, 0, 2)
    mkt = _feature_major(cache_mem_k)
    mvt = _feature_major(cache_mem_v)

    def by_seq_cols(a):
        return a.reshape(a.shape[0], seq, ns).transpose(0, 2, 1).reshape(a.shape[0], ns * seq)

    qs, v_new, xslots, kt_new, vt_new = _sample_in_a(xs3, norm1_g[0], w_in_a_r, q_norm_a[0], k_norm_a[0], xq_norm[0])
    slope_rows = jnp.repeat(jnp.asarray(SLOPES, F32), seq).reshape(A_HEADS * seq, 1)
    sink_rows = jnp.repeat(sinks_a[0].astype(F32), seq).reshape(A_HEADS * seq, 1)
    o_s, win_kt_s, win_vt_s = _sample_attn_win(
        qs, by_seq_cols(kt_new), by_seq_cols(vt_new),
        v_new.reshape(seq, ns, A_KV).transpose(1, 0, 2).reshape(ns * seq, A_KV), slope_rows, sink_rows,
        _feature_major(cache_win_k), _feature_major(cache_win_v), 0, seq)
    xo_s = _sample_attn_mem(xslots, mkt, mvt, 0, seq)
    ys1, conv_s0 = _ffn_sample(
        xs3, o_s, xo_s, w_out_a_r[:A_Q], w_out_a_r[A_Q:], 0, norm2_g[0], w_ffn_in_b, conv_w[0], conv_b[0], w_down_b,
        state_conv[0])

    o_b, vn_s, xslots = _sample_in_b(ys1, norm1_g[1], w_in_b_b, v_norm_b[0], w_s_b[0], b_s_b[0], xq_norm[1])
    xo_s = _sample_attn_mem(xslots, mkt, mvt, 1, seq)
    ys2, conv_s1 = _ffn_sample(
        ys1, o_b.reshape(seq, ns, B_WIDTH), xo_s, w_out_b_b[:B_WIDTH], w_out_b_b[B_WIDTH:], 1, norm2_g[1], w_ffn_in_b,
        conv_w[1], conv_b[1], w_down_b, state_conv[1])

    def position_major(ft):
        lead = ft.shape[:-2]
        nd = len(lead)
        return ft.reshape(*lead, A_KV_HEADS, HEAD_DIM, ft.shape[-1]).transpose(*range(nd), nd + 2, nd, nd + 1)

    return (
        y_prompt,
        ys2.transpose(1, 0, 2),
        position_major(win_k_p)[None],
        position_major(win_v_p)[None],
        chunk_v_p.reshape(1, nb, CHUNK, B_WIDTH),
        position_major(mem_kt_p),
        position_major(mem_vt_p),
        jnp.stack([conv_p0, conv_p1]),
        position_major(win_kt_s)[None],
        position_major(win_vt_s)[None],
        vn_s.reshape(seq, ns, B_WIDTH).transpose(1, 0, 2)[None],
        jnp.stack([conv_s0.transpose(1, 0, 2), conv_s1.transpose(1, 0, 2)]),
    )
```

```python
import functools
import math

import numpy as np
import jax
import jax.numpy as jnp
from jax import lax
from jax.experimental import pallas as pl
from jax.experimental.pallas import tpu as pltpu

F32 = jnp.float32
BF16 = jnp.bfloat16

D_MODEL = 1024
HEAD_DIM = 64
A_HEADS = 12
A_KV_HEADS = 4
A_GROUP = A_HEADS // A_KV_HEADS
WINDOW = 128
BLOCK = 128
B_WIDTH = 768
B_GROUPS = 4
B_GROUP_DIM = B_WIDTH // B_GROUPS
CHUNK = 128
X_HEADS = 4
X_WIDTH = X_HEADS * HEAD_DIM
N_MEM = 256
D_FF = 2816
CONV_W = 3
A_Q = A_HEADS * HEAD_DIM
A_KV = A_KV_HEADS * HEAD_DIM
EPS = 1e-6
NEG = -1e30
SCALE = HEAD_DIM ** -0.5
LOG2E = float(np.float32(1.0 / np.log(2.0)))
SQRT_HALF = float(np.sqrt(0.5).astype(np.float32))

LANES = 128
VMEM_LIMIT_BYTES = 56 * 1024 * 1024
GATE_WINDOW_START = (0, 128, 384, 512)
assert B_GROUP_DIM == 192 and B_GROUPS == 4
PROMPT_ROWS = 512
SAMPLE_MEM_SEQS = 16
SAMPLE_FFN_SEQS = 128
SAMPLE_SEQ_UNROLL = 4


def _alibi_slopes(n):
    def pow2(m):
        start = 2.0 ** (-8.0 / m)
        return [start ** (i + 1) for i in range(m)]

    p = 2 ** int(math.floor(math.log2(n)))
    s = pow2(p)
    if p < n:
        s = s + pow2(2 * p)[0::2][: n - p]
    return [float(np.float32(v)) for v in s]


SLOPES = _alibi_slopes(A_HEADS)

Q_LANE_ORDER = (0, 3, 1, 4, 2, 5, 6, 9, 7, 10, 8, 11)
assert all(Q_LANE_ORDER[p] // A_GROUP == 2 * (p // (2 * A_GROUP)) + p % 2 for p in range(A_HEADS))


_SLOTS = A_KV_HEADS // 2
assert Q_LANE_ORDER == tuple(np.arange(A_HEADS).reshape(_SLOTS, 2, A_GROUP).swapaxes(1, 2).reshape(-1).tolist())


def _reorder_heads(w, axis):
    lead, tail = w.shape[:axis], w.shape[axis + 1:]
    q = lax.slice_in_dim(w, 0, A_Q, axis=axis).reshape(*lead, _SLOTS, 2, A_GROUP, HEAD_DIM, *tail)
    q = jnp.swapaxes(q, axis + 1, axis + 2).reshape(*lead, A_Q, *tail)
    return jnp.concatenate([q, lax.slice_in_dim(w, A_Q, w.shape[axis], axis=axis)], axis=axis)


def _params(*sem):
    return pltpu.CompilerParams(dimension_semantics=sem, vmem_limit_bytes=VMEM_LIMIT_BYTES)


def _const_spec(shape):
    nd = len(shape)
    return pl.BlockSpec(shape, lambda *_: (0,) * nd, pipeline_mode=pl.Buffered(1))


def _layer_spec(layer, shape):
    nd = len(shape)
    return pl.BlockSpec((None,) + tuple(shape), lambda *_: (layer,) + (0,) * nd, pipeline_mode=pl.Buffered(1))


def _rms_rows(x, g):
    ms = jnp.mean(x * x, axis=-1, keepdims=True)
    return x * lax.rsqrt(ms + EPS) * g


def _pair_head_norm(z, g2):
    t, w = z.shape
    lo = lax.broadcasted_iota(jnp.int32, (t, LANES), 1) < HEAD_DIM
    out = []
    for p in range(w // LANES):
        zz = z[:, p * LANES:(p + 1) * LANES]
        sq = zz * zz
        s_lo = jnp.sum(jnp.where(lo, sq, 0.0), axis=-1, keepdims=True)
        s_hi = jnp.sum(jnp.where(lo, 0.0, sq), axis=-1, keepdims=True)
        r_lo = lax.rsqrt(s_lo * (1.0 / HEAD_DIM) + EPS)
        r_hi = lax.rsqrt(s_hi * (1.0 / HEAD_DIM) + EPS)
        out.append(zz * jnp.where(lo, r_lo, r_hi) * g2)
    return out


def _nt_dot(a, b):
    return lax.dot_general(a, b, (((1,), (1,)), ((), ())), preferred_element_type=F32)


def _dot(a, b):
    return jnp.dot(a, b, preferred_element_type=F32)


def _div_mod(i, n):
    if n & (n - 1) == 0:
        return i >> (n.bit_length() - 1), i & (n - 1)
    return i // n, i % n


def _gelu_exact(x):
    return 0.5 * x * (1.0 + lax.erf(x * SQRT_HALF))


def _conv_gate(a, am1, am2, up, cw_ref, cb_ref):
    c = cb_ref[...] + ((cw_ref[0:1, :] * am2 + cw_ref[1:2, :] * am1) + cw_ref[2:3, :] * a)
    return ((c * (1.0 / (1.0 + jnp.exp(-c)))) * up).astype(BF16)


def _memkv_kernel(mem_ref, g_ref, w_ref, kg_ref, mkb_ref, mvb_ref, mkt_ref, mvt_ref, *, depth):
    m = mem_ref[0]
    for layer in range(depth):
        h = _rms_rows(m, g_ref[layer]).astype(BF16)
        kv = _dot(h, w_ref[layer])
        mk = jnp.concatenate(_pair_head_norm(kv[:, :X_WIDTH], kg_ref[layer]), axis=1)
        mv = kv[:, X_WIDTH:]
        mkb_ref[layer, 0] = mk.astype(BF16)
        mvb_ref[layer, 0] = mv.astype(BF16)
        mkt_ref[layer, 0] = mk.T
        mvt_ref[layer, 0] = mv.T


def _memory_kv(mem, mem_norm_g, w_mem_kv_b, xk_norm):
    depth = w_mem_kv_b.shape[0]
    n = mem.shape[0]
    g = mem_norm_g.reshape(depth, 1, D_MODEL)
    kg2 = jnp.tile(xk_norm, (1, 2)).reshape(depth, 1, LANES)
    spec = lambda rows, cols: pl.BlockSpec((depth, 1, rows, cols), lambda b: (0, b, 0, 0))
    return pl.pallas_call(
        functools.partial(_memkv_kernel, depth=depth),
        grid=(n,),
        in_specs=[
            pl.BlockSpec((1, N_MEM, D_MODEL), lambda b: (b, 0, 0)),
            _const_spec((depth, 1, D_MODEL)),
            _const_spec((depth, D_MODEL, 2 * X_WIDTH)),
            _const_spec((depth, 1, LANES)),
        ],
        out_specs=[spec(N_MEM, X_WIDTH), spec(N_MEM, X_WIDTH), spec(X_WIDTH, N_MEM), spec(X_WIDTH, N_MEM)],
        out_shape=[
            jax.ShapeDtypeStruct((depth, n, N_MEM, X_WIDTH), BF16),
            jax.ShapeDtypeStruct((depth, n, N_MEM, X_WIDTH), BF16),
            jax.ShapeDtypeStruct((depth, n, X_WIDTH, N_MEM), F32),
            jax.ShapeDtypeStruct((depth, n, X_WIDTH, N_MEM), F32),
        ],
        compiler_params=_params("arbitrary"),
        name="memory_kv",
    )(mem, g, w_mem_kv_b, kg2)


def _half_masked_rows(pair, lo):
    return jnp.concatenate([jnp.where(lo, pair, 0.0), jnp.where(lo, 0.0, pair)], axis=0).astype(BF16)


def _prompt_mem_attention(xq_pairs, mk_ref, mv_ref, ocat, col0):
    t = xq_pairs[0].shape[0]
    lo = lax.broadcasted_iota(jnp.int32, (t, LANES), 1) < HEAD_DIM
    for j, pair in enumerate(xq_pairs):
        mk = mk_ref[0, 0, :, j * LANES:(j + 1) * LANES]
        mv = mv_ref[0, 0, :, j * LANES:(j + 1) * LANES]
        s = _nt_dot(_half_masked_rows(pair * (SCALE * LOG2E), lo), mk)
        m = jnp.max(s, axis=-1, keepdims=True)
        e = jnp.exp2(s - m)
        den = jnp.sum(e, axis=-1, keepdims=True)
        o = _dot(e.astype(BF16), mv) / den
        ocat[:, col0 + j * LANES:col0 + (j + 1) * LANES] = jnp.where(lo, o[:t], o[t:]).astype(BF16)


def _mixer_a_body(t, x, sinks_ref, g1_ref, win_ref, qg_ref, kg_ref, xqg_ref, mk_ref, mv_ref, wout_ref,
                  wk_ref, wv_ref, kbuf, vbuf, ocat, tb):
    h = _rms_rows(x, g1_ref[...]).astype(BF16)
    z = _dot(h, win_ref[...])
    q_pairs = _pair_head_norm(z[:, :A_Q], qg_ref[...])
    k_pairs = _pair_head_norm(z[:, A_Q:A_Q + A_KV], kg_ref[...])
    v = z[:, A_Q + A_KV:A_Q + 2 * A_KV]
    xq_pairs = _pair_head_norm(z[:, A_Q + 2 * A_KV:], xqg_ref[...])

    for p, kp in enumerate(k_pairs):
        wk_ref[0, p * LANES:(p + 1) * LANES, :] = kp[tb - BLOCK:, :].T
        kbuf[BLOCK:BLOCK + tb, p * LANES:(p + 1) * LANES] = kp.astype(BF16)
    wv_ref[0] = v[tb - BLOCK:, :].T
    vbuf[BLOCK:BLOCK + tb, :] = v.astype(BF16)

    qi = lax.broadcasted_iota(jnp.int32, (BLOCK, BLOCK), 0)
    kj = lax.broadcasted_iota(jnp.int32, (BLOCK, BLOCK), 1)
    from_prev = kj > qi
    key_pos = jnp.where(from_prev, kj, kj + BLOCK).astype(F32)
    query_pos = (lax.broadcasted_iota(jnp.int32, (BLOCK, 1), 0) + BLOCK).astype(F32)

    lo = lax.broadcasted_iota(jnp.int32, (BLOCK, LANES), 1) < HEAD_DIM
    pairs_per_slot = A_HEADS // A_KV_HEADS
    for s in range(tb // BLOCK):
        rows = slice(s * BLOCK, (s + 1) * BLOCK)
        prev_keys = (jnp.where(t > 0, BLOCK, 0) if s == 0 else BLOCK)
        valid = jnp.logical_not(from_prev) | (kj < prev_keys)
        for j in range(A_KV // LANES):
            kw = kbuf[s * BLOCK:(s + 2) * BLOCK, j * LANES:(j + 1) * LANES]
            vw = vbuf[s * BLOCK:(s + 2) * BLOCK, j * LANES:(j + 1) * LANES]
            q_rows = jnp.concatenate(
                [_half_masked_rows(q_pairs[pairs_per_slot * j + c][rows] * (SCALE * LOG2E), lo)
                 for c in range(pairs_per_slot)], axis=0)
            sc_all = _nt_dot(q_rows, kw)
            e_parts, den_parts = [], []
            for r in range(2 * pairs_per_slot):
                hq = Q_LANE_ORDER[2 * pairs_per_slot * j + r]
                slope2 = SLOPES[hq] * LOG2E
                both = sc_all[r * BLOCK:(r + 1) * BLOCK]
                sc = jnp.where(from_prev, both[:, :BLOCK], both[:, BLOCK:]) + slope2 * key_pos
                sc = jnp.where(valid, sc, NEG)
                sink = sinks_ref[hq] * LOG2E + slope2 * query_pos
                m = jnp.maximum(jnp.max(sc, axis=-1, keepdims=True), sink)
                e = jnp.exp2(sc - m)
                den_parts.append(jnp.sum(e, axis=-1, keepdims=True) + jnp.exp2(sink - m))
                e_parts.append(jnp.concatenate([jnp.where(from_prev, e, 0.0), jnp.where(from_prev, 0.0, e)],
                                               axis=1).astype(BF16))
            o_all = _dot(jnp.concatenate(e_parts, axis=0), vw)
            for c in range(pairs_per_slot):
                o_lo = o_all[2 * c * BLOCK:(2 * c + 1) * BLOCK] / den_parts[2 * c]
                o_hi = o_all[(2 * c + 1) * BLOCK:(2 * c + 2) * BLOCK] / den_parts[2 * c + 1]
                col = pairs_per_slot * j + c
                ocat[rows, col * LANES:(col + 1) * LANES] = jnp.where(lo, o_lo, o_hi).astype(BF16)

    _prompt_mem_attention(xq_pairs, mk_ref, mv_ref, ocat, A_Q)
    x1 = x + _dot(ocat[...], wout_ref[...])

    kbuf[0:BLOCK, :] = kbuf[tb:tb + BLOCK, :]
    vbuf[0:BLOCK, :] = vbuf[tb:tb + BLOCK, :]
    return x1


def _ffn_body(x, g2_ref, wfi_ref, cw_ref, cb_ref, wd_ref, tail_ref, tb):
    h = _rms_rows(x, g2_ref[...]).astype(BF16)
    gu = _dot(h, wfi_ref[...])
    a = gu[:, :D_FF]
    row = lax.broadcasted_iota(jnp.int32, (tb, 1), 0)
    prev_last = tail_ref[0, 7:8, :]
    am1 = jnp.where(row == 0, prev_last, pltpu.roll(a, 1, 0))
    am2 = jnp.where(row == 0, tail_ref[0, 6:7, :], jnp.where(row == 1, prev_last, pltpu.roll(a, 2, 0)))
    tail_ref[0] = a[tb - 8:, :]
    return x + _dot(_conv_gate(a, am1, am2, gu[:, D_FF:], cw_ref, cb_ref), wd_ref[...])


def _layer_a_kernel(sinks_ref, x_ref, g1_ref, win_ref, qg_ref, kg_ref, xqg_ref, mk_ref, mv_ref, wout_ref,
                    g2_ref, wfi_ref, cw_ref, cb_ref, wd_ref, y_ref, wk_ref, wv_ref, tail_ref, kbuf, vbuf, ocat, *, tb):
    t = pl.program_id(1)

    @pl.when(t == 0)
    def _():
        kbuf[0:BLOCK, :] = jnp.zeros((BLOCK, A_KV), BF16)
        vbuf[0:BLOCK, :] = jnp.zeros((BLOCK, A_KV), BF16)
        tail_ref[0] = jnp.zeros((8, D_FF), F32)

    x1 = _mixer_a_body(t, x_ref[0], sinks_ref, g1_ref, win_ref, qg_ref, kg_ref, xqg_ref, mk_ref, mv_ref, wout_ref,
                       wk_ref, wv_ref, kbuf, vbuf, ocat, tb)
    y_ref[0] = _ffn_body(x1, g2_ref, wfi_ref, cw_ref, cb_ref, wd_ref, tail_ref, tb)


def _layer_specs(tb, layer):
    xspec = pl.BlockSpec((1, tb, D_MODEL), lambda b, i: (b, i, 0))
    mspec = pl.BlockSpec((1, 1, N_MEM, X_WIDTH), lambda b, i: (layer, b, 0, 0))
    per_seq = lambda rows, width: pl.BlockSpec((1, rows, width), lambda b, i: (b, 0, 0))
    return xspec, mspec, per_seq(8, D_FF), per_seq


def _ffn_specs(layer):
    return [
        _const_spec((1, D_MODEL)),
        _layer_spec(layer, (D_MODEL, 2 * D_FF)),
        _const_spec((CONV_W, D_FF)),
        _const_spec((1, D_FF)),
        _layer_spec(layer, (D_FF, D_MODEL)),
    ]


def _layer_a_prompt(x, layer, norm1_g, w_in_b, q_norm, k_norm, xq_norm, sinks, mk_all, mv_all, w_out_b,
                    norm2_g, w_ffn_in_b, conv_w, conv_b, w_down_b):
    n, t, _ = x.shape
    tb = min(PROMPT_ROWS, t)
    a_in = w_in_b.shape[1]
    xspec, mspec, tailspec, per_seq = _layer_specs(tb, layer)
    wspec = per_seq(A_KV, BLOCK)
    y, wk, wv, tail = pl.pallas_call(
        functools.partial(_layer_a_kernel, tb=tb),
        grid=(n, t // tb),
        in_specs=[
            pl.BlockSpec(memory_space=pltpu.SMEM),
            xspec,
            _const_spec((1, D_MODEL)),
            _const_spec((D_MODEL, a_in)),
            _const_spec((1, LANES)),
            _const_spec((1, LANES)),
            _const_spec((1, LANES)),
            mspec,
            mspec,
            _const_spec((A_Q + X_WIDTH, D_MODEL)),
        ] + _ffn_specs(layer),
        out_specs=[xspec, wspec, wspec, tailspec],
        out_shape=[
            jax.ShapeDtypeStruct((n, t, D_MODEL), F32),
            jax.ShapeDtypeStruct((n, A_KV, BLOCK), F32),
            jax.ShapeDtypeStruct((n, A_KV, BLOCK), F32),
            jax.ShapeDtypeStruct((n, 8, D_FF), F32),
        ],
        scratch_shapes=[
            pltpu.VMEM((tb + BLOCK, A_KV), BF16),
            pltpu.VMEM((tb + BLOCK, A_KV), BF16),
            pltpu.VMEM((tb, A_Q + X_WIDTH), BF16),
        ],
        compiler_params=_params("arbitrary", "arbitrary"),
        name="layer_a_prompt",
    )(sinks, x, norm1_g.reshape(1, D_MODEL), w_in_b, jnp.tile(q_norm, 2).reshape(1, LANES),
      jnp.tile(k_norm, 2).reshape(1, LANES), jnp.tile(xq_norm, 2).reshape(1, LANES), mk_all, mv_all, w_out_b,
      norm2_g.reshape(1, D_MODEL), w_ffn_in_b, conv_w, conv_b.reshape(1, D_FF), w_down_b)
    return y, wk, wv, tail[:, 8 - (CONV_W - 1):, :]


def _mixer_b_body(x, g1_ref, win_ref, vg_ref, ws_ref, bias_ref, xqg_ref, mk_ref, mv_ref, wout_ref, cv_ref, ocat, tb):
    h = _rms_rows(x, g1_ref[...]).astype(BF16)
    z = _dot(h, win_ref[...])
    uv = _gelu_exact(z[:, :2 * B_WIDTH])
    u = uv[:, :B_WIDTH]
    vn = _rms_rows(uv[:, B_WIDTH:], vg_ref[...])
    cv_ref[0] = vn[tb - CHUNK:, :]
    xq_pairs = _pair_head_norm(z[:, 2 * B_WIDTH:], xqg_ref[...])

    wi = lax.broadcasted_iota(jnp.int32, (CHUNK, CHUNK), 0)
    wj = lax.broadcasted_iota(jnp.int32, (CHUNK, CHUNK), 1)
    causal = wi >= wj
    w_tril = [jnp.where(causal, ws_ref[g], 0.0).astype(BF16) for g in range(B_GROUPS)]
    lo = lax.broadcasted_iota(jnp.int32, (CHUNK, LANES), 1) < HEAD_DIM
    bias = bias_ref[...]
    for c in range(tb // CHUNK):
        vc = vn[c * CHUNK:(c + 1) * CHUNK, :].astype(BF16)
        r = [_dot(w_tril[g], vc[:, c0:c0 + 256]) for g, c0 in enumerate(GATE_WINDOW_START)]
        mixed = jnp.concatenate(
            [r[0][:, :128], jnp.where(lo, r[0][:, 128:], r[1][:, :128]), r[1][:, 128:],
             r[2][:, :128], jnp.where(lo, r[2][:, 128:], r[3][:, :128]), r[3][:, 128:]], axis=1)
        o = u[c * CHUNK:(c + 1) * CHUNK, :] * (mixed + bias)
        ocat[c * CHUNK:(c + 1) * CHUNK, 0:B_WIDTH] = o.astype(BF16)

    _prompt_mem_attention(xq_pairs, mk_ref, mv_ref, ocat, B_WIDTH)
    return x + _dot(ocat[...], wout_ref[...])


def _layer_b_kernel(x_ref, g1_ref, win_ref, vg_ref, ws_ref, bias_ref, xqg_ref, mk_ref, mv_ref, wout_ref,
                    g2_ref, wfi_ref, cw_ref, cb_ref, wd_ref, y_ref, cv_ref, tail_ref, ocat, *, tb):
    @pl.when(pl.program_id(1) == 0)
    def _():
        tail_ref[0] = jnp.zeros((8, D_FF), F32)

    x1 = _mixer_b_body(x_ref[0], g1_ref, win_ref, vg_ref, ws_ref, bias_ref, xqg_ref, mk_ref, mv_ref, wout_ref,
                       cv_ref, ocat, tb)
    y_ref[0] = _ffn_body(x1, g2_ref, wfi_ref, cw_ref, cb_ref, wd_ref, tail_ref, tb)


def _gate_bias_rows(b_s):
    return jnp.repeat(b_s.T, B_GROUP_DIM, axis=1)


def _layer_b_prompt(x, layer, norm1_g, w_in_b, v_norm, w_s, b_s, xq_norm, mk_all, mv_all, w_out_b,
                    norm2_g, w_ffn_in_b, conv_w, conv_b, w_down_b):
    n, t, _ = x.shape
    tb = min(PROMPT_ROWS, t)
    b_in = w_in_b.shape[1]
    xspec, mspec, tailspec, per_seq = _layer_specs(tb, layer)
    y, cv, tail = pl.pallas_call(
        functools.partial(_layer_b_kernel, tb=tb),
        grid=(n, t // tb),
        in_specs=[
            xspec,
            _const_spec((1, D_MODEL)),
            _const_spec((D_MODEL, b_in)),
            _const_spec((1, B_WIDTH)),
            _const_spec((B_GROUPS, CHUNK, CHUNK)),
            _const_spec((CHUNK, B_WIDTH)),
            _const_spec((1, LANES)),
            mspec,
            mspec,
            _const_spec((B_WIDTH + X_WIDTH, D_MODEL)),
        ] + _ffn_specs(layer),
        out_specs=[xspec, per_seq(CHUNK, B_WIDTH), tailspec],
        out_shape=[
            jax.ShapeDtypeStruct((n, t, D_MODEL), F32),
            jax.ShapeDtypeStruct((n, CHUNK, B_WIDTH), F32),
            jax.ShapeDtypeStruct((n, 8, D_FF), F32),
        ],
        scratch_shapes=[pltpu.VMEM((tb, B_WIDTH + X_WIDTH), BF16)],
        compiler_params=_params("arbitrary", "arbitrary"),
        name="layer_b_prompt",
    )(x, norm1_g.reshape(1, D_MODEL), w_in_b, v_norm.reshape(1, B_WIDTH), w_s, _gate_bias_rows(b_s),
      jnp.tile(xq_norm, 2).reshape(1, LANES), mk_all, mv_all, w_out_b,
      norm2_g.reshape(1, D_MODEL), w_ffn_in_b, conv_w, conv_b.reshape(1, D_FF), w_down_b)
    return y, cv, tail[:, 8 - (CONV_W - 1):, :]


def _stack_positions(ref, seq):
    return jnp.concatenate([ref[t] for t in range(seq)], axis=0)


def _ffn_sample_kernel(x_ref, om_ref, ox_ref, wom_ref, wox_ref, g2_ref, wfi_ref, cw_ref, cb_ref, wd_ref, p0_ref, p1_ref,
                       y_ref, atail_ref, *, seq, nb, mixer_slots):
    om = _compact_slots(om_ref, Q_SLOT_PAIRS, seq, nb) if mixer_slots else _stack_positions(om_ref, seq)
    ox = _compact_slots(ox_ref, X_SLOT_PAIRS, seq, nb)
    x1 = _stack_positions(x_ref, seq) + _dot(om, wom_ref[...]) + _dot(ox, wox_ref[...])
    h = _rms_rows(x1, g2_ref[...]).astype(BF16)
    gu = _dot(h, wfi_ref[...])
    a = gu[:, :D_FF]
    blocks = [p0_ref[...], p1_ref[...]] + [a[t * nb:(t + 1) * nb] for t in range(seq)]
    am2 = jnp.concatenate(blocks[:seq], axis=0)
    am1 = jnp.concatenate(blocks[1:seq + 1], axis=0)
    y = x1 + _dot(_conv_gate(a, am1, am2, gu[:, D_FF:], cw_ref, cb_ref), wd_ref[...])
    for t in range(seq):
        y_ref[t] = y[t * nb:(t + 1) * nb]
    for r in range(CONV_W - 1):
        atail_ref[r] = blocks[seq + r]


def _ffn_sample(x3, om3, ox3, w_om, w_ox, layer, norm2_g, w_ffn_in_b, conv_w, conv_b, w_down_b, prefix):
    s, n, _ = x3.shape
    nb = min(SAMPLE_FFN_SEQS, n)
    assert n % nb == 0 and s >= CONV_W - 1

    def blk(lead, width):
        return pl.BlockSpec((lead, nb, width), lambda i: (0, i, 0))

    def slots(heads):
        return pl.BlockSpec((nb * heads * s, LANES), lambda i: (i, 0))

    pspec = pl.BlockSpec((nb, D_FF), lambda i: (i, 0))
    mixer_slots = om3.ndim == 2
    assert ox3.shape == (n * X_HEADS * s, LANES)
    return pl.pallas_call(
        functools.partial(_ffn_sample_kernel, seq=s, nb=nb, mixer_slots=mixer_slots),
        grid=(n // nb,),
        in_specs=[
            blk(s, D_MODEL),
            slots(A_HEADS) if mixer_slots else blk(s, om3.shape[2]),
            slots(X_HEADS),
            _const_spec(w_om.shape),
            _const_spec(w_ox.shape),
            _const_spec((1, D_MODEL)),
            _layer_spec(layer, (D_MODEL, 2 * D_FF)),
            _const_spec((CONV_W, D_FF)),
            _const_spec((1, D_FF)),
            _layer_spec(layer, (D_FF, D_MODEL)),
            pspec,
            pspec,
        ],
        out_specs=[blk(s, D_MODEL), blk(CONV_W - 1, D_FF)],
        out_shape=[
            jax.ShapeDtypeStruct((s, n, D_MODEL), F32),
            jax.ShapeDtypeStruct((CONV_W - 1, n, D_FF), F32),
        ],
        compiler_params=_params("arbitrary"),
        name="ffn_sample",
    )(x3, om3, ox3, w_om, w_ox, norm2_g.reshape(1, D_MODEL), w_ffn_in_b, conv_w, conv_b.reshape(1, D_FF), w_down_b,
      prefix[:, 0, :], prefix[:, 1, :])


Q_SLOT_PAIRS = tuple((Q_LANE_ORDER[2 * c], Q_LANE_ORDER[2 * c + 1], c // A_GROUP) for c in range(A_HEADS // 2))
X_SLOT_PAIRS = tuple((2 * c, 2 * c + 1, c) for c in range(X_HEADS // 2))
for _slot_pairs in (Q_SLOT_PAIRS, X_SLOT_PAIRS):
    assert all(col == (head >= len(_slot_pairs)) for h_lo, h_hi, col in _slot_pairs for head in (h_lo, h_hi))


def _write_query_slots(pairs, slot_pairs, out_ref, seq, n):
    per_seq = 2 * len(slot_pairs) * seq
    lo = lax.broadcasted_iota(jnp.int32, (seq * n, LANES), 1) < HEAD_DIM
    for pair, (h_lo, h_hi, _) in zip(pairs, slot_pairs):
        scaled = pair * SCALE
        for head, own in ((h_lo, jnp.where(lo, scaled, 0.0)), (h_hi, jnp.where(lo, 0.0, scaled))):
            for t in range(seq):
                out_ref[pl.ds(head * seq + t, n, stride=per_seq), :] = own[t * n:(t + 1) * n]


def _upper_half_rows(per_seq):
    return lax.broadcasted_iota(jnp.int32, (per_seq, LANES), 0) >= per_seq // 2


def _read_seq_slots(ref, n, per_seq):
    q = ref[pl.ds(pl.multiple_of(n * per_seq, 8), per_seq), :]
    upper = _upper_half_rows(per_seq)
    return jnp.concatenate([jnp.where(upper, 0.0, q), jnp.where(upper, q, 0.0)], axis=1)


def _write_seq_slots(ref, n, per_seq, val):
    ref[pl.ds(pl.multiple_of(n * per_seq, 8), per_seq), :] = jnp.where(
        _upper_half_rows(per_seq), val[:, LANES:], val[:, :LANES])


def _compact_slots(o_ref, slot_pairs, seq, n):
    per_seq = 2 * len(slot_pairs) * seq
    lo = lax.broadcasted_iota(jnp.int32, (n, LANES), 1) < HEAD_DIM
    cols = []
    for h_lo, h_hi, _ in slot_pairs:
        blocks = [jnp.where(lo, o_ref[pl.ds(h_lo * seq + t, n, stride=per_seq), :],
                            o_ref[pl.ds(h_hi * seq + t, n, stride=per_seq), :]) for t in range(seq)]
        cols.append(jnp.concatenate(blocks, axis=0))
    return jnp.concatenate(cols, axis=1).astype(BF16)


def _sample_in_a_kernel(x_ref, g1_ref, win_ref, wkvt_ref, qg_ref, kgc_ref, xqg_ref,
                        qs_ref, v_ref, xs_ref, kt_ref, vt_ref, *, seq, n):
    h = _rms_rows(_stack_positions(x_ref, seq), g1_ref[...]).astype(BF16)
    z = _dot(h, win_ref[...])
    _write_query_slots(_pair_head_norm(z[:, :A_Q], qg_ref[...]), Q_SLOT_PAIRS, qs_ref, seq, n)
    v_ref[...] = z[:, A_Q + A_KV:A_Q + 2 * A_KV]
    _write_query_slots(_pair_head_norm(z[:, A_Q + 2 * A_KV:], xqg_ref[...]), X_SLOT_PAIRS, xs_ref, seq, n)
    kvt = _nt_dot(wkvt_ref[...], h)
    for hk in range(A_KV_HEADS):
        sl = slice(hk * HEAD_DIM, (hk + 1) * HEAD_DIM)
        kh = kvt[sl, :]
        ms = jnp.mean(kh * kh, axis=0, keepdims=True)
        kt_ref[sl, :] = kh * lax.rsqrt(ms + EPS) * kgc_ref[sl, :]
    vt_ref[...] = kvt[A_KV:, :]


def _sample_in_a(x3, norm1_g, w_in_r, q_norm, k_norm, xq_norm):
    s, n, _ = x3.shape
    rows = s * n
    w_kvt = w_in_r[:, A_Q:A_Q + 2 * A_KV].T
    slots = lambda heads: jax.ShapeDtypeStruct((n * heads * s, LANES), F32)
    full = lambda shape: pl.BlockSpec(shape, lambda i: (0,) * len(shape))
    return pl.pallas_call(
        functools.partial(_sample_in_a_kernel, seq=s, n=n),
        grid=(1,),
        in_specs=[
            _const_spec((s, n, D_MODEL)),
            _const_spec((1, D_MODEL)),
            _const_spec(w_in_r.shape),
            _const_spec(w_kvt.shape),
            _const_spec((1, LANES)),
            _const_spec((A_KV, 1)),
            _const_spec((1, LANES)),
        ],
        out_specs=[
            full((n * A_HEADS * s, LANES)), full((rows, A_KV)), full((n * X_HEADS * s, LANES)),
            full((A_KV, rows)), full((A_KV, rows)),
        ],
        out_shape=[
            slots(A_HEADS),
            jax.ShapeDtypeStruct((rows, A_KV), F32),
            slots(X_HEADS),
            jax.ShapeDtypeStruct((A_KV, rows), F32),
            jax.ShapeDtypeStruct((A_KV, rows), F32),
        ],
        compiler_params=_params("arbitrary"),
        name="sample_in_a",
    )(x3, norm1_g.reshape(1, D_MODEL), w_in_r, w_kvt, jnp.tile(q_norm, 2).reshape(1, LANES),
      jnp.tile(k_norm, 4).reshape(A_KV, 1), jnp.tile(xq_norm, 2).reshape(1, LANES))


def _sample_in_b_kernel(x_ref, g1_ref, win_ref, vg_ref, coef_ref, gbias_ref, xqg_ref, o_ref, vn_ref, xs_ref,
                        *, seq, n):
    h = _rms_rows(_stack_positions(x_ref, seq), g1_ref[...]).astype(BF16)
    z = _dot(h, win_ref[...])
    uv = _gelu_exact(z[:, :2 * B_WIDTH])
    u = uv[:, :B_WIDTH]
    vn = _rms_rows(uv[:, B_WIDTH:], vg_ref[...])
    vn_ref[...] = vn
    for t in range(seq):
        mixed = gbias_ref[t:t + 1, :] + coef_ref[t * seq:t * seq + 1, :] * vn[0:n]
        for j in range(1, t + 1):
            mixed = mixed + coef_ref[t * seq + j:t * seq + j + 1, :] * vn[j * n:(j + 1) * n]
        o_ref[t * n:(t + 1) * n, :] = (u[t * n:(t + 1) * n] * mixed).astype(o_ref.dtype)
    _write_query_slots(_pair_head_norm(z[:, 2 * B_WIDTH:], xqg_ref[...]), X_SLOT_PAIRS, xs_ref, seq, n)


def _sample_in_b(x3, norm1_g, w_in_b, v_norm, w_s, b_s, xq_norm):
    s, n, _ = x3.shape
    rows = s * n
    coef = jnp.repeat(w_s[:, :s, :s].transpose(1, 2, 0).reshape(s * s, B_GROUPS), B_GROUP_DIM, axis=1)
    gbias = jnp.repeat(b_s[:, :s].T, B_GROUP_DIM, axis=1)
    full = lambda shape: pl.BlockSpec(shape, lambda i: (0,) * len(shape))
    return pl.pallas_call(
        functools.partial(_sample_in_b_kernel, seq=s, n=n),
        grid=(1,),
        in_specs=[
            _const_spec((s, n, D_MODEL)),
            _const_spec((1, D_MODEL)),
            _const_spec(w_in_b.shape),
            _const_spec((1, B_WIDTH)),
            _const_spec((s * s, B_WIDTH)),
            _const_spec((s, B_WIDTH)),
            _const_spec((1, LANES)),
        ],
        out_specs=[full((rows, B_WIDTH)), full((rows, B_WIDTH)), full((n * X_HEADS * s, LANES))],
        out_shape=[
            jax.ShapeDtypeStruct((rows, B_WIDTH), BF16),
            jax.ShapeDtypeStruct((rows, B_WIDTH), F32),
            jax.ShapeDtypeStruct((n * X_HEADS * s, LANES), F32),
        ],
        compiler_params=_params("arbitrary"),
        name="sample_in_b",
    )(x3, norm1_g.reshape(1, D_MODEL), w_in_b, v_norm.reshape(1, B_WIDTH), coef, gbias,
      jnp.tile(xq_norm, 2).reshape(1, LANES))


def _sample_attn_win_kernel(qs_ref, ktn_ref, vtn_ref, vn_ref, slope_ref, sink_ref, ck_ref, cv_ref,
                            o_ref, wk_ref, wv_ref, *, sb, seq):
    rows = A_HEADS * seq
    tq = _div_mod(lax.broadcasted_iota(jnp.int32, (rows, WINDOW), 0), seq)[1]
    lane = lax.broadcasted_iota(jnp.int32, (rows, WINDOW), 1)
    slope = slope_ref[...]
    sink = sink_ref[...]
    d1 = tq + WINDOW - lane
    valid1 = d1 < WINDOW
    bias1 = slope * d1.astype(F32)
    new_seq, new_pos = _div_mod(lane, seq)
    d2 = tq - new_pos
    causal2 = d2 >= 0
    bias2 = slope * d2.astype(F32)
    ktn = ktn_ref[...]
    vtn = vtn_ref[...]
    ktn_b = ktn.astype(BF16)
    vn_b = vn_ref[...].astype(BF16)
    keep_old = lax.broadcasted_iota(jnp.int32, (A_KV, WINDOW), 1) < WINDOW - seq

    def body(n, carry):
        q = _read_seq_slots(qs_ref, n, rows).astype(BF16)
        ckt = ck_ref[n]
        cvt = cv_ref[n]
        s1 = jnp.where(valid1, _dot(q, ckt.astype(BF16)) - bias1, NEG)
        s2 = jnp.where(causal2 & (new_seq == n), _dot(q, ktn_b) - bias2, NEG)
        m = jnp.maximum(jnp.maximum(jnp.max(s1, axis=-1, keepdims=True), jnp.max(s2, axis=-1, keepdims=True)), sink)
        e1 = jnp.exp(s1 - m)
        e2 = jnp.exp(s2 - m)
        den = jnp.sum(e1, axis=-1, keepdims=True) + jnp.sum(e2, axis=-1, keepdims=True) + jnp.exp(sink - m)
        o = _nt_dot(e1.astype(BF16), cvt.astype(BF16)) + _dot(e2.astype(BF16), vn_b)
        _write_seq_slots(o_ref, n, rows, o / den)
        shift_new = WINDOW - seq - n * seq
        wk_ref[n] = jnp.where(keep_old, pltpu.roll(ckt, WINDOW - seq, 1), pltpu.roll(ktn, shift_new, 1))
        wv_ref[n] = jnp.where(keep_old, pltpu.roll(cvt, WINDOW - seq, 1), pltpu.roll(vtn, shift_new, 1))
        return carry

    lax.fori_loop(0, sb, body, 0, unroll=SAMPLE_SEQ_UNROLL)


def _sample_attn_win(qs, kt_new, vt_new, v_new, slope_rows, sink_rows, ckt, cvt, layer, seq):
    rows = A_HEADS * seq
    n = qs.shape[0] // rows
    assert LANES % seq == 0
    sb = LANES // seq
    assert n % sb == 0

    def seqs(*tail):
        return pl.BlockSpec((sb,) + tail, lambda i: (i,) + (0,) * len(tail))

    slot_rows = pl.BlockSpec((sb * rows, LANES), lambda i: (i, 0))
    cache = pl.BlockSpec((None, sb, A_KV, WINDOW), lambda i: (layer, i, 0, 0))
    win = jax.ShapeDtypeStruct((n, A_KV, WINDOW), F32)
    return pl.pallas_call(
        functools.partial(_sample_attn_win_kernel, sb=sb, seq=seq),
        grid=(n // sb,),
        in_specs=[
            slot_rows,
            pl.BlockSpec((A_KV, LANES), lambda i: (0, i)),
            pl.BlockSpec((A_KV, LANES), lambda i: (0, i)),
            pl.BlockSpec((LANES, A_KV), lambda i: (i, 0)),
            _const_spec((rows, 1)),
            _const_spec((rows, 1)),
            cache,
            cache,
        ],
        out_specs=[slot_rows, seqs(A_KV, WINDOW), seqs(A_KV, WINDOW)],
        out_shape=[jax.ShapeDtypeStruct((n * rows, LANES), F32), win, win],
        compiler_params=_params("arbitrary"),
        name="sample_attn_win",
    )(qs, kt_new, vt_new, v_new, slope_rows, sink_rows, ckt, cvt)


def _sample_attn_mem_kernel(xs_ref, mk_ref, mv_ref, xo_ref, *, sb, xrows):
    def body(n, carry):
        q = _read_seq_slots(xs_ref, n, xrows).astype(BF16)
        s = _dot(q, mk_ref[n].astype(BF16))
        m = jnp.max(s, axis=-1, keepdims=True)
        e = jnp.exp(s - m)
        den = jnp.sum(e, axis=-1, keepdims=True)
        _write_seq_slots(xo_ref, n, xrows, _nt_dot(e.astype(BF16), mv_ref[n].astype(BF16)) / den)
        return carry

    lax.fori_loop(0, sb, body, 0, unroll=SAMPLE_SEQ_UNROLL)


def _sample_attn_mem(xs, mkt, mvt, layer, seq):
    xrows = X_HEADS * seq
    n = xs.shape[0] // xrows
    sb = min(SAMPLE_MEM_SEQS, n)
    assert n % sb == 0
    qspec = pl.BlockSpec((sb * xrows, LANES), lambda i: (i, 0))
    cache = pl.BlockSpec((None, sb, X_WIDTH, N_MEM), lambda i: (layer, i, 0, 0))
    return pl.pallas_call(
        functools.partial(_sample_attn_mem_kernel, sb=sb, xrows=xrows),
        grid=(n // sb,),
        in_specs=[qspec, cache, cache],
        out_specs=qspec,
        out_shape=jax.ShapeDtypeStruct((n * xrows, LANES), F32),
        compiler_params=_params("arbitrary"),
        name="sample_attn_mem",
    )(xs, mkt, mvt)


def _feature_major(cache):
    l, n, p, kv, d = cache.shape
    return cache.transpose(0, 1, 3, 4, 2).reshape(l, n, kv * d, p)


def kernel(x_prompt, x_sample, cache_win_k, cache_win_v, cache_mem_k, cache_mem_v, state_conv, mem_prompt,
           norm1_g, norm2_g, mem_norm_g, w_in_a, q_norm_a, k_norm_a, sinks_a, w_out_a, w_in_b, v_norm_b, w_s_b,
           b_s_b, w_out_b, w_mem_kv, xq_norm, xk_norm, w_ffn_in, conv_w, conv_b, w_down):
    depth = norm1_g.shape[0]
    assert depth == 2 and w_in_a.shape[0] == 1 and w_in_b.shape[0] == 1
    nb = x_prompt.shape[0]
    ns, seq, _ = x_sample.shape
    assert cache_win_k.shape[2] == WINDOW

    w_in_a_b = w_in_a[0].astype(BF16)
    w_in_b_b = w_in_b[0].astype(BF16)
    w_out_a_b = w_out_a[0].astype(BF16)
    w_out_b_b = w_out_b[0].astype(BF16)
    w_mem_kv_b = w_mem_kv.astype(BF16)
    w_ffn_in_b = w_ffn_in.astype(BF16)
    w_down_b = w_down.astype(BF16)

    mk_all, mv_all, mem_kt_p, mem_vt_p = _memory_kv(mem_prompt, mem_norm_g, w_mem_kv_b, xk_norm)
    w_in_a_r = _reorder_heads(w_in_a_b, 1)
    w_out_a_r = _reorder_heads(w_out_a_b, 0)
    x2, win_k_p, win_v_p, conv_p0 = _layer_a_prompt(
        x_prompt, 0, norm1_g[0], w_in_a_r, q_norm_a[0], k_norm_a[0], xq_norm[0], sinks_a[0], mk_all, mv_all, w_out_a_r,
        norm2_g[0], w_ffn_in_b, conv_w[0], conv_b[0], w_down_b)
    y_prompt, chunk_v_p, conv_p1 = _layer_b_prompt(
        x2, 1, norm1_g[1], w_in_b_b, v_norm_b[0], w_s_b[0], b_s_b[0], xq_norm[1], mk_all, mv_all, w_out_b_b,
        norm2_g[1], w_ffn_in_b, conv_w[1], conv_b[1], w_down_b)

    xs3 = x_sample.transpose(1, 0, 2)
    mkt = _feature_major(cache_mem_k)
    mvt = _feature_major(cache_mem_v)

    def by_seq_cols(a):
        return a.reshape(a.shape[0], seq, ns).transpose(0, 2, 1).reshape(a.shape[0], ns * seq)

    qs, v_new, xslots, kt_new, vt_new = _sample_in_a(xs3, norm1_g[0], w_in_a_r, q_norm_a[0], k_norm_a[0], xq_norm[0])
    slope_rows = jnp.repeat(jnp.asarray(SLOPES, F32), seq).reshape(A_HEADS * seq, 1)
    sink_rows = jnp.repeat(sinks_a[0].astype(F32), seq).reshape(A_HEADS * seq, 1)
    o_s, win_kt_s, win_vt_s = _sample_attn_win(
        qs, by_seq_cols(kt_new), by_seq_cols(vt_new),
        v_new.reshape(seq, ns, A_KV).transpose(1, 0, 2).reshape(ns * seq, A_KV), slope_rows, sink_rows,
        _feature_major(cache_win_k), _feature_major(cache_win_v), 0, seq)
    xo_s = _sample_attn_mem(xslots, mkt, mvt, 0, seq)
    ys1, conv_s0 = _ffn_sample(
        xs3, o_s, xo_s, w_out_a_r[:A_Q], w_out_a_r[A_Q:], 0, norm2_g[0], w_ffn_in_b, conv_w[0], conv_b[0], w_down_b,
        state_conv[0])

    o_b, vn_s, xslots = _sample_in_b(ys1, norm1_g[1], w_in_b_b, v_norm_b[0], w_s_b[0], b_s_b[0], xq_norm[1])
    xo_s = _sample_attn_mem(xslots, mkt, mvt, 1, seq)
    ys2, conv_s1 = _ffn_sample(
        ys1, o_b.reshape(seq, ns, B_WIDTH), xo_s, w_out_b_b[:B_WIDTH], w_out_b_b[B_WIDTH:], 1, norm2_g[1], w_ffn_in_b,
        conv_w[1], conv_b[1], w_down_b, state_conv[1])

    def position_major(ft):
        lead = ft.shape[:-2]
        nd = len(lead)
        return ft.reshape(*lead, A_KV_HEADS, HEAD_DIM, ft.shape[-1]).transpose(*range(nd), nd + 2, nd, nd + 1)

    return (
        y_prompt,
        ys2.transpose(1, 0, 2),
        position_major(win_k_p)[None],
        position_major(win_v_p)[None],
        chunk_v_p.reshape(1, nb, CHUNK, B_WIDTH),
        position_major(mem_kt_p),
        position_major(mem_vt_p),
        jnp.stack([conv_p0, conv_p1]),
        position_major(win_kt_s)[None],
        position_major(win_vt_s)[None],
        vn_s.reshape(seq, ns, B_WIDTH).transpose(1, 0, 2)[None],
        jnp.stack([conv_s0.transpose(1, 0, 2), conv_s1.transpose(1, 0, 2)]),
    )
```

```python
import functools
import math

import numpy as np
import jax
import jax.numpy as jnp
from jax import lax
from jax.experimental import pallas as pl
from jax.experimental.pallas import tpu as pltpu

F32 = jnp.float32
BF16 = jnp.bfloat16

D_MODEL = 1024
HEAD_DIM = 64
A_HEADS = 12
A_KV_HEADS = 4
A_GROUP = A_HEADS // A_KV_HEADS
WINDOW = 128
BLOCK = 128
B_WIDTH = 768
B_GROUPS = 4
B_GROUP_DIM = B_WIDTH // B_GROUPS
CHUNK = 128
X_HEADS = 4
X_WIDTH = X_HEADS * HEAD_DIM
N_MEM = 256
D_FF = 2816
CONV_W = 3
A_Q = A_HEADS * HEAD_DIM
A_KV = A_KV_HEADS * HEAD_DIM
EPS = 1e-6
NEG = -1e30
SCALE = HEAD_DIM ** -0.5
LOG2E = float(np.float32(1.0 / np.log(2.0)))
SQRT_HALF = float(np.sqrt(0.5).astype(np.float32))

LANES = 128
VMEM_LIMIT_BYTES = 56 * 1024 * 1024
GATE_WINDOW_START = (0, 128, 384, 512)
assert B_GROUP_DIM == 192 and B_GROUPS == 4
PROMPT_ROWS = 512
SAMPLE_MEM_SEQS = 16
SAMPLE_FFN_SEQS = 128
SAMPLE_SEQ_UNROLL = 4


def _alibi_slopes(n):
    def pow2(m):
        start = 2.0 ** (-8.0 / m)
        return [start ** (i + 1) for i in range(m)]

    p = 2 ** int(math.floor(math.log2(n)))
    s = pow2(p)
    if p < n:
        s = s + pow2(2 * p)[0::2][: n - p]
    return [float(np.float32(v)) for v in s]


SLOPES = _alibi_slopes(A_HEADS)

Q_LANE_ORDER = (0, 3, 1, 4, 2, 5, 6, 9, 7, 10, 8, 11)
assert all(Q_LANE_ORDER[p] // A_GROUP == 2 * (p // (2 * A_GROUP)) + p % 2 for p in range(A_HEADS))


_SLOTS = A_KV_HEADS // 2
assert Q_LANE_ORDER == tuple(np.arange(A_HEADS).reshape(_SLOTS, 2, A_GROUP).swapaxes(1, 2).reshape(-1).tolist())


def _reorder_heads(w, axis):
    lead, tail = w.shape[:axis], w.shape[axis + 1:]
    q = lax.slice_in_dim(w, 0, A_Q, axis=axis).reshape(*lead, _SLOTS, 2, A_GROUP, HEAD_DIM, *tail)
    q = jnp.swapaxes(q, axis + 1, axis + 2).reshape(*lead, A_Q, *tail)
    return jnp.concatenate([q, lax.slice_in_dim(w, A_Q, w.shape[axis], axis=axis)], axis=axis)


def _params(*sem):
    return pltpu.CompilerParams(dimension_semantics=sem, vmem_limit_bytes=VMEM_LIMIT_BYTES)


def _const_spec(shape):
    nd = len(shape)
    return pl.BlockSpec(shape, lambda *_: (0,) * nd, pipeline_mode=pl.Buffered(1))


def _layer_spec(layer, shape):
    nd = len(shape)
    return pl.BlockSpec((None,) + tuple(shape), lambda *_: (layer,) + (0,) * nd, pipeline_mode=pl.Buffered(1))


def _rms_rows(x, g):
    ms = jnp.mean(x * x, axis=-1, keepdims=True)
    return x * lax.rsqrt(ms + EPS) * g


def _pair_head_norm(z, g2):
    t, w = z.shape
    lo = lax.broadcasted_iota(jnp.int32, (t, LANES), 1) < HEAD_DIM
    out = []
    for p in range(w // LANES):
        zz = z[:, p * LANES:(p + 1) * LANES]
        sq = zz * zz
        s_lo = jnp.sum(jnp.where(lo, sq, 0.0), axis=-1, keepdims=True)
        s_hi = jnp.sum(jnp.where(lo, 0.0, sq), axis=-1, keepdims=True)
        r_lo = lax.rsqrt(s_lo * (1.0 / HEAD_DIM) + EPS)
        r_hi = lax.rsqrt(s_hi * (1.0 / HEAD_DIM) + EPS)
        out.append(zz * jnp.where(lo, r_lo, r_hi) * g2)
    return out


def _nt_dot(a, b):
    return lax.dot_general(a, b, (((1,), (1,)), ((), ())), preferred_element_type=F32)


def _dot(a, b):
    return jnp.dot(a, b, preferred_element_type=F32)


def _div_mod(i, n):
    if n & (n - 1) == 0:
        return i >> (n.bit_length() - 1), i & (n - 1)
    return i // n, i % n


def _gelu_exact(x):
    return 0.5 * x * (1.0 + lax.erf(x * SQRT_HALF))


def _conv_gate(a, am1, am2, up, cw_ref, cb_ref):
    c = cb_ref[...] + ((cw_ref[0:1, :] * am2 + cw_ref[1:2, :] * am1) + cw_ref[2:3, :] * a)
    return ((c * (1.0 / (1.0 + jnp.exp(-c)))) * up).astype(BF16)


def _memkv_kernel(mem_ref, g_ref, w_ref, kg_ref, mkb_ref, mvb_ref, mkt_ref, mvt_ref, *, depth):
    m = mem_ref[0]
    for layer in range(depth):
        h = _rms_rows(m, g_ref[layer]).astype(BF16)
        kv = _dot(h, w_ref[layer])
        mk = jnp.concatenate(_pair_head_norm(kv[:, :X_WIDTH], kg_ref[layer]), axis=1)
        mv = kv[:, X_WIDTH:]
        mkb_ref[layer, 0] = mk.astype(BF16)
        mvb_ref[layer, 0] = mv.astype(BF16)
        mkt_ref[layer, 0] = mk.T
        mvt_ref[layer, 0] = mv.T


def _memory_kv(mem, mem_norm_g, w_mem_kv_b, xk_norm):
    depth = w_mem_kv_b.shape[0]
    n = mem.shape[0]
    g = mem_norm_g.reshape(depth, 1, D_MODEL)
    kg2 = jnp.tile(xk_norm, (1, 2)).reshape(depth, 1, LANES)
    spec = lambda rows, cols: pl.BlockSpec((depth, 1, rows, cols), lambda b: (0, b, 0, 0))
    return pl.pallas_call(
        functools.partial(_memkv_kernel, depth=depth),
        grid=(n,),
        in_specs=[
            pl.BlockSpec((1, N_MEM, D_MODEL), lambda b: (b, 0, 0)),
            _const_spec((depth, 1, D_MODEL)),
            _const_spec((depth, D_MODEL, 2 * X_WIDTH)),
            _const_spec((depth, 1, LANES)),
        ],
        out_specs=[spec(N_MEM, X_WIDTH), spec(N_MEM, X_WIDTH), spec(X_WIDTH, N_MEM), spec(X_WIDTH, N_MEM)],
        out_shape=[
            jax.ShapeDtypeStruct((depth, n, N_MEM, X_WIDTH), BF16),
            jax.ShapeDtypeStruct((depth, n, N_MEM, X_WIDTH), BF16),
            jax.ShapeDtypeStruct((depth, n, X_WIDTH, N_MEM), F32),
            jax.ShapeDtypeStruct((depth, n, X_WIDTH, N_MEM), F32),
        ],
        compiler_params=_params("arbitrary"),
        name="memory_kv",
    )(mem, g, w_mem_kv_b, kg2)


def _half_masked_rows(pair, lo):
    return jnp.concatenate([jnp.where(lo, pair, 0.0), jnp.where(lo, 0.0, pair)], axis=0).astype(BF16)


def _prompt_mem_attention(xq_pairs, mk_ref, mv_ref, ocat, col0):
    t = xq_pairs[0].shape[0]
    lo = lax.broadcasted_iota(jnp.int32, (t, LANES), 1) < HEAD_DIM
    for j, pair in enumerate(xq_pairs):
        mk = mk_ref[0, 0, :, j * LANES:(j + 1) * LANES]
        mv = mv_ref[0, 0, :, j * LANES:(j + 1) * LANES]
        s = _nt_dot(_half_masked_rows(pair * (SCALE * LOG2E), lo), mk)
        m = jnp.max(s, axis=-1, keepdims=True)
        e = jnp.exp2(s - m)
        den = jnp.sum(e, axis=-1, keepdims=True)
        o = _dot(e.astype(BF16), mv) / den
        ocat[:, col0 + j * LANES:col0 + (j + 1) * LANES] = jnp.where(lo, o[:t], o[t:]).astype(BF16)


def _mixer_a_body(t, x, sinks_ref, g1_ref, win_ref, qg_ref, kg_ref, xqg_ref, mk_ref, mv_ref, wout_ref,
                  wk_ref, wv_ref, kbuf, vbuf, ocat, tb):
    h = _rms_rows(x, g1_ref[...]).astype(BF16)
    z = _dot(h, win_ref[...])
    q_pairs = _pair_head_norm(z[:, :A_Q], qg_ref[...])
    k_pairs = _pair_head_norm(z[:, A_Q:A_Q + A_KV], kg_ref[...])
    v = z[:, A_Q + A_KV:A_Q + 2 * A_KV]
    xq_pairs = _pair_head_norm(z[:, A_Q + 2 * A_KV:], xqg_ref[...])

    for p, kp in enumerate(k_pairs):
        wk_ref[0, p * LANES:(p + 1) * LANES, :] = kp[tb - BLOCK:, :].T
        kbuf[BLOCK:BLOCK + tb, p * LANES:(p + 1) * LANES] = kp.astype(BF16)
    wv_ref[0] = v[tb - BLOCK:, :].T
    vbuf[BLOCK:BLOCK + tb, :] = v.astype(BF16)

    qi = lax.broadcasted_iota(jnp.int32, (BLOCK, BLOCK), 0)
    kj = lax.broadcasted_iota(jnp.int32, (BLOCK, BLOCK), 1)
    from_prev = kj > qi
    key_pos = jnp.where(from_prev, kj, kj + BLOCK).astype(F32)
    query_pos = (lax.broadcasted_iota(jnp.int32, (BLOCK, 1), 0) + BLOCK).astype(F32)

    lo = lax.broadcasted_iota(jnp.int32, (BLOCK, LANES), 1) < HEAD_DIM
    pairs_per_slot = A_HEADS // A_KV_HEADS
    for s in range(tb // BLOCK):
        rows = slice(s * BLOCK, (s + 1) * BLOCK)
        prev_keys = (jnp.where(t > 0, BLOCK, 0) if s == 0 else BLOCK)
        valid = jnp.logical_not(from_prev) | (kj < prev_keys)
        for j in range(A_KV // LANES):
            kw = kbuf[s * BLOCK:(s + 2) * BLOCK, j * LANES:(j + 1) * LANES]
            vw = vbuf[s * BLOCK:(s + 2) * BLOCK, j * LANES:(j + 1) * LANES]
            q_rows = jnp.concatenate(
                [_half_masked_rows(q_pairs[pairs_per_slot * j + c][rows] * (SCALE * LOG2E), lo)
                 for c in range(pairs_per_slot)], axis=0)
            sc_all = _nt_dot(q_rows, kw)
            e_parts, den_parts = [], []
            for r in range(2 * pairs_per_slot):
                hq = Q_LANE_ORDER[2 * pairs_per_slot * j + r]
                slope2 = SLOPES[hq] * LOG2E
                both = sc_all[r * BLOCK:(r + 1) * BLOCK]
                sc = jnp.where(from_prev, both[:, :BLOCK], both[:, BLOCK:]) + slope2 * key_pos
                sc = jnp.where(valid, sc, NEG)
                sink = sinks_ref[hq] * LOG2E + slope2 * query_pos
                m = jnp.maximum(jnp.max(sc, axis=-1, keepdims=True), sink)
                e = jnp.exp2(sc - m)
                den_parts.append(jnp.sum(e, axis=-1, keepdims=True) + jnp.exp2(sink - m))
                e_parts.append(jnp.concatenate([jnp.where(from_prev, e, 0.0), jnp.where(from_prev, 0.0, e)],
                                               axis=1).astype(BF16))
            o_all = _dot(jnp.concatenate(e_parts, axis=0), vw)
            for c in range(pairs_per_slot):
                o_lo = o_all[2 * c * BLOCK:(2 * c + 1) * BLOCK] / den_parts[2 * c]
                o_hi = o_all[(2 * c + 1) * BLOCK:(2 * c + 2) * BLOCK] / den_parts[2 * c + 1]
                col = pairs_per_slot * j + c
                ocat[rows, col * LANES:(col + 1) * LANES] = jnp.where(lo, o_lo, o_hi).astype(BF16)

    _prompt_mem_attention(xq_pairs, mk_ref, mv_ref, ocat, A_Q)
    x1 = x + _dot(ocat[...], wout_ref[...])

    kbuf[0:BLOCK, :] = kbuf[tb:tb + BLOCK, :]
    vbuf[0:BLOCK, :] = vbuf[tb:tb + BLOCK, :]
    return x1


def _ffn_body(x, g2_ref, wfi_ref, cw_ref, cb_ref, wd_ref, tail_ref, tb):
    h = _rms_rows(x, g2_ref[...]).astype(BF16)
    gu = _dot(h, wfi_ref[...])
    a = gu[:, :D_FF]
    row = lax.broadcasted_iota(jnp.int32, (tb, 1), 0)
    prev_last = tail_ref[0, 7:8, :]
    am1 = jnp.where(row == 0, prev_last, pltpu.roll(a, 1, 0))
    am2 = jnp.where(row == 0, tail_ref[0, 6:7, :], jnp.where(row == 1, prev_last, pltpu.roll(a, 2, 0)))
    tail_ref[0] = a[tb - 8:, :]
    return x + _dot(_conv_gate(a, am1, am2, gu[:, D_FF:], cw_ref, cb_ref), wd_ref[...])


def _layer_a_kernel(sinks_ref, x_ref, g1_ref, win_ref, qg_ref, kg_ref, xqg_ref, mk_ref, mv_ref, wout_ref,
                    g2_ref, wfi_ref, cw_ref, cb_ref, wd_ref, y_ref, wk_ref, wv_ref, tail_ref, kbuf, vbuf, ocat, *, tb):
    t = pl.program_id(1)

    @pl.when(t == 0)
    def _():
        kbuf[0:BLOCK, :] = jnp.zeros((BLOCK, A_KV), BF16)
        vbuf[0:BLOCK, :] = jnp.zeros((BLOCK, A_KV), BF16)
        tail_ref[0] = jnp.zeros((8, D_FF), F32)

    x1 = _mixer_a_body(t, x_ref[0], sinks_ref, g1_ref, win_ref, qg_ref, kg_ref, xqg_ref, mk_ref, mv_ref, wout_ref,
                       wk_ref, wv_ref, kbuf, vbuf, ocat, tb)
    y_ref[0] = _ffn_body(x1, g2_ref, wfi_ref, cw_ref, cb_ref, wd_ref, tail_ref, tb)


def _layer_specs(tb, layer):
    xspec = pl.BlockSpec((1, tb, D_MODEL), lambda b, i: (b, i, 0))
    mspec = pl.BlockSpec((1, 1, N_MEM, X_WIDTH), lambda b, i: (layer, b, 0, 0))
    per_seq = lambda rows, width: pl.BlockSpec((1, rows, width), lambda b, i: (b, 0, 0))
    return xspec, mspec, per_seq(8, D_FF), per_seq


def _ffn_specs(layer):
    return [
        _const_spec((1, D_MODEL)),
        _layer_spec(layer, (D_MODEL, 2 * D_FF)),
        _const_spec((CONV_W, D_FF)),
        _const_spec((1, D_FF)),
        _layer_spec(layer, (D_FF, D_MODEL)),
    ]


def _layer_a_prompt(x, layer, norm1_g, w_in_b, q_norm, k_norm, xq_norm, sinks, mk_all, mv_all, w_out_b,
                    norm2_g, w_ffn_in_b, conv_w, conv_b, w_down_b):
    n, t, _ = x.shape
    tb = min(PROMPT_ROWS, t)
    a_in = w_in_b.shape[1]
    xspec, mspec, tailspec, per_seq = _layer_specs(tb, layer)
    wspec = per_seq(A_KV, BLOCK)
    y, wk, wv, tail = pl.pallas_call(
        functools.partial(_layer_a_kernel, tb=tb),
        grid=(n, t // tb),
        in_specs=[
            pl.BlockSpec(memory_space=pltpu.SMEM),
            xspec,
            _const_spec((1, D_MODEL)),
            _const_spec((D_MODEL, a_in)),
            _const_spec((1, LANES)),
            _const_spec((1, LANES)),
            _const_spec((1, LANES)),
            mspec,
            mspec,
            _const_spec((A_Q + X_WIDTH, D_MODEL)),
        ] + _ffn_specs(layer),
        out_specs=[xspec, wspec, wspec, tailspec],
        out_shape=[
            jax.ShapeDtypeStruct((n, t, D_MODEL), F32),
            jax.ShapeDtypeStruct((n, A_KV, BLOCK), F32),
            jax.ShapeDtypeStruct((n, A_KV, BLOCK), F32),
            jax.ShapeDtypeStruct((n, 8, D_FF), F32),
        ],
        scratch_shapes=[
            pltpu.VMEM((tb + BLOCK, A_KV), BF16),
            pltpu.VMEM((tb + BLOCK, A_KV), BF16),
            pltpu.VMEM((tb, A_Q + X_WIDTH), BF16),
        ],
        compiler_params=_params("arbitrary", "arbitrary"),
        name="layer_a_prompt",
    )(sinks, x, norm1_g.reshape(1, D_MODEL), w_in_b, jnp.tile(q_norm, 2).reshape(1, LANES),
      jnp.tile(k_norm, 2).reshape(1, LANES), jnp.tile(xq_norm, 2).reshape(1, LANES), mk_all, mv_all, w_out_b,
      norm2_g.reshape(1, D_MODEL), w_ffn_in_b, conv_w, conv_b.reshape(1, D_FF), w_down_b)
    return y, wk, wv, tail[:, 8 - (CONV_W - 1):, :]


def _mixer_b_body(x, g1_ref, win_ref, vg_ref, ws_ref, bias_ref, xqg_ref, mk_ref, mv_ref, wout_ref, cv_ref, ocat, tb):
    h = _rms_rows(x, g1_ref[...]).astype(BF16)
    z = _dot(h, win_ref[...])
    uv = _gelu_exact(z[:, :2 * B_WIDTH])
    u = uv[:, :B_WIDTH]
    vn = _rms_rows(uv[:, B_WIDTH:], vg_ref[...])
    cv_ref[0] = vn[tb - CHUNK:, :]
    xq_pairs = _pair_head_norm(z[:, 2 * B_WIDTH:], xqg_ref[...])

    wi = lax.broadcasted_iota(jnp.int32, (CHUNK, CHUNK), 0)
    wj = lax.broadcasted_iota(jnp.int32, (CHUNK, CHUNK), 1)
    causal = wi >= wj
    w_tril = [jnp.where(causal, ws_ref[g], 0.0).astype(BF16) for g in range(B_GROUPS)]
    lo = lax.broadcasted_iota(jnp.int32, (CHUNK, LANES), 1) < HEAD_DIM
    bias = bias_ref[...]
    for c in range(tb // CHUNK):
        vc = vn[c * CHUNK:(c + 1) * CHUNK, :].astype(BF16)
        r = [_dot(w_tril[g], vc[:, c0:c0 + 256]) for g, c0 in enumerate(GATE_WINDOW_START)]
        mixed = jnp.concatenate(
            [r[0][:, :128], jnp.where(lo, r[0][:, 128:], r[1][:, :128]), r[1][:, 128:],
             r[2][:, :128], jnp.where(lo, r[2][:, 128:], r[3][:, :128]), r[3][:, 128:]], axis=1)
        o = u[c * CHUNK:(c + 1) * CHUNK, :] * (mixed + bias)
        ocat[c * CHUNK:(c + 1) * CHUNK, 0:B_WIDTH] = o.astype(BF16)

    _prompt_mem_attention(xq_pairs, mk_ref, mv_ref, ocat, B_WIDTH)
    return x + _dot(ocat[...], wout_ref[...])


def _layer_b_kernel(x_ref, g1_ref, win_ref, vg_ref, ws_ref, bias_ref, xqg_ref, mk_ref, mv_ref, wout_ref,
                    g2_ref, wfi_ref, cw_ref, cb_ref, wd_ref, y_ref, cv_ref, tail_ref, ocat, *, tb):
    @pl.when(pl.program_id(1) == 0)
    def _():
        tail_ref[0] = jnp.zeros((8, D_FF), F32)

    x1 = _mixer_b_body(x_ref[0], g1_ref, win_ref, vg_ref, ws_ref, bias_ref, xqg_ref, mk_ref, mv_ref, wout_ref,
                       cv_ref, ocat, tb)
    y_ref[0] = _ffn_body(x1, g2_ref, wfi_ref, cw_ref, cb_ref, wd_ref, tail_ref, tb)


def _gate_bias_rows(b_s):
    return jnp.repeat(b_s.T, B_GROUP_DIM, axis=1)


def _layer_b_prompt(x, layer, norm1_g, w_in_b, v_norm, w_s, b_s, xq_norm, mk_all, mv_all, w_out_b,
                    norm2_g, w_ffn_in_b, conv_w, conv_b, w_down_b):
    n, t, _ = x.shape
    tb = min(PROMPT_ROWS, t)
    b_in = w_in_b.shape[1]
    xspec, mspec, tailspec, per_seq = _layer_specs(tb, layer)
    y, cv, tail = pl.pallas_call(
        functools.partial(_layer_b_kernel, tb=tb),
        grid=(n, t // tb),
        in_specs=[
            xspec,
            _const_spec((1, D_MODEL)),
            _const_spec((D_MODEL, b_in)),
            _const_spec((1, B_WIDTH)),
            _const_spec((B_GROUPS, CHUNK, CHUNK)),
            _const_spec((CHUNK, B_WIDTH)),
            _const_spec((1, LANES)),
            mspec,
            mspec,
            _const_spec((B_WIDTH + X_WIDTH, D_MODEL)),
        ] + _ffn_specs(layer),
        out_specs=[xspec, per_seq(CHUNK, B_WIDTH), tailspec],
        out_shape=[
            jax.ShapeDtypeStruct((n, t, D_MODEL), F32),
            jax.ShapeDtypeStruct((n, CHUNK, B_WIDTH), F32),
            jax.ShapeDtypeStruct((n, 8, D_FF), F32),
        ],
        scratch_shapes=[pltpu.VMEM((tb, B_WIDTH + X_WIDTH), BF16)],
        compiler_params=_params("arbitrary", "arbitrary"),
        name="layer_b_prompt",
    )(x, norm1_g.reshape(1, D_MODEL), w_in_b, v_norm.reshape(1, B_WIDTH), w_s, _gate_bias_rows(b_s),
      jnp.tile(xq_norm, 2).reshape(1, LANES), mk_all, mv_all, w_out_b,
      norm2_g.reshape(1, D_MODEL), w_ffn_in_b, conv_w, conv_b.reshape(1, D_FF), w_down_b)
    return y, cv, tail[:, 8 - (CONV_W - 1):, :]


def _stack_positions(ref, seq):
    return jnp.concatenate([ref[t] for t in range(seq)], axis=0)


def _ffn_sample_kernel(x_ref, om_ref, ox_ref, wo_ref, g2_ref, wfi_ref, cw_ref, cb_ref, wd_ref, p0_ref, p1_ref,
                       y_ref, atail_ref, *, seq, nb, mixer_slots):
    om = _compact_slots(om_ref, Q_SLOT_PAIRS, seq, nb) if mixer_slots else _stack_positions(om_ref, seq)
    ox = _compact_slots(ox_ref, X_SLOT_PAIRS, seq, nb)
    km = om.shape[1]
    x1 = _stack_positions(x_ref, seq) + _dot(om, wo_ref[:km, :]) + _dot(ox, wo_ref[km:, :])
    h = _rms_rows(x1, g2_ref[...]).astype(BF16)
    gu = _dot(h, wfi_ref[...])
    a = gu[:, :D_FF]
    blocks = [p0_ref[...], p1_ref[...]] + [a[t * nb:(t + 1) * nb] for t in range(seq)]
    am2 = jnp.concatenate(blocks[:seq], axis=0)
    am1 = jnp.concatenate(blocks[1:seq + 1], axis=0)
    y = x1 + _dot(_conv_gate(a, am1, am2, gu[:, D_FF:], cw_ref, cb_ref), wd_ref[...])
    for t in range(seq):
        y_ref[t] = y[t * nb:(t + 1) * nb]
    for r in range(CONV_W - 1):
        atail_ref[r] = blocks[seq + r]


def _ffn_sample(x3, om3, ox3, w_out, layer, norm2_g, w_ffn_in_b, conv_w, conv_b, w_down_b, prefix):
    s, n, _ = x3.shape
    nb = min(SAMPLE_FFN_SEQS, n)
    assert n % nb == 0 and s >= CONV_W - 1

    def blk(lead, width):
        return pl.BlockSpec((lead, nb, width), lambda i: (0, i, 0))

    def slots(heads):
        return pl.BlockSpec((nb * heads * s, LANES), lambda i: (i, 0))

    prefix2 = prefix.reshape(n, (CONV_W - 1) * D_FF)
    pspecs = [pl.BlockSpec((nb, D_FF), lambda i, r=r: (i, r)) for r in range(CONV_W - 1)]
    mixer_slots = om3.ndim == 2
    assert ox3.shape == (n * X_HEADS * s, LANES)
    assert w_out.shape[0] == (A_Q if mixer_slots else om3.shape[2]) + X_WIDTH
    return pl.pallas_call(
        functools.partial(_ffn_sample_kernel, seq=s, nb=nb, mixer_slots=mixer_slots),
        grid=(n // nb,),
        in_specs=[
            blk(s, D_MODEL),
            slots(A_HEADS) if mixer_slots else blk(s, om3.shape[2]),
            slots(X_HEADS),
            _const_spec(w_out.shape),
            _const_spec((1, D_MODEL)),
            _layer_spec(layer, (D_MODEL, 2 * D_FF)),
            _const_spec((CONV_W, D_FF)),
            _const_spec((1, D_FF)),
            _layer_spec(layer, (D_FF, D_MODEL)),
            *pspecs,
        ],
        out_specs=[blk(s, D_MODEL), blk(CONV_W - 1, D_FF)],
        out_shape=[
            jax.ShapeDtypeStruct((s, n, D_MODEL), F32),
            jax.ShapeDtypeStruct((CONV_W - 1, n, D_FF), F32),
        ],
        compiler_params=_params("arbitrary"),
        name="ffn_sample",
    )(x3, om3, ox3, w_out, norm2_g.reshape(1, D_MODEL), w_ffn_in_b, conv_w, conv_b.reshape(1, D_FF), w_down_b,
      prefix2, prefix2)


Q_SLOT_PAIRS = tuple((Q_LANE_ORDER[2 * c], Q_LANE_ORDER[2 * c + 1], c // A_GROUP) for c in range(A_HEADS // 2))
X_SLOT_PAIRS = tuple((2 * c, 2 * c + 1, c) for c in range(X_HEADS // 2))
for _slot_pairs in (Q_SLOT_PAIRS, X_SLOT_PAIRS):
    assert all(col == (head >= len(_slot_pairs)) for h_lo, h_hi, col in _slot_pairs for head in (h_lo, h_hi))


def _write_query_slots(pairs, slot_pairs, out_ref, seq, n):
    per_seq = 2 * len(slot_pairs) * seq
    lo = lax.broadcasted_iota(jnp.int32, (seq * n, LANES), 1) < HEAD_DIM
    for pair, (h_lo, h_hi, _) in zip(pairs, slot_pairs):
        scaled = pair * SCALE
        for head, own in ((h_lo, jnp.where(lo, scaled, 0.0)), (h_hi, jnp.where(lo, 0.0, scaled))):
            for t in range(seq):
                out_ref[pl.ds(head * seq + t, n, stride=per_seq), :] = own[t * n:(t + 1) * n]


def _upper_half_rows(per_seq):
    return lax.broadcasted_iota(jnp.int32, (per_seq, LANES), 0) >= per_seq // 2


def _read_seq_slots(ref, n, per_seq):
    q = ref[pl.ds(pl.multiple_of(n * per_seq, 8), per_seq), :]
    upper = _upper_half_rows(per_seq)
    return jnp.concatenate([jnp.where(upper, 0.0, q), jnp.where(upper, q, 0.0)], axis=1)


def _write_seq_slots(ref, n, per_seq, val):
    ref[pl.ds(pl.multiple_of(n * per_seq, 8), per_seq), :] = jnp.where(
        _upper_half_rows(per_seq), val[:, LANES:], val[:, :LANES])


def _compact_slots(o_ref, slot_pairs, seq, n):
    per_seq = 2 * len(slot_pairs) * seq
    lo = lax.broadcasted_iota(jnp.int32, (n, LANES), 1) < HEAD_DIM
    cols = []
    for h_lo, h_hi, _ in slot_pairs:
        blocks = [jnp.where(lo, o_ref[pl.ds(h_lo * seq + t, n, stride=per_seq), :],
                            o_ref[pl.ds(h_hi * seq + t, n, stride=per_seq), :]) for t in range(seq)]
        cols.append(jnp.concatenate(blocks, axis=0))
    return jnp.concatenate(cols, axis=1).astype(BF16)


def _sample_in_a_kernel(x_ref, g1_ref, win_ref, wkvt_ref, qg_ref, kgc_ref, xqg_ref,
                        qs_ref, v_ref, xs_ref, kt_ref, vt_ref, *, seq, n):
    h = _rms_rows(_stack_positions(x_ref, seq), g1_ref[...]).astype(BF16)
    z = _dot(h, win_ref[...])
    _write_query_slots(_pair_head_norm(z[:, :A_Q], qg_ref[...]), Q_SLOT_PAIRS, qs_ref, seq, n)
    v_ref[...] = z[:, A_Q + A_KV:A_Q + 2 * A_KV]
    _write_query_slots(_pair_head_norm(z[:, A_Q + 2 * A_KV:], xqg_ref[...]), X_SLOT_PAIRS, xs_ref, seq, n)
    kvt = _nt_dot(wkvt_ref[...], h)
    for hk in range(A_KV_HEADS):
        sl = slice(hk * HEAD_DIM, (hk + 1) * HEAD_DIM)
        kh = kvt[sl, :]
        ms = jnp.mean(kh * kh, axis=0, keepdims=True)
        kt_ref[sl, :] = kh * lax.rsqrt(ms + EPS) * kgc_ref[sl, :]
    vt_ref[...] = kvt[A_KV:, :]


def _sample_in_a(x3, norm1_g, w_in_r, q_norm, k_norm, xq_norm):
    s, n, _ = x3.shape
    rows = s * n
    w_kvt = w_in_r[:, A_Q:A_Q + 2 * A_KV].T
    slots = lambda heads: jax.ShapeDtypeStruct((n * heads * s, LANES), F32)
    full = lambda shape: pl.BlockSpec(shape, lambda i: (0,) * len(shape))
    return pl.pallas_call(
        functools.partial(_sample_in_a_kernel, seq=s, n=n),
        grid=(1,),
        in_specs=[
            _const_spec((s, n, D_MODEL)),
            _const_spec((1, D_MODEL)),
            _const_spec(w_in_r.shape),
            _const_spec(w_kvt.shape),
            _const_spec((1, LANES)),
            _const_spec((A_KV, 1)),
            _const_spec((1, LANES)),
        ],
        out_specs=[
            full((n * A_HEADS * s, LANES)), full((rows, A_KV)), full((n * X_HEADS * s, LANES)),
            full((A_KV, rows)), full((A_KV, rows)),
        ],
        out_shape=[
            slots(A_HEADS),
            jax.ShapeDtypeStruct((rows, A_KV), F32),
            slots(X_HEADS),
            jax.ShapeDtypeStruct((A_KV, rows), F32),
            jax.ShapeDtypeStruct((A_KV, rows), F32),
        ],
        compiler_params=_params("arbitrary"),
        name="sample_in_a",
    )(x3, norm1_g.reshape(1, D_MODEL), w_in_r, w_kvt, jnp.tile(q_norm, 2).reshape(1, LANES),
      jnp.tile(k_norm, 4).reshape(A_KV, 1), jnp.tile(xq_norm, 2).reshape(1, LANES))


def _sample_in_b_kernel(x_ref, g1_ref, win_ref, vg_ref, coef_ref, gbias_ref, xqg_ref, o_ref, vn_ref, xs_ref,
                        *, seq, n):
    h = _rms_rows(_stack_positions(x_ref, seq), g1_ref[...]).astype(BF16)
    z = _dot(h, win_ref[...])
    uv = _gelu_exact(z[:, :2 * B_WIDTH])
    u = uv[:, :B_WIDTH]
    vn = _rms_rows(uv[:, B_WIDTH:], vg_ref[...])
    vn_ref[...] = vn
    for t in range(seq):
        mixed = gbias_ref[t:t + 1, :] + coef_ref[t * seq:t * seq + 1, :] * vn[0:n]
        for j in range(1, t + 1):
            mixed = mixed + coef_ref[t * seq + j:t * seq + j + 1, :] * vn[j * n:(j + 1) * n]
        o_ref[t * n:(t + 1) * n, :] = (u[t * n:(t + 1) * n] * mixed).astype(o_ref.dtype)
    _write_query_slots(_pair_head_norm(z[:, 2 * B_WIDTH:], xqg_ref[...]), X_SLOT_PAIRS, xs_ref, seq, n)


def _sample_in_b(x3, norm1_g, w_in_b, v_norm, w_s, b_s, xq_norm):
    s, n, _ = x3.shape
    rows = s * n
    coef = jnp.repeat(w_s[:, :s, :s].transpose(1, 2, 0).reshape(s * s, B_GROUPS), B_GROUP_DIM, axis=1)
    gbias = jnp.repeat(b_s[:, :s].T, B_GROUP_DIM, axis=1)
    full = lambda shape: pl.BlockSpec(shape, lambda i: (0,) * len(shape))
    return pl.pallas_call(
        functools.partial(_sample_in_b_kernel, seq=s, n=n),
        grid=(1,),
        in_specs=[
            _const_spec((s, n, D_MODEL)),
            _const_spec((1, D_MODEL)),
            _const_spec(w_in_b.shape),
            _const_spec((1, B_WIDTH)),
            _const_spec((s * s, B_WIDTH)),
            _const_spec((s, B_WIDTH)),
            _const_spec((1, LANES)),
        ],
        out_specs=[full((rows, B_WIDTH)), full((rows, B_WIDTH)), full((n * X_HEADS * s, LANES))],
        out_shape=[
            jax.ShapeDtypeStruct((rows, B_WIDTH), BF16),
            jax.ShapeDtypeStruct((rows, B_WIDTH), F32),
            jax.ShapeDtypeStruct((n * X_HEADS * s, LANES), F32),
        ],
        compiler_params=_params("arbitrary"),
        name="sample_in_b",
    )(x3, norm1_g.reshape(1, D_MODEL), w_in_b, v_norm.reshape(1, B_WIDTH), coef, gbias,
      jnp.tile(xq_norm, 2).reshape(1, LANES))


def _sample_attn_win_kernel(qs_ref, ktn_ref, vtn_ref, vn_ref, slope_ref, sink_ref, ck_ref, cv_ref,
                            o_ref, wk_ref, wv_ref, *, sb, seq):
    rows = A_HEADS * seq
    tq = _div_mod(lax.broadcasted_iota(jnp.int32, (rows, WINDOW), 0), seq)[1]
    lane = lax.broadcasted_iota(jnp.int32, (rows, WINDOW), 1)
    slope = slope_ref[...]
    sink = sink_ref[...]
    d1 = tq + WINDOW - lane
    valid1 = d1 < WINDOW
    bias1 = slope * d1.astype(F32)
    new_seq, new_pos = _div_mod(lane, seq)
    d2 = tq - new_pos
    causal2 = d2 >= 0
    bias2 = slope * d2.astype(F32)
    ktn = ktn_ref[...]
    vtn = vtn_ref[...]
    ktn_b = ktn.astype(BF16)
    vn_b = vn_ref[...].astype(BF16)
    keep_old = lax.broadcasted_iota(jnp.int32, (A_KV, WINDOW), 1) < WINDOW - seq

    def body(n, carry):
        q = _read_seq_slots(qs_ref, n, rows).astype(BF16)
        ckt = ck_ref[n]
        cvt = cv_ref[n]
        s1 = jnp.where(valid1, _dot(q, ckt.astype(BF16)) - bias1, NEG)
        s2 = jnp.where(causal2 & (new_seq == n), _dot(q, ktn_b) - bias2, NEG)
        m = jnp.maximum(jnp.maximum(jnp.max(s1, axis=-1, keepdims=True), jnp.max(s2, axis=-1, keepdims=True)), sink)
        e1 = jnp.exp(s1 - m)
        e2 = jnp.exp(s2 - m)
        den = jnp.sum(e1, axis=-1, keepdims=True) + jnp.sum(e2, axis=-1, keepdims=True) + jnp.exp(sink - m)
        o = _nt_dot(e1.astype(BF16), cvt.astype(BF16)) + _dot(e2.astype(BF16), vn_b)
        _write_seq_slots(o_ref, n, rows, o / den)
        shift_new = WINDOW - seq - n * seq
        wk_ref[n] = jnp.where(keep_old, pltpu.roll(ckt, WINDOW - seq, 1), pltpu.roll(ktn, shift_new, 1))
        wv_ref[n] = jnp.where(keep_old, pltpu.roll(cvt, WINDOW - seq, 1), pltpu.roll(vtn, shift_new, 1))
        return carry

    lax.fori_loop(0, sb, body, 0, unroll=SAMPLE_SEQ_UNROLL)


def _sample_attn_win(qs, kt_new, vt_new, v_new, slope_rows, sink_rows, ckt, cvt, layer, seq):
    rows = A_HEADS * seq
    n = qs.shape[0] // rows
    assert LANES % seq == 0
    sb = LANES // seq
    assert n % sb == 0

    def seqs(*tail):
        return pl.BlockSpec((sb,) + tail, lambda i: (i,) + (0,) * len(tail))

    slot_rows = pl.BlockSpec((sb * rows, LANES), lambda i: (i, 0))
    cache = pl.BlockSpec((None, sb, A_KV, WINDOW), lambda i: (layer, i, 0, 0))
    win = jax.ShapeDtypeStruct((n, A_KV, WINDOW), F32)
    return pl.pallas_call(
        functools.partial(_sample_attn_win_kernel, sb=sb, seq=seq),
        grid=(n // sb,),
        in_specs=[
            slot_rows,
            pl.BlockSpec((A_KV, LANES), lambda i: (0, i)),
            pl.BlockSpec((A_KV, LANES), lambda i: (0, i)),
            pl.BlockSpec((LANES, A_KV), lambda i: (i, 0)),
            _const_spec((rows, 1)),
            _const_spec((rows, 1)),
            cache,
            cache,
        ],
        out_specs=[slot_rows, seqs(A_KV, WINDOW), seqs(A_KV, WINDOW)],
        out_shape=[jax.ShapeDtypeStruct((n * rows, LANES), F32), win, win],
        compiler_params=_params("arbitrary"),
        name="sample_attn_win",
    )(qs, kt_new, vt_new, v_new, slope_rows, sink_rows, ckt, cvt)


def _sample_attn_mem_kernel(xs_ref, mk_ref, mv_ref, xo_ref, *, sb, xrows):
    def body(n, carry):
        q = _read_seq_slots(xs_ref, n, xrows).astype(BF16)
        s = _dot(q, mk_ref[n].astype(BF16))
        m = jnp.max(s, axis=-1, keepdims=True)
        e = jnp.exp(s - m)
        den = jnp.sum(e, axis=-1, keepdims=True)
        _write_seq_slots(xo_ref, n, xrows, _nt_dot(e.astype(BF16), mv_ref[n].astype(BF16)) / den)
        return carry

    lax.fori_loop(0, sb, body, 0, unroll=SAMPLE_SEQ_UNROLL)


def _sample_attn_mem(xs, mkt, mvt, layer, seq):
    xrows = X_HEADS * seq
    n = xs.shape[0] // xrows
    sb = min(SAMPLE_MEM_SEQS, n)
    assert n % sb == 0
    qspec = pl.BlockSpec((sb * xrows, LANES), lambda i: (i, 0))
    cache = pl.BlockSpec((None, sb, X_WIDTH, N_MEM), lambda i: (layer, i, 0, 0))
    return pl.pallas_call(
        functools.partial(_sample_attn_mem_kernel, sb=sb, xrows=xrows),
        grid=(n // sb,),
        in_specs=[qspec, cache, cache],
        out_specs=qspec,
        out_shape=jax.ShapeDtypeStruct((n * xrows, LANES), F32),
        compiler_params=_params("arbitrary"),
        name="sample_attn_mem",
    )(xs, mkt, mvt)


def _feature_major(cache):
    l, n, p, kv, d = cache.shape
    return cache.transpose(0, 1, 3, 4, 2).reshape(l, n, kv * d, p)


def kernel(x_prompt, x_sample, cache_win_k, cache_win_v, cache_mem_k, cache_mem_v, state_conv, mem_prompt,
           norm1_g, norm2_g, mem_norm_g, w_in_a, q_norm_a, k_norm_a, sinks_a, w_out_a, w_in_b, v_norm_b, w_s_b,
           b_s_b, w_out_b, w_mem_kv, xq_norm, xk_norm, w_ffn_in, conv_w, conv_b, w_down):
    depth = norm1_g.shape[0]
    assert depth == 2 and w_in_a.shape[0] == 1 and w_in_b.shape[0] == 1
    nb = x_prompt.shape[0]
    ns, seq, _ = x_sample.shape
    assert cache_win_k.shape[2] == WINDOW

    w_in_a_b = w_in_a[0].astype(BF16)
    w_in_b_b = w_in_b[0].astype(BF16)
    w_out_a_b = w_out_a[0].astype(BF16)
    w_out_b_b = w_out_b[0].astype(BF16)
    w_mem_kv_b = w_mem_kv.astype(BF16)
    w_ffn_in_b = w_ffn_in.astype(BF16)
    w_down_b = w_down.astype(BF16)

    mk_all, mv_all, mem_kt_p, mem_vt_p = _memory_kv(mem_prompt, mem_norm_g, w_mem_kv_b, xk_norm)
    w_in_a_r = _reorder_heads(w_in_a_b, 1)
    w_out_a_r = _reorder_heads(w_out_a_b, 0)
    x2, win_k_p, win_v_p, conv_p0 = _layer_a_prompt(
        x_prompt, 0, norm1_g[0], w_in_a_r, q_norm_a[0], k_norm_a[0], xq_norm[0], sinks_a[0], mk_all, mv_all, w_out_a_r,
        norm2_g[0], w_ffn_in_b, conv_w[0], conv_b[0], w_down_b)
    y_prompt, chunk_v_p, conv_p1 = _layer_b_prompt(
        x2, 1, norm1_g[1], w_in_b_b, v_norm_b[0], w_s_b[0], b_s_b[0], xq_norm[1], mk_all, mv_all, w_out_b_b,
        norm2_g[1], w_ffn_in_b, conv_w[1], conv_b[1], w_down_b)

    xs3 = x_sample.transpose(1, 0, 2)
    mkt = _feature_major(cache_mem_k)
    mvt = _feature_major(cache_mem_v)

    def by_seq_cols(a):
        return a.reshape(a.shape[0], seq, ns).transpose(0, 2, 1).reshape(a.shape[0], ns * seq)

    qs, v_new, xslots, kt_new, vt_new = _sample_in_a(xs3, norm1_g[0], w_in_a_r, q_norm_a[0], k_norm_a[0], xq_norm[0])
    slope_rows = jnp.repeat(jnp.asarray(SLOPES, F32), seq).reshape(A_HEADS * seq, 1)
    sink_rows = jnp.repeat(sinks_a[0].astype(F32), seq).reshape(A_HEADS * seq, 1)
    o_s, win_kt_s, win_vt_s = _sample_attn_win(
        qs, by_seq_cols(kt_new), by_seq_cols(vt_new),
        v_new.reshape(seq, ns, A_KV).transpose(1, 0, 2).reshape(ns * seq, A_KV), slope_rows, sink_rows,
        _feature_major(cache_win_k), _feature_major(cache_win_v), 0, seq)
    xo_s = _sample_attn_mem(xslots, mkt, mvt, 0, seq)
    ys1, conv_s0 = _ffn_sample(
        xs3, o_s, xo_s, w_out_a_r, 0, norm2_g[0], w_ffn_in_b, conv_w[0], conv_b[0], w_down_b, state_conv[0])

    o_b, vn_s, xslots = _sample_in_b(ys1, norm1_g[1], w_in_b_b, v_norm_b[0], w_s_b[0], b_s_b[0], xq_norm[1])
    xo_s = _sample_attn_mem(xslots, mkt, mvt, 1, seq)
    ys2, conv_s1 = _ffn_sample(
        ys1, o_b.reshape(seq, ns, B_WIDTH), xo_s, w_out_b_b, 1, norm2_g[1], w_ffn_in_b, conv_w[1], conv_b[1], w_down_b,
        state_conv[1])

    def position_major(ft):
        lead = ft.shape[:-2]
        nd = len(lead)
        return ft.reshape(*lead, A_KV_HEADS, HEAD_DIM, ft.shape[-1]).transpose(*range(nd), nd + 2, nd, nd + 1)

    return (
        y_prompt,
        ys2.transpose(1, 0, 2),
        position_major(win_k_p)[None],
        position_major(win_v_p)[None],
        chunk_v_p.reshape(1, nb, CHUNK, B_WIDTH),
        position_major(mem_kt_p),
        position_major(mem_vt_p),
        jnp.stack([conv_p0, conv_p1]),
        position_major(win_kt_s)[None],
        position_major(win_vt_s)[None],
        vn_s.reshape(seq, ns, B_WIDTH).transpose(1, 0, 2)[None],
        jnp.stack([conv_s0.transpose(1, 0, 2), conv_s1.transpose(1, 0, 2)]),
    )
```
